```python
import math
import jax, jax.numpy as jnp
from jax import lax
import numpy as np

D_MODEL = 2048
BATCH = 1
SEQ = 8192
DEPTH = 2

N_EVEN = (DEPTH + 1) // 2
N_ODD = DEPTH // 2

S5_WIDTH = D_MODEL // 2
S5_GROUP = 16
S5_GROUPS = S5_WIDTH // S5_GROUP
S5_STATE = 64
S5_DT_MIN = 1e-3
S5_DT_MAX = 1e-1
S5_MAX_RE = -1e-4

GLA_HEADS = 4
GLA_DV = (D_MODEL // 2) // GLA_HEADS
GLA_DK = GLA_DV // 2
GLA_GATE_RANK = 16
GLA_GATE_TEMP = 16.0
GLA_CHUNK = 16
GLA_EPS = 1e-6

GLA_QK = GLA_HEADS * GLA_DK
GLA_V = GLA_HEADS * GLA_DV
AB_IN_WIDTH = S5_WIDTH + 2 * GLA_QK + 2 * GLA_V + GLA_GATE_RANK
AB_OUT_WIDTH = S5_WIDTH + GLA_V

ATT_HEADS = 16
ATT_HEAD_DIM = D_MODEL // ATT_HEADS
DILATED_GROUPS = ((128, 1), (512, 4), (2048, 16))
MAX_WINDOW = 2048
ATT_BLOCK = 128

N_EXPERTS = 32
TOP_K = 4
D_FF = D_MODEL
SWIGLU_LIMIT = 7.0
SWIGLU_ALPHA = 1.702
MOE_BLOCK = 128

DN_ALPHA = (2 * DEPTH) ** 0.25
DN_BETA = (8 * DEPTH) ** -0.25
LN_EPS = 1e-5

kernel_name = "hybrid_s5_gla_dilated_moe_deepnorm"


def layer_norm(x, g, b):
    xf = x.astype(jnp.float32)
    mu = jnp.mean(xf, axis=-1, keepdims=True)
    var = jnp.mean(jnp.square(xf - mu), axis=-1, keepdims=True)
    return ((xf - mu) * lax.rsqrt(var + LN_EPS) * g + b).astype(x.dtype)


def s5_mixer(u, lam_re, lam_im, log_step, b_re, b_im, c_re, c_im, d_skip, w_glu, b_glu):
    f32 = jnp.float32
    bsz, L, _ = u.shape
    uf = u.astype(f32)
    ug = uf.reshape(bsz, L, S5_GROUPS, S5_GROUP)
    lr = jnp.minimum(lam_re.astype(f32), S5_MAX_RE)
    li = lam_im.astype(f32)
    dt = jnp.exp(log_step.astype(f32))[:, None]
    mag = jnp.exp(lr * dt)
    ab_re = mag * jnp.cos(li * dt)
    ab_im = mag * jnp.sin(li * dt)
    den = lr * lr + li * li
    nr = ab_re - 1.0
    f_re = (nr * lr + ab_im * li) / den
    f_im = (ab_im * lr - nr * li) / den
    br = b_re.astype(f32)
    bi = b_im.astype(f32)
    bb_re = f_re[..., None] * br - f_im[..., None] * bi
    bb_im = f_re[..., None] * bi + f_im[..., None] * br
    x_re = jnp.einsum('blgh,gph->blgp', ug, bb_re)
    x_im = jnp.einsum('blgh,gph->blgp', ug, bb_im)
    a_re = jnp.broadcast_to(ab_re, x_re.shape)
    a_im = jnp.broadcast_to(ab_im, x_im.shape)

    def combine(left, right):
        a1r, a1i, b1r, b1i = left
        a2r, a2i, b2r, b2i = right
        return (a2r * a1r - a2i * a1i,
                a2r * a1i + a2i * a1r,
                a2r * b1r - a2i * b1i + b2r,
                a2r * b1i + a2i * b1r + b2i)

    _, _, s_re, s_im = lax.associative_scan(combine, (a_re, a_im, x_re, x_im), axis=1)
    y = (jnp.einsum('blgp,ghp->blgh', s_re, c_re.astype(f32))
         - jnp.einsum('blgp,ghp->blgh', s_im, c_im.astype(f32)))
    y = y.reshape(bsz, L, S5_WIDTH) + d_skip.astype(f32) * uf
    z = jax.nn.gelu(y)
    out = z * jax.nn.sigmoid(z @ w_glu.astype(f32) + b_glu.astype(f32))
    return out.astype(u.dtype)


def gla_mixer(q, k, v, g_low, r, w_gate2, b_gate2, norm_g):
    f32 = jnp.float32
    bsz, L, _ = q.shape
    C = GLA_CHUNK
    nC = L // C
    H, dk, dv = GLA_HEADS, GLA_DK, GLA_DV
    qf = q.astype(f32).reshape(bsz, nC, C, H, dk) * (GLA_DK ** -0.5)
    kf = k.astype(f32).reshape(bsz, nC, C, H, dk)
    vf = v.astype(f32).reshape(bsz, nC, C, H, dv)
    log_a = jax.nn.log_sigmoid(g_low.astype(f32) @ w_gate2.astype(f32) + b_gate2.astype(f32)) / GLA_GATE_TEMP
    log_a = log_a.reshape(bsz, nC, C, H, dk)
    bcum = jnp.cumsum(log_a, axis=2)
    b_last = bcum[:, :, -1]
    causal = jnp.tril(jnp.ones((C, C), dtype=bool))
    diff = bcum[:, :, :, None] - bcum[:, :, None, :]
    diff = jnp.where(causal[:, :, None, None], diff, -jnp.inf)
    att = jnp.sum(qf[:, :, :, None] * kf[:, :, None] * jnp.exp(diff), axis=-1)
    o_intra = jnp.einsum('bnijh,bnjhv->bnihv', att, vf)
    k_dec = kf * jnp.exp(b_last[:, :, None] - bcum)
    kv = jnp.einsum('bnjhk,bnjhv->bnhkv', k_dec, vf)
    decay = jnp.exp(b_last)

    def step(S, inp):
        dec, upd = inp
        return dec[..., None] * S + upd, S

    S0 = jnp.zeros((bsz, H, dk, dv), f32)
    _, S_in = lax.scan(step, S0, (jnp.moveaxis(decay, 1, 0), jnp.moveaxis(kv, 1, 0)))
    S_in = jnp.moveaxis(S_in, 0, 1)
    o_inter = jnp.einsum('bnihk,bnhkv->bnihv', qf * jnp.exp(bcum), S_in)
    o = o_intra + o_inter
    o = o * lax.rsqrt(jnp.mean(o * o, axis=-1, keepdims=True) + GLA_EPS) * norm_g.astype(f32)
    o = o.reshape(bsz, L, H * dv) * jax.nn.silu(r.astype(f32))
    return o.astype(q.dtype)


def even_mixer(x, w_in, lam_re, lam_im, log_step, b_re, b_im, c_re, c_im, d_skip, w_glu, b_glu,
               w_gate2, b_gate2, norm_g, w_out):
    h = x @ w_in
    s1 = S5_WIDTH
    s2 = s1 + GLA_QK
    s3 = s2 + GLA_QK
    s4 = s3 + GLA_V
    s5 = s4 + GLA_GATE_RANK
    u, q, k, v, g_low, r = jnp.split(h, [s1, s2, s3, s4, s5], axis=-1)
    ya = s5_mixer(u, lam_re, lam_im, log_step, b_re, b_im, c_re, c_im, d_skip, w_glu, b_glu)
    yb = gla_mixer(q, k, v, g_low, r, w_gate2, b_gate2, norm_g)
    return jnp.concatenate([ya, yb], axis=-1) @ w_out


def dilated_attention(q, k, v):
    bsz, L, H, Dh = q.shape
    kp = jnp.pad(k, ((0, 0), (MAX_WINDOW, 0), (0, 0), (0, 0)))
    vp = jnp.pad(v, ((0, 0), (MAX_WINDOW, 0), (0, 0), (0, 0)))
    nblk = L // ATT_BLOCK
    qb = jnp.moveaxis(q.reshape(bsz, nblk, ATT_BLOCK, H, Dh), 1, 0)
    starts = jnp.arange(nblk, dtype=jnp.int32) * ATT_BLOCK
    qi = np.arange(ATT_BLOCK, dtype=np.int32)
    scale = Dh ** -0.5

    def block(args):
        q_blk, start = args
        k_win = lax.dynamic_slice_in_dim(kp, start, ATT_BLOCK + MAX_WINDOW, axis=1)
        v_win = lax.dynamic_slice_in_dim(vp, start, ATT_BLOCK + MAX_WINDOW, axis=1)
        outs, lses = [], []
        for window, dil in DILATED_GROUPS:
            j = np.arange(window // dil + 1, dtype=np.int32)
            rel = (qi[:, None] - dil * j[None, :]).astype(np.int32)
            kg = k_win[:, rel + MAX_WINDOW]
            vg = v_win[:, rel + MAX_WINDOW]
            s = jnp.einsum('bqhd,bqjhd->bhqj', q_blk, kg).astype(jnp.float32) * scale
            s = jnp.where((start + rel >= 0)[None, None], s, -jnp.inf)
            lse = jax.nn.logsumexp(s, axis=-1)
            p = jnp.exp(s - lse[..., None])
            outs.append(jnp.einsum('bhqj,bqjhd->bqhd', p, vg.astype(jnp.float32)))
            lses.append(lse)
        w = jax.nn.softmax(jnp.stack(lses), axis=0)
        w = jnp.moveaxis(w, 2, 3)[..., None]
        return jnp.sum(w * jnp.stack(outs), axis=0)

    out = lax.map(block, (qb, starts))
    return jnp.moveaxis(out, 0, 1).reshape(bsz, L, H, Dh).astype(q.dtype)


def odd_mixer(x, w_qkv, w_o):
    bsz, L, _ = x.shape
    qkv = (x @ w_qkv).reshape(bsz, L, 3, ATT_HEADS, ATT_HEAD_DIM)
    y = dilated_attention(qkv[:, :, 0], qkv[:, :, 1], qkv[:, :, 2])
    return y.reshape(bsz, L, D_MODEL) @ w_o


def moe(x, w_router, b_router, w_gu, b_gu, w_down, b_down):
    bsz, L, D = x.shape
    T = bsz * L
    xt = x.reshape(T, D)
    logits = (xt @ w_router + b_router).astype(jnp.float32)
    top_val, top_idx = lax.top_k(logits, TOP_K)
    gates = jax.nn.softmax(top_val, axis=-1)
    A = T * TOP_K
    flat_e = top_idx.reshape(A).astype(jnp.int32)
    flat_t = jnp.arange(A, dtype=jnp.int32) // TOP_K
    flat_g = gates.reshape(A)
    order = jnp.argsort(flat_e)
    se, st, sg = flat_e[order], flat_t[order], flat_g[order]
    counts = jnp.bincount(flat_e, length=N_EXPERTS)
    start = jnp.cumsum(counts) - counts
    padded = (counts + MOE_BLOCK - 1) // MOE_BLOCK * MOE_BLOCK
    pend = jnp.cumsum(padded)
    pstart = pend - padded
    dest = pstart[se] + jnp.arange(A, dtype=jnp.int32) - start[se]
    P = A + N_EXPERTS * MOE_BLOCK
    n_blocks = P // MOE_BLOCK
    buf_t = jnp.zeros((P,), jnp.int32).at[dest].set(st)
    buf_g = jnp.zeros((P,), jnp.float32).at[dest].set(sg)
    blk_e = jnp.minimum(jnp.searchsorted(pend, jnp.arange(n_blocks, dtype=jnp.int32) * MOE_BLOCK, side='right'),
                        N_EXPERTS - 1).astype(jnp.int32)

    def expert_block(args):
        e, tok = args
        h = xt[tok] @ w_gu[e] + b_gu[e]
        gate = jnp.minimum(h[:, :D_FF], SWIGLU_LIMIT)
        up = jnp.clip(h[:, D_FF:], -SWIGLU_LIMIT, SWIGLU_LIMIT)
        act = (up + 1.0) * (gate * jax.nn.sigmoid(SWIGLU_ALPHA * gate))
        return act @ w_down[e] + b_down[e]

    yb = lax.map(expert_block, (blk_e, buf_t.reshape(n_blocks, MOE_BLOCK)))
    y = yb.reshape(P, D) * buf_g[:, None]
    out = jax.ops.segment_sum(y, buf_t, num_segments=T)
    return out.reshape(bsz, L, D).astype(x.dtype)


def setup_inputs(seed: int = 0) -> dict:
    key = jax.random.key(seed)
    ks = jax.random.split(key, 32)
    nrm = jax.random.normal
    f32 = jnp.float32
    lam_im0 = jnp.pi * jnp.arange(S5_STATE, dtype=f32)
    return {
        "x": nrm(ks[0], (BATCH, SEQ, D_MODEL), f32),
        "ab_w_in": nrm(ks[1], (N_EVEN, D_MODEL, AB_IN_WIDTH), f32) * D_MODEL ** -0.5,
        "s5_lam_re": -0.5 + 0.01 * nrm(ks[2], (N_EVEN, S5_GROUPS, S5_STATE), f32),
        "s5_lam_im": lam_im0 + 0.01 * nrm(ks[3], (N_EVEN, S5_GROUPS, S5_STATE), f32),
        "s5_log_step": jax.random.uniform(ks[4], (N_EVEN, S5_GROUPS), f32,
                                          minval=math.log(S5_DT_MIN), maxval=math.log(S5_DT_MAX)),
        "s5_b_re": nrm(ks[5], (N_EVEN, S5_GROUPS, S5_STATE, S5_GROUP), f32) * (2 * S5_GROUP) ** -0.5,
        "s5_b_im": nrm(ks[6], (N_EVEN, S5_GROUPS, S5_STATE, S5_GROUP), f32) * (2 * S5_GROUP) ** -0.5,
        "s5_c_re": nrm(ks[7], (N_EVEN, S5_GROUPS, S5_GROUP, S5_STATE), f32) * (2 * S5_STATE) ** -0.5,
        "s5_c_im": nrm(ks[8], (N_EVEN, S5_GROUPS, S5_GROUP, S5_STATE), f32) * (2 * S5_STATE) ** -0.5,
        "s5_d": nrm(ks[9], (N_EVEN, S5_WIDTH), f32),
        "s5_w_glu": nrm(ks[10], (N_EVEN, S5_WIDTH, S5_WIDTH), f32) * S5_WIDTH ** -0.5,
        "s5_b_glu": 0.01 * nrm(ks[11], (N_EVEN, S5_WIDTH), f32),
        "gla_w_gate2": nrm(ks[12], (N_EVEN, GLA_GATE_RANK, GLA_QK), f32) * GLA_GATE_RANK ** -0.5,
        "gla_b_gate2": 0.01 * nrm(ks[13], (N_EVEN, GLA_QK), f32),
        "gla_norm_g": 1.0 + 0.01 * nrm(ks[14], (N_EVEN, GLA_DV), f32),
        "ab_w_out": nrm(ks[15], (N_EVEN, AB_OUT_WIDTH, D_MODEL), f32) * (AB_OUT_WIDTH ** -0.5 * DN_BETA),
        "c_w_qkv": nrm(ks[16], (N_ODD, D_MODEL, 3 * D_MODEL), f32) * D_MODEL ** -0.5,
        "c_w_o": nrm(ks[17], (N_ODD, D_MODEL, D_MODEL), f32) * (D_MODEL ** -0.5 * DN_BETA),
        "ln1_g": 1.0 + 0.01 * nrm(ks[18], (DEPTH, D_MODEL), f32),
        "ln1_b": 0.01 * nrm(ks[19], (DEPTH, D_MODEL), f32),
        "moe_w_router": nrm(ks[20], (DEPTH, D_MODEL, N_EXPERTS), f32) * D_MODEL ** -0.5,
        "moe_b_router": 0.01 * nrm(ks[21], (DEPTH, N_EXPERTS), f32),
        "moe_w_gu": nrm(ks[22], (DEPTH, N_EXPERTS, D_MODEL, 2 * D_FF), f32) * D_MODEL ** -0.5,
        "moe_b_gu": 0.01 * nrm(ks[23], (DEPTH, N_EXPERTS, 2 * D_FF), f32),
        "moe_w_down": nrm(ks[24], (DEPTH, N_EXPERTS, D_FF, D_MODEL), f32) * (D_FF ** -0.5 * DN_BETA),
        "moe_b_down": 0.01 * nrm(ks[25], (DEPTH, N_EXPERTS, D_MODEL), f32),
        "ln2_g": 1.0 + 0.01 * nrm(ks[26], (DEPTH, D_MODEL), f32),
        "ln2_b": 0.01 * nrm(ks[27], (DEPTH, D_MODEL), f32),
    }


def reference(x, ab_w_in, s5_lam_re, s5_lam_im, s5_log_step, s5_b_re, s5_b_im, s5_c_re, s5_c_im,
              s5_d, s5_w_glu, s5_b_glu, gla_w_gate2, gla_b_gate2, gla_norm_g, ab_w_out,
              c_w_qkv, c_w_o, ln1_g, ln1_b, moe_w_router, moe_b_router, moe_w_gu, moe_b_gu,
              moe_w_down, moe_b_down, ln2_g, ln2_b):
    for layer in range(DEPTH):
        i = layer // 2
        if layer % 2 == 0:
            mix = even_mixer(x, ab_w_in[i], s5_lam_re[i], s5_lam_im[i], s5_log_step[i], s5_b_re[i],
                             s5_b_im[i], s5_c_re[i], s5_c_im[i], s5_d[i], s5_w_glu[i], s5_b_glu[i],
                             gla_w_gate2[i], gla_b_gate2[i], gla_norm_g[i], ab_w_out[i])
        else:
            mix = odd_mixer(x, c_w_qkv[i], c_w_o[i])
        x = layer_norm(DN_ALPHA * x + mix, ln1_g[layer], ln1_b[layer])
        ffn = moe(x, moe_w_router[layer], moe_b_router[layer], moe_w_gu[layer], moe_b_gu[layer],
                  moe_w_down[layer], moe_b_down[layer])
        x = layer_norm(DN_ALPHA * x + ffn, ln2_g[layer], ln2_b[layer])
    return x
```

```python
import functools
import math

import jax
import jax.numpy as jnp
from jax import lax
from jax.experimental import pallas as pl
from jax.experimental.pallas import tpu as pltpu

F32 = jnp.float32
BF16 = jnp.bfloat16
I32 = jnp.int32
HIGHEST = lax.Precision.HIGHEST

LANES = 128
VMEM_LIMIT_BYTES = 56 * 1024 * 1024

S5_GROUP = 16
S5_STATE = 64
S5_MAX_RE = -1e-4
S5_CHUNK = 16
GLA_HEADS = 4
GLA_GATE_TEMP = 16.0
GLA_EPS = 1e-6
GLA_BLOCK = 64
GLA_SUB = 16
ATT_HEADS = 16
DIL = 16
DILATED_GROUPS = ((128, 1), (512, 4), (2048, 16))
TOP_K = 4
SWIGLU_LIMIT = 7.0
SWIGLU_ALPHA = 1.702
MOE_BLOCK = 128
MOE_CHUNK_BLOCKS = 12
LN_EPS = 1e-5
NEG_BIG = -1e30


def _cparams(semantics):
    return pltpu.CompilerParams(dimension_semantics=semantics, vmem_limit_bytes=VMEM_LIMIT_BYTES)


def _layer_norm_rows(z, g, b):
    mu = jnp.mean(z, axis=-1, keepdims=True)
    zc = z - mu
    var = jnp.mean(zc * zc, axis=-1, keepdims=True)
    return zc * lax.rsqrt(var + LN_EPS) * g + b


def _proj_in_kernel(x_ref, w_ref, wg_ref, h_ref, g_ref, xb_ref):
    @pl.when(pl.program_id(1) == 0)
    def _():
        x = x_ref[...]
        xb_ref[...] = x.astype(BF16)
        g_ref[...] = jnp.dot(x, wg_ref[...], preferred_element_type=F32, precision=HIGHEST)

    h_ref[...] = jnp.dot(xb_ref[...], w_ref[...], preferred_element_type=F32)


def _proj_in(x, w_main, w_gate, *, tm=1024, tn=512):
    T, D = x.shape
    N = w_main.shape[1]
    return pl.pallas_call(
        _proj_in_kernel,
        name="proj_in",
        grid=(T // tm, N // tn),
        in_specs=[pl.BlockSpec((tm, D), lambda i, j: (i, 0)),
                  pl.BlockSpec((D, tn), lambda i, j: (0, j)),
                  pl.BlockSpec((D, LANES), lambda i, j: (0, 0))],
        out_specs=[pl.BlockSpec((tm, tn), lambda i, j: (i, j)),
                   pl.BlockSpec((tm, LANES), lambda i, j: (i, 0))],
        out_shape=[jax.ShapeDtypeStruct((T, N), F32), jax.ShapeDtypeStruct((T, LANES), F32)],
        scratch_shapes=[pltpu.VMEM((tm, D), BF16)],
        compiler_params=_cparams(("parallel", "arbitrary")),
    )(x, w_main, w_gate)


def _proj_perm_kernel(x_ref, w_ref, o_ref, xb_ref):
    @pl.when(pl.program_id(1) == 0)
    def _():
        xb_ref[...] = x_ref[...].astype(BF16)

    o_ref[...] = jnp.dot(xb_ref[...], w_ref[...], preferred_element_type=F32).astype(o_ref.dtype)


def _proj_perm(x, w, *, tn=512):
    T, D = x.shape
    N = w.shape[1]
    na = T // DIL
    x3 = x.reshape(na, DIL * D)
    return pl.pallas_call(
        _proj_perm_kernel,
        name="proj_perm",
        grid=(DIL, N // tn),
        in_specs=[pl.BlockSpec((na, D), lambda r, j: (0, r)),
                  pl.BlockSpec((D, tn), lambda r, j: (0, j))],
        out_specs=pl.BlockSpec((na, tn), lambda r, j: (r, j)),
        out_shape=jax.ShapeDtypeStruct((T, N), BF16),
        scratch_shapes=[pltpu.VMEM((na, D), BF16)],
        compiler_params=_cparams(("parallel", "arbitrary")),
    )(x3, w)


def _proj_ln_kernel(*refs, n_lhs, alpha, n_tiles):
    lhs_refs = refs[:n_lhs]
    w_refs = refs[n_lhs:2 * n_lhs]
    res_ref, g_ref, b_ref, o_ref, acc_ref = refs[2 * n_lhs:]
    j = pl.program_id(1)
    y = jnp.dot(lhs_refs[0][...], w_refs[0][...], preferred_element_type=F32)
    for a_ref, w_ref in zip(lhs_refs[1:], w_refs[1:]):
        y = y + jnp.dot(a_ref[...], w_ref[...], preferred_element_type=F32)
    acc_ref[j] = y

    @pl.when(j == n_tiles - 1)
    def _():
        tn = acc_ref.shape[2]
        z = [alpha * res_ref[:, t * tn:(t + 1) * tn] + acc_ref[t] for t in range(n_tiles)]
        n = float(n_tiles * tn)
        mu = sum(jnp.sum(zt, axis=-1, keepdims=True) for zt in z) / n
        zc = [zt - mu for zt in z]
        var = sum(jnp.sum(zt * zt, axis=-1, keepdims=True) for zt in zc) / n
        rstd = lax.rsqrt(var + LN_EPS)
        for t in range(n_tiles):
            sl = slice(t * tn, (t + 1) * tn)
            o_ref[:, sl] = zc[t] * rstd * g_ref[:, sl] + b_ref[:, sl]


def _proj_ln(lhs, ws, res, g, b, *, alpha, permuted, tm=512, tn=512):
    T, N = res.shape
    n_lhs = len(lhs)
    n_tiles = N // tn
    if permuted:
        tm = T // DIL
        res_in = res.reshape(tm, DIL * N)
        res_spec = pl.BlockSpec((tm, N), lambda i, j: (0, i))
        out_spec = pl.BlockSpec((tm, N), lambda i, j: (0, i))
        out_shape = jax.ShapeDtypeStruct((tm, DIL * N), F32)
    else:
        res_in = res
        res_spec = pl.BlockSpec((tm, N), lambda i, j: (i, 0))
        out_spec = pl.BlockSpec((tm, N), lambda i, j: (i, 0))
        out_shape = jax.ShapeDtypeStruct((T, N), F32)
    in_specs = ([pl.BlockSpec((tm, a.shape[1]), lambda i, j: (i, 0)) for a in lhs]
                + [pl.BlockSpec((w.shape[0], tn), lambda i, j: (0, j)) for w in ws]
                + [res_spec,
                   pl.BlockSpec((1, N), lambda i, j: (0, 0)),
                   pl.BlockSpec((1, N), lambda i, j: (0, 0))])
    out = pl.pallas_call(
        functools.partial(_proj_ln_kernel, n_lhs=n_lhs, alpha=alpha, n_tiles=n_tiles),
        name="proj_ln",
        grid=(T // tm, n_tiles),
        in_specs=in_specs,
        out_specs=out_spec,
        out_shape=out_shape,
        scratch_shapes=[pltpu.VMEM((n_tiles, tm, tn), F32)],
        compiler_params=_cparams(("parallel", "arbitrary")),
    )(*lhs, *ws, res_in, g.reshape(1, N), b.reshape(1, N))
    return out.reshape(T, N)


def _s5_tables(lam_re, lam_im, log_step, b_re, b_im, c_re, c_im, n_chunks):
    C = S5_CHUNK
    G, P = lam_re.shape
    H = b_re.shape[-1]
    lr = jnp.minimum(lam_re.astype(F32), S5_MAX_RE)
    li = lam_im.astype(F32)
    dt = jnp.exp(log_step.astype(F32))[:, None]
    mag = jnp.exp(lr * dt)
    a_re = mag * jnp.cos(li * dt)
    a_im = mag * jnp.sin(li * dt)
    den = lr * lr + li * li
    nr = a_re - 1.0
    f_re = (nr * lr + a_im * li) / den
    f_im = (a_im * lr - nr * li) / den
    br = b_re.astype(F32)
    bi = b_im.astype(F32)
    bb_re = f_re[..., None] * br - f_im[..., None] * bi
    bb_im = f_re[..., None] * bi + f_im[..., None] * br
    pw_re = [jnp.ones_like(a_re)]
    pw_im = [jnp.zeros_like(a_im)]
    for _ in range(C):
        pr, pi = pw_re[-1], pw_im[-1]
        pw_re.append(pr * a_re - pi * a_im)
        pw_im.append(pr * a_im + pi * a_re)
    pw_re = jnp.stack(pw_re)
    pw_im = jnp.stack(pw_im)
    ab_re = pw_re[:C, :, :, None] * bb_re[None] - pw_im[:C, :, :, None] * bb_im[None]
    ab_im = pw_re[:C, :, :, None] * bb_im[None] + pw_im[:C, :, :, None] * bb_re[None]
    cr = c_re.astype(F32)
    ci = c_im.astype(F32)
    z_re = jnp.transpose(ab_re[::-1], (1, 0, 3, 2)).reshape(G, C * H, P)
    z_im = jnp.transpose(ab_im[::-1], (1, 0, 3, 2)).reshape(G, C * H, P)
    zmat = jnp.concatenate([z_re, z_im], axis=-1)
    kern = (jnp.einsum('gop,kgph->kgoh', cr, ab_re, precision=HIGHEST)
            - jnp.einsum('gop,kgph->kgoh', ci, ab_im, precision=HIGHEST))
    lag = jnp.arange(C)[None, :] - jnp.arange(C)[:, None]
    kl = kern[jnp.clip(lag, 0, C - 1)]
    kl = jnp.where((lag >= 0)[:, :, None, None, None], kl, 0.0)
    mmat = jnp.transpose(kl, (2, 0, 4, 1, 3)).reshape(G, C * H, C * H)
    ca_re = cr[None] * pw_re[1:, :, None, :] - ci[None] * pw_im[1:, :, None, :]
    ca_im = cr[None] * pw_im[1:, :, None, :] + ci[None] * pw_re[1:, :, None, :]
    n_re = jnp.transpose(ca_re, (1, 3, 0, 2)).reshape(G, P, C * H)
    n_im = jnp.transpose(-ca_im, (1, 3, 0, 2)).reshape(G, P, C * H)
    nmat = jnp.concatenate([n_re, n_im], axis=1)
    n_steps = max(1, (n_chunks - 1).bit_length())
    qr, qi = pw_re[C], pw_im[C]
    a1, a2 = [], []
    for _ in range(n_steps):
        a1.append(jnp.concatenate([qr, qr], axis=-1))
        a2.append(jnp.concatenate([-qi, qi], axis=-1))
        qr, qi = qr * qr - qi * qi, 2.0 * qr * qi
    a1 = jnp.stack(a1, axis=1)
    a2 = jnp.stack(a2, axis=1)
    return zmat.astype(BF16), mmat.astype(BF16), nmat.astype(BF16), a1, a2


def _s5_kernel(u_ref, z_ref, m_ref, n_ref, a1_ref, a2_ref, y_ref, *, n_steps):
    u = u_ref[...]
    nc = u.shape[0]
    s = jnp.dot(u, z_ref[...], preferred_element_type=F32)
    half = s.shape[1] // 2
    row = lax.broadcasted_iota(I32, s.shape, 0)
    for k in range(n_steps):
        sh = 1 << k
        prev = jnp.where(row >= sh, pltpu.roll(s, sh, axis=0), 0.0)
        s = s + a1_ref[k:k + 1, :] * prev + a2_ref[k:k + 1, :] * pltpu.roll(prev, half, axis=1)
    s_in = jnp.where(row >= 1, pltpu.roll(s, 1, axis=0), 0.0)
    y_ref[...] = (jnp.dot(u, m_ref[...], preferred_element_type=F32)
                  + jnp.dot(s_in.astype(BF16), n_ref[...], preferred_element_type=F32))


def _s5_core(u, tables):
    zmat, mmat, nmat, a1, a2 = tables
    T, W = u.shape
    G = zmat.shape[0]
    C, H = S5_CHUNK, S5_GROUP
    nc = T // C
    CH = C * H
    P2 = zmat.shape[2]
    n_steps = a1.shape[1]
    ug = u.astype(BF16).reshape(nc, C, G, H).transpose(2, 0, 1, 3).reshape(G, nc, CH)
    yg = pl.pallas_call(
        functools.partial(_s5_kernel, n_steps=n_steps),
        name="s5_core",
        grid=(G,),
        in_specs=[pl.BlockSpec((None, nc, CH), lambda g: (g, 0, 0)),
                  pl.BlockSpec((None, CH, P2), lambda g: (g, 0, 0)),
                  pl.BlockSpec((None, CH, CH), lambda g: (g, 0, 0)),
                  pl.BlockSpec((None, P2, CH), lambda g: (g, 0, 0)),
                  pl.BlockSpec((None, n_steps, P2), lambda g: (g, 0, 0)),
                  pl.BlockSpec((None, n_steps, P2), lambda g: (g, 0, 0))],
        out_specs=pl.BlockSpec((None, nc, CH), lambda g: (g, 0, 0)),
        out_shape=jax.ShapeDtypeStruct((G, nc, CH), F32),
        compiler_params=_cparams(("parallel",)),
    )(ug, zmat, mmat, nmat, a1, a2)
    return yg.reshape(G, nc, C, H).transpose(1, 2, 0, 3).reshape(T, W)


def _s5_glu_kernel(y_ref, u_ref, d_ref, w_ref, b_ref, o_ref):
    y = y_ref[...] + d_ref[...] * u_ref[...]
    c0 = math.sqrt(2.0 / math.pi)
    z = 0.5 * y * (1.0 + jnp.tanh(c0 * (y + 0.044715 * (y * y * y))))
    lin = jnp.dot(z.astype(BF16), w_ref[...], preferred_element_type=F32) + b_ref[...]
    o_ref[...] = (z * jax.nn.sigmoid(lin)).astype(o_ref.dtype)


def _s5_glu(y, h, d_skip, w_glu, b_glu, *, tm=512):
    T, W = y.shape
    return pl.pallas_call(
        _s5_glu_kernel,
        name="s5_glu",
        grid=(T // tm,),
        in_specs=[pl.BlockSpec((tm, W), lambda i: (i, 0)),
                  pl.BlockSpec((tm, W), lambda i: (i, 0)),
                  pl.BlockSpec((1, W), lambda i: (0, 0)),
                  pl.BlockSpec((W, W), lambda i: (0, 0)),
                  pl.BlockSpec((1, W), lambda i: (0, 0))],
        out_specs=pl.BlockSpec((tm, W), lambda i: (i, 0)),
        out_shape=jax.ShapeDtypeStruct((T, W), BF16),
        compiler_params=_cparams(("parallel",)),
    )(y, h, d_skip.reshape(1, W), w_glu, b_glu.reshape(1, W))


def _gla_kernel(q_ref, k_ref, v_ref, r_ref, g_ref, w2_ref, b2_ref, ng_ref, o_ref, st_ref, *, dk, dv):
    @pl.when(pl.program_id(0) == 0)
    def _():
        st_ref[...] = jnp.zeros_like(st_ref)

    cb = q_ref.shape[0]
    n_sub = cb // GLA_SUB
    scale = dk ** -0.5
    logit = jnp.dot(g_ref[...], w2_ref[...], preferred_element_type=F32, precision=HIGHEST) + b2_ref[...]
    log_a = (jnp.minimum(logit, 0.0) - jnp.log(1.0 + jnp.exp(-jnp.abs(logit)))) / GLA_GATE_TEMP
    ri = lax.broadcasted_iota(I32, (cb, cb), 0)
    ci = lax.broadcasted_iota(I32, (cb, cb), 1)
    tri = (ri >= ci).astype(F32)
    bcum = jnp.dot(tri, log_a, preferred_element_type=F32, precision=HIGHEST)
    for hh in range(GLA_HEADS):
        ks = slice(hh * dk, (hh + 1) * dk)
        vs = slice(hh * dv, (hh + 1) * dv)
        b = bcum[:, ks]
        q = q_ref[:, ks] * scale
        k = k_ref[:, ks]
        v = v_ref[:, vs].astype(BF16)
        refs = [jnp.zeros((1, dk), F32)] + [b[a * GLA_SUB - 1:a * GLA_SUB, :] for a in range(1, n_sub)]
        refmat = jnp.concatenate([jnp.broadcast_to(r, (GLA_SUB, dk)) for r in refs], axis=0)
        qe = (q * jnp.exp(b - refmat)).astype(BF16)
        st = st_ref[hh]
        o_inter = lax.dot_general((q * jnp.exp(b)).astype(BF16), st.astype(BF16),
                                  (((1,), (1,)), ((), ())), preferred_element_type=F32)
        o_rows = []
        for a in range(n_sub):
            hi = (a + 1) * GLA_SUB
            ke = (k[:hi] * jnp.exp(refs[a] - b[:hi])).astype(BF16)
            att = lax.dot_general(qe[a * GLA_SUB:hi], ke, (((1,), (1,)), ((), ())),
                                  preferred_element_type=F32)
            row_a = lax.broadcasted_iota(I32, (GLA_SUB, hi), 0) + a * GLA_SUB
            att = jnp.where(lax.broadcasted_iota(I32, (GLA_SUB, hi), 1) <= row_a, att, 0.0)
            o_rows.append(jnp.dot(att.astype(BF16), v[:hi], preferred_element_type=F32))
        o = jnp.concatenate(o_rows, axis=0) + o_inter
        o = o * lax.rsqrt(jnp.mean(o * o, axis=-1, keepdims=True) + GLA_EPS) * ng_ref[...]
        r = r_ref[:, vs]
        o_ref[:, vs] = (o * (r * jax.nn.sigmoid(r))).astype(o_ref.dtype)
        b_last = b[cb - 1:cb, :]
        kd = (k * jnp.exp(b_last - b)).astype(BF16)
        upd = lax.dot_general(v, kd, (((0,), (0,)), ((), ())), preferred_element_type=F32)
        st_ref[hh] = st * jnp.exp(b_last) + upd


def _gla(h, g_low, w_gate2, b_gate2, norm_g, *, width):
    T = h.shape[0]
    qk = width // 2
    dk = qk // GLA_HEADS
    dv = width // GLA_HEADS
    cb = GLA_BLOCK
    w2 = jnp.zeros((LANES, qk), F32).at[:w_gate2.shape[0]].set(w_gate2.astype(F32))
    return pl.pallas_call(
        functools.partial(_gla_kernel, dk=dk, dv=dv),
        name="gla",
        grid=(T // cb,),
        in_specs=[pl.BlockSpec((cb, qk), lambda i: (i, 2)),
                  pl.BlockSpec((cb, qk), lambda i: (i, 3)),
                  pl.BlockSpec((cb, width), lambda i: (i, 2)),
                  pl.BlockSpec((cb, width), lambda i: (i, 3)),
                  pl.BlockSpec((cb, LANES), lambda i: (i, 0)),
                  pl.BlockSpec((LANES, qk), lambda i: (0, 0)),
                  pl.BlockSpec((1, qk), lambda i: (0, 0)),
                  pl.BlockSpec((1, dv), lambda i: (0, 0))],
        out_specs=pl.BlockSpec((cb, width), lambda i: (i, 0)),
        out_shape=jax.ShapeDtypeStruct((T, width), BF16),
        scratch_shapes=[pltpu.VMEM((GLA_HEADS, dv, dk), F32)],
        compiler_params=_cparams(("arbitrary",)),
    )(h, h, h, h, g_low, w2, b_gate2.reshape(1, qk).astype(F32), norm_g.reshape(1, dv).astype(F32))


def _attn_kernel(q_ref, k_ref, v_ref, o_ref, acc_ref, m_ref, l_ref, *, na, scale):
    T = q_ref.shape[0]
    dh = q_ref.shape[1]
    acc_ref[...] = jnp.zeros_like(acc_ref)
    m_ref[...] = jnp.full_like(m_ref, NEG_BIG)
    l_ref[...] = jnp.zeros_like(l_ref)

    def token_ids(n_chunks, rows, a_start, r_of_chunk, axis, shape):
        idx = lax.broadcasted_iota(I32, shape, axis)
        c = idx // rows
        i = idx - c * rows
        return DIL * (a_start + i) + r_of_chunk(c)

    def attend(q_starts, k_starts, q_rows, k_rows, a_q, a_k, r_of_chunk, window):
        n_chunks = len(q_starts)
        mq, mk = n_chunks * q_rows, n_chunks * k_rows
        cat = lambda ref, starts, rows: jnp.concatenate(
            [ref[pl.ds(pl.multiple_of(s, 16), rows), :] for s in starts], axis=0)
        qb = cat(q_ref, q_starts, q_rows)
        kb = cat(k_ref, k_starts, k_rows)
        vb = cat(v_ref, k_starts, k_rows)
        s = lax.dot_general(qb, kb, (((1,), (1,)), ((), ())), preferred_element_type=F32) * scale
        tq = token_ids(n_chunks, q_rows, a_q, r_of_chunk, 0, (mq, mk))
        tk = token_ids(n_chunks, k_rows, a_k, r_of_chunk, 1, (mq, mk))
        delta = tq - tk
        s = jnp.where((delta >= 0) & (delta <= window), s, NEG_BIG)
        m_old = cat(m_ref, q_starts, q_rows)
        l_old = cat(l_ref, q_starts, q_rows)
        acc_old = cat(acc_ref, q_starts, q_rows)
        m_new = jnp.maximum(m_old, jnp.max(s, axis=1, keepdims=True))
        alpha = jnp.exp(m_old - m_new)
        p = jnp.exp(s - m_new[:, :1])
        l_new = alpha * l_old + jnp.sum(p, axis=1, keepdims=True)
        acc_new = alpha * acc_old + jnp.dot(p.astype(BF16), vb, preferred_element_type=F32)
        for c, st in enumerate(q_starts):
            st = pl.multiple_of(st, 16)
            rs = slice(c * q_rows, (c + 1) * q_rows)
            m_ref[pl.ds(st, q_rows), :] = m_new[rs]
            l_ref[pl.ds(st, q_rows), :] = l_new[rs]
            acc_ref[pl.ds(st, q_rows), :] = acc_new[rs]

    w1 = DILATED_GROUPS[0][0]

    def body1(blk, carry):
        a0 = blk * 16
        ak = jnp.maximum(a0 - 16, 0)
        attend([r * na + a0 for r in range(DIL)], [r * na + ak for r in range(DIL)],
               16, 32, a0, ak, lambda c: c, w1)
        return carry

    lax.fori_loop(0, na // 16, body1, 0)

    w4 = DILATED_GROUPS[1][0]

    def body4(it, carry):
        rho = it // (na // 32)
        blk = it - rho * (na // 32)
        a0 = blk * 32
        ak = jnp.maximum(a0 - 32, 0)
        attend([(rho + 4 * sg) * na + a0 for sg in range(4)], [(rho + 4 * sg) * na + ak for sg in range(4)],
               32, 64, a0, ak, lambda c: rho + 4 * c, w4)
        return carry

    lax.fori_loop(0, 4 * (na // 32), body4, 0)

    w16 = DILATED_GROUPS[2][0]

    def body16(it, carry):
        r = it // (na // 128)
        blk = it - r * (na // 128)
        a0 = blk * 128
        ak = jnp.maximum(a0 - 128, 0)
        attend([r * na + a0], [r * na + ak], 128, 256, a0, ak, lambda c: r, w16)
        return carry

    lax.fori_loop(0, DIL * (na // 128), body16, 0)

    o_ref[...] = (acc_ref[...] / l_ref[...]).astype(o_ref.dtype)


def _dilated_attention(qkv, d_model):
    T = qkv.shape[0]
    dh = d_model // ATT_HEADS
    na = T // DIL
    return pl.pallas_call(
        functools.partial(_attn_kernel, na=na, scale=dh ** -0.5),
        name="dilated_attn",
        grid=(ATT_HEADS,),
        in_specs=[pl.BlockSpec((T, dh), lambda h: (0, h)),
                  pl.BlockSpec((T, dh), lambda h: (0, ATT_HEADS + h)),
                  pl.BlockSpec((T, dh), lambda h: (0, 2 * ATT_HEADS + h))],
        out_specs=pl.BlockSpec((T, dh), lambda h: (0, h)),
        out_shape=jax.ShapeDtypeStruct((T, d_model), BF16),
        scratch_shapes=[pltpu.VMEM((T, dh), F32), pltpu.VMEM((T, dh), F32), pltpu.VMEM((T, dh), F32)],
        compiler_params=_cparams(("parallel",)),
    )(qkv, qkv, qkv)


def _route_kernel(x_ref, w_ref, b_ref, idx_ref, gate_ref, rank_ref, cnt_ref, carry_ref):
    @pl.when(pl.program_id(0) == 0)
    def _():
        carry_ref[...] = jnp.zeros_like(carry_ref)

    tb = x_ref.shape[0]
    lg = jnp.dot(x_ref[...], w_ref[...], preferred_element_type=F32, precision=HIGHEST) + b_ref[...]
    lane = lax.broadcasted_iota(I32, lg.shape, 1)
    vals, hots = [], []
    idx_out = jnp.zeros(lg.shape, I32)
    for k in range(TOP_K):
        m = jnp.max(lg, axis=1, keepdims=True)
        sel = jnp.min(jnp.where(lg == m, lane, LANES), axis=1, keepdims=True)
        hot = lane == sel
        vals.append(m)
        hots.append(hot)
        idx_out = jnp.where(lane == k, sel, idx_out)
        lg = jnp.where(hot, -jnp.inf, lg)
    ex = [jnp.exp(v - vals[0]) for v in vals]
    den = sum(ex)
    gate_out = jnp.zeros(lg.shape, F32)
    for k in range(TOP_K):
        gate_out = jnp.where(lane == k, ex[k] / den, gate_out)
    chosen = sum(h.astype(F32) for h in hots)
    ri = lax.broadcasted_iota(I32, (tb, tb), 0)
    ci = lax.broadcasted_iota(I32, (tb, tb), 1)
    before = jnp.dot((ri > ci).astype(BF16), chosen.astype(BF16), preferred_element_type=F32) + carry_ref[...]
    rank_out = jnp.zeros(lg.shape, I32)
    for k in range(TOP_K):
        rk = jnp.sum(jnp.where(hots[k], before, 0.0), axis=1, keepdims=True).astype(I32)
        rank_out = jnp.where(lane == k, rk, rank_out)
    idx_ref[...] = idx_out
    gate_ref[...] = gate_out
    rank_ref[...] = rank_out
    carry_ref[...] = carry_ref[...] + jnp.sum(chosen, axis=0, keepdims=True)
    cnt_ref[...] = carry_ref[...].astype(I32)


def _route(x, w_router, b_router, *, tb=512):
    T, D = x.shape
    E = w_router.shape[1]
    wr = jnp.zeros((D, LANES), F32).at[:, :E].set(w_router.astype(F32))
    br = jnp.full((1, LANES), NEG_BIG, F32).at[0, :E].set(b_router.astype(F32))
    row = lambda dt: jax.ShapeDtypeStruct((T, LANES), dt)
    idx, gate, rank, cnt = pl.pallas_call(
        _route_kernel,
        name="moe_route",
        grid=(T // tb,),
        in_specs=[pl.BlockSpec((tb, D), lambda i: (i, 0)),
                  pl.BlockSpec((D, LANES), lambda i: (0, 0)),
                  pl.BlockSpec((1, LANES), lambda i: (0, 0))],
        out_specs=[pl.BlockSpec((tb, LANES), lambda i: (i, 0)),
                   pl.BlockSpec((tb, LANES), lambda i: (i, 0)),
                   pl.BlockSpec((tb, LANES), lambda i: (i, 0)),
                   pl.BlockSpec((1, LANES), lambda i: (0, 0))],
        out_shape=[row(I32), row(F32), row(I32), jax.ShapeDtypeStruct((1, LANES), I32)],
        scratch_shapes=[pltpu.VMEM((1, LANES), F32)],
        compiler_params=_cparams(("arbitrary",)),
    )(x, wr, br)
    return idx[:, :TOP_K], gate[:, :TOP_K], rank[:, :TOP_K], cnt[0, :E]


def _zero_tail_kernel(row_ref, o_ref, z_ref, sem):
    z_ref[...] = jnp.zeros_like(z_ref)
    n = row_ref.shape[0]

    def copy(e):
        return pltpu.make_async_copy(z_ref, o_ref.at[pl.ds(pl.multiple_of(row_ref[e], MOE_BLOCK), MOE_BLOCK)], sem)

    def start(e, c):
        copy(e).start()
        return c

    def wait(e, c):
        copy(e).wait()
        return c

    lax.fori_loop(0, n, start, 0)
    lax.fori_loop(0, n, wait, 0)


def _zero_tails(tail_rows, n_rows, d):
    return pl.pallas_call(
        _zero_tail_kernel,
        name="moe_zero_tails",
        grid_spec=pltpu.PrefetchScalarGridSpec(
            num_scalar_prefetch=1, grid=(1,),
            in_specs=[],
            out_specs=pl.BlockSpec(memory_space=pl.ANY),
            scratch_shapes=[pltpu.VMEM((MOE_BLOCK, d), F32), pltpu.SemaphoreType.DMA(())]),
        out_shape=jax.ShapeDtypeStruct((n_rows, d), F32),
        compiler_params=_cparams(("arbitrary",)),
    )(tail_rows)


def _dispatch_kernel(dest_ref, x_ref, xs_in_ref, xs_ref, sem):
    del xs_in_ref
    tb = x_ref.shape[0]
    base = pl.program_id(0) * tb * TOP_K

    def copy(i, k):
        return pltpu.make_async_copy(x_ref.at[pl.ds(i, 1)],
                                     xs_ref.at[pl.ds(dest_ref[base + i * TOP_K + k], 1)], sem)

    def start(i, c):
        for k in range(TOP_K):
            copy(i, k).start()
        return c

    def wait(i, c):
        for k in range(TOP_K):
            copy(i, k).wait()
        return c

    lax.fori_loop(0, tb, start, 0)
    lax.fori_loop(0, tb, wait, 0)


def _dispatch(x, dest_flat, xs_init, *, tb=256):
    T, D = x.shape
    return pl.pallas_call(
        _dispatch_kernel,
        name="moe_dispatch",
        grid_spec=pltpu.PrefetchScalarGridSpec(
            num_scalar_prefetch=1, grid=(T // tb,),
            in_specs=[pl.BlockSpec((tb, D), lambda i, dest: (i, 0)),
                      pl.BlockSpec(memory_space=pl.ANY)],
            out_specs=pl.BlockSpec(memory_space=pl.ANY),
            scratch_shapes=[pltpu.SemaphoreType.DMA(())]),
        out_shape=jax.ShapeDtypeStruct(xs_init.shape, xs_init.dtype),
        input_output_aliases={2: 0},
        compiler_params=_cparams(("arbitrary",)),
    )(dest_flat, x, xs_init)


def _moe_kernel(item_e_ref, item_row_ref, item_nb_ref, xs_ref, wg_ref, wu_ref, bg_ref, bu_ref, wd_ref, bd_ref,
                ys_ref, xbuf_ref, stage_ref, acc_ref, wgb_ref, wub_ref, wdb_ref, in_sem, out_sem, *, n_f):
    it = pl.program_id(0)
    f = pl.program_id(1)
    nb = item_nb_ref[it]
    row0 = item_row_ref[it]
    blk = MOE_BLOCK

    def in_copy(j, slot):
        src = xs_ref.at[pl.ds(pl.multiple_of(row0 + j * blk, blk), blk)]
        return pltpu.make_async_copy(src, stage_ref.at[slot], in_sem.at[slot])

    def out_copy(j):
        dst = ys_ref.at[pl.ds(pl.multiple_of(row0 + j * blk, blk), blk)]
        return pltpu.make_async_copy(acc_ref.at[pl.ds(pl.multiple_of(j * blk, blk), blk)], dst, out_sem)

    @pl.when((f == 0) & (nb > 0))
    def _():
        in_copy(0, 0).start()

        def load(j, c):
            slot = j % 2

            @pl.when(j + 1 < nb)
            def _():
                in_copy(j + 1, 1 - slot).start()

            in_copy(j, slot).wait()
            rows = pl.ds(pl.multiple_of(j * blk, blk), blk)
            xbuf_ref[rows, :] = stage_ref[slot].astype(BF16)
            acc_ref[rows, :] = jnp.broadcast_to(bd_ref[...], (blk, acc_ref.shape[1]))
            return c

        lax.fori_loop(0, nb, load, 0)

    @pl.when(nb > 0)
    def _():
        wgb_ref[...] = wg_ref[...].astype(BF16)
        wub_ref[...] = wu_ref[...].astype(BF16)
        wdb_ref[...] = wd_ref[...].astype(BF16)

        def compute(j, c):
            rows = pl.ds(pl.multiple_of(j * blk, blk), blk)
            xb = xbuf_ref[rows, :]
            hg = jnp.dot(xb, wgb_ref[...], preferred_element_type=F32) + bg_ref[...]
            hu = jnp.dot(xb, wub_ref[...], preferred_element_type=F32) + bu_ref[...]
            gate = jnp.minimum(hg, SWIGLU_LIMIT)
            up = jnp.clip(hu, -SWIGLU_LIMIT, SWIGLU_LIMIT)
            act = (up + 1.0) * (gate * jax.nn.sigmoid(SWIGLU_ALPHA * gate))
            acc_ref[rows, :] += jnp.dot(act.astype(BF16), wdb_ref[...], preferred_element_type=F32)
            return c

        lax.fori_loop(0, nb, compute, 0)

    @pl.when((f == n_f - 1) & (nb > 0))
    def _():
        def start(j, c):
            out_copy(j).start()
            return c

        def wait(j, c):
            out_copy(j).wait()
            return c

        lax.fori_loop(0, nb, start, 0)
        lax.fori_loop(0, nb, wait, 0)


def _moe_experts(xs, items, w_gu, b_gu, w_down, b_down, *, tf=256):
    item_e, item_row, item_nb = items
    n_items = item_e.shape[0]
    P, D = xs.shape
    E, _, F2 = w_gu.shape
    F = F2 // 2
    n_f = F // tf
    rc = MOE_CHUNK_BLOCKS * MOE_BLOCK

    def fe(f, nb_ref, it):
        return jnp.where(nb_ref[it] > 0, f, n_f - 1)

    in_specs = [
        pl.BlockSpec(memory_space=pl.ANY),
        pl.BlockSpec((None, D, tf), lambda it, f, e, r, nb: (e[it], 0, fe(f, nb, it))),
        pl.BlockSpec((None, D, tf), lambda it, f, e, r, nb: (e[it], 0, n_f + fe(f, nb, it))),
        pl.BlockSpec((None, 1, tf), lambda it, f, e, r, nb: (e[it], 0, fe(f, nb, it))),
        pl.BlockSpec((None, 1, tf), lambda it, f, e, r, nb: (e[it], 0, n_f + fe(f, nb, it))),
        pl.BlockSpec((None, tf, D), lambda it, f, e, r, nb: (e[it], fe(f, nb, it), 0)),
        pl.BlockSpec((None, 1, D), lambda it, f, e, r, nb: (e[it], 0, 0)),
    ]
    return pl.pallas_call(
        functools.partial(_moe_kernel, n_f=n_f),
        name="moe_experts",
        grid_spec=pltpu.PrefetchScalarGridSpec(
            num_scalar_prefetch=3, grid=(n_items, n_f),
            in_specs=in_specs,
            out_specs=pl.BlockSpec(memory_space=pl.ANY),
            scratch_shapes=[pltpu.VMEM((rc, D), BF16),
                            pltpu.VMEM((2, MOE_BLOCK, D), F32),
                            pltpu.VMEM((rc, D), F32),
                            pltpu.VMEM((D, tf), BF16),
                            pltpu.VMEM((D, tf), BF16),
                            pltpu.VMEM((tf, D), BF16),
                            pltpu.SemaphoreType.DMA((2,)),
                            pltpu.SemaphoreType.DMA(())]),
        out_shape=jax.ShapeDtypeStruct((P, D), F32),
        compiler_params=_cparams(("arbitrary", "arbitrary")),
    )(item_e, item_row, item_nb, xs, w_gu, w_gu, b_gu.reshape(E, 1, F2), b_gu.reshape(E, 1, F2),
      w_down, b_down.reshape(E, 1, D))


def _combine_kernel(dest_ref, ys_ref, gate_ref, x_ref, g_ref, b_ref, o_ref, buf_ref, sem, *, alpha):
    tb = x_ref.shape[0]
    base = pl.program_id(0) * tb * TOP_K

    def copy(i, k):
        return pltpu.make_async_copy(ys_ref.at[pl.ds(dest_ref[base + i * TOP_K + k], 1)],
                                     buf_ref.at[k, pl.ds(i, 1)], sem)

    def start(i, c):
        for k in range(TOP_K):
            copy(i, k).start()
        return c

    def wait(i, c):
        for k in range(TOP_K):
            copy(i, k).wait()
        return c

    lax.fori_loop(0, tb, start, 0)
    lax.fori_loop(0, tb, wait, 0)
    gate = gate_ref[...]
    ffn = gate[:, 0:1] * buf_ref[0]
    for k in range(1, TOP_K):
        ffn = ffn + gate[:, k:k + 1] * buf_ref[k]
    o_ref[...] = _layer_norm_rows(alpha * x_ref[...] + ffn, g_ref[...], b_ref[...])


def _combine_ln(ys, dest_flat, gates, x, g, b, *, alpha, tb=128):
    T, D = x.shape
    gate_pad = jnp.zeros((T, LANES), F32).at[:, :TOP_K].set(gates)
    return pl.pallas_call(
        functools.partial(_combine_kernel, alpha=alpha),
        name="moe_combine",
        grid_spec=pltpu.PrefetchScalarGridSpec(
            num_scalar_prefetch=1, grid=(T // tb,),
            in_specs=[pl.BlockSpec(memory_space=pl.ANY),
                      pl.BlockSpec((tb, LANES), lambda i, dest: (i, 0)),
                      pl.BlockSpec((tb, D), lambda i, dest: (i, 0)),
                      pl.BlockSpec((1, D), lambda i, dest: (0, 0)),
                      pl.BlockSpec((1, D), lambda i, dest: (0, 0))],
            out_specs=pl.BlockSpec((tb, D), lambda i, dest: (i, 0)),
            scratch_shapes=[pltpu.VMEM((TOP_K, tb, D), F32), pltpu.SemaphoreType.DMA(())]),
        out_shape=jax.ShapeDtypeStruct((T, D), F32),
        compiler_params=_cparams(("arbitrary",)),
    )(dest_flat, ys, gate_pad, x, g.reshape(1, D), b.reshape(1, D))


def _moe_layout(idx, rank, counts, n_assign):
    E = counts.shape[0]
    blk = MOE_BLOCK
    nblk = (counts + blk - 1) // blk
    bend = jnp.cumsum(nblk)
    bstart = bend - nblk
    dest = (bstart * blk)[idx] + rank
    tail = jnp.where(nblk > 0, (bend - 1) * blk, 0).astype(I32)
    cb = MOE_CHUNK_BLOCKS
    n_items_max = E + (n_assign // blk + E) // cb + 1
    per_e = (nblk + cb - 1) // cb
    iend = jnp.cumsum(per_e)
    istart = iend - per_e
    ids = jnp.arange(n_items_max, dtype=I32)
    e_of = jnp.minimum(jnp.searchsorted(iend, ids, side='right'), E - 1).astype(I32)
    valid = ids < iend[-1]
    last_e = jnp.max(jnp.where(per_e > 0, jnp.arange(E, dtype=I32), 0))
    e_of = jnp.where(valid, e_of, last_e)
    local = ids - istart[e_of]
    first_blk = bstart[e_of] + local * cb
    nb = jnp.where(valid, jnp.minimum(cb, nblk[e_of] - local * cb), 0)
    row = jnp.where(valid, first_blk * blk, 0)
    return dest.astype(I32), tail, (e_of, row.astype(I32), nb.astype(I32))


def _moe(x, w_router, b_router, w_gu, b_gu, w_down, b_down, ln_g, ln_b, *, alpha):
    T, D = x.shape
    E = w_router.shape[1]
    idx, gates, rank, counts = _route(x, w_router, b_router)
    n_assign = T * TOP_K
    dest, tail, items = _moe_layout(idx, rank, counts, n_assign)
    dest_flat = dest.reshape(n_assign)
    n_rows = n_assign + E * MOE_BLOCK
    xs = _dispatch(x, dest_flat, _zero_tails(tail, n_rows, D))
    ys = _moe_experts(xs, items, w_gu, b_gu, w_down, b_down)
    return _combine_ln(ys, dest_flat, gates, x, ln_g, ln_b, alpha=alpha)


def _even_mixer_ln(x, w_in, lam_re, lam_im, log_step, b_re, b_im, c_re, c_im, d_skip, w_glu, b_glu,
                   w_gate2, b_gate2, norm_g, w_out, ln_g, ln_b, *, alpha):
    T, D = x.shape
    W = d_skip.shape[0]
    qk = W // 2
    rank = w_gate2.shape[0]
    s4 = W + 2 * qk + W
    w_main = jnp.concatenate([w_in[:, :s4], w_in[:, s4 + rank:]], axis=1).astype(BF16)
    w_gate = jnp.zeros((D, LANES), F32).at[:, :rank].set(w_in[:, s4:s4 + rank].astype(F32))
    h, g_low = _proj_in(x, w_main, w_gate)
    tables = _s5_tables(lam_re, lam_im, log_step, b_re, b_im, c_re, c_im, T // S5_CHUNK)
    y = _s5_core(h[:, :W], tables)
    ya = _s5_glu(y, h, d_skip.astype(F32), w_glu.astype(BF16), b_glu.astype(F32))
    yb = _gla(h, g_low, w_gate2, b_gate2, norm_g, width=W)
    w_out_b = w_out.astype(BF16)
    return _proj_ln([ya, yb], [w_out_b[:W], w_out_b[W:]], x, ln_g, ln_b, alpha=alpha, permuted=False)


def _odd_mixer_ln(x, w_qkv, w_o, ln_g, ln_b, *, alpha):
    T, D = x.shape
    qkv = _proj_perm(x, w_qkv.astype(BF16))
    y = _dilated_attention(qkv, D)
    return _proj_ln([y], [w_o.astype(BF16)], x, ln_g, ln_b, alpha=alpha, permuted=True)


def kernel(x, ab_w_in, s5_lam_re, s5_lam_im, s5_log_step, s5_b_re, s5_b_im, s5_c_re, s5_c_im, s5_d, s5_w_glu, s5_b_glu, gla_w_gate2, gla_b_gate2, gla_norm_g, ab_w_out, c_w_qkv, c_w_o, ln1_g, ln1_b, moe_w_router, moe_b_router, moe_w_gu, moe_b_gu, moe_w_down, moe_b_down, ln2_g, ln2_b):
    bsz, L, D = x.shape
    depth = ln1_g.shape[0]
    alpha = (2 * depth) ** 0.25
    outs = []
    for bi in range(bsz):
        xt = x[bi].astype(F32)
        for layer in range(depth):
            i = layer // 2
            if layer % 2 == 0:
                xt = _even_mixer_ln(xt, ab_w_in[i], s5_lam_re[i], s5_lam_im[i], s5_log_step[i], s5_b_re[i],
                                    s5_b_im[i], s5_c_re[i], s5_c_im[i], s5_d[i], s5_w_glu[i], s5_b_glu[i],
                                    gla_w_gate2[i], gla_b_gate2[i], gla_norm_g[i], ab_w_out[i],
                                    ln1_g[layer], ln1_b[layer], alpha=alpha)
            else:
                xt = _odd_mixer_ln(xt, c_w_qkv[i], c_w_o[i], ln1_g[layer], ln1_b[layer], alpha=alpha)
            xt = _moe(xt, moe_w_router[layer], moe_b_router[layer], moe_w_gu[layer], moe_b_gu[layer],
                      moe_w_down[layer], moe_b_down[layer], ln2_g[layer], ln2_b[layer], alpha=alpha)
        outs.append(xt)
    return jnp.stack(outs).astype(x.dtype)
```

```python
import functools
import math

import jax
import jax.numpy as jnp
from jax import lax
from jax.experimental import pallas as pl
from jax.experimental.pallas import tpu as pltpu

F32 = jnp.float32
BF16 = jnp.bfloat16
I32 = jnp.int32
HIGHEST = lax.Precision.HIGHEST

LANES = 128
VMEM_LIMIT_BYTES = 56 * 1024 * 1024

S5_GROUP = 16
S5_STATE = 64
S5_MAX_RE = -1e-4
S5_CHUNK = 16
GLA_HEADS = 4
GLA_GATE_TEMP = 16.0
GLA_EPS = 1e-6
GLA_BLOCK = 64
GLA_SUB = 16
ATT_HEADS = 16
DIL = 16
DILATED_GROUPS = ((128, 1), (512, 4), (2048, 16))
TOP_K = 4
SWIGLU_LIMIT = 7.0
SWIGLU_ALPHA = 1.702
MOE_BLOCK = 128
MOE_CHUNK_BLOCKS = 12
LN_EPS = 1e-5
NEG_BIG = -1e30


def _cparams(semantics):
    return pltpu.CompilerParams(dimension_semantics=semantics, vmem_limit_bytes=VMEM_LIMIT_BYTES)


def _layer_norm_rows(z, g, b):
    mu = jnp.mean(z, axis=-1, keepdims=True)
    zc = z - mu
    var = jnp.mean(zc * zc, axis=-1, keepdims=True)
    return zc * lax.rsqrt(var + LN_EPS) * g + b


def _proj_in_kernel(x_ref, w_ref, wg_ref, h_ref, g_ref, xb_ref):
    @pl.when(pl.program_id(1) == 0)
    def _():
        x = x_ref[...]
        xb_ref[...] = x.astype(BF16)
        g_ref[...] = jnp.dot(x, wg_ref[...], preferred_element_type=F32, precision=HIGHEST)

    h_ref[...] = jnp.dot(xb_ref[...], w_ref[...], preferred_element_type=F32)


def _proj_in(x, w_main, w_gate, *, tm=1024, tn=512):
    T, D = x.shape
    N = w_main.shape[1]
    return pl.pallas_call(
        _proj_in_kernel,
        name="proj_in",
        grid=(T // tm, N // tn),
        in_specs=[pl.BlockSpec((tm, D), lambda i, j: (i, 0)),
                  pl.BlockSpec((D, tn), lambda i, j: (0, j)),
                  pl.BlockSpec((D, LANES), lambda i, j: (0, 0))],
        out_specs=[pl.BlockSpec((tm, tn), lambda i, j: (i, j)),
                   pl.BlockSpec((tm, LANES), lambda i, j: (i, 0))],
        out_shape=[jax.ShapeDtypeStruct((T, N), F32), jax.ShapeDtypeStruct((T, LANES), F32)],
        scratch_shapes=[pltpu.VMEM((tm, D), BF16)],
        compiler_params=_cparams(("parallel", "arbitrary")),
    )(x, w_main, w_gate)


def _proj_perm_kernel(x_ref, w_ref, o_ref, xb_ref):
    @pl.when(pl.program_id(1) == 0)
    def _():
        xb_ref[...] = x_ref[...].astype(BF16)

    o_ref[...] = jnp.dot(xb_ref[...], w_ref[...], preferred_element_type=F32).astype(o_ref.dtype)


def _proj_perm(x, w, *, tn=512):
    T, D = x.shape
    N = w.shape[1]
    na = T // DIL
    x3 = x.reshape(na, DIL * D)
    return pl.pallas_call(
        _proj_perm_kernel,
        name="proj_perm",
        grid=(DIL, N // tn),
        in_specs=[pl.BlockSpec((na, D), lambda r, j: (0, r)),
                  pl.BlockSpec((D, tn), lambda r, j: (0, j))],
        out_specs=pl.BlockSpec((na, tn), lambda r, j: (r, j)),
        out_shape=jax.ShapeDtypeStruct((T, N), BF16),
        scratch_shapes=[pltpu.VMEM((na, D), BF16)],
        compiler_params=_cparams(("parallel", "arbitrary")),
    )(x3, w)


def _proj_ln_kernel(*refs, n_lhs, alpha, n_tiles):
    lhs_refs = refs[:n_lhs]
    w_refs = refs[n_lhs:2 * n_lhs]
    res_ref, g_ref, b_ref, o_ref, acc_ref = refs[2 * n_lhs:]
    j = pl.program_id(1)
    y = jnp.dot(lhs_refs[0][...], w_refs[0][...], preferred_element_type=F32)
    for a_ref, w_ref in zip(lhs_refs[1:], w_refs[1:]):
        y = y + jnp.dot(a_ref[...], w_ref[...], preferred_element_type=F32)
    acc_ref[j] = y

    @pl.when(j == n_tiles - 1)
    def _():
        tn = acc_ref.shape[2]
        z = [alpha * res_ref[:, t * tn:(t + 1) * tn] + acc_ref[t] for t in range(n_tiles)]
        n = float(n_tiles * tn)
        mu = sum(jnp.sum(zt, axis=-1, keepdims=True) for zt in z) / n
        zc = [zt - mu for zt in z]
        var = sum(jnp.sum(zt * zt, axis=-1, keepdims=True) for zt in zc) / n
        rstd = lax.rsqrt(var + LN_EPS)
        for t in range(n_tiles):
            sl = slice(t * tn, (t + 1) * tn)
            o_ref[:, sl] = zc[t] * rstd * g_ref[:, sl] + b_ref[:, sl]


def _proj_ln(lhs, ws, res, g, b, *, alpha, permuted, tm=512, tn=512):
    T, N = res.shape
    n_lhs = len(lhs)
    n_tiles = N // tn
    if permuted:
        tm = T // DIL
        res_in = res.reshape(tm, DIL * N)
        res_spec = pl.BlockSpec((tm, N), lambda i, j: (0, i))
        out_spec = pl.BlockSpec((tm, N), lambda i, j: (0, i))
        out_shape = jax.ShapeDtypeStruct((tm, DIL * N), F32)
    else:
        res_in = res
        res_spec = pl.BlockSpec((tm, N), lambda i, j: (i, 0))
        out_spec = pl.BlockSpec((tm, N), lambda i, j: (i, 0))
        out_shape = jax.ShapeDtypeStruct((T, N), F32)
    in_specs = ([pl.BlockSpec((tm, a.shape[1]), lambda i, j: (i, 0)) for a in lhs]
                + [pl.BlockSpec((w.shape[0], tn), lambda i, j: (0, j)) for w in ws]
                + [res_spec,
                   pl.BlockSpec((1, N), lambda i, j: (0, 0)),
                   pl.BlockSpec((1, N), lambda i, j: (0, 0))])
    out = pl.pallas_call(
        functools.partial(_proj_ln_kernel, n_lhs=n_lhs, alpha=alpha, n_tiles=n_tiles),
        name="proj_ln",
        grid=(T // tm, n_tiles),
        in_specs=in_specs,
        out_specs=out_spec,
        out_shape=out_shape,
        scratch_shapes=[pltpu.VMEM((n_tiles, tm, tn), F32)],
        compiler_params=_cparams(("parallel", "arbitrary")),
    )(*lhs, *ws, res_in, g.reshape(1, N), b.reshape(1, N))
    return out.reshape(T, N)


def _s5_tables(lam_re, lam_im, log_step, b_re, b_im, c_re, c_im, n_chunks):
    C = S5_CHUNK
    G, P = lam_re.shape
    H = b_re.shape[-1]
    lr = jnp.minimum(lam_re.astype(F32), S5_MAX_RE)
    li = lam_im.astype(F32)
    dt = jnp.exp(log_step.astype(F32))[:, None]
    mag = jnp.exp(lr * dt)
    a_re = mag * jnp.cos(li * dt)
    a_im = mag * jnp.sin(li * dt)
    den = lr * lr + li * li
    nr = a_re - 1.0
    f_re = (nr * lr + a_im * li) / den
    f_im = (a_im * lr - nr * li) / den
    br = b_re.astype(F32)
    bi = b_im.astype(F32)
    bb_re = f_re[..., None] * br - f_im[..., None] * bi
    bb_im = f_re[..., None] * bi + f_im[..., None] * br
    pw_re = [jnp.ones_like(a_re)]
    pw_im = [jnp.zeros_like(a_im)]
    for _ in range(C):
        pr, pi = pw_re[-1], pw_im[-1]
        pw_re.append(pr * a_re - pi * a_im)
        pw_im.append(pr * a_im + pi * a_re)
    pw_re = jnp.stack(pw_re)
    pw_im = jnp.stack(pw_im)
    ab_re = pw_re[:C, :, :, None] * bb_re[None] - pw_im[:C, :, :, None] * bb_im[None]
    ab_im = pw_re[:C, :, :, None] * bb_im[None] + pw_im[:C, :, :, None] * bb_re[None]
    cr = c_re.astype(F32)
    ci = c_im.astype(F32)
    z_re = jnp.transpose(ab_re[::-1], (1, 0, 3, 2)).reshape(G, C * H, P)
    z_im = jnp.transpose(ab_im[::-1], (1, 0, 3, 2)).reshape(G, C * H, P)
    zmat = jnp.concatenate([z_re, z_im], axis=-1)
    kern = (jnp.einsum('gop,kgph->kgoh', cr, ab_re, precision=HIGHEST)
            - jnp.einsum('gop,kgph->kgoh', ci, ab_im, precision=HIGHEST))
    lag = jnp.arange(C)[None, :] - jnp.arange(C)[:, None]
    kl = kern[jnp.clip(lag, 0, C - 1)]
    kl = jnp.where((lag >= 0)[:, :, None, None, None], kl, 0.0)
    mmat = jnp.transpose(kl, (2, 0, 4, 1, 3)).reshape(G, C * H, C * H)
    ca_re = cr[None] * pw_re[1:, :, None, :] - ci[None] * pw_im[1:, :, None, :]
    ca_im = cr[None] * pw_im[1:, :, None, :] + ci[None] * pw_re[1:, :, None, :]
    n_re = jnp.transpose(ca_re, (1, 3, 0, 2)).reshape(G, P, C * H)
    n_im = jnp.transpose(-ca_im, (1, 3, 0, 2)).reshape(G, P, C * H)
    nmat = jnp.concatenate([n_re, n_im], axis=1)
    n_steps = max(1, (n_chunks - 1).bit_length())
    qr, qi = pw_re[C], pw_im[C]
    a1, a2 = [], []
    for _ in range(n_steps):
        a1.append(jnp.concatenate([qr, qr], axis=-1))
        a2.append(jnp.concatenate([-qi, qi], axis=-1))
        qr, qi = qr * qr - qi * qi, 2.0 * qr * qi
    a1 = jnp.stack(a1, axis=1)
    a2 = jnp.stack(a2, axis=1)
    return zmat.astype(BF16), mmat.astype(BF16), nmat.astype(BF16), a1, a2


def _s5_kernel(u_ref, z_ref, m_ref, n_ref, a1_ref, a2_ref, y_ref, *, n_steps):
    u = u_ref[...]
    nc = u.shape[0]
    s = jnp.dot(u, z_ref[...], preferred_element_type=F32)
    half = s.shape[1] // 2
    row = lax.broadcasted_iota(I32, s.shape, 0)
    for k in range(n_steps):
        sh = 1 << k
        prev = jnp.where(row >= sh, pltpu.roll(s, sh, axis=0), 0.0)
        s = s + a1_ref[k:k + 1, :] * prev + a2_ref[k:k + 1, :] * pltpu.roll(prev, half, axis=1)
    s_in = jnp.where(row >= 1, pltpu.roll(s, 1, axis=0), 0.0)
    y_ref[...] = (jnp.dot(u, m_ref[...], preferred_element_type=F32)
                  + jnp.dot(s_in.astype(BF16), n_ref[...], preferred_element_type=F32))


def _s5_core(u, tables):
    zmat, mmat, nmat, a1, a2 = tables
    T, W = u.shape
    G = zmat.shape[0]
    C, H = S5_CHUNK, S5_GROUP
    nc = T // C
    CH = C * H
    P2 = zmat.shape[2]
    n_steps = a1.shape[1]
    ug = u.astype(BF16).reshape(nc, C, G, H).transpose(2, 0, 1, 3).reshape(G, nc, CH)
    yg = pl.pallas_call(
        functools.partial(_s5_kernel, n_steps=n_steps),
        name="s5_core",
        grid=(G,),
        in_specs=[pl.BlockSpec((None, nc, CH), lambda g: (g, 0, 0)),
                  pl.BlockSpec((None, CH, P2), lambda g: (g, 0, 0)),
                  pl.BlockSpec((None, CH, CH), lambda g: (g, 0, 0)),
                  pl.BlockSpec((None, P2, CH), lambda g: (g, 0, 0)),
                  pl.BlockSpec((None, n_steps, P2), lambda g: (g, 0, 0)),
                  pl.BlockSpec((None, n_steps, P2), lambda g: (g, 0, 0))],
        out_specs=pl.BlockSpec((None, nc, CH), lambda g: (g, 0, 0)),
        out_shape=jax.ShapeDtypeStruct((G, nc, CH), F32),
        compiler_params=_cparams(("parallel",)),
    )(ug, zmat, mmat, nmat, a1, a2)
    return yg.reshape(G, nc, C, H).transpose(1, 2, 0, 3).reshape(T, W)


def _s5_glu_kernel(y_ref, u_ref, d_ref, w_ref, b_ref, o_ref):
    y = y_ref[...] + d_ref[...] * u_ref[...]
    c0 = math.sqrt(2.0 / math.pi)
    z = 0.5 * y * (1.0 + jnp.tanh(c0 * (y + 0.044715 * (y * y * y))))
    lin = jnp.dot(z.astype(BF16), w_ref[...], preferred_element_type=F32) + b_ref[...]
    o_ref[...] = (z * jax.nn.sigmoid(lin)).astype(o_ref.dtype)


def _s5_glu(y, h, d_skip, w_glu, b_glu, *, tm=512):
    T, W = y.shape
    return pl.pallas_call(
        _s5_glu_kernel,
        name="s5_glu",
        grid=(T // tm,),
        in_specs=[pl.BlockSpec((tm, W), lambda i: (i, 0)),
                  pl.BlockSpec((tm, W), lambda i: (i, 0)),
                  pl.BlockSpec((1, W), lambda i: (0, 0)),
                  pl.BlockSpec((W, W), lambda i: (0, 0)),
                  pl.BlockSpec((1, W), lambda i: (0, 0))],
        out_specs=pl.BlockSpec((tm, W), lambda i: (i, 0)),
        out_shape=jax.ShapeDtypeStruct((T, W), BF16),
        compiler_params=_cparams(("parallel",)),
    )(y, h, d_skip.reshape(1, W), w_glu, b_glu.reshape(1, W))


def _gla_kernel(q_ref, k_ref, v_ref, r_ref, g_ref, w2_ref, b2_ref, ng_ref, o_ref, st_ref, *, dk, dv):
    @pl.when(pl.program_id(0) == 0)
    def _():
        st_ref[...] = jnp.zeros_like(st_ref)

    cb = q_ref.shape[0]
    n_sub = cb // GLA_SUB
    scale = dk ** -0.5
    logit = jnp.dot(g_ref[...], w2_ref[...], preferred_element_type=F32, precision=HIGHEST) + b2_ref[...]
    log_a = (jnp.minimum(logit, 0.0) - jnp.log(1.0 + jnp.exp(-jnp.abs(logit)))) / GLA_GATE_TEMP
    ri = lax.broadcasted_iota(I32, (cb, cb), 0)
    ci = lax.broadcasted_iota(I32, (cb, cb), 1)
    tri = (ri >= ci).astype(F32)
    bcum = jnp.dot(tri, log_a, preferred_element_type=F32, precision=HIGHEST)
    for hh in range(GLA_HEADS):
        ks = slice(hh * dk, (hh + 1) * dk)
        vs = slice(hh * dv, (hh + 1) * dv)
        b = bcum[:, ks]
        q = q_ref[:, ks] * scale
        k = k_ref[:, ks]
        v = v_ref[:, vs].astype(BF16)
        refs = [jnp.zeros((1, dk), F32)] + [b[a * GLA_SUB - 1:a * GLA_SUB, :] for a in range(1, n_sub)]
        refmat = jnp.concatenate([jnp.broadcast_to(r, (GLA_SUB, dk)) for r in refs], axis=0)
        qe = (q * jnp.exp(b - refmat)).astype(BF16)
        st = st_ref[hh]
        o_inter = lax.dot_general((q * jnp.exp(b)).astype(BF16), st.astype(BF16),
                                  (((1,), (1,)), ((), ())), preferred_element_type=F32)
        o_rows = []
        for a in range(n_sub):
            hi = (a + 1) * GLA_SUB
            ke = (k[:hi] * jnp.exp(refs[a] - b[:hi])).astype(BF16)
            att = lax.dot_general(qe[a * GLA_SUB:hi], ke, (((1,), (1,)), ((), ())),
                                  preferred_element_type=F32)
            row_a = lax.broadcasted_iota(I32, (GLA_SUB, hi), 0) + a * GLA_SUB
            att = jnp.where(lax.broadcasted_iota(I32, (GLA_SUB, hi), 1) <= row_a, att, 0.0)
            o_rows.append(jnp.dot(att.astype(BF16), v[:hi], preferred_element_type=F32))
        o = jnp.concatenate(o_rows, axis=0) + o_inter
        o = o * lax.rsqrt(jnp.mean(o * o, axis=-1, keepdims=True) + GLA_EPS) * ng_ref[...]
        r = r_ref[:, vs]
        o_ref[:, vs] = (o * (r * jax.nn.sigmoid(r))).astype(o_ref.dtype)
        b_last = b[cb - 1:cb, :]
        kd = (k * jnp.exp(b_last - b)).astype(BF16)
        upd = lax.dot_general(v, kd, (((0,), (0,)), ((), ())), preferred_element_type=F32)
        st_ref[hh] = st * jnp.exp(b_last) + upd


def _gla(h, g_low, w_gate2, b_gate2, norm_g, *, width):
    T = h.shape[0]
    qk = width // 2
    dk = qk // GLA_HEADS
    dv = width // GLA_HEADS
    cb = GLA_BLOCK
    w2 = jnp.zeros((LANES, qk), F32).at[:w_gate2.shape[0]].set(w_gate2.astype(F32))
    return pl.pallas_call(
        functools.partial(_gla_kernel, dk=dk, dv=dv),
        name="gla",
        grid=(T // cb,),
        in_specs=[pl.BlockSpec((cb, qk), lambda i: (i, 2)),
                  pl.BlockSpec((cb, qk), lambda i: (i, 3)),
                  pl.BlockSpec((cb, width), lambda i: (i, 2)),
                  pl.BlockSpec((cb, width), lambda i: (i, 3)),
                  pl.BlockSpec((cb, LANES), lambda i: (i, 0)),
                  pl.BlockSpec((LANES, qk), lambda i: (0, 0)),
                  pl.BlockSpec((1, qk), lambda i: (0, 0)),
                  pl.BlockSpec((1, dv), lambda i: (0, 0))],
        out_specs=pl.BlockSpec((cb, width), lambda i: (i, 0)),
        out_shape=jax.ShapeDtypeStruct((T, width), BF16),
        scratch_shapes=[pltpu.VMEM((GLA_HEADS, dv, dk), F32)],
        compiler_params=_cparams(("arbitrary",)),
    )(h, h, h, h, g_low, w2, b_gate2.reshape(1, qk).astype(F32), norm_g.reshape(1, dv).astype(F32))


def _attn_kernel(q_ref, k_ref, v_ref, o_ref, acc_ref, m_ref, l_ref, *, na, scale):
    T = q_ref.shape[0]
    dh = q_ref.shape[1]
    acc_ref[...] = jnp.zeros_like(acc_ref)
    m_ref[...] = jnp.full_like(m_ref, NEG_BIG)
    l_ref[...] = jnp.zeros_like(l_ref)

    def token_ids(n_chunks, rows, a_start, r_of_chunk, axis, shape):
        idx = lax.broadcasted_iota(I32, shape, axis)
        c = idx // rows
        i = idx - c * rows
        return DIL * (a_start + i) + r_of_chunk(c)

    def cat(ref, starts, rows):
        return jnp.concatenate([ref[pl.ds(pl.multiple_of(s, 16), rows), :] for s in starts], axis=0)

    def attend(blocks, q_rows, k_rows, window):
        loaded = []
        for q_starts, k_starts, a_q, a_k, r_of_chunk in blocks:
            loaded.append((cat(q_ref, q_starts, q_rows), cat(k_ref, k_starts, k_rows), cat(v_ref, k_starts, k_rows),
                           cat(m_ref, q_starts, q_rows), cat(l_ref, q_starts, q_rows),
                           cat(acc_ref, q_starts, q_rows)))
        results = []
        for (q_starts, k_starts, a_q, a_k, r_of_chunk), (qb, kb, vb, m_old, l_old, acc_old) in zip(blocks, loaded):
            n_chunks = len(q_starts)
            mq, mk = n_chunks * q_rows, n_chunks * k_rows
            s = lax.dot_general(qb, kb, (((1,), (1,)), ((), ())), preferred_element_type=F32) * scale
            tq = token_ids(n_chunks, q_rows, a_q, r_of_chunk, 0, (mq, mk))
            tk = token_ids(n_chunks, k_rows, a_k, r_of_chunk, 1, (mq, mk))
            delta = tq - tk
            s = jnp.where((delta >= 0) & (delta <= window), s, NEG_BIG)
            m_new = jnp.maximum(m_old, jnp.max(s, axis=1, keepdims=True))
            alpha = jnp.exp(m_old - m_new)
            p = jnp.exp(s - m_new[:, :1])
            l_new = alpha * l_old + jnp.sum(p, axis=1, keepdims=True)
            acc_new = alpha * acc_old + jnp.dot(p.astype(BF16), vb, preferred_element_type=F32)
            results.append((m_new, l_new, acc_new))
        for (q_starts, *_), (m_new, l_new, acc_new) in zip(blocks, results):
            for c, st in enumerate(q_starts):
                st = pl.multiple_of(st, 16)
                rs = slice(c * q_rows, (c + 1) * q_rows)
                m_ref[pl.ds(st, q_rows), :] = m_new[rs]
                l_ref[pl.ds(st, q_rows), :] = l_new[rs]
                acc_ref[pl.ds(st, q_rows), :] = acc_new[rs]

    w1 = DILATED_GROUPS[0][0]
    u1 = 2

    def body1(it, carry):
        blocks = []
        for j in range(u1):
            a0 = (it * u1 + j) * 16
            ak = jnp.maximum(a0 - 16, 0)
            blocks.append(([r * na + a0 for r in range(DIL)], [r * na + ak for r in range(DIL)],
                           a0, ak, lambda c: c))
        attend(blocks, 16, 32, w1)
        return carry

    lax.fori_loop(0, na // (16 * u1), body1, 0)

    w4 = DILATED_GROUPS[1][0]

    def body4(it, carry):
        a0 = it * 32
        ak = jnp.maximum(a0 - 32, 0)
        blocks = [([(rho + 4 * sg) * na + a0 for sg in range(4)], [(rho + 4 * sg) * na + ak for sg in range(4)],
                   a0, ak, (lambda c, rho=rho: rho + 4 * c)) for rho in range(4)]
        attend(blocks, 32, 64, w4)
        return carry

    lax.fori_loop(0, na // 32, body4, 0)

    w16 = DILATED_GROUPS[2][0]
    u16 = 4

    def body16(it, carry):
        rg = it // (na // 128)
        a0 = (it - rg * (na // 128)) * 128
        ak = jnp.maximum(a0 - 128, 0)
        blocks = [([(rg * u16 + j) * na + a0], [(rg * u16 + j) * na + ak], a0, ak,
                   (lambda c, r=rg * u16 + j: r)) for j in range(u16)]
        attend(blocks, 128, 256, w16)
        return carry

    lax.fori_loop(0, (DIL // u16) * (na // 128), body16, 0)

    o_ref[...] = (acc_ref[...] / l_ref[...]).astype(o_ref.dtype)


def _dilated_attention(qkv, d_model):
    T = qkv.shape[0]
    dh = d_model // ATT_HEADS
    na = T // DIL
    return pl.pallas_call(
        functools.partial(_attn_kernel, na=na, scale=dh ** -0.5),
        name="dilated_attn",
        grid=(ATT_HEADS,),
        in_specs=[pl.BlockSpec((T, dh), lambda h: (0, h)),
                  pl.BlockSpec((T, dh), lambda h: (0, ATT_HEADS + h)),
                  pl.BlockSpec((T, dh), lambda h: (0, 2 * ATT_HEADS + h))],
        out_specs=pl.BlockSpec((T, dh), lambda h: (0, h)),
        out_shape=jax.ShapeDtypeStruct((T, d_model), BF16),
        scratch_shapes=[pltpu.VMEM((T, dh), F32), pltpu.VMEM((T, dh), F32), pltpu.VMEM((T, dh), F32)],
        compiler_params=_cparams(("parallel",)),
    )(qkv, qkv, qkv)


def _route_kernel(x_ref, w_ref, b_ref, idx_ref, gate_ref, rank_ref, cnt_ref, carry_ref):
    @pl.when(pl.program_id(0) == 0)
    def _():
        carry_ref[...] = jnp.zeros_like(carry_ref)

    tb = x_ref.shape[0]
    lg = jnp.dot(x_ref[...], w_ref[...], preferred_element_type=F32, precision=HIGHEST) + b_ref[...]
    lane = lax.broadcasted_iota(I32, lg.shape, 1)
    vals, hots = [], []
    idx_out = jnp.zeros(lg.shape, I32)
    for k in range(TOP_K):
        m = jnp.max(lg, axis=1, keepdims=True)
        sel = jnp.min(jnp.where(lg == m, lane, LANES), axis=1, keepdims=True)
        hot = lane == sel
        vals.append(m)
        hots.append(hot)
        idx_out = jnp.where(lane == k, sel, idx_out)
        lg = jnp.where(hot, -jnp.inf, lg)
    ex = [jnp.exp(v - vals[0]) for v in vals]
    den = sum(ex)
    gate_out = jnp.zeros(lg.shape, F32)
    for k in range(TOP_K):
        gate_out = jnp.where(lane == k, ex[k] / den, gate_out)
    chosen = sum(h.astype(F32) for h in hots)
    ri = lax.broadcasted_iota(I32, (tb, tb), 0)
    ci = lax.broadcasted_iota(I32, (tb, tb), 1)
    before = jnp.dot((ri > ci).astype(BF16), chosen.astype(BF16), preferred_element_type=F32) + carry_ref[...]
    rank_out = jnp.zeros(lg.shape, I32)
    for k in range(TOP_K):
        rk = jnp.sum(jnp.where(hots[k], before, 0.0), axis=1, keepdims=True).astype(I32)
        rank_out = jnp.where(lane == k, rk, rank_out)
    idx_ref[...] = idx_out
    gate_ref[...] = gate_out
    rank_ref[...] = rank_out
    carry_ref[...] = carry_ref[...] + jnp.sum(chosen, axis=0, keepdims=True)
    cnt_ref[...] = carry_ref[...].astype(I32)


def _route(x, w_router, b_router, *, tb=512):
    T, D = x.shape
    E = w_router.shape[1]
    wr = jnp.zeros((D, LANES), F32).at[:, :E].set(w_router.astype(F32))
    br = jnp.full((1, LANES), NEG_BIG, F32).at[0, :E].set(b_router.astype(F32))
    row = lambda dt: jax.ShapeDtypeStruct((T, LANES), dt)
    idx, gate, rank, cnt = pl.pallas_call(
        _route_kernel,
        name="moe_route",
        grid=(T // tb,),
        in_specs=[pl.BlockSpec((tb, D), lambda i: (i, 0)),
                  pl.BlockSpec((D, LANES), lambda i: (0, 0)),
                  pl.BlockSpec((1, LANES), lambda i: (0, 0))],
        out_specs=[pl.BlockSpec((tb, LANES), lambda i: (i, 0)),
                   pl.BlockSpec((tb, LANES), lambda i: (i, 0)),
                   pl.BlockSpec((tb, LANES), lambda i: (i, 0)),
                   pl.BlockSpec((1, LANES), lambda i: (0, 0))],
        out_shape=[row(I32), row(F32), row(I32), jax.ShapeDtypeStruct((1, LANES), I32)],
        scratch_shapes=[pltpu.VMEM((1, LANES), F32)],
        compiler_params=_cparams(("arbitrary",)),
    )(x, wr, br)
    return idx[:, :TOP_K], gate[:, :TOP_K], rank[:, :TOP_K], cnt[0, :E]


def _zero_tail_kernel(row_ref, o_ref, z_ref, sem):
    z_ref[...] = jnp.zeros_like(z_ref)
    n = row_ref.shape[0]

    def copy(e):
        return pltpu.make_async_copy(z_ref, o_ref.at[pl.ds(pl.multiple_of(row_ref[e], MOE_BLOCK), MOE_BLOCK)], sem)

    def start(e, c):
        copy(e).start()
        return c

    def wait(e, c):
        copy(e).wait()
        return c

    lax.fori_loop(0, n, start, 0)
    lax.fori_loop(0, n, wait, 0)


def _zero_tails(tail_rows, n_rows, d):
    return pl.pallas_call(
        _zero_tail_kernel,
        name="moe_zero_tails",
        grid_spec=pltpu.PrefetchScalarGridSpec(
            num_scalar_prefetch=1, grid=(1,),
            in_specs=[],
            out_specs=pl.BlockSpec(memory_space=pl.ANY),
            scratch_shapes=[pltpu.VMEM((MOE_BLOCK, d), F32), pltpu.SemaphoreType.DMA(())]),
        out_shape=jax.ShapeDtypeStruct((n_rows, d), F32),
        compiler_params=_cparams(("arbitrary",)),
    )(tail_rows)


def _dispatch_kernel(dest_ref, x_ref, xs_in_ref, xs_ref, sem):
    del xs_in_ref
    tb = x_ref.shape[0]
    base = pl.program_id(0) * tb * TOP_K

    def copy(i, k):
        return pltpu.make_async_copy(x_ref.at[pl.ds(i, 1)],
                                     xs_ref.at[pl.ds(dest_ref[base + i * TOP_K + k], 1)], sem)

    def start(i, c):
        for k in range(TOP_K):
            copy(i, k).start()
        return c

    def wait(i, c):
        for k in range(TOP_K):
            copy(i, k).wait()
        return c

    lax.fori_loop(0, tb, start, 0)
    lax.fori_loop(0, tb, wait, 0)


def _dispatch(x, dest_flat, xs_init, *, tb=256):
    T, D = x.shape
    return pl.pallas_call(
        _dispatch_kernel,
        name="moe_dispatch",
        grid_spec=pltpu.PrefetchScalarGridSpec(
            num_scalar_prefetch=1, grid=(T // tb,),
            in_specs=[pl.BlockSpec((tb, D), lambda i, dest: (i, 0)),
                      pl.BlockSpec(memory_space=pl.ANY)],
            out_specs=pl.BlockSpec(memory_space=pl.ANY),
            scratch_shapes=[pltpu.SemaphoreType.DMA(())]),
        out_shape=jax.ShapeDtypeStruct(xs_init.shape, xs_init.dtype),
        input_output_aliases={2: 0},
        compiler_params=_cparams(("arbitrary",)),
    )(dest_flat, x, xs_init)


def _moe_kernel(item_e_ref, item_row_ref, item_nb_ref, xs_ref, wg_ref, wu_ref, bg_ref, bu_ref, wd_ref, bd_ref,
                ys_ref, xbuf_ref, stage_ref, acc_ref, wgb_ref, wub_ref, wdb_ref, in_sem, out_sem, *, n_f):
    it = pl.program_id(0)
    f = pl.program_id(1)
    nb = item_nb_ref[it]
    row0 = item_row_ref[it]
    blk = MOE_BLOCK

    def in_copy(j, slot):
        src = xs_ref.at[pl.ds(pl.multiple_of(row0 + j * blk, blk), blk)]
        return pltpu.make_async_copy(src, stage_ref.at[slot], in_sem.at[slot])

    def out_copy(j):
        dst = ys_ref.at[pl.ds(pl.multiple_of(row0 + j * blk, blk), blk)]
        return pltpu.make_async_copy(acc_ref.at[pl.ds(pl.multiple_of(j * blk, blk), blk)], dst, out_sem)

    @pl.when((f == 0) & (nb > 0))
    def _():
        in_copy(0, 0).start()

        def load(j, c):
            slot = j % 2

            @pl.when(j + 1 < nb)
            def _():
                in_copy(j + 1, 1 - slot).start()

            in_copy(j, slot).wait()
            rows = pl.ds(pl.multiple_of(j * blk, blk), blk)
            xbuf_ref[rows, :] = stage_ref[slot].astype(BF16)
            acc_ref[rows, :] = jnp.broadcast_to(bd_ref[...], (blk, acc_ref.shape[1]))
            return c

        lax.fori_loop(0, nb, load, 0)

    @pl.when(nb > 0)
    def _():
        wgb_ref[...] = wg_ref[...].astype(BF16)
        wub_ref[...] = wu_ref[...].astype(BF16)
        wdb_ref[...] = wd_ref[...].astype(BF16)

        def compute(r0, n_rows):
            rows = pl.ds(pl.multiple_of(r0, blk), n_rows)
            xb = xbuf_ref[rows, :]
            hg = jnp.dot(xb, wgb_ref[...], preferred_element_type=F32) + bg_ref[...]
            hu = jnp.dot(xb, wub_ref[...], preferred_element_type=F32) + bu_ref[...]
            gate = jnp.minimum(hg, SWIGLU_LIMIT)
            up = jnp.clip(hu, -SWIGLU_LIMIT, SWIGLU_LIMIT)
            act = (up + 1.0) * (gate * jax.nn.sigmoid(SWIGLU_ALPHA * gate))
            acc_ref[rows, :] += jnp.dot(act.astype(BF16), wdb_ref[...], preferred_element_type=F32)

        n4 = nb // 4

        def quad(j, c):
            compute(j * (4 * blk), 4 * blk)
            return c

        lax.fori_loop(0, n4, quad, 0)

        @pl.when((nb & 2) != 0)
        def _():
            compute(n4 * (4 * blk), 2 * blk)

        @pl.when((nb & 1) != 0)
        def _():
            compute((nb - 1) * blk, blk)

    @pl.when((f == n_f - 1) & (nb > 0))
    def _():
        def start(j, c):
            out_copy(j).start()
            return c

        def wait(j, c):
            out_copy(j).wait()
            return c

        lax.fori_loop(0, nb, start, 0)
        lax.fori_loop(0, nb, wait, 0)


def _moe_experts(xs, items, w_gu, b_gu, w_down, b_down, layer, *, tf=256):
    item_e, item_row, item_nb = items
    n_items = item_e.shape[0]
    P, D = xs.shape
    L, E, _, F2 = w_gu.shape
    F = F2 // 2
    n_f = F // tf
    rc = MOE_CHUNK_BLOCKS * MOE_BLOCK

    def fe(f, nb_ref, it):
        return jnp.where(nb_ref[it] > 0, f, n_f - 1)

    in_specs = [
        pl.BlockSpec(memory_space=pl.ANY),
        pl.BlockSpec((None, None, D, tf), lambda it, f, e, r, nb: (layer, e[it], 0, fe(f, nb, it))),
        pl.BlockSpec((None, None, D, tf), lambda it, f, e, r, nb: (layer, e[it], 0, n_f + fe(f, nb, it))),
        pl.BlockSpec((None, None, 1, tf), lambda it, f, e, r, nb: (layer, e[it], 0, fe(f, nb, it))),
        pl.BlockSpec((None, None, 1, tf), lambda it, f, e, r, nb: (layer, e[it], 0, n_f + fe(f, nb, it))),
        pl.BlockSpec((None, None, tf, D), lambda it, f, e, r, nb: (layer, e[it], fe(f, nb, it), 0)),
        pl.BlockSpec((None, None, 1, D), lambda it, f, e, r, nb: (layer, e[it], 0, 0)),
    ]
    return pl.pallas_call(
        functools.partial(_moe_kernel, n_f=n_f),
        name="moe_experts",
        grid_spec=pltpu.PrefetchScalarGridSpec(
            num_scalar_prefetch=3, grid=(n_items, n_f),
            in_specs=in_specs,
            out_specs=pl.BlockSpec(memory_space=pl.ANY),
            scratch_shapes=[pltpu.VMEM((rc, D), BF16),
                            pltpu.VMEM((2, MOE_BLOCK, D), F32),
                            pltpu.VMEM((rc, D), F32),
                            pltpu.VMEM((D, tf), BF16),
                            pltpu.VMEM((D, tf), BF16),
                            pltpu.VMEM((tf, D), BF16),
                            pltpu.SemaphoreType.DMA((2,)),
                            pltpu.SemaphoreType.DMA(())]),
        out_shape=jax.ShapeDtypeStruct((P, D), F32),
        compiler_params=_cparams(("arbitrary", "arbitrary")),
    )(item_e, item_row, item_nb, xs, w_gu, w_gu, b_gu.reshape(L, E, 1, F2), b_gu.reshape(L, E, 1, F2),
      w_down, b_down.reshape(L, E, 1, D))


def _combine_kernel(dest_ref, ys_ref, gate_ref, x_ref, g_ref, b_ref, o_ref, buf_ref, sem, *, alpha):
    tb = x_ref.shape[0]
    base = pl.program_id(0) * tb * TOP_K

    def copy(i, k):
        return pltpu.make_async_copy(ys_ref.at[pl.ds(dest_ref[base + i * TOP_K + k], 1)],
                                     buf_ref.at[k, pl.ds(i, 1)], sem)

    def start(i, c):
        for k in range(TOP_K):
            copy(i, k).start()
        return c

    def wait(i, c):
        for k in range(TOP_K):
            copy(i, k).wait()
        return c

    lax.fori_loop(0, tb, start, 0)
    lax.fori_loop(0, tb, wait, 0)
    gate = gate_ref[...]
    ffn = gate[:, 0:1] * buf_ref[0]
    for k in range(1, TOP_K):
        ffn = ffn + gate[:, k:k + 1] * buf_ref[k]
    o_ref[...] = _layer_norm_rows(alpha * x_ref[...] + ffn, g_ref[...], b_ref[...])


def _combine_ln(ys, dest_flat, gates, x, g, b, *, alpha, tb=128):
    T, D = x.shape
    gate_pad = jnp.zeros((T, LANES), F32).at[:, :TOP_K].set(gates)
    return pl.pallas_call(
        functools.partial(_combine_kernel, alpha=alpha),
        name="moe_combine",
        grid_spec=pltpu.PrefetchScalarGridSpec(
            num_scalar_prefetch=1, grid=(T // tb,),
            in_specs=[pl.BlockSpec(memory_space=pl.ANY),
                      pl.BlockSpec((tb, LANES), lambda i, dest: (i, 0)),
                      pl.BlockSpec((tb, D), lambda i, dest: (i, 0)),
                      pl.BlockSpec((1, D), lambda i, dest: (0, 0)),
                      pl.BlockSpec((1, D), lambda i, dest: (0, 0))],
            out_specs=pl.BlockSpec((tb, D), lambda i, dest: (i, 0)),
            scratch_shapes=[pltpu.VMEM((TOP_K, tb, D), F32), pltpu.SemaphoreType.DMA(())]),
        out_shape=jax.ShapeDtypeStruct((T, D), F32),
        compiler_params=_cparams(("arbitrary",)),
    )(dest_flat, ys, gate_pad, x, g.reshape(1, D), b.reshape(1, D))


def _moe_layout(idx, rank, counts, n_assign):
    E = counts.shape[0]
    blk = MOE_BLOCK
    nblk = (counts + blk - 1) // blk
    bend = jnp.cumsum(nblk)
    bstart = bend - nblk
    dest = (bstart * blk)[idx] + rank
    tail = jnp.where(nblk > 0, (bend - 1) * blk, 0).astype(I32)
    cb = MOE_CHUNK_BLOCKS
    n_items_max = E + (n_assign // blk + E) // cb + 1
    per_e = (nblk + cb - 1) // cb
    iend = jnp.cumsum(per_e)
    istart = iend - per_e
    ids = jnp.arange(n_items_max, dtype=I32)
    e_of = jnp.minimum(jnp.searchsorted(iend, ids, side='right'), E - 1).astype(I32)
    valid = ids < iend[-1]
    last_e = jnp.max(jnp.where(per_e > 0, jnp.arange(E, dtype=I32), 0))
    e_of = jnp.where(valid, e_of, last_e)
    local = ids - istart[e_of]
    first_blk = bstart[e_of] + local * cb
    nb = jnp.where(valid, jnp.minimum(cb, nblk[e_of] - local * cb), 0)
    row = jnp.where(valid, first_blk * blk, 0)
    return dest.astype(I32), tail, (e_of, row.astype(I32), nb.astype(I32))


def _moe(x, w_router, b_router, w_gu, b_gu, w_down, b_down, ln_g, ln_b, *, alpha, layer):
    T, D = x.shape
    E = w_router.shape[1]
    idx, gates, rank, counts = _route(x, w_router, b_router)
    n_assign = T * TOP_K
    dest, tail, items = _moe_layout(idx, rank, counts, n_assign)
    dest_flat = dest.reshape(n_assign)
    n_rows = n_assign + E * MOE_BLOCK
    xs = _dispatch(x, dest_flat, _zero_tails(tail, n_rows, D))
    ys = _moe_experts(xs, items, w_gu, b_gu, w_down, b_down, layer)
    return _combine_ln(ys, dest_flat, gates, x, ln_g, ln_b, alpha=alpha)


def _even_mixer_ln(x, w_in, lam_re, lam_im, log_step, b_re, b_im, c_re, c_im, d_skip, w_glu, b_glu,
                   w_gate2, b_gate2, norm_g, w_out, ln_g, ln_b, *, alpha):
    T, D = x.shape
    W = d_skip.shape[0]
    qk = W // 2
    rank = w_gate2.shape[0]
    s4 = W + 2 * qk + W
    w_main = jnp.concatenate([w_in[:, :s4], w_in[:, s4 + rank:]], axis=1).astype(BF16)
    w_gate = jnp.zeros((D, LANES), F32).at[:, :rank].set(w_in[:, s4:s4 + rank].astype(F32))
    h, g_low = _proj_in(x, w_main, w_gate)
    tables = _s5_tables(lam_re, lam_im, log_step, b_re, b_im, c_re, c_im, T // S5_CHUNK)
    y = _s5_core(h[:, :W], tables)
    ya = _s5_glu(y, h, d_skip.astype(F32), w_glu.astype(BF16), b_glu.astype(F32))
    yb = _gla(h, g_low, w_gate2, b_gate2, norm_g, width=W)
    w_out_b = w_out.astype(BF16)
    return _proj_ln([ya, yb], [w_out_b[:W], w_out_b[W:]], x, ln_g, ln_b, alpha=alpha, permuted=False)


def _odd_mixer_ln(x, w_qkv, w_o, ln_g, ln_b, *, alpha):
    T, D = x.shape
    qkv = _proj_perm(x, w_qkv.astype(BF16))
    y = _dilated_attention(qkv, D)
    return _proj_ln([y], [w_o.astype(BF16)], x, ln_g, ln_b, alpha=alpha, permuted=True)


def kernel(x, ab_w_in, s5_lam_re, s5_lam_im, s5_log_step, s5_b_re, s5_b_im, s5_c_re, s5_c_im, s5_d, s5_w_glu, s5_b_glu, gla_w_gate2, gla_b_gate2, gla_norm_g, ab_w_out, c_w_qkv, c_w_o, ln1_g, ln1_b, moe_w_router, moe_b_router, moe_w_gu, moe_b_gu, moe_w_down, moe_b_down, ln2_g, ln2_b):
    bsz, L, D = x.shape
    depth = ln1_g.shape[0]
    alpha = (2 * depth) ** 0.25
    outs = []
    for bi in range(bsz):
        xt = x[bi].astype(F32)
        for layer in range(depth):
            i = layer // 2
            if layer % 2 == 0:
                xt = _even_mixer_ln(xt, ab_w_in[i], s5_lam_re[i], s5_lam_im[i], s5_log_step[i], s5_b_re[i],
                                    s5_b_im[i], s5_c_re[i], s5_c_im[i], s5_d[i], s5_w_glu[i], s5_b_glu[i],
                                    gla_w_gate2[i], gla_b_gate2[i], gla_norm_g[i], ab_w_out[i],
                                    ln1_g[layer], ln1_b[layer], alpha=alpha)
            else:
                xt = _odd_mixer_ln(xt, c_w_qkv[i], c_w_o[i], ln1_g[layer], ln1_b[layer], alpha=alpha)
            xt = _moe(xt, moe_w_router[layer], moe_b_router[layer], moe_w_gu, moe_b_gu,
                      moe_w_down, moe_b_down, ln2_g[layer], ln2_b[layer], alpha=alpha, layer=layer)
        outs.append(xt)
    return jnp.stack(outs).astype(x.dtype)
```

```python
import functools
import math

import jax
import jax.numpy as jnp
from jax import lax
from jax.experimental import pallas as pl
from jax.experimental.pallas import tpu as pltpu

F32 = jnp.float32
BF16 = jnp.bfloat16
I32 = jnp.int32
HIGHEST = lax.Precision.HIGHEST

LANES = 128
VMEM_LIMIT_BYTES = 56 * 1024 * 1024

S5_GROUP = 16
S5_STATE = 64
S5_MAX_RE = -1e-4
S5_CHUNK = 16
GLA_HEADS = 4
GLA_GATE_TEMP = 16.0
GLA_EPS = 1e-6
GLA_BLOCK = 64
GLA_SUB = 16
ATT_HEADS = 16
DIL = 16
DILATED_GROUPS = ((128, 1), (512, 4), (2048, 16))
TOP_K = 4
SWIGLU_LIMIT = 7.0
SWIGLU_ALPHA = 1.702
MOE_BLOCK = 128
MOE_CHUNK_BLOCKS = 10
LN_EPS = 1e-5
NEG_BIG = -1e30


def _cparams(semantics):
    return pltpu.CompilerParams(dimension_semantics=semantics, vmem_limit_bytes=VMEM_LIMIT_BYTES)


def _layer_norm_rows(z, g, b):
    mu = jnp.mean(z, axis=-1, keepdims=True)
    zc = z - mu
    var = jnp.mean(zc * zc, axis=-1, keepdims=True)
    return zc * lax.rsqrt(var + LN_EPS) * g + b


def _proj_in_kernel(x_ref, w_ref, wg_ref, h_ref, g_ref, xb_ref):
    @pl.when(pl.program_id(1) == 0)
    def _():
        xb = x_ref[...].astype(BF16)
        xb_ref[...] = xb
        g_ref[...] = jnp.dot(xb, wg_ref[...], preferred_element_type=F32)

    h_ref[...] = jnp.dot(xb_ref[...], w_ref[...], preferred_element_type=F32)


def _proj_in(x, w_main, w_gate, *, tm=1024, tn=512):
    T, D = x.shape
    N = w_main.shape[1]
    return pl.pallas_call(
        _proj_in_kernel,
        name="proj_in",
        grid=(T // tm, N // tn),
        in_specs=[pl.BlockSpec((tm, D), lambda i, j: (i, 0)),
                  pl.BlockSpec((D, tn), lambda i, j: (0, j)),
                  pl.BlockSpec((D, LANES), lambda i, j: (0, 0))],
        out_specs=[pl.BlockSpec((tm, tn), lambda i, j: (i, j)),
                   pl.BlockSpec((tm, LANES), lambda i, j: (i, 0))],
        out_shape=[jax.ShapeDtypeStruct((T, N), F32), jax.ShapeDtypeStruct((T, LANES), F32)],
        scratch_shapes=[pltpu.VMEM((tm, D), BF16)],
        compiler_params=_cparams(("parallel", "arbitrary")),
    )(x, w_main, w_gate)


def _proj_perm_kernel(x0_ref, x1_ref, w_ref, o_ref, xb_ref, *, n_scaled, scale):
    j = pl.program_id(1)
    na = x0_ref.shape[0]

    @pl.when(j == 0)
    def _():
        xb_ref[:na, :] = x0_ref[...].astype(BF16)
        xb_ref[na:, :] = x1_ref[...].astype(BF16)

    y = jnp.dot(xb_ref[...], w_ref[...], preferred_element_type=F32)
    o_ref[...] = (y * jnp.where(j < n_scaled, scale, 1.0)).astype(o_ref.dtype)


def _proj_perm(x, w, *, scaled_cols, scale, tn=1024):
    T, D = x.shape
    N = w.shape[1]
    na = T // DIL
    x3 = x.reshape(na, DIL * D)
    return pl.pallas_call(
        functools.partial(_proj_perm_kernel, n_scaled=scaled_cols // tn, scale=scale),
        name="proj_perm",
        grid=(DIL // 2, N // tn),
        in_specs=[pl.BlockSpec((na, D), lambda r, j: (0, 2 * r)),
                  pl.BlockSpec((na, D), lambda r, j: (0, 2 * r + 1)),
                  pl.BlockSpec((D, tn), lambda r, j: (0, j))],
        out_specs=pl.BlockSpec((2 * na, tn), lambda r, j: (r, j)),
        out_shape=jax.ShapeDtypeStruct((T, N), BF16),
        scratch_shapes=[pltpu.VMEM((2 * na, D), BF16)],
        compiler_params=_cparams(("parallel", "arbitrary")),
    )(x3, x3, w)


def _proj_ln_kernel(*refs, n_lhs, alpha, n_tiles):
    lhs_refs = refs[:n_lhs]
    w_refs = refs[n_lhs:2 * n_lhs]
    res_ref, g_ref, b_ref, o_ref, acc_ref = refs[2 * n_lhs:]
    j = pl.program_id(1)
    y = jnp.dot(lhs_refs[0][...], w_refs[0][...], preferred_element_type=F32)
    for a_ref, w_ref in zip(lhs_refs[1:], w_refs[1:]):
        y = y + jnp.dot(a_ref[...], w_ref[...], preferred_element_type=F32)
    acc_ref[j] = y

    @pl.when(j == n_tiles - 1)
    def _():
        tn = acc_ref.shape[2]
        z = [alpha * res_ref[:, t * tn:(t + 1) * tn] + acc_ref[t] for t in range(n_tiles)]
        n = float(n_tiles * tn)
        mu = sum(jnp.sum(zt, axis=-1, keepdims=True) for zt in z) / n
        zc = [zt - mu for zt in z]
        var = sum(jnp.sum(zt * zt, axis=-1, keepdims=True) for zt in zc) / n
        rstd = lax.rsqrt(var + LN_EPS)
        for t in range(n_tiles):
            sl = slice(t * tn, (t + 1) * tn)
            o_ref[:, sl] = zc[t] * rstd * g_ref[:, sl] + b_ref[:, sl]


def _proj_ln(lhs, ws, res, g, b, *, alpha, permuted, tm=512, tn=512):
    T, N = res.shape
    n_lhs = len(lhs)
    n_tiles = N // tn
    if permuted:
        tm = T // DIL
        res_in = res.reshape(tm, DIL * N)
        res_spec = pl.BlockSpec((tm, N), lambda i, j: (0, i))
        out_spec = pl.BlockSpec((tm, N), lambda i, j: (0, i))
        out_shape = jax.ShapeDtypeStruct((tm, DIL * N), F32)
    else:
        res_in = res
        res_spec = pl.BlockSpec((tm, N), lambda i, j: (i, 0))
        out_spec = pl.BlockSpec((tm, N), lambda i, j: (i, 0))
        out_shape = jax.ShapeDtypeStruct((T, N), F32)
    in_specs = ([pl.BlockSpec((tm, a.shape[1]), lambda i, j: (i, 0)) for a in lhs]
                + [pl.BlockSpec((w.shape[0], tn), lambda i, j: (0, j)) for w in ws]
                + [res_spec,
                   pl.BlockSpec((1, N), lambda i, j: (0, 0)),
                   pl.BlockSpec((1, N), lambda i, j: (0, 0))])
    out = pl.pallas_call(
        functools.partial(_proj_ln_kernel, n_lhs=n_lhs, alpha=alpha, n_tiles=n_tiles),
        name="proj_ln",
        grid=(T // tm, n_tiles),
        in_specs=in_specs,
        out_specs=out_spec,
        out_shape=out_shape,
        scratch_shapes=[pltpu.VMEM((n_tiles, tm, tn), F32)],
        compiler_params=_cparams(("parallel", "arbitrary")),
    )(*lhs, *ws, res_in, g.reshape(1, N), b.reshape(1, N))
    return out.reshape(T, N)


def _s5_tables(lam_re, lam_im, log_step, b_re, b_im, c_re, c_im, n_chunks):
    C = S5_CHUNK
    G, P = lam_re.shape
    H = b_re.shape[-1]
    lr = jnp.minimum(lam_re.astype(F32), S5_MAX_RE)
    li = lam_im.astype(F32)
    dt = jnp.exp(log_step.astype(F32))[:, None]
    mag = jnp.exp(lr * dt)
    a_re = mag * jnp.cos(li * dt)
    a_im = mag * jnp.sin(li * dt)
    den = lr * lr + li * li
    nr = a_re - 1.0
    f_re = (nr * lr + a_im * li) / den
    f_im = (a_im * lr - nr * li) / den
    br = b_re.astype(F32)
    bi = b_im.astype(F32)
    bb_re = f_re[..., None] * br - f_im[..., None] * bi
    bb_im = f_re[..., None] * bi + f_im[..., None] * br
    pw_re = [jnp.ones_like(a_re)]
    pw_im = [jnp.zeros_like(a_im)]
    for _ in range(C):
        pr, pi = pw_re[-1], pw_im[-1]
        pw_re.append(pr * a_re - pi * a_im)
        pw_im.append(pr * a_im + pi * a_re)
    pw_re = jnp.stack(pw_re)
    pw_im = jnp.stack(pw_im)
    ab_re = pw_re[:C, :, :, None] * bb_re[None] - pw_im[:C, :, :, None] * bb_im[None]
    ab_im = pw_re[:C, :, :, None] * bb_im[None] + pw_im[:C, :, :, None] * bb_re[None]
    cr = c_re.astype(F32)
    ci = c_im.astype(F32)
    z_re = jnp.transpose(ab_re[::-1], (1, 0, 3, 2)).reshape(G, C * H, P)
    z_im = jnp.transpose(ab_im[::-1], (1, 0, 3, 2)).reshape(G, C * H, P)
    zmat = jnp.concatenate([z_re, z_im], axis=-1)
    kern = (jnp.einsum('gop,kgph->kgoh', cr, ab_re, precision=HIGHEST)
            - jnp.einsum('gop,kgph->kgoh', ci, ab_im, precision=HIGHEST))
    lag = jnp.arange(C)[None, :] - jnp.arange(C)[:, None]
    kl = kern[jnp.clip(lag, 0, C - 1)]
    kl = jnp.where((lag >= 0)[:, :, None, None, None], kl, 0.0)
    mmat = jnp.transpose(kl, (2, 0, 4, 1, 3)).reshape(G, C * H, C * H)
    ca_re = cr[None] * pw_re[1:, :, None, :] - ci[None] * pw_im[1:, :, None, :]
    ca_im = cr[None] * pw_im[1:, :, None, :] + ci[None] * pw_re[1:, :, None, :]
    n_re = jnp.transpose(ca_re, (1, 3, 0, 2)).reshape(G, P, C * H)
    n_im = jnp.transpose(-ca_im, (1, 3, 0, 2)).reshape(G, P, C * H)
    nmat = jnp.concatenate([n_re, n_im], axis=1)
    n_steps = max(1, (n_chunks - 1).bit_length())
    qr, qi = pw_re[C], pw_im[C]
    a1, a2 = [], []
    for _ in range(n_steps):
        a1.append(jnp.concatenate([qr, qr], axis=-1))
        a2.append(jnp.concatenate([-qi, qi], axis=-1))
        qr, qi = qr * qr - qi * qi, 2.0 * qr * qi
    a1 = jnp.stack(a1, axis=1)
    a2 = jnp.stack(a2, axis=1)
    return zmat.astype(BF16), mmat.astype(BF16), nmat.astype(BF16), a1, a2


def _s5_kernel(u_ref, z_ref, m_ref, n_ref, a1_ref, a2_ref, y_ref, *, n_steps):
    u = u_ref[...]
    nc = u.shape[0]
    s = jnp.dot(u, z_ref[...], preferred_element_type=F32)
    half = s.shape[1] // 2
    row = lax.broadcasted_iota(I32, s.shape, 0)
    for k in range(n_steps):
        sh = 1 << k
        prev = jnp.where(row >= sh, pltpu.roll(s, sh, axis=0), 0.0)
        s = s + a1_ref[k:k + 1, :] * prev + a2_ref[k:k + 1, :] * pltpu.roll(prev, half, axis=1)
    s_in = jnp.where(row >= 1, pltpu.roll(s, 1, axis=0), 0.0)
    y_ref[...] = (jnp.dot(u, m_ref[...], preferred_element_type=F32)
                  + jnp.dot(s_in.astype(BF16), n_ref[...], preferred_element_type=F32))


def _s5_core(u, tables):
    zmat, mmat, nmat, a1, a2 = tables
    T, W = u.shape
    G = zmat.shape[0]
    C, H = S5_CHUNK, S5_GROUP
    nc = T // C
    CH = C * H
    P2 = zmat.shape[2]
    n_steps = a1.shape[1]
    ug = u.astype(BF16).reshape(nc, C, G, H).transpose(2, 0, 1, 3).reshape(G, nc, CH)
    yg = pl.pallas_call(
        functools.partial(_s5_kernel, n_steps=n_steps),
        name="s5_core",
        grid=(G,),
        in_specs=[pl.BlockSpec((None, nc, CH), lambda g: (g, 0, 0)),
                  pl.BlockSpec((None, CH, P2), lambda g: (g, 0, 0)),
                  pl.BlockSpec((None, CH, CH), lambda g: (g, 0, 0)),
                  pl.BlockSpec((None, P2, CH), lambda g: (g, 0, 0)),
                  pl.BlockSpec((None, n_steps, P2), lambda g: (g, 0, 0)),
                  pl.BlockSpec((None, n_steps, P2), lambda g: (g, 0, 0))],
        out_specs=pl.BlockSpec((None, nc, CH), lambda g: (g, 0, 0)),
        out_shape=jax.ShapeDtypeStruct((G, nc, CH), F32),
        compiler_params=_cparams(("parallel",)),
    )(ug, zmat, mmat, nmat, a1, a2)
    return yg.reshape(G, nc, C, H).transpose(1, 2, 0, 3).reshape(T, W)


def _s5_glu_kernel(y_ref, u_ref, d_ref, w_ref, b_ref, o_ref):
    y = y_ref[...] + d_ref[...] * u_ref[...]
    c0 = math.sqrt(2.0 / math.pi)
    z = 0.5 * y * (1.0 + jnp.tanh(c0 * (y + 0.044715 * (y * y * y))))
    lin = jnp.dot(z.astype(BF16), w_ref[...], preferred_element_type=F32) + b_ref[...]
    o_ref[...] = (z * jax.nn.sigmoid(lin)).astype(o_ref.dtype)


def _s5_glu(y, h, d_skip, w_glu, b_glu, *, tm=512):
    T, W = y.shape
    return pl.pallas_call(
        _s5_glu_kernel,
        name="s5_glu",
        grid=(T // tm,),
        in_specs=[pl.BlockSpec((tm, W), lambda i: (i, 0)),
                  pl.BlockSpec((tm, W), lambda i: (i, 0)),
                  pl.BlockSpec((1, W), lambda i: (0, 0)),
                  pl.BlockSpec((W, W), lambda i: (0, 0)),
                  pl.BlockSpec((1, W), lambda i: (0, 0))],
        out_specs=pl.BlockSpec((tm, W), lambda i: (i, 0)),
        out_shape=jax.ShapeDtypeStruct((T, W), BF16),
        compiler_params=_cparams(("parallel",)),
    )(y, h, d_skip.reshape(1, W), w_glu, b_glu.reshape(1, W))


def _gla_kernel(q_ref, k_ref, v_ref, r_ref, g_ref, w2_ref, b2_ref, ng_ref, o_ref, st_ref, *, dk, dv):
    @pl.when(pl.program_id(0) == 0)
    def _():
        st_ref[...] = jnp.zeros_like(st_ref)

    cb = q_ref.shape[0]
    n_sub = cb // GLA_SUB
    scale = dk ** -0.5
    logit = jnp.dot(g_ref[...], w2_ref[...], preferred_element_type=F32, precision=HIGHEST) + b2_ref[...]
    log_a = (jnp.minimum(logit, 0.0) - jnp.log(1.0 + jnp.exp(-jnp.abs(logit)))) / GLA_GATE_TEMP
    ri = lax.broadcasted_iota(I32, (cb, cb), 0)
    ci = lax.broadcasted_iota(I32, (cb, cb), 1)
    tri = (ri >= ci).astype(F32)
    bcum = jnp.dot(tri, log_a, preferred_element_type=F32, precision=HIGHEST)
    for hh in range(GLA_HEADS):
        ks = slice(hh * dk, (hh + 1) * dk)
        vs = slice(hh * dv, (hh + 1) * dv)
        b = bcum[:, ks]
        q = q_ref[:, ks] * scale
        k = k_ref[:, ks]
        v = v_ref[:, vs].astype(BF16)
        refs = [jnp.zeros((1, dk), F32)] + [b[a * GLA_SUB - 1:a * GLA_SUB, :] for a in range(1, n_sub)]
        refmat = jnp.concatenate([jnp.broadcast_to(r, (GLA_SUB, dk)) for r in refs], axis=0)
        qe = (q * jnp.exp(b - refmat)).astype(BF16)
        st = st_ref[hh]
        o_inter = lax.dot_general((q * jnp.exp(b)).astype(BF16), st.astype(BF16),
                                  (((1,), (1,)), ((), ())), preferred_element_type=F32)
        o_rows = []
        for a in range(n_sub):
            hi = (a + 1) * GLA_SUB
            ke = (k[:hi] * jnp.exp(refs[a] - b[:hi])).astype(BF16)
            att = lax.dot_general(qe[a * GLA_SUB:hi], ke, (((1,), (1,)), ((), ())),
                                  preferred_element_type=F32)
            row_a = lax.broadcasted_iota(I32, (GLA_SUB, hi), 0) + a * GLA_SUB
            att = jnp.where(lax.broadcasted_iota(I32, (GLA_SUB, hi), 1) <= row_a, att, 0.0)
            o_rows.append(jnp.dot(att.astype(BF16), v[:hi], preferred_element_type=F32))
        o = jnp.concatenate(o_rows, axis=0) + o_inter
        o = o * lax.rsqrt(jnp.mean(o * o, axis=-1, keepdims=True) + GLA_EPS) * ng_ref[...]
        r = r_ref[:, vs]
        o_ref[:, vs] = (o * (r * jax.nn.sigmoid(r))).astype(o_ref.dtype)
        b_last = b[cb - 1:cb, :]
        kd = (k * jnp.exp(b_last - b)).astype(BF16)
        upd = lax.dot_general(v, kd, (((0,), (0,)), ((), ())), preferred_element_type=F32)
        st_ref[hh] = st * jnp.exp(b_last) + upd


def _gla(h, g_low, w_gate2, b_gate2, norm_g, *, width):
    T = h.shape[0]
    qk = width // 2
    dk = qk // GLA_HEADS
    dv = width // GLA_HEADS
    cb = GLA_BLOCK
    w2 = jnp.zeros((LANES, qk), F32).at[:w_gate2.shape[0]].set(w_gate2.astype(F32))
    return pl.pallas_call(
        functools.partial(_gla_kernel, dk=dk, dv=dv),
        name="gla",
        grid=(T // cb,),
        in_specs=[pl.BlockSpec((cb, qk), lambda i: (i, 2)),
                  pl.BlockSpec((cb, qk), lambda i: (i, 3)),
                  pl.BlockSpec((cb, width), lambda i: (i, 2)),
                  pl.BlockSpec((cb, width), lambda i: (i, 3)),
                  pl.BlockSpec((cb, LANES), lambda i: (i, 0)),
                  pl.BlockSpec((LANES, qk), lambda i: (0, 0)),
                  pl.BlockSpec((1, qk), lambda i: (0, 0)),
                  pl.BlockSpec((1, dv), lambda i: (0, 0))],
        out_specs=pl.BlockSpec((cb, width), lambda i: (i, 0)),
        out_shape=jax.ShapeDtypeStruct((T, width), BF16),
        scratch_shapes=[pltpu.VMEM((GLA_HEADS, dv, dk), F32)],
        compiler_params=_cparams(("arbitrary",)),
    )(h, h, h, h, g_low, w2, b_gate2.reshape(1, qk).astype(F32), norm_g.reshape(1, dv).astype(F32))


def _attn_kernel(q_ref, k_ref, v_ref, o_ref, acc_ref, m_ref, l_ref, pat1_ref, pat4_ref, pat16_ref, *, na):
    acc_ref[...] = jnp.zeros_like(acc_ref)
    m_ref[...] = jnp.full_like(m_ref, NEG_BIG)
    l_ref[...] = jnp.zeros_like(l_ref)

    def token_offsets(rows, r_step, n, axis):
        shape = (n, 1) if axis == 0 else (1, n)
        idx = lax.broadcasted_iota(I32, shape, axis)
        c = idx // rows
        return DIL * (idx - c * rows) + r_step * c

    def delta_pattern(n_chunks, q_rows, k_rows, r_step):
        return (token_offsets(q_rows, r_step, n_chunks * q_rows, 0)
                - token_offsets(k_rows, r_step, n_chunks * k_rows, 1))

    def cat(ref, starts, rows):
        return jnp.concatenate([ref[pl.ds(pl.multiple_of(s, 16), rows), :] for s in starts], axis=0)

    def attend(blocks, q_rows, k_rows, window, pat_ref):
        loaded = []
        for q_starts, k_starts, a_q, a_k in blocks:
            loaded.append((cat(q_ref, q_starts, q_rows), cat(k_ref, k_starts, k_rows), cat(v_ref, k_starts, k_rows),
                           cat(m_ref, q_starts, q_rows), cat(l_ref, q_starts, q_rows),
                           cat(acc_ref, q_starts, q_rows)))
        results = []
        for (q_starts, k_starts, a_q, a_k), (qb, kb, vb, m_old, l_old, acc_old) in zip(blocks, loaded):
            s = lax.dot_general(qb, kb, (((1,), (1,)), ((), ())), preferred_element_type=F32)
            off = DIL * (a_q - a_k)
            pat = pat_ref[...]
            s = jnp.where((pat >= -off) & (pat <= window - off), s, NEG_BIG)
            m_new = jnp.maximum(m_old, jnp.max(s, axis=1, keepdims=True))
            alpha = jnp.exp(m_old - m_new)
            p = jnp.exp(s - m_new[:, :1])
            l_new = alpha * l_old + jnp.sum(p, axis=1, keepdims=True)
            acc_new = alpha * acc_old + jnp.dot(p.astype(BF16), vb, preferred_element_type=F32)
            results.append((m_new, l_new, acc_new))
        for (q_starts, *_), (m_new, l_new, acc_new) in zip(blocks, results):
            for c, st in enumerate(q_starts):
                st = pl.multiple_of(st, 16)
                rs = slice(c * q_rows, (c + 1) * q_rows)
                m_ref[pl.ds(st, q_rows), :] = m_new[rs]
                l_ref[pl.ds(st, q_rows), :] = l_new[rs]
                acc_ref[pl.ds(st, q_rows), :] = acc_new[rs]

    w1 = DILATED_GROUPS[0][0]
    u1 = 2
    pat1_ref[...] = delta_pattern(DIL, 16, 32, 1)

    def body1(it, carry):
        blocks = []
        for j in range(u1):
            a0 = (it * u1 + j) * 16
            ak = jnp.maximum(a0 - 16, 0)
            blocks.append(([r * na + a0 for r in range(DIL)], [r * na + ak for r in range(DIL)], a0, ak))
        attend(blocks, 16, 32, w1, pat1_ref)
        return carry

    lax.fori_loop(0, na // (16 * u1), body1, 0)

    w4 = DILATED_GROUPS[1][0]
    pat4_ref[...] = delta_pattern(4, 32, 64, 4)

    def body4(it, carry):
        a0 = it * 32
        ak = jnp.maximum(a0 - 32, 0)
        blocks = [([(rho + 4 * sg) * na + a0 for sg in range(4)], [(rho + 4 * sg) * na + ak for sg in range(4)],
                   a0, ak) for rho in range(4)]
        attend(blocks, 32, 64, w4, pat4_ref)
        return carry

    lax.fori_loop(0, na // 32, body4, 0)

    w16 = DILATED_GROUPS[2][0]
    u16 = 4
    pat16_ref[...] = delta_pattern(1, 128, 256, 0)

    def body16(it, carry):
        rg = it // (na // 128)
        a0 = (it - rg * (na // 128)) * 128
        ak = jnp.maximum(a0 - 128, 0)
        blocks = [([(rg * u16 + j) * na + a0], [(rg * u16 + j) * na + ak], a0, ak) for j in range(u16)]
        attend(blocks, 128, 256, w16, pat16_ref)
        return carry

    lax.fori_loop(0, (DIL // u16) * (na // 128), body16, 0)

    o_ref[...] = (acc_ref[...] / l_ref[...]).astype(o_ref.dtype)


def _dilated_attention(qkv, d_model):
    T = qkv.shape[0]
    dh = d_model // ATT_HEADS
    assert dh == LANES
    na = T // DIL
    return pl.pallas_call(
        functools.partial(_attn_kernel, na=na),
        name="dilated_attn",
        grid=(ATT_HEADS,),
        in_specs=[pl.BlockSpec((T, dh), lambda h: (0, h)),
                  pl.BlockSpec((T, dh), lambda h: (0, ATT_HEADS + h)),
                  pl.BlockSpec((T, dh), lambda h: (0, 2 * ATT_HEADS + h))],
        out_specs=pl.BlockSpec((T, dh), lambda h: (0, h)),
        out_shape=jax.ShapeDtypeStruct((T, d_model), BF16),
        scratch_shapes=[pltpu.VMEM((T, dh), F32), pltpu.VMEM((T, dh), F32), pltpu.VMEM((T, dh), F32),
                        pltpu.VMEM((256, 512), I32), pltpu.VMEM((128, 256), I32), pltpu.VMEM((128, 256), I32)],
        compiler_params=_cparams(("parallel",)),
    )(qkv, qkv, qkv)


def _route_kernel(x_ref, w_ref, b_ref, idx_ref, gate_ref, rank_ref, cnt_ref, carry_ref):
    @pl.when(pl.program_id(0) == 0)
    def _():
        carry_ref[...] = jnp.zeros_like(carry_ref)

    tb = x_ref.shape[0]
    lg = jnp.dot(x_ref[...], w_ref[...], preferred_element_type=F32, precision=HIGHEST) + b_ref[...]
    lane = lax.broadcasted_iota(I32, lg.shape, 1)
    vals, hots = [], []
    idx_out = jnp.zeros(lg.shape, I32)
    for k in range(TOP_K):
        m = jnp.max(lg, axis=1, keepdims=True)
        sel = jnp.min(jnp.where(lg == m, lane, LANES), axis=1, keepdims=True)
        hot = lane == sel
        vals.append(m)
        hots.append(hot)
        idx_out = jnp.where(lane == k, sel, idx_out)
        lg = jnp.where(hot, -jnp.inf, lg)
    ex = [jnp.exp(v - vals[0]) for v in vals]
    den = sum(ex)
    gate_out = jnp.zeros(lg.shape, F32)
    for k in range(TOP_K):
        gate_out = jnp.where(lane == k, ex[k] / den, gate_out)
    chosen = sum(h.astype(F32) for h in hots)
    ri = lax.broadcasted_iota(I32, (tb, tb), 0)
    ci = lax.broadcasted_iota(I32, (tb, tb), 1)
    before = jnp.dot((ri > ci).astype(BF16), chosen.astype(BF16), preferred_element_type=F32) + carry_ref[...]
    rank_out = jnp.zeros(lg.shape, I32)
    for k in range(TOP_K):
        rk = jnp.sum(jnp.where(hots[k], before, 0.0), axis=1, keepdims=True).astype(I32)
        rank_out = jnp.where(lane == k, rk, rank_out)
    idx_ref[...] = idx_out
    gate_ref[...] = gate_out
    rank_ref[...] = rank_out
    carry_ref[...] = carry_ref[...] + jnp.sum(chosen, axis=0, keepdims=True)
    cnt_ref[...] = carry_ref[...].astype(I32)


def _route(x, w_router, b_router, *, tb=512):
    T, D = x.shape
    E = w_router.shape[1]
    wr = jnp.zeros((D, LANES), F32).at[:, :E].set(w_router.astype(F32))
    br = jnp.full((1, LANES), NEG_BIG, F32).at[0, :E].set(b_router.astype(F32))
    row = lambda dt: jax.ShapeDtypeStruct((T, LANES), dt)
    idx, gate, rank, cnt = pl.pallas_call(
        _route_kernel,
        name="moe_route",
        grid=(T // tb,),
        in_specs=[pl.BlockSpec((tb, D), lambda i: (i, 0)),
                  pl.BlockSpec((D, LANES), lambda i: (0, 0)),
                  pl.BlockSpec((1, LANES), lambda i: (0, 0))],
        out_specs=[pl.BlockSpec((tb, LANES), lambda i: (i, 0)),
                   pl.BlockSpec((tb, LANES), lambda i: (i, 0)),
                   pl.BlockSpec((tb, LANES), lambda i: (i, 0)),
                   pl.BlockSpec((1, LANES), lambda i: (0, 0))],
        out_shape=[row(I32), row(F32), row(I32), jax.ShapeDtypeStruct((1, LANES), I32)],
        scratch_shapes=[pltpu.VMEM((1, LANES), F32)],
        compiler_params=_cparams(("arbitrary",)),
    )(x, wr, br)
    return idx[:, :TOP_K], gate[:, :TOP_K], rank[:, :TOP_K], cnt[0, :E]


def _zero_tail_kernel(row_ref, o_ref, z_ref, sem):
    z_ref[...] = jnp.zeros_like(z_ref)
    n = row_ref.shape[0]

    def copy(e):
        return pltpu.make_async_copy(z_ref, o_ref.at[pl.ds(pl.multiple_of(row_ref[e], MOE_BLOCK), MOE_BLOCK)], sem)

    def start(e, c):
        copy(e).start()
        return c

    def wait(e, c):
        copy(e).wait()
        return c

    lax.fori_loop(0, n, start, 0)
    lax.fori_loop(0, n, wait, 0)


def _zero_tails(tail_rows, n_rows, d):
    return pl.pallas_call(
        _zero_tail_kernel,
        name="moe_zero_tails",
        grid_spec=pltpu.PrefetchScalarGridSpec(
            num_scalar_prefetch=1, grid=(1,),
            in_specs=[],
            out_specs=pl.BlockSpec(memory_space=pl.ANY),
            scratch_shapes=[pltpu.VMEM((MOE_BLOCK, d), F32), pltpu.SemaphoreType.DMA(())]),
        out_shape=jax.ShapeDtypeStruct((n_rows, d), F32),
        compiler_params=_cparams(("arbitrary",)),
    )(tail_rows)


def _dispatch_kernel(dest_ref, x_ref, xs_in_ref, xs_ref, sem):
    del xs_in_ref
    tb = x_ref.shape[0]
    base = pl.program_id(0) * tb * TOP_K

    def copy(i, k):
        return pltpu.make_async_copy(x_ref.at[pl.ds(i, 1)],
                                     xs_ref.at[pl.ds(dest_ref[base + i * TOP_K + k], 1)], sem)

    def start(i, c):
        for k in range(TOP_K):
            copy(i, k).start()
        return c

    def wait(i, c):
        for k in range(TOP_K):
            copy(i, k).wait()
        return c

    lax.fori_loop(0, tb, start, 0)
    lax.fori_loop(0, tb, wait, 0)


def _dispatch(x, dest_flat, xs_init, *, tb=256):
    T, D = x.shape
    return pl.pallas_call(
        _dispatch_kernel,
        name="moe_dispatch",
        grid_spec=pltpu.PrefetchScalarGridSpec(
            num_scalar_prefetch=1, grid=(T // tb,),
            in_specs=[pl.BlockSpec((tb, D), lambda i, dest: (i, 0)),
                      pl.BlockSpec(memory_space=pl.ANY)],
            out_specs=pl.BlockSpec(memory_space=pl.ANY),
            scratch_shapes=[pltpu.SemaphoreType.DMA(())]),
        out_shape=jax.ShapeDtypeStruct(xs_init.shape, xs_init.dtype),
        input_output_aliases={2: 0},
        compiler_params=_cparams(("arbitrary",)),
    )(dest_flat, x, xs_init)


def _moe_kernel(item_e_ref, item_row_ref, item_nb_ref, xs_ref, wg_ref, wu_ref, bg_ref, bu_ref, wd_ref, bd_ref,
                ys_ref, xbuf_ref, acc_ref, wgb_ref, wub_ref, wdb_ref, in_sem, out_sem, *, n_f):
    it = pl.program_id(0)
    f = pl.program_id(1)
    nb = item_nb_ref[it]
    row0 = item_row_ref[it]
    blk = MOE_BLOCK

    def acc_rows(j):
        return pl.ds(pl.multiple_of(j * blk, blk), blk)

    def in_copy(j):
        src = xs_ref.at[pl.ds(pl.multiple_of(row0 + j * blk, blk), blk)]
        return pltpu.make_async_copy(src, acc_ref.at[acc_rows(j)], in_sem)

    def out_copy(j):
        dst = ys_ref.at[pl.ds(pl.multiple_of(row0 + j * blk, blk), blk)]
        return pltpu.make_async_copy(acc_ref.at[acc_rows(j)], dst, out_sem)

    def for_blocks(fn):
        def body(j, c):
            fn(j)
            return c

        lax.fori_loop(0, nb, body, 0)

    @pl.when((f == 0) & (nb > 0))
    def _():
        for_blocks(lambda j: in_copy(j).start())
        for_blocks(lambda j: in_copy(j).wait())

        def to_bf16(j):
            xbuf_ref[acc_rows(j), :] = acc_ref[acc_rows(j), :].astype(BF16)
            acc_ref[acc_rows(j), :] = jnp.broadcast_to(bd_ref[...], (blk, acc_ref.shape[1]))

        for_blocks(to_bf16)

    @pl.when(nb > 0)
    def _():
        def compute(r0, n_rows, wg, wu, wd):
            rows = pl.ds(pl.multiple_of(r0, blk), n_rows)
            xb = xbuf_ref[rows, :]
            hg = jnp.dot(xb, wg, preferred_element_type=F32) + bg_ref[...]
            hu = jnp.dot(xb, wu, preferred_element_type=F32) + bu_ref[...]
            gate = jnp.minimum(hg, SWIGLU_LIMIT)
            up = jnp.clip(hu, -SWIGLU_LIMIT, SWIGLU_LIMIT)
            act = (up + 1.0) * (gate * jax.nn.sigmoid(SWIGLU_ALPHA * gate))
            acc_ref[rows, :] += jnp.dot(act.astype(BF16), wd, preferred_element_type=F32)

        def cast_weights():
            wg = wg_ref[...].astype(BF16)
            wu = wu_ref[...].astype(BF16)
            wd = wd_ref[...].astype(BF16)
            wgb_ref[...] = wg
            wub_ref[...] = wu
            wdb_ref[...] = wd
            return wg, wu, wd

        def compute_cached(r0, n_rows):
            compute(r0, n_rows, wgb_ref[...], wub_ref[...], wdb_ref[...])

        n4 = nb // 4

        @pl.when(n4 > 0)
        def _():
            compute(0, 4 * blk, *cast_weights())

            def quad(j, c):
                compute_cached(j * (4 * blk), 4 * blk)
                return c

            lax.fori_loop(1, n4, quad, 0)

        @pl.when(n4 == 0)
        def _():
            cast_weights()

        @pl.when((nb & 2) != 0)
        def _():
            compute_cached(n4 * (4 * blk), 2 * blk)

        @pl.when((nb & 1) != 0)
        def _():
            compute_cached((nb - 1) * blk, blk)

    @pl.when((f == n_f - 1) & (nb > 0))
    def _():
        for_blocks(lambda j: out_copy(j).start())
        for_blocks(lambda j: out_copy(j).wait())


def _moe_experts(xs, items, w_gu, b_gu, w_down, b_down, layer, *, tf=512):
    item_e, item_row, item_nb = items
    n_items = item_e.shape[0]
    P, D = xs.shape
    L, E, _, F2 = w_gu.shape
    F = F2 // 2
    n_f = F // tf
    rc = MOE_CHUNK_BLOCKS * MOE_BLOCK

    def fe(f, nb_ref, it):
        return jnp.where(nb_ref[it] > 0, f, n_f - 1)

    in_specs = [
        pl.BlockSpec(memory_space=pl.ANY),
        pl.BlockSpec((None, None, D, tf), lambda it, f, e, r, nb: (layer, e[it], 0, fe(f, nb, it))),
        pl.BlockSpec((None, None, D, tf), lambda it, f, e, r, nb: (layer, e[it], 0, n_f + fe(f, nb, it))),
        pl.BlockSpec((None, None, 1, tf), lambda it, f, e, r, nb: (layer, e[it], 0, fe(f, nb, it))),
        pl.BlockSpec((None, None, 1, tf), lambda it, f, e, r, nb: (layer, e[it], 0, n_f + fe(f, nb, it))),
        pl.BlockSpec((None, None, tf, D), lambda it, f, e, r, nb: (layer, e[it], fe(f, nb, it), 0)),
        pl.BlockSpec((None, None, 1, D), lambda it, f, e, r, nb: (layer, e[it], 0, 0)),
    ]
    return pl.pallas_call(
        functools.partial(_moe_kernel, n_f=n_f),
        name="moe_experts",
        grid_spec=pltpu.PrefetchScalarGridSpec(
            num_scalar_prefetch=3, grid=(n_items, n_f),
            in_specs=in_specs,
            out_specs=pl.BlockSpec(memory_space=pl.ANY),
            scratch_shapes=[pltpu.VMEM((rc, D), BF16),
                            pltpu.VMEM((rc, D), F32),
                            pltpu.VMEM((D, tf), BF16),
                            pltpu.VMEM((D, tf), BF16),
                            pltpu.VMEM((tf, D), BF16),
                            pltpu.SemaphoreType.DMA(()),
                            pltpu.SemaphoreType.DMA(())]),
        out_shape=jax.ShapeDtypeStruct((P, D), F32),
        compiler_params=_cparams(("arbitrary", "arbitrary")),
    )(item_e, item_row, item_nb, xs, w_gu, w_gu, b_gu.reshape(L, E, 1, F2), b_gu.reshape(L, E, 1, F2),
      w_down, b_down.reshape(L, E, 1, D))


def _combine_kernel(dest_ref, ys_ref, gate_ref, x_ref, g_ref, b_ref, o_ref, buf_ref, sem, *, alpha):
    tb = x_ref.shape[0]
    base = pl.program_id(0) * tb * TOP_K

    def copy(i, k):
        return pltpu.make_async_copy(ys_ref.at[pl.ds(dest_ref[base + i * TOP_K + k], 1)],
                                     buf_ref.at[k, pl.ds(i, 1)], sem)

    def start(i, c):
        for k in range(TOP_K):
            copy(i, k).start()
        return c

    def wait(i, c):
        for k in range(TOP_K):
            copy(i, k).wait()
        return c

    lax.fori_loop(0, tb, start, 0)
    lax.fori_loop(0, tb, wait, 0)
    gate = gate_ref[...]
    ffn = gate[:, 0:1] * buf_ref[0]
    for k in range(1, TOP_K):
        ffn = ffn + gate[:, k:k + 1] * buf_ref[k]
    o_ref[...] = _layer_norm_rows(alpha * x_ref[...] + ffn, g_ref[...], b_ref[...])


def _combine_ln(ys, dest_flat, gates, x, g, b, *, alpha, tb=128):
    T, D = x.shape
    gate_pad = jnp.zeros((T, LANES), F32).at[:, :TOP_K].set(gates)
    return pl.pallas_call(
        functools.partial(_combine_kernel, alpha=alpha),
        name="moe_combine",
        grid_spec=pltpu.PrefetchScalarGridSpec(
            num_scalar_prefetch=1, grid=(T // tb,),
            in_specs=[pl.BlockSpec(memory_space=pl.ANY),
                      pl.BlockSpec((tb, LANES), lambda i, dest: (i, 0)),
                      pl.BlockSpec((tb, D), lambda i, dest: (i, 0)),
                      pl.BlockSpec((1, D), lambda i, dest: (0, 0)),
                      pl.BlockSpec((1, D), lambda i, dest: (0, 0))],
            out_specs=pl.BlockSpec((tb, D), lambda i, dest: (i, 0)),
            scratch_shapes=[pltpu.VMEM((TOP_K, tb, D), F32), pltpu.SemaphoreType.DMA(())]),
        out_shape=jax.ShapeDtypeStruct((T, D), F32),
        compiler_params=_cparams(("arbitrary",)),
    )(dest_flat, ys, gate_pad, x, g.reshape(1, D), b.reshape(1, D))


def _moe_layout(idx, rank, counts, n_assign):
    E = counts.shape[0]
    blk = MOE_BLOCK
    nblk = (counts + blk - 1) // blk
    bend = jnp.cumsum(nblk)
    bstart = bend - nblk
    dest = (bstart * blk)[idx] + rank
    tail = jnp.where(nblk > 0, (bend - 1) * blk, 0).astype(I32)
    cb = MOE_CHUNK_BLOCKS
    n_items_max = E + (n_assign // blk + E) // cb + 1
    per_e = (nblk + cb - 1) // cb
    iend = jnp.cumsum(per_e)
    istart = iend - per_e
    ids = jnp.arange(n_items_max, dtype=I32)
    e_of = jnp.minimum(jnp.searchsorted(iend, ids, side='right'), E - 1).astype(I32)
    valid = ids < iend[-1]
    last_e = jnp.max(jnp.where(per_e > 0, jnp.arange(E, dtype=I32), 0))
    e_of = jnp.where(valid, e_of, last_e)
    local = ids - istart[e_of]
    first_blk = bstart[e_of] + local * cb
    nb = jnp.where(valid, jnp.minimum(cb, nblk[e_of] - local * cb), 0)
    row = jnp.where(valid, first_blk * blk, 0)
    return dest.astype(I32), tail, (e_of, row.astype(I32), nb.astype(I32))


def _moe(x, w_router, b_router, w_gu, b_gu, w_down, b_down, ln_g, ln_b, *, alpha, layer):
    T, D = x.shape
    E = w_router.shape[1]
    idx, gates, rank, counts = _route(x, w_router, b_router)
    n_assign = T * TOP_K
    dest, tail, items = _moe_layout(idx, rank, counts, n_assign)
    dest_flat = dest.reshape(n_assign)
    n_rows = n_assign + E * MOE_BLOCK
    xs = _dispatch(x, dest_flat, _zero_tails(tail, n_rows, D))
    ys = _moe_experts(xs, items, w_gu, b_gu, w_down, b_down, layer)
    return _combine_ln(ys, dest_flat, gates, x, ln_g, ln_b, alpha=alpha)


def _even_mixer_ln(x, w_in, lam_re, lam_im, log_step, b_re, b_im, c_re, c_im, d_skip, w_glu, b_glu,
                   w_gate2, b_gate2, norm_g, w_out, ln_g, ln_b, *, alpha):
    T, D = x.shape
    W = d_skip.shape[0]
    qk = W // 2
    rank = w_gate2.shape[0]
    s4 = W + 2 * qk + W
    w_main = jnp.concatenate([w_in[:, :s4], w_in[:, s4 + rank:]], axis=1).astype(BF16)
    w_gate = jnp.zeros((D, LANES), BF16).at[:, :rank].set(w_in[:, s4:s4 + rank].astype(BF16))
    h, g_low = _proj_in(x, w_main, w_gate)
    tables = _s5_tables(lam_re, lam_im, log_step, b_re, b_im, c_re, c_im, T // S5_CHUNK)
    y = _s5_core(h[:, :W], tables)
    ya = _s5_glu(y, h, d_skip.astype(F32), w_glu.astype(BF16), b_glu.astype(F32))
    yb = _gla(h, g_low, w_gate2, b_gate2, norm_g, width=W)
    w_out_b = w_out.astype(BF16)
    return _proj_ln([ya, yb], [w_out_b[:W], w_out_b[W:]], x, ln_g, ln_b, alpha=alpha, permuted=False)


def _odd_mixer_ln(x, w_qkv, w_o, ln_g, ln_b, *, alpha):
    T, D = x.shape
    qkv = _proj_perm(x, w_qkv.astype(BF16), scaled_cols=D, scale=(D // ATT_HEADS) ** -0.5)
    y = _dilated_attention(qkv, D)
    return _proj_ln([y], [w_o.astype(BF16)], x, ln_g, ln_b, alpha=alpha, permuted=True)


def kernel(x, ab_w_in, s5_lam_re, s5_lam_im, s5_log_step, s5_b_re, s5_b_im, s5_c_re, s5_c_im, s5_d, s5_w_glu, s5_b_glu, gla_w_gate2, gla_b_gate2, gla_norm_g, ab_w_out, c_w_qkv, c_w_o, ln1_g, ln1_b, moe_w_router, moe_b_router, moe_w_gu, moe_b_gu, moe_w_down, moe_b_down, ln2_g, ln2_b):
    bsz, L, D = x.shape
    depth = ln1_g.shape[0]
    alpha = (2 * depth) ** 0.25
    outs = []
    for bi in range(bsz):
        xt = x[bi].astype(F32)
        for layer in range(depth):
            i = layer // 2
            if layer % 2 == 0:
                xt = _even_mixer_ln(xt, ab_w_in[i], s5_lam_re[i], s5_lam_im[i], s5_log_step[i], s5_b_re[i],
                                    s5_b_im[i], s5_c_re[i], s5_c_im[i], s5_d[i], s5_w_glu[i], s5_b_glu[i],
                                    gla_w_gate2[i], gla_b_gate2[i], gla_norm_g[i], ab_w_out[i],
                                    ln1_g[layer], ln1_b[layer], alpha=alpha)
            else:
                xt = _odd_mixer_ln(xt, c_w_qkv[i], c_w_o[i], ln1_g[layer], ln1_b[layer], alpha=alpha)
            xt = _moe(xt, moe_w_router[layer], moe_b_router[layer], moe_w_gu, moe_b_gu,
                      moe_w_down, moe_b_down, ln2_g[layer], ln2_b[layer], alpha=alpha, layer=layer)
        outs.append(xt)
    return jnp.stack(outs).astype(x.dtype)
```

```python
import functools
import math

import jax
import jax.numpy as jnp
from jax import lax
from jax.experimental import pallas as pl
from jax.experimental.pallas import tpu as pltpu

F32 = jnp.float32
BF16 = jnp.bfloat16
I32 = jnp.int32
HIGHEST = lax.Precision.HIGHEST

LANES = 128
VMEM_LIMIT_BYTES = 56 * 1024 * 1024

S5_GROUP = 16
S5_STATE = 64
S5_MAX_RE = -1e-4
S5_CHUNK = 16
GLA_HEADS = 4
GLA_GATE_TEMP = 16.0
GLA_EPS = 1e-6
GLA_BLOCK = 64
GLA_SUB = 16
ATT_HEADS = 16
DIL = 16
DILATED_GROUPS = ((128, 1), (512, 4), (2048, 16))
TOP_K = 4
SWIGLU_LIMIT = 7.0
SWIGLU_ALPHA = 1.702
MOE_BLOCK = 128
MOE_CHUNK_BLOCKS = 10
LN_EPS = 1e-5
NEG_BIG = -1e30


def _cparams(semantics):
    return pltpu.CompilerParams(dimension_semantics=semantics, vmem_limit_bytes=VMEM_LIMIT_BYTES)


def _layer_norm_rows(z, g, b):
    mu = jnp.mean(z, axis=-1, keepdims=True)
    zc = z - mu
    var = jnp.mean(zc * zc, axis=-1, keepdims=True)
    return zc * lax.rsqrt(var + LN_EPS) * g + b


def _proj_in_kernel(x_ref, w_ref, wg_ref, h_ref, g_ref, xb_ref):
    @pl.when(pl.program_id(1) == 0)
    def _():
        xb = x_ref[...].astype(BF16)
        xb_ref[...] = xb
        g_ref[...] = jnp.dot(xb, wg_ref[...], preferred_element_type=F32)

    h_ref[...] = jnp.dot(xb_ref[...], w_ref[...], preferred_element_type=F32)


def _proj_in(x, w_main, w_gate, *, tm=1024, tn=512):
    T, D = x.shape
    N = w_main.shape[1]
    return pl.pallas_call(
        _proj_in_kernel,
        name="proj_in",
        grid=(T // tm, N // tn),
        in_specs=[pl.BlockSpec((tm, D), lambda i, j: (i, 0)),
                  pl.BlockSpec((D, tn), lambda i, j: (0, j)),
                  pl.BlockSpec((D, LANES), lambda i, j: (0, 0))],
        out_specs=[pl.BlockSpec((tm, tn), lambda i, j: (i, j)),
                   pl.BlockSpec((tm, LANES), lambda i, j: (i, 0))],
        out_shape=[jax.ShapeDtypeStruct((T, N), F32), jax.ShapeDtypeStruct((T, LANES), F32)],
        scratch_shapes=[pltpu.VMEM((tm, D), BF16)],
        compiler_params=_cparams(("parallel", "arbitrary")),
    )(x, w_main, w_gate)


def _proj_perm_kernel(x0_ref, x1_ref, w_ref, o_ref, xb_ref, *, n_scaled, scale):
    j = pl.program_id(1)
    na = x0_ref.shape[0]

    @pl.when(j == 0)
    def _():
        xb_ref[:na, :] = x0_ref[...].astype(BF16)
        xb_ref[na:, :] = x1_ref[...].astype(BF16)

    y = jnp.dot(xb_ref[...], w_ref[...], preferred_element_type=F32)
    o_ref[...] = (y * jnp.where(j < n_scaled, scale, 1.0)).astype(o_ref.dtype)


def _proj_perm(x, w, *, scaled_cols, scale, tn=1024):
    T, D = x.shape
    N = w.shape[1]
    na = T // DIL
    x3 = x.reshape(na, DIL * D)
    return pl.pallas_call(
        functools.partial(_proj_perm_kernel, n_scaled=scaled_cols // tn, scale=scale),
        name="proj_perm",
        grid=(DIL // 2, N // tn),
        in_specs=[pl.BlockSpec((na, D), lambda r, j: (0, 2 * r)),
                  pl.BlockSpec((na, D), lambda r, j: (0, 2 * r + 1)),
                  pl.BlockSpec((D, tn), lambda r, j: (0, j))],
        out_specs=pl.BlockSpec((2 * na, tn), lambda r, j: (r, j)),
        out_shape=jax.ShapeDtypeStruct((T, N), BF16),
        scratch_shapes=[pltpu.VMEM((2 * na, D), BF16)],
        compiler_params=_cparams(("parallel", "arbitrary")),
    )(x3, x3, w)


def _proj_ln_kernel(*refs, n_lhs, alpha, n_tiles):
    lhs_refs = refs[:n_lhs]
    w_refs = refs[n_lhs:2 * n_lhs]
    res_ref, g_ref, b_ref, o_ref, acc_ref = refs[2 * n_lhs:]
    j = pl.program_id(1)
    y = jnp.dot(lhs_refs[0][...], w_refs[0][...], preferred_element_type=F32)
    for a_ref, w_ref in zip(lhs_refs[1:], w_refs[1:]):
        y = y + jnp.dot(a_ref[...], w_ref[...], preferred_element_type=F32)
    acc_ref[j] = y

    @pl.when(j == n_tiles - 1)
    def _():
        tn = acc_ref.shape[2]
        z = [alpha * res_ref[:, t * tn:(t + 1) * tn] + acc_ref[t] for t in range(n_tiles)]
        n = float(n_tiles * tn)
        mu = sum(jnp.sum(zt, axis=-1, keepdims=True) for zt in z) / n
        zc = [zt - mu for zt in z]
        var = sum(jnp.sum(zt * zt, axis=-1, keepdims=True) for zt in zc) / n
        rstd = lax.rsqrt(var + LN_EPS)
        for t in range(n_tiles):
            sl = slice(t * tn, (t + 1) * tn)
            o_ref[:, sl] = zc[t] * rstd * g_ref[:, sl] + b_ref[:, sl]


def _proj_ln(lhs, ws, res, g, b, *, alpha, permuted, tm=512, tn=512):
    T, N = res.shape
    n_lhs = len(lhs)
    n_tiles = N // tn
    if permuted:
        tm = T // DIL
        res_in = res.reshape(tm, DIL * N)
        res_spec = pl.BlockSpec((tm, N), lambda i, j: (0, i))
        out_spec = pl.BlockSpec((tm, N), lambda i, j: (0, i))
        out_shape = jax.ShapeDtypeStruct((tm, DIL * N), F32)
    else:
        res_in = res
        res_spec = pl.BlockSpec((tm, N), lambda i, j: (i, 0))
        out_spec = pl.BlockSpec((tm, N), lambda i, j: (i, 0))
        out_shape = jax.ShapeDtypeStruct((T, N), F32)
    in_specs = ([pl.BlockSpec((tm, a.shape[1]), lambda i, j: (i, 0)) for a in lhs]
                + [pl.BlockSpec((w.shape[0], tn), lambda i, j: (0, j)) for w in ws]
                + [res_spec,
                   pl.BlockSpec((1, N), lambda i, j: (0, 0)),
                   pl.BlockSpec((1, N), lambda i, j: (0, 0))])
    out = pl.pallas_call(
        functools.partial(_proj_ln_kernel, n_lhs=n_lhs, alpha=alpha, n_tiles=n_tiles),
        name="proj_ln",
        grid=(T // tm, n_tiles),
        in_specs=in_specs,
        out_specs=out_spec,
        out_shape=out_shape,
        scratch_shapes=[pltpu.VMEM((n_tiles, tm, tn), F32)],
        compiler_params=_cparams(("parallel", "arbitrary")),
    )(*lhs, *ws, res_in, g.reshape(1, N), b.reshape(1, N))
    return out.reshape(T, N)


def _s5_tables(lam_re, lam_im, log_step, b_re, b_im, c_re, c_im, n_chunks):
    C = S5_CHUNK
    G, P = lam_re.shape
    H = b_re.shape[-1]
    lr = jnp.minimum(lam_re.astype(F32), S5_MAX_RE)
    li = lam_im.astype(F32)
    dt = jnp.exp(log_step.astype(F32))[:, None]
    mag = jnp.exp(lr * dt)
    a_re = mag * jnp.cos(li * dt)
    a_im = mag * jnp.sin(li * dt)
    den = lr * lr + li * li
    nr = a_re - 1.0
    f_re = (nr * lr + a_im * li) / den
    f_im = (a_im * lr - nr * li) / den
    br = b_re.astype(F32)
    bi = b_im.astype(F32)
    bb_re = f_re[..., None] * br - f_im[..., None] * bi
    bb_im = f_re[..., None] * bi + f_im[..., None] * br
    pw_re = [jnp.ones_like(a_re)]
    pw_im = [jnp.zeros_like(a_im)]
    for _ in range(C):
        pr, pi = pw_re[-1], pw_im[-1]
        pw_re.append(pr * a_re - pi * a_im)
        pw_im.append(pr * a_im + pi * a_re)
    pw_re = jnp.stack(pw_re)
    pw_im = jnp.stack(pw_im)
    ab_re = pw_re[:C, :, :, None] * bb_re[None] - pw_im[:C, :, :, None] * bb_im[None]
    ab_im = pw_re[:C, :, :, None] * bb_im[None] + pw_im[:C, :, :, None] * bb_re[None]
    cr = c_re.astype(F32)
    ci = c_im.astype(F32)
    gpt = LANES // H
    M = G // gpt
    eye = jnp.eye(gpt, dtype=F32)
    z_re = jnp.transpose(ab_re[::-1], (1, 0, 3, 2)).reshape(M, gpt, C, H, P)
    z_im = jnp.transpose(ab_im[::-1], (1, 0, 3, 2)).reshape(M, gpt, C, H, P)
    spread_z = lambda z: jnp.einsum('mgjhp,gk->mjghkp', z, eye).reshape(M, C * gpt * H, gpt * P)
    wz = jnp.concatenate([spread_z(z_re), spread_z(z_im)], axis=-1)
    kern = (jnp.einsum('gop,kgph->kgoh', cr, ab_re, precision=HIGHEST)
            - jnp.einsum('gop,kgph->kgoh', ci, ab_im, precision=HIGHEST))
    lag = jnp.arange(C)[None, :] - jnp.arange(C)[:, None]
    kl = kern[jnp.clip(lag, 0, C - 1)]
    kl = jnp.where((lag >= 0)[:, :, None, None, None], kl, 0.0)
    mmat = jnp.transpose(kl, (2, 0, 4, 1, 3)).reshape(M, gpt, C, H, C, H)
    wy = jnp.einsum('mgjhio,gk->mjghiko', mmat, eye).reshape(M, C * gpt * H, C * gpt * H)
    ca_re = cr[None] * pw_re[1:, :, None, :] - ci[None] * pw_im[1:, :, None, :]
    ca_im = cr[None] * pw_im[1:, :, None, :] + ci[None] * pw_re[1:, :, None, :]
    n_re = jnp.transpose(ca_re, (1, 3, 0, 2)).reshape(M, gpt, P, C, H)
    n_im = jnp.transpose(-ca_im, (1, 3, 0, 2)).reshape(M, gpt, P, C, H)
    spread_n = lambda n: jnp.einsum('mgpio,gk->mgpiko', n, eye).reshape(M, gpt * P, C * gpt * H)
    wn = jnp.concatenate([spread_n(n_re), spread_n(n_im)], axis=1)
    n_steps = max(1, (n_chunks - 1).bit_length())
    qr, qi = pw_re[C].reshape(M, gpt * P), pw_im[C].reshape(M, gpt * P)
    ar, ai = [], []
    for _ in range(n_steps):
        ar.append(qr)
        ai.append(qi)
        qr, qi = qr * qr - qi * qi, 2.0 * qr * qi
    ar = jnp.stack(ar, axis=1)
    ai = jnp.stack(ai, axis=1)
    return wz.astype(BF16), wy.astype(BF16), wn.astype(BF16), ar, ai


def _s5_kernel(u_ref, wz_ref, wy_ref, wn_ref, ar_ref, ai_ref, y_ref, ub_ref, sin_ref, *, n_steps):
    nc, C, _ = u_ref.shape

    @pl.when(pl.program_id(1) == 0)
    def _():
        for j in range(C):
            ub_ref[:, j * LANES:(j + 1) * LANES] = u_ref[:, j, :].astype(BF16)
        s = jnp.dot(ub_ref[...], wz_ref[...], preferred_element_type=F32)
        q = s.shape[1] // 2
        s_re, s_im = s[:, :q], s[:, q:]
        row = lax.broadcasted_iota(I32, (nc, q), 0)
        for k in range(n_steps):
            sh = 1 << k
            p_re = jnp.where(row >= sh, pltpu.roll(s_re, sh, axis=0), 0.0)
            p_im = jnp.where(row >= sh, pltpu.roll(s_im, sh, axis=0), 0.0)
            a_re, a_im = ar_ref[k:k + 1, :], ai_ref[k:k + 1, :]
            s_re, s_im = s_re + a_re * p_re - a_im * p_im, s_im + a_re * p_im + a_im * p_re
        sin_ref[:, :q] = jnp.where(row >= 1, pltpu.roll(s_re, 1, axis=0), 0.0).astype(BF16)
        sin_ref[:, q:] = jnp.where(row >= 1, pltpu.roll(s_im, 1, axis=0), 0.0).astype(BF16)

    y = (jnp.dot(ub_ref[...], wy_ref[...], preferred_element_type=F32)
         + jnp.dot(sin_ref[...], wn_ref[...], preferred_element_type=F32))
    for i in range(y_ref.shape[1]):
        y_ref[:, i, :] = y[:, i * LANES:(i + 1) * LANES]


def _s5_core(h, width, tables):
    wz, wy, wn, ar, ai = tables
    T, hw = h.shape
    M, CL, Q2 = wz.shape
    C = S5_CHUNK
    nc = T // C
    n_steps = ar.shape[1]
    assert CL == C * LANES and width == M * LANES
    halves = 2
    ch, clh = C // halves, CL // halves
    y = pl.pallas_call(
        functools.partial(_s5_kernel, n_steps=n_steps),
        name="s5_core",
        grid=(M, halves),
        in_specs=[pl.BlockSpec((nc, C, LANES), lambda m, p: (0, 0, m)),
                  pl.BlockSpec((None, CL, Q2), lambda m, p: (m, 0, 0)),
                  pl.BlockSpec((None, CL, clh), lambda m, p: (m, 0, p)),
                  pl.BlockSpec((None, Q2, clh), lambda m, p: (m, 0, p)),
                  pl.BlockSpec((None, n_steps, Q2 // 2), lambda m, p: (m, 0, 0)),
                  pl.BlockSpec((None, n_steps, Q2 // 2), lambda m, p: (m, 0, 0))],
        out_specs=pl.BlockSpec((nc, ch, LANES), lambda m, p: (0, p, m)),
        out_shape=jax.ShapeDtypeStruct((nc, C, width), F32),
        scratch_shapes=[pltpu.VMEM((nc, CL), BF16), pltpu.VMEM((nc, Q2), BF16)],
        compiler_params=_cparams(("parallel", "arbitrary")),
    )(h.reshape(nc, C, hw), wz, wy, wn, ar, ai)
    return y.reshape(T, width)


def _s5_glu_kernel(y_ref, u_ref, d_ref, w_ref, b_ref, o_ref):
    y = y_ref[...] + d_ref[...] * u_ref[...]
    c0 = math.sqrt(2.0 / math.pi)
    z = 0.5 * y * (1.0 + jnp.tanh(c0 * (y + 0.044715 * (y * y * y))))
    lin = jnp.dot(z.astype(BF16), w_ref[...], preferred_element_type=F32) + b_ref[...]
    o_ref[...] = (z * jax.nn.sigmoid(lin)).astype(o_ref.dtype)


def _s5_glu(y, h, d_skip, w_glu, b_glu, *, tm=512):
    T, W = y.shape
    return pl.pallas_call(
        _s5_glu_kernel,
        name="s5_glu",
        grid=(T // tm,),
        in_specs=[pl.BlockSpec((tm, W), lambda i: (i, 0)),
                  pl.BlockSpec((tm, W), lambda i: (i, 0)),
                  pl.BlockSpec((1, W), lambda i: (0, 0)),
                  pl.BlockSpec((W, W), lambda i: (0, 0)),
                  pl.BlockSpec((1, W), lambda i: (0, 0))],
        out_specs=pl.BlockSpec((tm, W), lambda i: (i, 0)),
        out_shape=jax.ShapeDtypeStruct((T, W), BF16),
        compiler_params=_cparams(("parallel",)),
    )(y, h, d_skip.reshape(1, W), w_glu, b_glu.reshape(1, W))


def _gla_kernel(q_ref, k_ref, v_ref, r_ref, g_ref, w2_ref, b2_ref, ng_ref, o_ref, st_ref, *, dk, dv):
    @pl.when(pl.program_id(0) == 0)
    def _():
        st_ref[...] = jnp.zeros_like(st_ref)

    cb = q_ref.shape[0]
    n_sub = cb // GLA_SUB
    scale = dk ** -0.5
    logit = jnp.dot(g_ref[...], w2_ref[...], preferred_element_type=F32, precision=HIGHEST) + b2_ref[...]
    log_a = (jnp.minimum(logit, 0.0) - jnp.log(1.0 + jnp.exp(-jnp.abs(logit)))) / GLA_GATE_TEMP
    ri = lax.broadcasted_iota(I32, (cb, cb), 0)
    ci = lax.broadcasted_iota(I32, (cb, cb), 1)
    tri = (ri >= ci).astype(F32)
    bcum = jnp.dot(tri, log_a, preferred_element_type=F32, precision=HIGHEST)
    for hh in range(GLA_HEADS):
        ks = slice(hh * dk, (hh + 1) * dk)
        vs = slice(hh * dv, (hh + 1) * dv)
        b = bcum[:, ks]
        q = q_ref[:, ks] * scale
        k = k_ref[:, ks]
        v = v_ref[:, vs].astype(BF16)
        refs = [jnp.zeros((1, dk), F32)] + [b[a * GLA_SUB - 1:a * GLA_SUB, :] for a in range(1, n_sub)]
        refmat = jnp.concatenate([jnp.broadcast_to(r, (GLA_SUB, dk)) for r in refs], axis=0)
        qe = (q * jnp.exp(b - refmat)).astype(BF16)
        st = st_ref[hh]
        o_inter = lax.dot_general((q * jnp.exp(b)).astype(BF16), st.astype(BF16),
                                  (((1,), (1,)), ((), ())), preferred_element_type=F32)
        o_rows = []
        for a in range(n_sub):
            hi = (a + 1) * GLA_SUB
            ke = (k[:hi] * jnp.exp(refs[a] - b[:hi])).astype(BF16)
            att = lax.dot_general(qe[a * GLA_SUB:hi], ke, (((1,), (1,)), ((), ())),
                                  preferred_element_type=F32)
            row_a = lax.broadcasted_iota(I32, (GLA_SUB, hi), 0) + a * GLA_SUB
            att = jnp.where(lax.broadcasted_iota(I32, (GLA_SUB, hi), 1) <= row_a, att, 0.0)
            o_rows.append(jnp.dot(att.astype(BF16), v[:hi], preferred_element_type=F32))
        o = jnp.concatenate(o_rows, axis=0) + o_inter
        o = o * lax.rsqrt(jnp.mean(o * o, axis=-1, keepdims=True) + GLA_EPS) * ng_ref[...]
        r = r_ref[:, vs]
        o_ref[:, vs] = (o * (r * jax.nn.sigmoid(r))).astype(o_ref.dtype)
        b_last = b[cb - 1:cb, :]
        kd = (k * jnp.exp(b_last - b)).astype(BF16)
        upd = lax.dot_general(v, kd, (((0,), (0,)), ((), ())), preferred_element_type=F32)
        st_ref[hh] = st * jnp.exp(b_last) + upd


def _gla(h, g_low, w_gate2, b_gate2, norm_g, *, width):
    T = h.shape[0]
    qk = width // 2
    dk = qk // GLA_HEADS
    dv = width // GLA_HEADS
    cb = GLA_BLOCK
    w2 = jnp.zeros((LANES, qk), F32).at[:w_gate2.shape[0]].set(w_gate2.astype(F32))
    return pl.pallas_call(
        functools.partial(_gla_kernel, dk=dk, dv=dv),
        name="gla",
        grid=(T // cb,),
        in_specs=[pl.BlockSpec((cb, qk), lambda i: (i, 2)),
                  pl.BlockSpec((cb, qk), lambda i: (i, 3)),
                  pl.BlockSpec((cb, width), lambda i: (i, 2)),
                  pl.BlockSpec((cb, width), lambda i: (i, 3)),
                  pl.BlockSpec((cb, LANES), lambda i: (i, 0)),
                  pl.BlockSpec((LANES, qk), lambda i: (0, 0)),
                  pl.BlockSpec((1, qk), lambda i: (0, 0)),
                  pl.BlockSpec((1, dv), lambda i: (0, 0))],
        out_specs=pl.BlockSpec((cb, width), lambda i: (i, 0)),
        out_shape=jax.ShapeDtypeStruct((T, width), BF16),
        scratch_shapes=[pltpu.VMEM((GLA_HEADS, dv, dk), F32)],
        compiler_params=_cparams(("arbitrary",)),
    )(h, h, h, h, g_low, w2, b_gate2.reshape(1, qk).astype(F32), norm_g.reshape(1, dv).astype(F32))


def _attn_kernel(q_ref, k_ref, v_ref, o_ref, acc_ref, m_ref, l_ref, v1_ref, pat1_ref, pat4_ref, pat16_ref, *, na):
    dh = v_ref.shape[1]
    v1_ref[:, :dh] = v_ref[...]
    v1_ref[:, dh:] = jnp.ones_like(v_ref)
    acc_ref[...] = jnp.zeros_like(acc_ref)
    m_ref[...] = jnp.full_like(m_ref, NEG_BIG)
    l_ref[...] = jnp.zeros_like(l_ref)

    def token_offsets(rows, r_step, n, axis):
        shape = (n, 1) if axis == 0 else (1, n)
        idx = lax.broadcasted_iota(I32, shape, axis)
        c = idx // rows
        return DIL * (idx - c * rows) + r_step * c

    def delta_pattern(n_chunks, q_rows, k_rows, r_step):
        return (token_offsets(q_rows, r_step, n_chunks * q_rows, 0)
                - token_offsets(k_rows, r_step, n_chunks * k_rows, 1))

    def cat(ref, starts, rows):
        return jnp.concatenate([ref[pl.ds(pl.multiple_of(s, 16), rows), :] for s in starts], axis=0)

    def attend(blocks, q_rows, k_rows, window, pat_ref):
        loaded = []
        for q_starts, k_starts, a_q, a_k in blocks:
            loaded.append((cat(q_ref, q_starts, q_rows), cat(k_ref, k_starts, k_rows), cat(v1_ref, k_starts, k_rows),
                           cat(m_ref, q_starts, q_rows), cat(l_ref, q_starts, q_rows),
                           cat(acc_ref, q_starts, q_rows)))
        results = []
        for (q_starts, k_starts, a_q, a_k), (qb, kb, vb, m_old, l_old, acc_old) in zip(blocks, loaded):
            s = lax.dot_general(qb, kb, (((1,), (1,)), ((), ())), preferred_element_type=F32)
            off = DIL * (a_q - a_k)
            pat = pat_ref[...]
            s = jnp.where((pat >= -off) & (pat <= window - off), s, NEG_BIG)
            m_new = jnp.maximum(m_old, jnp.max(s, axis=1, keepdims=True))
            alpha = jnp.exp(m_old - m_new)
            p = jnp.exp(s - m_new[:, :1])
            pv = jnp.dot(p.astype(BF16), vb, preferred_element_type=F32)
            l_new = alpha * l_old + pv[:, dh:]
            acc_new = alpha * acc_old + pv[:, :dh]
            results.append((m_new, l_new, acc_new))
        for (q_starts, *_), (m_new, l_new, acc_new) in zip(blocks, results):
            for c, st in enumerate(q_starts):
                st = pl.multiple_of(st, 16)
                rs = slice(c * q_rows, (c + 1) * q_rows)
                m_ref[pl.ds(st, q_rows), :] = m_new[rs]
                l_ref[pl.ds(st, q_rows), :] = l_new[rs]
                acc_ref[pl.ds(st, q_rows), :] = acc_new[rs]

    w1 = DILATED_GROUPS[0][0]
    u1 = 2
    pat1_ref[...] = delta_pattern(DIL, 16, 32, 1)

    def body1(it, carry):
        blocks = []
        for j in range(u1):
            a0 = (it * u1 + j) * 16
            ak = jnp.maximum(a0 - 16, 0)
            blocks.append(([r * na + a0 for r in range(DIL)], [r * na + ak for r in range(DIL)], a0, ak))
        attend(blocks, 16, 32, w1, pat1_ref)
        return carry

    lax.fori_loop(0, na // (16 * u1), body1, 0)

    w4 = DILATED_GROUPS[1][0]
    pat4_ref[...] = delta_pattern(4, 32, 64, 4)

    def body4(it, carry):
        a0 = it * 32
        ak = jnp.maximum(a0 - 32, 0)
        blocks = [([(rho + 4 * sg) * na + a0 for sg in range(4)], [(rho + 4 * sg) * na + ak for sg in range(4)],
                   a0, ak) for rho in range(4)]
        attend(blocks, 32, 64, w4, pat4_ref)
        return carry

    lax.fori_loop(0, na // 32, body4, 0)

    w16 = DILATED_GROUPS[2][0]
    u16 = 4
    pat16_ref[...] = delta_pattern(1, 128, 256, 0)

    def body16(it, carry):
        rg = it // (na // 128)
        a0 = (it - rg * (na // 128)) * 128
        ak = jnp.maximum(a0 - 128, 0)
        blocks = [([(rg * u16 + j) * na + a0], [(rg * u16 + j) * na + ak], a0, ak) for j in range(u16)]
        attend(blocks, 128, 256, w16, pat16_ref)
        return carry

    lax.fori_loop(0, (DIL // u16) * (na // 128), body16, 0)

    o_ref[...] = (acc_ref[...] / l_ref[...]).astype(o_ref.dtype)


def _dilated_attention(qkv, d_model):
    T = qkv.shape[0]
    dh = d_model // ATT_HEADS
    assert dh == LANES
    na = T // DIL
    return pl.pallas_call(
        functools.partial(_attn_kernel, na=na),
        name="dilated_attn",
        grid=(ATT_HEADS,),
        in_specs=[pl.BlockSpec((T, dh), lambda h: (0, h)),
                  pl.BlockSpec((T, dh), lambda h: (0, ATT_HEADS + h)),
                  pl.BlockSpec((T, dh), lambda h: (0, 2 * ATT_HEADS + h))],
        out_specs=pl.BlockSpec((T, dh), lambda h: (0, h)),
        out_shape=jax.ShapeDtypeStruct((T, d_model), BF16),
        scratch_shapes=[pltpu.VMEM((T, dh), F32), pltpu.VMEM((T, dh), F32), pltpu.VMEM((T, dh), F32),
                        pltpu.VMEM((T, 2 * dh), BF16), pltpu.VMEM((256, 512), I32), pltpu.VMEM((128, 256), I32), pltpu.VMEM((128, 256), I32)],
        compiler_params=_cparams(("parallel",)),
    )(qkv, qkv, qkv)


def _route_kernel(x_ref, w_ref, b_ref, idx_ref, gate_ref, rank_ref, cnt_ref, carry_ref):
    @pl.when(pl.program_id(0) == 0)
    def _():
        carry_ref[...] = jnp.zeros_like(carry_ref)

    tb = x_ref.shape[0]
    lg = jnp.dot(x_ref[...], w_ref[...], preferred_element_type=F32, precision=HIGHEST) + b_ref[...]
    lane = lax.broadcasted_iota(I32, lg.shape, 1)
    vals, hots = [], []
    idx_out = jnp.zeros(lg.shape, I32)
    for k in range(TOP_K):
        m = jnp.max(lg, axis=1, keepdims=True)
        sel = jnp.min(jnp.where(lg == m, lane, LANES), axis=1, keepdims=True)
        hot = lane == sel
        vals.append(m)
        hots.append(hot)
        idx_out = jnp.where(lane == k, sel, idx_out)
        lg = jnp.where(hot, -jnp.inf, lg)
    ex = [jnp.exp(v - vals[0]) for v in vals]
    den = sum(ex)
    gate_out = jnp.zeros(lg.shape, F32)
    for k in range(TOP_K):
        gate_out = jnp.where(lane == k, ex[k] / den, gate_out)
    chosen = sum(h.astype(F32) for h in hots)
    ri = lax.broadcasted_iota(I32, (tb, tb), 0)
    ci = lax.broadcasted_iota(I32, (tb, tb), 1)
    before = jnp.dot((ri > ci).astype(BF16), chosen.astype(BF16), preferred_element_type=F32) + carry_ref[...]
    rank_out = jnp.zeros(lg.shape, I32)
    for k in range(TOP_K):
        rk = jnp.sum(jnp.where(hots[k], before, 0.0), axis=1, keepdims=True).astype(I32)
        rank_out = jnp.where(lane == k, rk, rank_out)
    idx_ref[...] = idx_out
    gate_ref[...] = gate_out
    rank_ref[...] = rank_out
    carry_ref[...] = carry_ref[...] + jnp.sum(chosen, axis=0, keepdims=True)
    cnt_ref[...] = carry_ref[...].astype(I32)


def _route(x, w_router, b_router, *, tb=512):
    T, D = x.shape
    E = w_router.shape[1]
    wr = jnp.zeros((D, LANES), F32).at[:, :E].set(w_router.astype(F32))
    br = jnp.full((1, LANES), NEG_BIG, F32).at[0, :E].set(b_router.astype(F32))
    row = lambda dt: jax.ShapeDtypeStruct((T, LANES), dt)
    idx, gate, rank, cnt = pl.pallas_call(
        _route_kernel,
        name="moe_route",
        grid=(T // tb,),
        in_specs=[pl.BlockSpec((tb, D), lambda i: (i, 0)),
                  pl.BlockSpec((D, LANES), lambda i: (0, 0)),
                  pl.BlockSpec((1, LANES), lambda i: (0, 0))],
        out_specs=[pl.BlockSpec((tb, LANES), lambda i: (i, 0)),
                   pl.BlockSpec((tb, LANES), lambda i: (i, 0)),
                   pl.BlockSpec((tb, LANES), lambda i: (i, 0)),
                   pl.BlockSpec((1, LANES), lambda i: (0, 0))],
        out_shape=[row(I32), row(F32), row(I32), jax.ShapeDtypeStruct((1, LANES), I32)],
        scratch_shapes=[pltpu.VMEM((1, LANES), F32)],
        compiler_params=_cparams(("arbitrary",)),
    )(x, wr, br)
    return idx[:, :TOP_K], gate[:, :TOP_K], rank[:, :TOP_K], cnt[0, :E]


def _zero_tail_kernel(row_ref, o_ref, z_ref, sem):
    z_ref[...] = jnp.zeros_like(z_ref)
    n = row_ref.shape[0]

    def copy(e):
        return pltpu.make_async_copy(z_ref, o_ref.at[pl.ds(pl.multiple_of(row_ref[e], MOE_BLOCK), MOE_BLOCK)], sem)

    def start(e, c):
        copy(e).start()
        return c

    def wait(e, c):
        copy(e).wait()
        return c

    lax.fori_loop(0, n, start, 0)
    lax.fori_loop(0, n, wait, 0)


def _zero_tails(tail_rows, n_rows, d):
    return pl.pallas_call(
        _zero_tail_kernel,
        name="moe_zero_tails",
        grid_spec=pltpu.PrefetchScalarGridSpec(
            num_scalar_prefetch=1, grid=(1,),
            in_specs=[],
            out_specs=pl.BlockSpec(memory_space=pl.ANY),
            scratch_shapes=[pltpu.VMEM((MOE_BLOCK, d), F32), pltpu.SemaphoreType.DMA(())]),
        out_shape=jax.ShapeDtypeStruct((n_rows, d), F32),
        compiler_params=_cparams(("arbitrary",)),
    )(tail_rows)


def _dispatch_kernel(dest_ref, x_ref, xs_in_ref, xs_ref, sem):
    del xs_in_ref
    tb = x_ref.shape[0]
    base = pl.program_id(0) * tb * TOP_K

    def copy(i, k):
        return pltpu.make_async_copy(x_ref.at[pl.ds(i, 1)],
                                     xs_ref.at[pl.ds(dest_ref[base + i * TOP_K + k], 1)], sem)

    def start(i, c):
        for k in range(TOP_K):
            copy(i, k).start()
        return c

    def wait(i, c):
        for k in range(TOP_K):
            copy(i, k).wait()
        return c

    lax.fori_loop(0, tb, start, 0)
    lax.fori_loop(0, tb, wait, 0)


def _dispatch(x, dest_flat, xs_init, *, tb=256):
    T, D = x.shape
    return pl.pallas_call(
        _dispatch_kernel,
        name="moe_dispatch",
        grid_spec=pltpu.PrefetchScalarGridSpec(
            num_scalar_prefetch=1, grid=(T // tb,),
            in_specs=[pl.BlockSpec((tb, D), lambda i, dest: (i, 0)),
                      pl.BlockSpec(memory_space=pl.ANY)],
            out_specs=pl.BlockSpec(memory_space=pl.ANY),
            scratch_shapes=[pltpu.SemaphoreType.DMA(())]),
        out_shape=jax.ShapeDtypeStruct(xs_init.shape, xs_init.dtype),
        input_output_aliases={2: 0},
        compiler_params=_cparams(("arbitrary",)),
    )(dest_flat, x, xs_init)


def _moe_kernel(item_e_ref, item_row_ref, item_nb_ref, xs_ref, wg_ref, wu_ref, bg_ref, bu_ref, wd_ref, bd_ref,
                ys_ref, xbuf_ref, acc_ref, wgb_ref, wub_ref, wdb_ref, in_sem, out_sem, *, n_f):
    it = pl.program_id(0)
    f = pl.program_id(1)
    nb = item_nb_ref[it]
    row0 = item_row_ref[it]
    blk = MOE_BLOCK

    def acc_rows(j):
        return pl.ds(pl.multiple_of(j * blk, blk), blk)

    def in_copy(j):
        src = xs_ref.at[pl.ds(pl.multiple_of(row0 + j * blk, blk), blk)]
        return pltpu.make_async_copy(src, acc_ref.at[acc_rows(j)], in_sem)

    def out_copy(j):
        dst = ys_ref.at[pl.ds(pl.multiple_of(row0 + j * blk, blk), blk)]
        return pltpu.make_async_copy(acc_ref.at[acc_rows(j)], dst, out_sem)

    def for_blocks(fn):
        def body(j, c):
            fn(j)
            return c

        lax.fori_loop(0, nb, body, 0)

    @pl.when((f == 0) & (nb > 0))
    def _():
        for_blocks(lambda j: in_copy(j).start())
        for_blocks(lambda j: in_copy(j).wait())

        def to_bf16(j):
            xbuf_ref[acc_rows(j), :] = acc_ref[acc_rows(j), :].astype(BF16)
            acc_ref[acc_rows(j), :] = jnp.broadcast_to(bd_ref[...], (blk, acc_ref.shape[1]))

        for_blocks(to_bf16)

    @pl.when(nb > 0)
    def _():
        def compute(r0, n_rows, wg, wu, wd):
            rows = pl.ds(pl.multiple_of(r0, blk), n_rows)
            xb = xbuf_ref[rows, :]
            hg = jnp.dot(xb, wg, preferred_element_type=F32) + bg_ref[...]
            hu = jnp.dot(xb, wu, preferred_element_type=F32) + bu_ref[...]
            gate = jnp.minimum(hg, SWIGLU_LIMIT)
            up = jnp.clip(hu, -SWIGLU_LIMIT, SWIGLU_LIMIT)
            act = (up + 1.0) * (gate * jax.nn.sigmoid(SWIGLU_ALPHA * gate))
            acc_ref[rows, :] += jnp.dot(act.astype(BF16), wd, preferred_element_type=F32)

        def cast_weights():
            wg = wg_ref[...].astype(BF16)
            wu = wu_ref[...].astype(BF16)
            wd = wd_ref[...].astype(BF16)
            wgb_ref[...] = wg
            wub_ref[...] = wu
            wdb_ref[...] = wd
            return wg, wu, wd

        def compute_cached(r0, n_rows):
            compute(r0, n_rows, wgb_ref[...], wub_ref[...], wdb_ref[...])

        n4 = nb // 4

        @pl.when(n4 > 0)
        def _():
            compute(0, 4 * blk, *cast_weights())

            def quad(j, c):
                compute_cached(j * (4 * blk), 4 * blk)
                return c

            lax.fori_loop(1, n4, quad, 0)

        @pl.when(n4 == 0)
        def _():
            cast_weights()

        @pl.when((nb & 2) != 0)
        def _():
            compute_cached(n4 * (4 * blk), 2 * blk)

        @pl.when((nb & 1) != 0)
        def _():
            compute_cached((nb - 1) * blk, blk)

    @pl.when((f == n_f - 1) & (nb > 0))
    def _():
        for_blocks(lambda j: out_copy(j).start())
        for_blocks(lambda j: out_copy(j).wait())


def _moe_experts(xs, items, w_gu, b_gu, w_down, b_down, layer, *, tf=512):
    item_e, item_row, item_nb = items
    n_items = item_e.shape[0]
    P, D = xs.shape
    L, E, _, F2 = w_gu.shape
    F = F2 // 2
    n_f = F // tf
    rc = MOE_CHUNK_BLOCKS * MOE_BLOCK

    def fe(f, nb_ref, it):
        return jnp.where(nb_ref[it] > 0, f, n_f - 1)

    in_specs = [
        pl.BlockSpec(memory_space=pl.ANY),
        pl.BlockSpec((None, None, D, tf), lambda it, f, e, r, nb: (layer, e[it], 0, fe(f, nb, it))),
        pl.BlockSpec((None, None, D, tf), lambda it, f, e, r, nb: (layer, e[it], 0, n_f + fe(f, nb, it))),
        pl.BlockSpec((None, None, 1, tf), lambda it, f, e, r, nb: (layer, e[it], 0, fe(f, nb, it))),
        pl.BlockSpec((None, None, 1, tf), lambda it, f, e, r, nb: (layer, e[it], 0, n_f + fe(f, nb, it))),
        pl.BlockSpec((None, None, tf, D), lambda it, f, e, r, nb: (layer, e[it], fe(f, nb, it), 0)),
        pl.BlockSpec((None, None, 1, D), lambda it, f, e, r, nb: (layer, e[it], 0, 0)),
    ]
    return pl.pallas_call(
        functools.partial(_moe_kernel, n_f=n_f),
        name="moe_experts",
        grid_spec=pltpu.PrefetchScalarGridSpec(
            num_scalar_prefetch=3, grid=(n_items, n_f),
            in_specs=in_specs,
            out_specs=pl.BlockSpec(memory_space=pl.ANY),
            scratch_shapes=[pltpu.VMEM((rc, D), BF16),
                            pltpu.VMEM((rc, D), F32),
                            pltpu.VMEM((D, tf), BF16),
                            pltpu.VMEM((D, tf), BF16),
                            pltpu.VMEM((tf, D), BF16),
                            pltpu.SemaphoreType.DMA(()),
                            pltpu.SemaphoreType.DMA(())]),
        out_shape=jax.ShapeDtypeStruct((P, D), F32),
        compiler_params=_cparams(("arbitrary", "arbitrary")),
    )(item_e, item_row, item_nb, xs, w_gu, w_gu, b_gu.reshape(L, E, 1, F2), b_gu.reshape(L, E, 1, F2),
      w_down, b_down.reshape(L, E, 1, D))


def _combine_kernel(dest_ref, ys_ref, gate_ref, x_ref, g_ref, b_ref, o_ref, buf_ref, sem, *, alpha):
    tb = x_ref.shape[0]
    step = pl.program_id(0)
    slot = step % 2

    def copy(blk, sl, i, k):
        src = ys_ref.at[pl.ds(dest_ref[(blk * tb + i) * TOP_K + k], 1)]
        return pltpu.make_async_copy(src, buf_ref.at[sl, k, pl.ds(i, 1)], sem.at[sl])

    def for_rows(fn):
        def body(i, c):
            for k in range(TOP_K):
                fn(i, k)
            return c

        lax.fori_loop(0, tb, body, 0)

    @pl.when(step == 0)
    def _():
        for_rows(lambda i, k: copy(0, 0, i, k).start())

    @pl.when(step + 1 < pl.num_programs(0))
    def _():
        for_rows(lambda i, k: copy(step + 1, 1 - slot, i, k).start())

    for_rows(lambda i, k: copy(step, slot, i, k).wait())
    gate = gate_ref[...]
    ffn = gate[:, 0:1] * buf_ref[slot, 0]
    for k in range(1, TOP_K):
        ffn = ffn + gate[:, k:k + 1] * buf_ref[slot, k]
    o_ref[...] = _layer_norm_rows(alpha * x_ref[...] + ffn, g_ref[...], b_ref[...])


def _combine_ln(ys, dest_flat, gates, x, g, b, *, alpha, tb=128):
    T, D = x.shape
    gate_pad = jnp.zeros((T, LANES), F32).at[:, :TOP_K].set(gates)
    return pl.pallas_call(
        functools.partial(_combine_kernel, alpha=alpha),
        name="moe_combine",
        grid_spec=pltpu.PrefetchScalarGridSpec(
            num_scalar_prefetch=1, grid=(T // tb,),
            in_specs=[pl.BlockSpec(memory_space=pl.ANY),
                      pl.BlockSpec((tb, LANES), lambda i, dest: (i, 0)),
                      pl.BlockSpec((tb, D), lambda i, dest: (i, 0)),
                      pl.BlockSpec((1, D), lambda i, dest: (0, 0)),
                      pl.BlockSpec((1, D), lambda i, dest: (0, 0))],
            out_specs=pl.BlockSpec((tb, D), lambda i, dest: (i, 0)),
            scratch_shapes=[pltpu.VMEM((2, TOP_K, tb, D), F32), pltpu.SemaphoreType.DMA((2,))]),
        out_shape=jax.ShapeDtypeStruct((T, D), F32),
        compiler_params=_cparams(("arbitrary",)),
    )(dest_flat, ys, gate_pad, x, g.reshape(1, D), b.reshape(1, D))


def _moe_layout(idx, rank, counts, n_assign):
    E = counts.shape[0]
    blk = MOE_BLOCK
    nblk = (counts + blk - 1) // blk
    bend = jnp.cumsum(nblk)
    bstart = bend - nblk
    dest = (bstart * blk)[idx] + rank
    tail = jnp.where(nblk > 0, (bend - 1) * blk, 0).astype(I32)
    cb = MOE_CHUNK_BLOCKS
    n_items_max = E + (n_assign // blk + E) // cb + 1
    per_e = (nblk + cb - 1) // cb
    iend = jnp.cumsum(per_e)
    istart = iend - per_e
    ids = jnp.arange(n_items_max, dtype=I32)
    e_of = jnp.minimum(jnp.searchsorted(iend, ids, side='right'), E - 1).astype(I32)
    valid = ids < iend[-1]
    last_e = jnp.max(jnp.where(per_e > 0, jnp.arange(E, dtype=I32), 0))
    e_of = jnp.where(valid, e_of, last_e)
    local = ids - istart[e_of]
    first_blk = bstart[e_of] + local * cb
    nb = jnp.where(valid, jnp.minimum(cb, nblk[e_of] - local * cb), 0)
    row = jnp.where(valid, first_blk * blk, 0)
    return dest.astype(I32), tail, (e_of, row.astype(I32), nb.astype(I32))


def _moe(x, w_router, b_router, w_gu, b_gu, w_down, b_down, ln_g, ln_b, *, alpha, layer):
    T, D = x.shape
    E = w_router.shape[1]
    idx, gates, rank, counts = _route(x, w_router, b_router)
    n_assign = T * TOP_K
    dest, tail, items = _moe_layout(idx, rank, counts, n_assign)
    dest_flat = dest.reshape(n_assign)
    n_rows = n_assign + E * MOE_BLOCK
    xs = _dispatch(x, dest_flat, _zero_tails(tail, n_rows, D))
    ys = _moe_experts(xs, items, w_gu, b_gu, w_down, b_down, layer)
    return _combine_ln(ys, dest_flat, gates, x, ln_g, ln_b, alpha=alpha)


def _even_mixer_ln(x, w_in, lam_re, lam_im, log_step, b_re, b_im, c_re, c_im, d_skip, w_glu, b_glu,
                   w_gate2, b_gate2, norm_g, w_out, ln_g, ln_b, *, alpha):
    T, D = x.shape
    W = d_skip.shape[0]
    qk = W // 2
    rank = w_gate2.shape[0]
    s4 = W + 2 * qk + W
    w_main = jnp.concatenate([w_in[:, :s4], w_in[:, s4 + rank:]], axis=1).astype(BF16)
    w_gate = jnp.zeros((D, LANES), BF16).at[:, :rank].set(w_in[:, s4:s4 + rank].astype(BF16))
    h, g_low = _proj_in(x, w_main, w_gate)
    tables = _s5_tables(lam_re, lam_im, log_step, b_re, b_im, c_re, c_im, T // S5_CHUNK)
    y = _s5_core(h, W, tables)
    ya = _s5_glu(y, h, d_skip.astype(F32), w_glu.astype(BF16), b_glu.astype(F32))
    yb = _gla(h, g_low, w_gate2, b_gate2, norm_g, width=W)
    w_out_b = w_out.astype(BF16)
    return _proj_ln([ya, yb], [w_out_b[:W], w_out_b[W:]], x, ln_g, ln_b, alpha=alpha, permuted=False)


def _odd_mixer_ln(x, w_qkv, w_o, ln_g, ln_b, *, alpha):
    T, D = x.shape
    qkv = _proj_perm(x, w_qkv.astype(BF16), scaled_cols=D, scale=(D // ATT_HEADS) ** -0.5)
    y = _dilated_attention(qkv, D)
    return _proj_ln([y], [w_o.astype(BF16)], x, ln_g, ln_b, alpha=alpha, permuted=True)


def kernel(x, ab_w_in, s5_lam_re, s5_lam_im, s5_log_step, s5_b_re, s5_b_im, s5_c_re, s5_c_im, s5_d, s5_w_glu, s5_b_glu, gla_w_gate2, gla_b_gate2, gla_norm_g, ab_w_out, c_w_qkv, c_w_o, ln1_g, ln1_b, moe_w_router, moe_b_router, moe_w_gu, moe_b_gu, moe_w_down, moe_b_down, ln2_g, ln2_b):
    bsz, L, D = x.shape
    depth = ln1_g.shape[0]
    alpha = (2 * depth) ** 0.25
    outs = []
    for bi in range(bsz):
        xt = x[bi].astype(F32)
        for layer in range(depth):
            i = layer // 2
            if layer % 2 == 0:
                xt = _even_mixer_ln(xt, ab_w_in[i], s5_lam_re[i], s5_lam_im[i], s5_log_step[i], s5_b_re[i],
                                    s5_b_im[i], s5_c_re[i], s5_c_im[i], s5_d[i], s5_w_glu[i], s5_b_glu[i],
                                    gla_w_gate2[i], gla_b_gate2[i], gla_norm_g[i], ab_w_out[i],
                                    ln1_g[layer], ln1_b[layer], alpha=alpha)
            else:
                xt = _odd_mixer_ln(xt, c_w_qkv[i], c_w_o[i], ln1_g[layer], ln1_b[layer], alpha=alpha)
            xt = _moe(xt, moe_w_router[layer], moe_b_router[layer], moe_w_gu, moe_b_gu,
                      moe_w_down, moe_b_down, ln2_g[layer], ln2_b[layer], alpha=alpha, layer=layer)
        outs.append(xt.astype(x.dtype))
    return outs[0].reshape(1, L, D) if bsz == 1 else jnp.stack(outs)
```

```python
import functools
import math

import jax
import jax.numpy as jnp
from jax import lax
from jax.experimental import pallas as pl
from jax.experimental.pallas import tpu as pltpu

F32 = jnp.float32
BF16 = jnp.bfloat16
I32 = jnp.int32
HIGHEST = lax.Precision.HIGHEST

LANES = 128
VMEM_LIMIT_BYTES = 56 * 1024 * 1024

S5_GROUP = 16
S5_STATE = 64
S5_MAX_RE = -1e-4
S5_CHUNK = 16
GLA_HEADS = 4
GLA_GATE_TEMP = 16.0
GLA_EPS = 1e-6
GLA_BLOCK = 64
GLA_SUB = 16
ATT_HEADS = 16
DIL = 16
DILATED_GROUPS = ((128, 1), (512, 4), (2048, 16))
TOP_K = 4
SWIGLU_LIMIT = 7.0
SWIGLU_ALPHA = 1.702
MOE_BLOCK = 128
MOE_CHUNK_BLOCKS = 10
LN_EPS = 1e-5
NEG_BIG = -1e30


def _cparams(semantics):
    return pltpu.CompilerParams(dimension_semantics=semantics, vmem_limit_bytes=VMEM_LIMIT_BYTES)


def _layer_norm_rows(z, g, b):
    mu = jnp.mean(z, axis=-1, keepdims=True)
    zc = z - mu
    var = jnp.mean(zc * zc, axis=-1, keepdims=True)
    return zc * lax.rsqrt(var + LN_EPS) * g + b


def _proj_in_kernel(x_ref, w_ref, wg_ref, h_ref, g_ref, xb_ref):
    @pl.when(pl.program_id(1) == 0)
    def _():
        xb = x_ref[...].astype(BF16)
        xb_ref[...] = xb
        g_ref[...] = jnp.dot(xb, wg_ref[...], preferred_element_type=F32)

    h_ref[...] = jnp.dot(xb_ref[...], w_ref[...], preferred_element_type=F32)


def _proj_in(x, w_main, w_gate, *, tm=1024, tn=512):
    T, D = x.shape
    N = w_main.shape[1]
    return pl.pallas_call(
        _proj_in_kernel,
        name="proj_in",
        grid=(T // tm, N // tn),
        in_specs=[pl.BlockSpec((tm, D), lambda i, j: (i, 0)),
                  pl.BlockSpec((D, tn), lambda i, j: (0, j)),
                  pl.BlockSpec((D, LANES), lambda i, j: (0, 0))],
        out_specs=[pl.BlockSpec((tm, tn), lambda i, j: (i, j)),
                   pl.BlockSpec((tm, LANES), lambda i, j: (i, 0))],
        out_shape=[jax.ShapeDtypeStruct((T, N), F32), jax.ShapeDtypeStruct((T, LANES), F32)],
        scratch_shapes=[pltpu.VMEM((tm, D), BF16)],
        compiler_params=_cparams(("parallel", "arbitrary")),
    )(x, w_main, w_gate)


def _proj_perm_kernel(x0_ref, x1_ref, w_ref, o_ref, xb_ref, *, n_scaled, scale):
    j = pl.program_id(1)
    na = x0_ref.shape[0]

    @pl.when(j == 0)
    def _():
        xb_ref[:na, :] = x0_ref[...].astype(BF16)
        xb_ref[na:, :] = x1_ref[...].astype(BF16)

    y = jnp.dot(xb_ref[...], w_ref[...], preferred_element_type=F32)
    o_ref[...] = (y * jnp.where(j < n_scaled, scale, 1.0)).astype(o_ref.dtype)


def _proj_perm(x, w, *, scaled_cols, scale, tn=1024):
    T, D = x.shape
    N = w.shape[1]
    na = T // DIL
    x3 = x.reshape(na, DIL * D)
    return pl.pallas_call(
        functools.partial(_proj_perm_kernel, n_scaled=scaled_cols // tn, scale=scale),
        name="proj_perm",
        grid=(DIL // 2, N // tn),
        in_specs=[pl.BlockSpec((na, D), lambda r, j: (0, 2 * r)),
                  pl.BlockSpec((na, D), lambda r, j: (0, 2 * r + 1)),
                  pl.BlockSpec((D, tn), lambda r, j: (0, j))],
        out_specs=pl.BlockSpec((2 * na, tn), lambda r, j: (r, j)),
        out_shape=jax.ShapeDtypeStruct((T, N), BF16),
        scratch_shapes=[pltpu.VMEM((2 * na, D), BF16)],
        compiler_params=_cparams(("parallel", "arbitrary")),
    )(x3, x3, w)


def _proj_ln_kernel(*refs, n_lhs, alpha, n_tiles):
    lhs_refs = refs[:n_lhs]
    w_refs = refs[n_lhs:2 * n_lhs]
    res_ref, g_ref, b_ref, o_ref, acc_ref = refs[2 * n_lhs:]
    j = pl.program_id(1)
    y = jnp.dot(lhs_refs[0][...], w_refs[0][...], preferred_element_type=F32)
    for a_ref, w_ref in zip(lhs_refs[1:], w_refs[1:]):
        y = y + jnp.dot(a_ref[...], w_ref[...], preferred_element_type=F32)
    acc_ref[j] = y

    @pl.when(j == n_tiles - 1)
    def _():
        tn = acc_ref.shape[2]
        z = [alpha * res_ref[:, t * tn:(t + 1) * tn] + acc_ref[t] for t in range(n_tiles)]
        n = float(n_tiles * tn)
        mu = sum(jnp.sum(zt, axis=-1, keepdims=True) for zt in z) / n
        zc = [zt - mu for zt in z]
        var = sum(jnp.sum(zt * zt, axis=-1, keepdims=True) for zt in zc) / n
        rstd = lax.rsqrt(var + LN_EPS)
        for t in range(n_tiles):
            sl = slice(t * tn, (t + 1) * tn)
            o_ref[:, sl] = zc[t] * rstd * g_ref[:, sl] + b_ref[:, sl]


def _proj_ln(lhs, ws, res, g, b, *, alpha, permuted, tm=512, tn=512):
    T, N = res.shape
    n_lhs = len(lhs)
    n_tiles = N // tn
    if permuted:
        tm = T // DIL
        res_in = res.reshape(tm, DIL * N)
        res_spec = pl.BlockSpec((tm, N), lambda i, j: (0, i))
        out_spec = pl.BlockSpec((tm, N), lambda i, j: (0, i))
        out_shape = jax.ShapeDtypeStruct((tm, DIL * N), F32)
    else:
        res_in = res
        res_spec = pl.BlockSpec((tm, N), lambda i, j: (i, 0))
        out_spec = pl.BlockSpec((tm, N), lambda i, j: (i, 0))
        out_shape = jax.ShapeDtypeStruct((T, N), F32)
    in_specs = ([pl.BlockSpec((tm, a.shape[1]), lambda i, j: (i, 0)) for a in lhs]
                + [pl.BlockSpec((w.shape[0], tn), lambda i, j: (0, j)) for w in ws]
                + [res_spec,
                   pl.BlockSpec((1, N), lambda i, j: (0, 0)),
                   pl.BlockSpec((1, N), lambda i, j: (0, 0))])
    out = pl.pallas_call(
        functools.partial(_proj_ln_kernel, n_lhs=n_lhs, alpha=alpha, n_tiles=n_tiles),
        name="proj_ln",
        grid=(T // tm, n_tiles),
        in_specs=in_specs,
        out_specs=out_spec,
        out_shape=out_shape,
        scratch_shapes=[pltpu.VMEM((n_tiles, tm, tn), F32)],
        compiler_params=_cparams(("parallel", "arbitrary")),
    )(*lhs, *ws, res_in, g.reshape(1, N), b.reshape(1, N))
    return out.reshape(T, N)


def _s5_tables(lam_re, lam_im, log_step, b_re, b_im, c_re, c_im, n_chunks):
    C = S5_CHUNK
    G, P = lam_re.shape
    H = b_re.shape[-1]
    lr = jnp.minimum(lam_re.astype(F32), S5_MAX_RE)
    li = lam_im.astype(F32)
    dt = jnp.exp(log_step.astype(F32))[:, None]
    mag = jnp.exp(lr * dt)
    a_re = mag * jnp.cos(li * dt)
    a_im = mag * jnp.sin(li * dt)
    den = lr * lr + li * li
    nr = a_re - 1.0
    f_re = (nr * lr + a_im * li) / den
    f_im = (a_im * lr - nr * li) / den
    br = b_re.astype(F32)
    bi = b_im.astype(F32)
    bb_re = f_re[..., None] * br - f_im[..., None] * bi
    bb_im = f_re[..., None] * bi + f_im[..., None] * br
    pw_re = [jnp.ones_like(a_re)]
    pw_im = [jnp.zeros_like(a_im)]
    for _ in range(C):
        pr, pi = pw_re[-1], pw_im[-1]
        pw_re.append(pr * a_re - pi * a_im)
        pw_im.append(pr * a_im + pi * a_re)
    pw_re = jnp.stack(pw_re)
    pw_im = jnp.stack(pw_im)
    ab_re = pw_re[:C, :, :, None] * bb_re[None] - pw_im[:C, :, :, None] * bb_im[None]
    ab_im = pw_re[:C, :, :, None] * bb_im[None] + pw_im[:C, :, :, None] * bb_re[None]
    cr = c_re.astype(F32)
    ci = c_im.astype(F32)
    gpt = LANES // H
    M = G // gpt
    same = jnp.eye(gpt, dtype=bool)
    def spread_z(ab):
        zc = jnp.transpose(ab.reshape(C, M, gpt, P, H), (1, 0, 2, 4, 3))
        wide = jnp.where(same[None, None, :, None, :, None], zc[:, :, :, :, None, :], 0.0)
        return wide.astype(BF16).reshape(M, C * gpt * H, gpt * P)

    wz = jnp.concatenate([spread_z(ab_re[::-1]), spread_z(ab_im[::-1])], axis=-1)
    kern = (jnp.einsum('gop,kgph->kgoh', cr, ab_re, precision=HIGHEST)
            - jnp.einsum('gop,kgph->kgoh', ci, ab_im, precision=HIGHEST))
    lag = jnp.arange(C)[None, :] - jnp.arange(C)[:, None]
    kl = kern[jnp.clip(lag, 0, C - 1)]
    kl = jnp.where((lag >= 0)[:, :, None, None, None], kl, 0.0)
    mc = jnp.transpose(kl.reshape(C, C, M, gpt, H, H), (2, 0, 3, 5, 1, 4))
    wy = jnp.where(same[None, None, :, None, None, :, None], mc[:, :, :, :, :, None, :], 0.0)
    wy = wy.astype(BF16).reshape(M, C * gpt * H, C * gpt * H)
    ca_re = cr[None] * pw_re[1:, :, None, :] - ci[None] * pw_im[1:, :, None, :]
    ca_im = cr[None] * pw_im[1:, :, None, :] + ci[None] * pw_re[1:, :, None, :]

    def spread_n(ca):
        nc_ = jnp.transpose(ca.reshape(C, M, gpt, H, P), (1, 2, 4, 0, 3))
        wide = jnp.where(same[None, :, None, None, :, None], nc_[:, :, :, :, None, :], 0.0)
        return wide.astype(BF16).reshape(M, gpt * P, C * gpt * H)

    wn = jnp.concatenate([spread_n(ca_re), spread_n(-ca_im)], axis=1)
    n_steps = max(1, (n_chunks - 1).bit_length())
    qr, qi = pw_re[C].reshape(M, gpt * P), pw_im[C].reshape(M, gpt * P)
    ar, ai = [], []
    for _ in range(n_steps):
        ar.append(qr)
        ai.append(qi)
        qr, qi = qr * qr - qi * qi, 2.0 * qr * qi
    ar = jnp.stack(ar, axis=1)
    ai = jnp.stack(ai, axis=1)
    return wz, wy, wn, ar, ai


def _s5_kernel(u_ref, wz_ref, wy_ref, wn_ref, ar_ref, ai_ref, y_ref, ub_ref, sin_ref, *, n_steps):
    C = S5_CHUNK
    nc = u_ref.shape[0] // C
    half = pl.program_id(1)

    @pl.when(half == 0)
    def _():
        for j in range(C):
            ub_ref[:, j * LANES:(j + 1) * LANES] = u_ref[pl.ds(j, nc, stride=C), :].astype(BF16)
        s = jnp.dot(ub_ref[...], wz_ref[...], preferred_element_type=F32)
        q = s.shape[1] // 2
        s_re, s_im = s[:, :q], s[:, q:]
        row = lax.broadcasted_iota(I32, (nc, q), 0)
        for k in range(n_steps):
            sh = 1 << k
            p_re = jnp.where(row >= sh, pltpu.roll(s_re, sh, axis=0), 0.0)
            p_im = jnp.where(row >= sh, pltpu.roll(s_im, sh, axis=0), 0.0)
            a_re, a_im = ar_ref[k:k + 1, :], ai_ref[k:k + 1, :]
            s_re, s_im = s_re + a_re * p_re - a_im * p_im, s_im + a_re * p_im + a_im * p_re
        sin_ref[:, :q] = jnp.where(row >= 1, pltpu.roll(s_re, 1, axis=0), 0.0).astype(BF16)
        sin_ref[:, q:] = jnp.where(row >= 1, pltpu.roll(s_im, 1, axis=0), 0.0).astype(BF16)

    y = (jnp.dot(ub_ref[...], wy_ref[...], preferred_element_type=F32)
         + jnp.dot(sin_ref[...], wn_ref[...], preferred_element_type=F32))
    n_half = y.shape[1] // LANES
    for i in range(n_half):
        y_ref[pl.ds(half * n_half + i, nc, stride=C), :] = y[:, i * LANES:(i + 1) * LANES]


def _s5_core(h, width, tables):
    wz, wy, wn, ar, ai = tables
    T, hw = h.shape
    M, CL, Q2 = wz.shape
    C = S5_CHUNK
    nc = T // C
    n_steps = ar.shape[1]
    assert CL == C * LANES and width == M * LANES
    halves = 2
    clh = CL // halves
    return pl.pallas_call(
        functools.partial(_s5_kernel, n_steps=n_steps),
        name="s5_core",
        grid=(M, halves),
        in_specs=[pl.BlockSpec((T, LANES), lambda m, p: (0, m)),
                  pl.BlockSpec((None, CL, Q2), lambda m, p: (m, 0, 0)),
                  pl.BlockSpec((None, CL, clh), lambda m, p: (m, 0, p)),
                  pl.BlockSpec((None, Q2, clh), lambda m, p: (m, 0, p)),
                  pl.BlockSpec((None, n_steps, Q2 // 2), lambda m, p: (m, 0, 0)),
                  pl.BlockSpec((None, n_steps, Q2 // 2), lambda m, p: (m, 0, 0))],
        out_specs=pl.BlockSpec((T, LANES), lambda m, p: (0, m)),
        out_shape=jax.ShapeDtypeStruct((T, width), F32),
        scratch_shapes=[pltpu.VMEM((nc, CL), BF16), pltpu.VMEM((nc, Q2), BF16)],
        compiler_params=_cparams(("parallel", "arbitrary")),
    )(h, wz, wy, wn, ar, ai)


def _s5_glu_kernel(y_ref, u_ref, d_ref, w_ref, b_ref, o_ref):
    y = y_ref[...] + d_ref[...] * u_ref[...]
    c0 = math.sqrt(2.0 / math.pi)
    z = 0.5 * y * (1.0 + jnp.tanh(c0 * (y + 0.044715 * (y * y * y))))
    lin = jnp.dot(z.astype(BF16), w_ref[...], preferred_element_type=F32) + b_ref[...]
    o_ref[...] = (z * jax.nn.sigmoid(lin)).astype(o_ref.dtype)


def _s5_glu(y, h, d_skip, w_glu, b_glu, *, tm=512):
    T, W = y.shape
    return pl.pallas_call(
        _s5_glu_kernel,
        name="s5_glu",
        grid=(T // tm,),
        in_specs=[pl.BlockSpec((tm, W), lambda i: (i, 0)),
                  pl.BlockSpec((tm, W), lambda i: (i, 0)),
                  pl.BlockSpec((1, W), lambda i: (0, 0)),
                  pl.BlockSpec((W, W), lambda i: (0, 0)),
                  pl.BlockSpec((1, W), lambda i: (0, 0))],
        out_specs=pl.BlockSpec((tm, W), lambda i: (i, 0)),
        out_shape=jax.ShapeDtypeStruct((T, W), BF16),
        compiler_params=_cparams(("parallel",)),
    )(y, h, d_skip.reshape(1, W), w_glu, b_glu.reshape(1, W))


def _gla_kernel(q_ref, k_ref, v_ref, r_ref, g_ref, w2_ref, b2_ref, ng_ref, o_ref, st_ref, *, dk, dv):
    @pl.when(pl.program_id(0) == 0)
    def _():
        st_ref[...] = jnp.zeros_like(st_ref)

    cb = q_ref.shape[0]
    n_sub = cb // GLA_SUB
    scale = dk ** -0.5
    logit = jnp.dot(g_ref[...], w2_ref[...], preferred_element_type=F32, precision=HIGHEST) + b2_ref[...]
    log_a = (jnp.minimum(logit, 0.0) - jnp.log(1.0 + jnp.exp(-jnp.abs(logit)))) / GLA_GATE_TEMP
    ri = lax.broadcasted_iota(I32, (cb, cb), 0)
    ci = lax.broadcasted_iota(I32, (cb, cb), 1)
    tri = (ri >= ci).astype(F32)
    bcum = jnp.dot(tri, log_a, preferred_element_type=F32, precision=HIGHEST)
    for hh in range(GLA_HEADS):
        ks = slice(hh * dk, (hh + 1) * dk)
        vs = slice(hh * dv, (hh + 1) * dv)
        b = bcum[:, ks]
        q = q_ref[:, ks] * scale
        k = k_ref[:, ks]
        v = v_ref[:, vs].astype(BF16)
        refs = [jnp.zeros((1, dk), F32)] + [b[a * GLA_SUB - 1:a * GLA_SUB, :] for a in range(1, n_sub)]
        refmat = jnp.concatenate([jnp.broadcast_to(r, (GLA_SUB, dk)) for r in refs], axis=0)
        qe = (q * jnp.exp(b - refmat)).astype(BF16)
        st = st_ref[hh]
        o_inter = lax.dot_general((q * jnp.exp(b)).astype(BF16), st.astype(BF16),
                                  (((1,), (1,)), ((), ())), preferred_element_type=F32)
        o_rows = []
        for a in range(n_sub):
            hi = (a + 1) * GLA_SUB
            ke = (k[:hi] * jnp.exp(refs[a] - b[:hi])).astype(BF16)
            att = lax.dot_general(qe[a * GLA_SUB:hi], ke, (((1,), (1,)), ((), ())),
                                  preferred_element_type=F32)
            row_a = lax.broadcasted_iota(I32, (GLA_SUB, hi), 0) + a * GLA_SUB
            att = jnp.where(lax.broadcasted_iota(I32, (GLA_SUB, hi), 1) <= row_a, att, 0.0)
            o_rows.append(jnp.dot(att.astype(BF16), v[:hi], preferred_element_type=F32))
        o = jnp.concatenate(o_rows, axis=0) + o_inter
        o = o * lax.rsqrt(jnp.mean(o * o, axis=-1, keepdims=True) + GLA_EPS) * ng_ref[...]
        r = r_ref[:, vs]
        o_ref[:, vs] = (o * (r * jax.nn.sigmoid(r))).astype(o_ref.dtype)
        b_last = b[cb - 1:cb, :]
        kd = (k * jnp.exp(b_last - b)).astype(BF16)
        upd = lax.dot_general(v, kd, (((0,), (0,)), ((), ())), preferred_element_type=F32)
        st_ref[hh] = st * jnp.exp(b_last) + upd


def _gla(h, g_low, w_gate2, b_gate2, norm_g, *, width):
    T = h.shape[0]
    qk = width // 2
    dk = qk // GLA_HEADS
    dv = width // GLA_HEADS
    cb = GLA_BLOCK
    w2 = jnp.zeros((LANES, qk), F32).at[:w_gate2.shape[0]].set(w_gate2.astype(F32))
    return pl.pallas_call(
        functools.partial(_gla_kernel, dk=dk, dv=dv),
        name="gla",
        grid=(T // cb,),
        in_specs=[pl.BlockSpec((cb, qk), lambda i: (i, 2)),
                  pl.BlockSpec((cb, qk), lambda i: (i, 3)),
                  pl.BlockSpec((cb, width), lambda i: (i, 2)),
                  pl.BlockSpec((cb, width), lambda i: (i, 3)),
                  pl.BlockSpec((cb, LANES), lambda i: (i, 0)),
                  pl.BlockSpec((LANES, qk), lambda i: (0, 0)),
                  pl.BlockSpec((1, qk), lambda i: (0, 0)),
                  pl.BlockSpec((1, dv), lambda i: (0, 0))],
        out_specs=pl.BlockSpec((cb, width), lambda i: (i, 0)),
        out_shape=jax.ShapeDtypeStruct((T, width), BF16),
        scratch_shapes=[pltpu.VMEM((GLA_HEADS, dv, dk), F32)],
        compiler_params=_cparams(("arbitrary",)),
    )(h, h, h, h, g_low, w2, b_gate2.reshape(1, qk).astype(F32), norm_g.reshape(1, dv).astype(F32))


def _attn_kernel(q_ref, k_ref, v_ref, o_ref, acc_ref, m_ref, l_ref, v1_ref, pat1_ref, pat4_ref, pat16_ref, *, na):
    dh = v_ref.shape[1]
    v1_ref[:, :dh] = v_ref[...]
    v1_ref[:, dh:] = jnp.ones_like(v_ref)
    acc_ref[...] = jnp.zeros_like(acc_ref)
    m_ref[...] = jnp.full_like(m_ref, NEG_BIG)
    l_ref[...] = jnp.zeros_like(l_ref)

    def token_offsets(rows, r_step, n, axis):
        shape = (n, 1) if axis == 0 else (1, n)
        idx = lax.broadcasted_iota(I32, shape, axis)
        c = idx // rows
        return DIL * (idx - c * rows) + r_step * c

    def delta_pattern(n_chunks, q_rows, k_rows, r_step):
        return (token_offsets(q_rows, r_step, n_chunks * q_rows, 0)
                - token_offsets(k_rows, r_step, n_chunks * k_rows, 1))

    def cat(ref, starts, rows):
        return jnp.concatenate([ref[pl.ds(pl.multiple_of(s, 16), rows), :] for s in starts], axis=0)

    def attend(blocks, q_rows, k_rows, window, pat_ref):
        loaded = []
        for q_starts, k_starts, a_q, a_k in blocks:
            loaded.append((cat(q_ref, q_starts, q_rows), cat(k_ref, k_starts, k_rows), cat(v1_ref, k_starts, k_rows),
                           cat(m_ref, q_starts, q_rows), cat(l_ref, q_starts, q_rows),
                           cat(acc_ref, q_starts, q_rows)))
        results = []
        for (q_starts, k_starts, a_q, a_k), (qb, kb, vb, m_old, l_old, acc_old) in zip(blocks, loaded):
            s = lax.dot_general(qb, kb, (((1,), (1,)), ((), ())), preferred_element_type=F32)
            off = DIL * (a_q - a_k)
            pat = pat_ref[...]
            s = jnp.where((pat >= -off) & (pat <= window - off), s, NEG_BIG)
            m_new = jnp.maximum(m_old, jnp.max(s, axis=1, keepdims=True))
            alpha = jnp.exp(m_old - m_new)
            p = jnp.exp(s - m_new[:, :1])
            pv = jnp.dot(p.astype(BF16), vb, preferred_element_type=F32)
            l_new = alpha * l_old + pv[:, dh:]
            acc_new = alpha * acc_old + pv[:, :dh]
            results.append((m_new, l_new, acc_new))
        for (q_starts, *_), (m_new, l_new, acc_new) in zip(blocks, results):
            for c, st in enumerate(q_starts):
                st = pl.multiple_of(st, 16)
                rs = slice(c * q_rows, (c + 1) * q_rows)
                m_ref[pl.ds(st, q_rows), :] = m_new[rs]
                l_ref[pl.ds(st, q_rows), :] = l_new[rs]
                acc_ref[pl.ds(st, q_rows), :] = acc_new[rs]

    w1 = DILATED_GROUPS[0][0]
    u1 = 2
    pat1_ref[...] = delta_pattern(DIL, 16, 32, 1)

    def body1(it, carry):
        blocks = []
        for j in range(u1):
            a0 = (it * u1 + j) * 16
            ak = jnp.maximum(a0 - 16, 0)
            blocks.append(([r * na + a0 for r in range(DIL)], [r * na + ak for r in range(DIL)], a0, ak))
        attend(blocks, 16, 32, w1, pat1_ref)
        return carry

    lax.fori_loop(0, na // (16 * u1), body1, 0)

    w4 = DILATED_GROUPS[1][0]
    pat4_ref[...] = delta_pattern(4, 32, 64, 4)

    def body4(it, carry):
        a0 = it * 32
        ak = jnp.maximum(a0 - 32, 0)
        blocks = [([(rho + 4 * sg) * na + a0 for sg in range(4)], [(rho + 4 * sg) * na + ak for sg in range(4)],
                   a0, ak) for rho in range(4)]
        attend(blocks, 32, 64, w4, pat4_ref)
        return carry

    lax.fori_loop(0, na // 32, body4, 0)

    w16 = DILATED_GROUPS[2][0]
    u16 = 4
    pat16_ref[...] = delta_pattern(1, 128, 256, 0)

    def body16(it, carry):
        rg = it // (na // 128)
        a0 = (it - rg * (na // 128)) * 128
        ak = jnp.maximum(a0 - 128, 0)
        blocks = [([(rg * u16 + j) * na + a0], [(rg * u16 + j) * na + ak], a0, ak) for j in range(u16)]
        attend(blocks, 128, 256, w16, pat16_ref)
        return carry

    lax.fori_loop(0, (DIL // u16) * (na // 128), body16, 0)

    o_ref[...] = (acc_ref[...] / l_ref[...]).astype(o_ref.dtype)


def _dilated_attention(qkv, d_model):
    T = qkv.shape[0]
    dh = d_model // ATT_HEADS
    assert dh == LANES
    na = T // DIL
    return pl.pallas_call(
        functools.partial(_attn_kernel, na=na),
        name="dilated_attn",
        grid=(ATT_HEADS,),
        in_specs=[pl.BlockSpec((T, dh), lambda h: (0, h)),
                  pl.BlockSpec((T, dh), lambda h: (0, ATT_HEADS + h)),
                  pl.BlockSpec((T, dh), lambda h: (0, 2 * ATT_HEADS + h))],
        out_specs=pl.BlockSpec((T, dh), lambda h: (0, h)),
        out_shape=jax.ShapeDtypeStruct((T, d_model), BF16),
        scratch_shapes=[pltpu.VMEM((T, dh), F32), pltpu.VMEM((T, dh), F32), pltpu.VMEM((T, dh), F32),
                        pltpu.VMEM((T, 2 * dh), BF16), pltpu.VMEM((256, 512), I32), pltpu.VMEM((128, 256), I32), pltpu.VMEM((128, 256), I32)],
        compiler_params=_cparams(("parallel",)),
    )(qkv, qkv, qkv)


def _route_kernel(x_ref, w_ref, b_ref, idx_ref, gate_ref, rank_ref, cnt_ref, carry_ref):
    @pl.when(pl.program_id(0) == 0)
    def _():
        carry_ref[...] = jnp.zeros_like(carry_ref)

    tb = x_ref.shape[0]
    lg = jnp.dot(x_ref[...], w_ref[...], preferred_element_type=F32, precision=HIGHEST) + b_ref[...]
    lane = lax.broadcasted_iota(I32, lg.shape, 1)
    vals, hots = [], []
    idx_out = jnp.zeros(lg.shape, I32)
    for k in range(TOP_K):
        m = jnp.max(lg, axis=1, keepdims=True)
        sel = jnp.min(jnp.where(lg == m, lane, LANES), axis=1, keepdims=True)
        hot = lane == sel
        vals.append(m)
        hots.append(hot)
        idx_out = jnp.where(lane == k, sel, idx_out)
        lg = jnp.where(hot, -jnp.inf, lg)
    ex = [jnp.exp(v - vals[0]) for v in vals]
    den = sum(ex)
    gate_out = jnp.zeros(lg.shape, F32)
    for k in range(TOP_K):
        gate_out = jnp.where(lane == k, ex[k] / den, gate_out)
    chosen = sum(h.astype(F32) for h in hots)
    ri = lax.broadcasted_iota(I32, (tb, tb), 0)
    ci = lax.broadcasted_iota(I32, (tb, tb), 1)
    before = jnp.dot((ri > ci).astype(BF16), chosen.astype(BF16), preferred_element_type=F32) + carry_ref[...]
    rank_out = jnp.zeros(lg.shape, I32)
    for k in range(TOP_K):
        rk = jnp.sum(jnp.where(hots[k], before, 0.0), axis=1, keepdims=True).astype(I32)
        rank_out = jnp.where(lane == k, rk, rank_out)
    idx_ref[...] = idx_out
    gate_ref[...] = gate_out
    rank_ref[...] = rank_out
    carry_ref[...] = carry_ref[...] + jnp.sum(chosen, axis=0, keepdims=True)
    cnt_ref[...] = carry_ref[...].astype(I32)


def _route(x, w_router, b_router, *, tb=512):
    T, D = x.shape
    E = w_router.shape[1]
    wr = jnp.zeros((D, LANES), F32).at[:, :E].set(w_router.astype(F32))
    br = jnp.full((1, LANES), NEG_BIG, F32).at[0, :E].set(b_router.astype(F32))
    row = lambda dt: jax.ShapeDtypeStruct((T, LANES), dt)
    idx, gate, rank, cnt = pl.pallas_call(
        _route_kernel,
        name="moe_route",
        grid=(T // tb,),
        in_specs=[pl.BlockSpec((tb, D), lambda i: (i, 0)),
                  pl.BlockSpec((D, LANES), lambda i: (0, 0)),
                  pl.BlockSpec((1, LANES), lambda i: (0, 0))],
        out_specs=[pl.BlockSpec((tb, LANES), lambda i: (i, 0)),
                   pl.BlockSpec((tb, LANES), lambda i: (i, 0)),
                   pl.BlockSpec((tb, LANES), lambda i: (i, 0)),
                   pl.BlockSpec((1, LANES), lambda i: (0, 0))],
        out_shape=[row(I32), row(F32), row(I32), jax.ShapeDtypeStruct((1, LANES), I32)],
        scratch_shapes=[pltpu.VMEM((1, LANES), F32)],
        compiler_params=_cparams(("arbitrary",)),
    )(x, wr, br)
    return idx[:, :TOP_K], gate[:, :TOP_K], rank[:, :TOP_K], cnt[0, :E]


def _zero_tail_kernel(row_ref, o_ref, z_ref, sem):
    z_ref[...] = jnp.zeros_like(z_ref)
    n = row_ref.shape[0]

    def copy(e):
        return pltpu.make_async_copy(z_ref, o_ref.at[pl.ds(pl.multiple_of(row_ref[e], MOE_BLOCK), MOE_BLOCK)], sem)

    def start(e, c):
        copy(e).start()
        return c

    def wait(e, c):
        copy(e).wait()
        return c

    lax.fori_loop(0, n, start, 0)
    lax.fori_loop(0, n, wait, 0)


def _zero_tails(tail_rows, n_rows, d):
    return pl.pallas_call(
        _zero_tail_kernel,
        name="moe_zero_tails",
        grid_spec=pltpu.PrefetchScalarGridSpec(
            num_scalar_prefetch=1, grid=(1,),
            in_specs=[],
            out_specs=pl.BlockSpec(memory_space=pl.ANY),
            scratch_shapes=[pltpu.VMEM((MOE_BLOCK, d), F32), pltpu.SemaphoreType.DMA(())]),
        out_shape=jax.ShapeDtypeStruct((n_rows, d), F32),
        compiler_params=_cparams(("arbitrary",)),
    )(tail_rows)


def _dispatch_kernel(dest_ref, x_ref, xs_in_ref, xs_ref, sem):
    del xs_in_ref
    tb = x_ref.shape[0]
    base = pl.program_id(0) * tb * TOP_K

    def copy(i, k):
        return pltpu.make_async_copy(x_ref.at[pl.ds(i, 1)],
                                     xs_ref.at[pl.ds(dest_ref[base + i * TOP_K + k], 1)], sem)

    def start(i, c):
        for k in range(TOP_K):
            copy(i, k).start()
        return c

    lax.fori_loop(0, tb, start, 0)
    for _ in range(TOP_K):
        pltpu.make_async_copy(x_ref, xs_ref.at[pl.ds(0, tb)], sem).wait()


def _dispatch(x, dest_flat, xs_init, *, tb=256):
    T, D = x.shape
    return pl.pallas_call(
        _dispatch_kernel,
        name="moe_dispatch",
        grid_spec=pltpu.PrefetchScalarGridSpec(
            num_scalar_prefetch=1, grid=(T // tb,),
            in_specs=[pl.BlockSpec((tb, D), lambda i, dest: (i, 0)),
                      pl.BlockSpec(memory_space=pl.ANY)],
            out_specs=pl.BlockSpec(memory_space=pl.ANY),
            scratch_shapes=[pltpu.SemaphoreType.DMA(())]),
        out_shape=jax.ShapeDtypeStruct(xs_init.shape, xs_init.dtype),
        input_output_aliases={2: 0},
        compiler_params=_cparams(("arbitrary",)),
    )(dest_flat, x, xs_init)


def _moe_kernel(item_e_ref, item_row_ref, item_nb_ref, xs_ref, wg_ref, wu_ref, bg_ref, bu_ref, wd_ref, bd_ref,
                ys_ref, xbuf_ref, acc_ref, wgb_ref, wub_ref, wdb_ref, in_sem, out_sem, *, n_f):
    it = pl.program_id(0)
    f = pl.program_id(1)
    nb = item_nb_ref[it]
    row0 = item_row_ref[it]
    blk = MOE_BLOCK

    def acc_rows(j):
        return pl.ds(pl.multiple_of(j * blk, blk), blk)

    def in_copy(j):
        src = xs_ref.at[pl.ds(pl.multiple_of(row0 + j * blk, blk), blk)]
        return pltpu.make_async_copy(src, acc_ref.at[acc_rows(j)], in_sem)

    def out_copy(j):
        dst = ys_ref.at[pl.ds(pl.multiple_of(row0 + j * blk, blk), blk)]
        return pltpu.make_async_copy(acc_ref.at[acc_rows(j)], dst, out_sem)

    def for_blocks(fn):
        def body(j, c):
            fn(j)
            return c

        lax.fori_loop(0, nb, body, 0)

    @pl.when((f == 0) & (nb > 0))
    def _():
        for_blocks(lambda j: in_copy(j).start())
        for_blocks(lambda j: in_copy(j).wait())

        def to_bf16(j):
            xbuf_ref[acc_rows(j), :] = acc_ref[acc_rows(j), :].astype(BF16)
            acc_ref[acc_rows(j), :] = jnp.broadcast_to(bd_ref[...], (blk, acc_ref.shape[1]))

        for_blocks(to_bf16)

    @pl.when(nb > 0)
    def _():
        def compute(r0, n_rows, wg, wu, wd):
            rows = pl.ds(pl.multiple_of(r0, blk), n_rows)
            xb = xbuf_ref[rows, :]
            hg = jnp.dot(xb, wg, preferred_element_type=F32) + bg_ref[...]
            hu = jnp.dot(xb, wu, preferred_element_type=F32) + bu_ref[...]
            gate = jnp.minimum(hg, SWIGLU_LIMIT)
            up = jnp.clip(hu, -SWIGLU_LIMIT, SWIGLU_LIMIT)
            act = (up + 1.0) * (gate * jax.nn.sigmoid(SWIGLU_ALPHA * gate))
            acc_ref[rows, :] += jnp.dot(act.astype(BF16), wd, preferred_element_type=F32)

        def cast_weights():
            wg = wg_ref[...].astype(BF16)
            wu = wu_ref[...].astype(BF16)
            wd = wd_ref[...].astype(BF16)
            wgb_ref[...] = wg
            wub_ref[...] = wu
            wdb_ref[...] = wd
            return wg, wu, wd

        def compute_cached(r0, n_rows):
            compute(r0, n_rows, wgb_ref[...], wub_ref[...], wdb_ref[...])

        n4 = nb // 4

        @pl.when(n4 > 0)
        def _():
            compute(0, 4 * blk, *cast_weights())

            def quad(j, c):
                compute_cached(j * (4 * blk), 4 * blk)
                return c

            lax.fori_loop(1, n4, quad, 0)

        @pl.when(n4 == 0)
        def _():
            cast_weights()

        @pl.when((nb & 2) != 0)
        def _():
            compute_cached(n4 * (4 * blk), 2 * blk)

        @pl.when((nb & 1) != 0)
        def _():
            compute_cached((nb - 1) * blk, blk)

    @pl.when((f == n_f - 1) & (nb > 0))
    def _():
        for_blocks(lambda j: out_copy(j).start())
        for_blocks(lambda j: out_copy(j).wait())


def _moe_experts(xs, items, w_gu, b_gu, w_down, b_down, layer, *, tf=512):
    item_e, item_row, item_nb = items
    n_items = item_e.shape[0]
    P, D = xs.shape
    L, E, _, F2 = w_gu.shape
    F = F2 // 2
    n_f = F // tf
    rc = MOE_CHUNK_BLOCKS * MOE_BLOCK

    def fe(f, nb_ref, it):
        return jnp.where(nb_ref[it] > 0, f, n_f - 1)

    in_specs = [
        pl.BlockSpec(memory_space=pl.ANY),
        pl.BlockSpec((None, None, D, tf), lambda it, f, e, r, nb: (layer, e[it], 0, fe(f, nb, it))),
        pl.BlockSpec((None, None, D, tf), lambda it, f, e, r, nb: (layer, e[it], 0, n_f + fe(f, nb, it))),
        pl.BlockSpec((None, None, 1, tf), lambda it, f, e, r, nb: (layer, e[it], 0, fe(f, nb, it))),
        pl.BlockSpec((None, None, 1, tf), lambda it, f, e, r, nb: (layer, e[it], 0, n_f + fe(f, nb, it))),
        pl.BlockSpec((None, None, tf, D), lambda it, f, e, r, nb: (layer, e[it], fe(f, nb, it), 0)),
        pl.BlockSpec((None, None, 1, D), lambda it, f, e, r, nb: (layer, e[it], 0, 0)),
    ]
    return pl.pallas_call(
        functools.partial(_moe_kernel, n_f=n_f),
        name="moe_experts",
        grid_spec=pltpu.PrefetchScalarGridSpec(
            num_scalar_prefetch=3, grid=(n_items, n_f),
            in_specs=in_specs,
            out_specs=pl.BlockSpec(memory_space=pl.ANY),
            scratch_shapes=[pltpu.VMEM((rc, D), BF16),
                            pltpu.VMEM((rc, D), F32),
                            pltpu.VMEM((D, tf), BF16),
                            pltpu.VMEM((D, tf), BF16),
                            pltpu.VMEM((tf, D), BF16),
                            pltpu.SemaphoreType.DMA(()),
                            pltpu.SemaphoreType.DMA(())]),
        out_shape=jax.ShapeDtypeStruct((P, D), F32),
        compiler_params=_cparams(("arbitrary", "arbitrary")),
    )(item_e, item_row, item_nb, xs, w_gu, w_gu, b_gu.reshape(L, E, 1, F2), b_gu.reshape(L, E, 1, F2),
      w_down, b_down.reshape(L, E, 1, D))


def _combine_kernel(dest_ref, ys_ref, gate_ref, x_ref, g_ref, b_ref, o_ref, buf_ref, sem, *, alpha):
    tb = x_ref.shape[0]
    step = pl.program_id(0)
    slot = step % 2

    def copy(blk, sl, i, k):
        src = ys_ref.at[pl.ds(dest_ref[(blk * tb + i) * TOP_K + k], 1)]
        return pltpu.make_async_copy(src, buf_ref.at[sl, k, pl.ds(i, 1)], sem.at[sl])

    def for_rows(fn):
        def body(i, c):
            for k in range(TOP_K):
                fn(i, k)
            return c

        lax.fori_loop(0, tb, body, 0)

    @pl.when(step == 0)
    def _():
        for_rows(lambda i, k: copy(0, 0, i, k).start())

    @pl.when(step + 1 < pl.num_programs(0))
    def _():
        for_rows(lambda i, k: copy(step + 1, 1 - slot, i, k).start())

    for k in range(TOP_K):
        pltpu.make_async_copy(ys_ref.at[pl.ds(0, tb)], buf_ref.at[slot, k], sem.at[slot]).wait()
    gate = gate_ref[...]
    ffn = gate[:, 0:1] * buf_ref[slot, 0]
    for k in range(1, TOP_K):
        ffn = ffn + gate[:, k:k + 1] * buf_ref[slot, k]
    o_ref[...] = _layer_norm_rows(alpha * x_ref[...] + ffn, g_ref[...], b_ref[...])


def _combine_ln(ys, dest_flat, gates, x, g, b, *, alpha, tb=128):
    T, D = x.shape
    gate_pad = jnp.zeros((T, LANES), F32).at[:, :TOP_K].set(gates)
    return pl.pallas_call(
        functools.partial(_combine_kernel, alpha=alpha),
        name="moe_combine",
        grid_spec=pltpu.PrefetchScalarGridSpec(
            num_scalar_prefetch=1, grid=(T // tb,),
            in_specs=[pl.BlockSpec(memory_space=pl.ANY),
                      pl.BlockSpec((tb, LANES), lambda i, dest: (i, 0)),
                      pl.BlockSpec((tb, D), lambda i, dest: (i, 0)),
                      pl.BlockSpec((1, D), lambda i, dest: (0, 0)),
                      pl.BlockSpec((1, D), lambda i, dest: (0, 0))],
            out_specs=pl.BlockSpec((tb, D), lambda i, dest: (i, 0)),
            scratch_shapes=[pltpu.VMEM((2, TOP_K, tb, D), F32), pltpu.SemaphoreType.DMA((2,))]),
        out_shape=jax.ShapeDtypeStruct((T, D), F32),
        compiler_params=_cparams(("arbitrary",)),
    )(dest_flat, ys, gate_pad, x, g.reshape(1, D), b.reshape(1, D))


def _moe_layout(idx, rank, counts, n_assign):
    E = counts.shape[0]
    blk = MOE_BLOCK
    nblk = (counts + blk - 1) // blk
    bend = jnp.cumsum(nblk)
    bstart = bend - nblk
    dest = (bstart * blk)[idx] + rank
    tail = jnp.where(nblk > 0, (bend - 1) * blk, 0).astype(I32)
    cb = MOE_CHUNK_BLOCKS
    n_items_max = E + (n_assign // blk + E) // cb + 1
    per_e = (nblk + cb - 1) // cb
    iend = jnp.cumsum(per_e)
    istart = iend - per_e
    ids = jnp.arange(n_items_max, dtype=I32)
    e_of = jnp.minimum(jnp.searchsorted(iend, ids, side='right'), E - 1).astype(I32)
    valid = ids < iend[-1]
    last_e = jnp.max(jnp.where(per_e > 0, jnp.arange(E, dtype=I32), 0))
    e_of = jnp.where(valid, e_of, last_e)
    local = ids - istart[e_of]
    first_blk = bstart[e_of] + local * cb
    nb = jnp.where(valid, jnp.minimum(cb, nblk[e_of] - local * cb), 0)
    row = jnp.where(valid, first_blk * blk, 0)
    return dest.astype(I32), tail, (e_of, row.astype(I32), nb.astype(I32))


def _moe(x, w_router, b_router, w_gu, b_gu, w_down, b_down, ln_g, ln_b, *, alpha, layer):
    T, D = x.shape
    E = w_router.shape[1]
    idx, gates, rank, counts = _route(x, w_router, b_router)
    n_assign = T * TOP_K
    dest, tail, items = _moe_layout(idx, rank, counts, n_assign)
    dest_flat = dest.reshape(n_assign)
    n_rows = n_assign + E * MOE_BLOCK
    xs = _dispatch(x, dest_flat, _zero_tails(tail, n_rows, D))
    ys = _moe_experts(xs, items, w_gu, b_gu, w_down, b_down, layer)
    return _combine_ln(ys, dest_flat, gates, x, ln_g, ln_b, alpha=alpha)


def _even_mixer_ln(x, w_in, lam_re, lam_im, log_step, b_re, b_im, c_re, c_im, d_skip, w_glu, b_glu,
                   w_gate2, b_gate2, norm_g, w_out, ln_g, ln_b, *, alpha):
    T, D = x.shape
    W = d_skip.shape[0]
    qk = W // 2
    rank = w_gate2.shape[0]
    s4 = W + 2 * qk + W
    w_main = jnp.concatenate([w_in[:, :s4], w_in[:, s4 + rank:]], axis=1).astype(BF16)
    w_gate = jnp.zeros((D, LANES), BF16).at[:, :rank].set(w_in[:, s4:s4 + rank].astype(BF16))
    h, g_low = _proj_in(x, w_main, w_gate)
    tables = _s5_tables(lam_re, lam_im, log_step, b_re, b_im, c_re, c_im, T // S5_CHUNK)
    y = _s5_core(h, W, tables)
    ya = _s5_glu(y, h, d_skip.astype(F32), w_glu.astype(BF16), b_glu.astype(F32))
    yb = _gla(h, g_low, w_gate2, b_gate2, norm_g, width=W)
    w_out_b = w_out.astype(BF16)
    return _proj_ln([ya, yb], [w_out_b[:W], w_out_b[W:]], x, ln_g, ln_b, alpha=alpha, permuted=False)


def _odd_mixer_ln(x, w_qkv, w_o, ln_g, ln_b, *, alpha):
    T, D = x.shape
    qkv = _proj_perm(x, w_qkv.astype(BF16), scaled_cols=D, scale=(D // ATT_HEADS) ** -0.5)
    y = _dilated_attention(qkv, D)
    return _proj_ln([y], [w_o.astype(BF16)], x, ln_g, ln_b, alpha=alpha, permuted=True)


def kernel(x, ab_w_in, s5_lam_re, s5_lam_im, s5_log_step, s5_b_re, s5_b_im, s5_c_re, s5_c_im, s5_d, s5_w_glu, s5_b_glu, gla_w_gate2, gla_b_gate2, gla_norm_g, ab_w_out, c_w_qkv, c_w_o, ln1_g, ln1_b, moe_w_router, moe_b_router, moe_w_gu, moe_b_gu, moe_w_down, moe_b_down, ln2_g, ln2_b):
    bsz, L, D = x.shape
    depth = ln1_g.shape[0]
    alpha = (2 * depth) ** 0.25
    outs = []
    for bi in range(bsz):
        xt = x[bi].astype(F32)
        for layer in range(depth):
            i = layer // 2
            if layer % 2 == 0:
                xt = _even_mixer_ln(xt, ab_w_in[i], s5_lam_re[i], s5_lam_im[i], s5_log_step[i], s5_b_re[i],
                                    s5_b_im[i], s5_c_re[i], s5_c_im[i], s5_d[i], s5_w_glu[i], s5_b_glu[i],
                                    gla_w_gate2[i], gla_b_gate2[i], gla_norm_g[i], ab_w_out[i],
                                    ln1_g[layer], ln1_b[layer], alpha=alpha)
            else:
                xt = _odd_mixer_ln(xt, c_w_qkv[i], c_w_o[i], ln1_g[layer], ln1_b[layer], alpha=alpha)
            xt = _moe(xt, moe_w_router[layer], moe_b_router[layer], moe_w_gu, moe_b_gu,
                      moe_w_down, moe_b_down, ln2_g[layer], ln2_b[layer], alpha=alpha, layer=layer)
        outs.append(xt.astype(x.dtype))
    return outs[0].reshape(1, L, D) if bsz == 1 else jnp.stack(outs)
```

```python
import functools
import math

import jax
import jax.numpy as jnp
from jax import lax
from jax.experimental import pallas as pl
from jax.experimental.pallas import tpu as pltpu

F32 = jnp.float32
BF16 = jnp.bfloat16
I32 = jnp.int32
HIGHEST = lax.Precision.HIGHEST

LANES = 128
VMEM_LIMIT_BYTES = 56 * 1024 * 1024

S5_GROUP = 16
S5_STATE = 64
S5_MAX_RE = -1e-4
S5_CHUNK = 16
GLA_HEADS = 4
GLA_GATE_TEMP = 16.0
GLA_EPS = 1e-6
GLA_BLOCK = 64
GLA_SUB = 16
ATT_HEADS = 16
DIL = 16
DILATED_GROUPS = ((128, 1), (512, 4), (2048, 16))
TOP_K = 4
SWIGLU_LIMIT = 7.0
SWIGLU_ALPHA = 1.702
MOE_BLOCK = 128
MOE_CHUNK_BLOCKS = 10
LN_EPS = 1e-5
NEG_BIG = -1e30


def _cparams(semantics):
    return pltpu.CompilerParams(dimension_semantics=semantics, vmem_limit_bytes=VMEM_LIMIT_BYTES)


def _layer_norm_rows(z, g, b):
    mu = jnp.mean(z, axis=-1, keepdims=True)
    zc = z - mu
    var = jnp.mean(zc * zc, axis=-1, keepdims=True)
    return zc * lax.rsqrt(var + LN_EPS) * g + b


def _proj_in_kernel(x_ref, w_ref, wg_ref, h_ref, g_ref, xb_ref):
    @pl.when(pl.program_id(1) == 0)
    def _():
        xb = x_ref[...].astype(BF16)
        xb_ref[...] = xb
        g_ref[...] = jnp.dot(xb, wg_ref[...], preferred_element_type=F32)

    h_ref[...] = jnp.dot(xb_ref[...], w_ref[...], preferred_element_type=F32)


def _proj_in(x, w_main, w_gate, *, tm=1024, tn=512):
    T, D = x.shape
    N = w_main.shape[1]
    return pl.pallas_call(
        _proj_in_kernel,
        name="proj_in",
        grid=(T // tm, N // tn),
        in_specs=[pl.BlockSpec((tm, D), lambda i, j: (i, 0)),
                  pl.BlockSpec((D, tn), lambda i, j: (0, j)),
                  pl.BlockSpec((D, LANES), lambda i, j: (0, 0))],
        out_specs=[pl.BlockSpec((tm, tn), lambda i, j: (i, j)),
                   pl.BlockSpec((tm, LANES), lambda i, j: (i, 0))],
        out_shape=[jax.ShapeDtypeStruct((T, N), F32), jax.ShapeDtypeStruct((T, LANES), F32)],
        scratch_shapes=[pltpu.VMEM((tm, D), BF16)],
        compiler_params=_cparams(("parallel", "arbitrary")),
    )(x, w_main, w_gate)


def _proj_perm_kernel(x0_ref, x1_ref, w_ref, o_ref, xb_ref, *, n_scaled, scale):
    j = pl.program_id(1)
    na = x0_ref.shape[0]

    @pl.when(j == 0)
    def _():
        xb_ref[:na, :] = x0_ref[...].astype(BF16)
        xb_ref[na:, :] = x1_ref[...].astype(BF16)

    y = jnp.dot(xb_ref[...], w_ref[...], preferred_element_type=F32)
    o_ref[...] = (y * jnp.where(j < n_scaled, scale, 1.0)).astype(o_ref.dtype)


def _proj_perm(x, w, *, scaled_cols, scale, tn=1024):
    T, D = x.shape
    N = w.shape[1]
    na = T // DIL
    x3 = x.reshape(na, DIL * D)
    return pl.pallas_call(
        functools.partial(_proj_perm_kernel, n_scaled=scaled_cols // tn, scale=scale),
        name="proj_perm",
        grid=(DIL // 2, N // tn),
        in_specs=[pl.BlockSpec((na, D), lambda r, j: (0, 2 * r)),
                  pl.BlockSpec((na, D), lambda r, j: (0, 2 * r + 1)),
                  pl.BlockSpec((D, tn), lambda r, j: (0, j))],
        out_specs=pl.BlockSpec((2 * na, tn), lambda r, j: (r, j)),
        out_shape=jax.ShapeDtypeStruct((T, N), BF16),
        scratch_shapes=[pltpu.VMEM((2 * na, D), BF16)],
        compiler_params=_cparams(("parallel", "arbitrary")),
    )(x3, x3, w)


def _proj_ln_kernel(*refs, n_lhs, alpha, n_tiles):
    lhs_refs = refs[:n_lhs]
    w_refs = refs[n_lhs:2 * n_lhs]
    res_ref, g_ref, b_ref, o_ref, acc_ref = refs[2 * n_lhs:]
    j = pl.program_id(1)
    y = jnp.dot(lhs_refs[0][...], w_refs[0][...], preferred_element_type=F32)
    for a_ref, w_ref in zip(lhs_refs[1:], w_refs[1:]):
        y = y + jnp.dot(a_ref[...], w_ref[...], preferred_element_type=F32)
    acc_ref[j] = y

    @pl.when(j == n_tiles - 1)
    def _():
        tn = acc_ref.shape[2]
        z = [alpha * res_ref[:, t * tn:(t + 1) * tn] + acc_ref[t] for t in range(n_tiles)]
        n = float(n_tiles * tn)
        mu = sum(jnp.sum(zt, axis=-1, keepdims=True) for zt in z) / n
        zc = [zt - mu for zt in z]
        var = sum(jnp.sum(zt * zt, axis=-1, keepdims=True) for zt in zc) / n
        rstd = lax.rsqrt(var + LN_EPS)
        for t in range(n_tiles):
            sl = slice(t * tn, (t + 1) * tn)
            o_ref[:, sl] = zc[t] * rstd * g_ref[:, sl] + b_ref[:, sl]


def _proj_ln(lhs, ws, res, g, b, *, alpha, permuted, tm=512, tn=512):
    T, N = res.shape
    n_lhs = len(lhs)
    n_tiles = N // tn
    if permuted:
        tm = T // DIL
        res_in = res.reshape(tm, DIL * N)
        res_spec = pl.BlockSpec((tm, N), lambda i, j: (0, i))
        out_spec = pl.BlockSpec((tm, N), lambda i, j: (0, i))
        out_shape = jax.ShapeDtypeStruct((tm, DIL * N), F32)
    else:
        res_in = res
        res_spec = pl.BlockSpec((tm, N), lambda i, j: (i, 0))
        out_spec = pl.BlockSpec((tm, N), lambda i, j: (i, 0))
        out_shape = jax.ShapeDtypeStruct((T, N), F32)
    in_specs = ([pl.BlockSpec((tm, a.shape[1]), lambda i, j: (i, 0)) for a in lhs]
                + [pl.BlockSpec((w.shape[0], tn), lambda i, j: (0, j)) for w in ws]
                + [res_spec,
                   pl.BlockSpec((1, N), lambda i, j: (0, 0)),
                   pl.BlockSpec((1, N), lambda i, j: (0, 0))])
    out = pl.pallas_call(
        functools.partial(_proj_ln_kernel, n_lhs=n_lhs, alpha=alpha, n_tiles=n_tiles),
        name="proj_ln",
        grid=(T // tm, n_tiles),
        in_specs=in_specs,
        out_specs=out_spec,
        out_shape=out_shape,
        scratch_shapes=[pltpu.VMEM((n_tiles, tm, tn), F32)],
        compiler_params=_cparams(("parallel", "arbitrary")),
    )(*lhs, *ws, res_in, g.reshape(1, N), b.reshape(1, N))
    return out.reshape(T, N)


def _s5_tables(lam_re, lam_im, log_step, b_re, b_im, c_re, c_im, n_chunks):
    C = S5_CHUNK
    G, P = lam_re.shape
    H = b_re.shape[-1]
    lr = jnp.minimum(lam_re.astype(F32), S5_MAX_RE)
    li = lam_im.astype(F32)
    dt = jnp.exp(log_step.astype(F32))[:, None]
    mag = jnp.exp(lr * dt)
    a_re = mag * jnp.cos(li * dt)
    a_im = mag * jnp.sin(li * dt)
    den = lr * lr + li * li
    nr = a_re - 1.0
    f_re = (nr * lr + a_im * li) / den
    f_im = (a_im * lr - nr * li) / den
    br = b_re.astype(F32)
    bi = b_im.astype(F32)
    bb_re = f_re[..., None] * br - f_im[..., None] * bi
    bb_im = f_re[..., None] * bi + f_im[..., None] * br
    pw_re = [jnp.ones_like(a_re)]
    pw_im = [jnp.zeros_like(a_im)]
    for _ in range(C):
        pr, pi = pw_re[-1], pw_im[-1]
        pw_re.append(pr * a_re - pi * a_im)
        pw_im.append(pr * a_im + pi * a_re)
    pw_re = jnp.stack(pw_re)
    pw_im = jnp.stack(pw_im)
    ab_re = pw_re[:C, :, :, None] * bb_re[None] - pw_im[:C, :, :, None] * bb_im[None]
    ab_im = pw_re[:C, :, :, None] * bb_im[None] + pw_im[:C, :, :, None] * bb_re[None]
    cr = c_re.astype(F32)
    ci = c_im.astype(F32)
    z_re = jnp.transpose(ab_re[::-1], (1, 0, 3, 2)).reshape(G, C * H, P)
    z_im = jnp.transpose(ab_im[::-1], (1, 0, 3, 2)).reshape(G, C * H, P)
    zmat = jnp.concatenate([z_re, z_im], axis=-1)
    kern = (jnp.einsum('gop,kgph->kgoh', cr, ab_re, precision=HIGHEST)
            - jnp.einsum('gop,kgph->kgoh', ci, ab_im, precision=HIGHEST))
    lag = jnp.arange(C)[None, :] - jnp.arange(C)[:, None]
    kl = kern[jnp.clip(lag, 0, C - 1)]
    kl = jnp.where((lag >= 0)[:, :, None, None, None], kl, 0.0)
    mmat = jnp.transpose(kl, (2, 0, 4, 1, 3)).reshape(G, C * H, C * H)
    ca_re = cr[None] * pw_re[1:, :, None, :] - ci[None] * pw_im[1:, :, None, :]
    ca_im = cr[None] * pw_im[1:, :, None, :] + ci[None] * pw_re[1:, :, None, :]
    n_re = jnp.transpose(ca_re, (1, 3, 0, 2)).reshape(G, P, C * H)
    n_im = jnp.transpose(-ca_im, (1, 3, 0, 2)).reshape(G, P, C * H)
    nmat = jnp.concatenate([n_re, n_im], axis=1)
    n_steps = max(1, (n_chunks - 1).bit_length())
    qr, qi = pw_re[C], pw_im[C]
    a1, a2 = [], []
    for _ in range(n_steps):
        a1.append(jnp.concatenate([qr, qr], axis=-1))
        a2.append(jnp.concatenate([-qi, qi], axis=-1))
        qr, qi = qr * qr - qi * qi, 2.0 * qr * qi
    a1 = jnp.stack(a1, axis=1)
    a2 = jnp.stack(a2, axis=1)
    return zmat.astype(BF16), mmat.astype(BF16), nmat.astype(BF16), a1, a2


def _s5_kernel(u_ref, z_ref, m_ref, n_ref, a1_ref, a2_ref, y_ref, ub_ref, *, n_steps):
    C, H = S5_CHUNK, S5_GROUP
    nc = u_ref.shape[0] // C
    gpt = LANES // H
    ch = C * H
    for j in range(C):
        ub_ref[:, j * LANES:(j + 1) * LANES] = u_ref[pl.ds(j, nc, stride=C), :].astype(BF16)
    row = lax.broadcasted_iota(I32, (nc, LANES), 0)
    sr = lax.broadcasted_iota(I32, (gpt * LANES, LANES), 0)
    sc = lax.broadcasted_iota(I32, (gpt * LANES, LANES), 1)
    sel_hit = sc == H * (sr // LANES) + sr % H
    sel_grp = (sr % LANES) // H
    pr = lax.broadcasted_iota(I32, (ch, C * LANES), 0)
    pc = lax.broadcasted_iota(I32, (ch, C * LANES), 1)
    put_tile = pc // LANES == pr // H
    put_lane = pc % LANES - pr % H

    def group(gl, first):
        sel = jnp.where(sel_hit & (sel_grp == gl), 1.0, 0.0).astype(BF16)
        half_w = gpt * LANES
        u = jnp.concatenate(
            [jnp.dot(ub_ref[:, t * half_w:(t + 1) * half_w], sel, preferred_element_type=F32)
             for t in range(C * LANES // half_w)], axis=1).astype(BF16)
        s = jnp.dot(u, z_ref[gl], preferred_element_type=F32)
        half = s.shape[1] // 2
        a1 = a1_ref[gl]
        a2 = a2_ref[gl]
        for k in range(n_steps):
            sh = 1 << k
            prev = jnp.where(row >= sh, pltpu.roll(s, sh, axis=0), 0.0)
            s = s + a1[k:k + 1, :] * prev + a2[k:k + 1, :] * pltpu.roll(prev, half, axis=1)
        s_in = jnp.where(row >= 1, pltpu.roll(s, 1, axis=0), 0.0)
        put = jnp.where(put_tile & (put_lane == H * gl), 1.0, 0.0).astype(BF16)
        m_wide = jnp.dot(m_ref[gl], put, preferred_element_type=F32).astype(BF16)
        n_wide = jnp.dot(n_ref[gl], put, preferred_element_type=F32).astype(BF16)
        y = (jnp.dot(u, m_wide, preferred_element_type=F32)
             + jnp.dot(s_in.astype(BF16), n_wide, preferred_element_type=F32))
        for i in range(C):
            rows = pl.ds(i, nc, stride=C)
            piece = y[:, i * LANES:(i + 1) * LANES]
            y_ref[rows, :] = piece if first else y_ref[rows, :] + piece

    group(0, True)

    def later(gl, c):
        group(gl, False)
        return c

    lax.fori_loop(1, gpt, later, 0)


def _s5_core(h, width, tables):
    zmat, mmat, nmat, a1, a2 = tables
    T = h.shape[0]
    G, CH, P2 = zmat.shape
    C, H = S5_CHUNK, S5_GROUP
    nc = T // C
    n_steps = a1.shape[1]
    gpt = LANES // H
    assert CH == C * H and width == G * H
    return pl.pallas_call(
        functools.partial(_s5_kernel, n_steps=n_steps),
        name="s5_core",
        grid=(G // gpt,),
        in_specs=[pl.BlockSpec((T, LANES), lambda m: (0, m)),
                  pl.BlockSpec((gpt, CH, P2), lambda m: (m, 0, 0)),
                  pl.BlockSpec((gpt, CH, CH), lambda m: (m, 0, 0)),
                  pl.BlockSpec((gpt, P2, CH), lambda m: (m, 0, 0)),
                  pl.BlockSpec((gpt, n_steps, P2), lambda m: (m, 0, 0)),
                  pl.BlockSpec((gpt, n_steps, P2), lambda m: (m, 0, 0))],
        out_specs=pl.BlockSpec((T, LANES), lambda m: (0, m)),
        out_shape=jax.ShapeDtypeStruct((T, width), F32),
        scratch_shapes=[pltpu.VMEM((nc, C * LANES), BF16)],
        compiler_params=_cparams(("parallel",)),
    )(h, zmat, mmat, nmat, a1, a2)


def _s5_glu_kernel(y_ref, u_ref, d_ref, w_ref, b_ref, o_ref):
    y = y_ref[...] + d_ref[...] * u_ref[...]
    c0 = math.sqrt(2.0 / math.pi)
    z = 0.5 * y * (1.0 + jnp.tanh(c0 * (y + 0.044715 * (y * y * y))))
    lin = jnp.dot(z.astype(BF16), w_ref[...], preferred_element_type=F32) + b_ref[...]
    o_ref[...] = (z * jax.nn.sigmoid(lin)).astype(o_ref.dtype)


def _s5_glu(y, h, d_skip, w_glu, b_glu, *, tm=512):
    T, W = y.shape
    return pl.pallas_call(
        _s5_glu_kernel,
        name="s5_glu",
        grid=(T // tm,),
        in_specs=[pl.BlockSpec((tm, W), lambda i: (i, 0)),
                  pl.BlockSpec((tm, W), lambda i: (i, 0)),
                  pl.BlockSpec((1, W), lambda i: (0, 0)),
                  pl.BlockSpec((W, W), lambda i: (0, 0)),
                  pl.BlockSpec((1, W), lambda i: (0, 0))],
        out_specs=pl.BlockSpec((tm, W), lambda i: (i, 0)),
        out_shape=jax.ShapeDtypeStruct((T, W), BF16),
        compiler_params=_cparams(("parallel",)),
    )(y, h, d_skip.reshape(1, W), w_glu, b_glu.reshape(1, W))


def _gla_kernel(q_ref, k_ref, v_ref, r_ref, g_ref, w2_ref, b2_ref, ng_ref, o_ref, st_ref, *, dk, dv):
    @pl.when(pl.program_id(0) == 0)
    def _():
        st_ref[...] = jnp.zeros_like(st_ref)

    cb = q_ref.shape[0]
    n_sub = cb // GLA_SUB
    scale = dk ** -0.5
    logit = jnp.dot(g_ref[...], w2_ref[...], preferred_element_type=F32, precision=HIGHEST) + b2_ref[...]
    log_a = (jnp.minimum(logit, 0.0) - jnp.log(1.0 + jnp.exp(-jnp.abs(logit)))) / GLA_GATE_TEMP
    ri = lax.broadcasted_iota(I32, (cb, cb), 0)
    ci = lax.broadcasted_iota(I32, (cb, cb), 1)
    tri = (ri >= ci).astype(F32)
    bcum = jnp.dot(tri, log_a, preferred_element_type=F32, precision=HIGHEST)
    for hh in range(GLA_HEADS):
        ks = slice(hh * dk, (hh + 1) * dk)
        vs = slice(hh * dv, (hh + 1) * dv)
        b = bcum[:, ks]
        q = q_ref[:, ks] * scale
        k = k_ref[:, ks]
        v = v_ref[:, vs].astype(BF16)
        refs = [jnp.zeros((1, dk), F32)] + [b[a * GLA_SUB - 1:a * GLA_SUB, :] for a in range(1, n_sub)]
        refmat = jnp.concatenate([jnp.broadcast_to(r, (GLA_SUB, dk)) for r in refs], axis=0)
        qe = (q * jnp.exp(b - refmat)).astype(BF16)
        st = st_ref[hh]
        o_inter = lax.dot_general((q * jnp.exp(b)).astype(BF16), st.astype(BF16),
                                  (((1,), (1,)), ((), ())), preferred_element_type=F32)
        o_rows = []
        for a in range(n_sub):
            hi = (a + 1) * GLA_SUB
            ke = (k[:hi] * jnp.exp(refs[a] - b[:hi])).astype(BF16)
            att = lax.dot_general(qe[a * GLA_SUB:hi], ke, (((1,), (1,)), ((), ())),
                                  preferred_element_type=F32)
            row_a = lax.broadcasted_iota(I32, (GLA_SUB, hi), 0) + a * GLA_SUB
            att = jnp.where(lax.broadcasted_iota(I32, (GLA_SUB, hi), 1) <= row_a, att, 0.0)
            o_rows.append(jnp.dot(att.astype(BF16), v[:hi], preferred_element_type=F32))
        o = jnp.concatenate(o_rows, axis=0) + o_inter
        o = o * lax.rsqrt(jnp.mean(o * o, axis=-1, keepdims=True) + GLA_EPS) * ng_ref[...]
        r = r_ref[:, vs]
        o_ref[:, vs] = (o * (r * jax.nn.sigmoid(r))).astype(o_ref.dtype)
        b_last = b[cb - 1:cb, :]
        kd = (k * jnp.exp(b_last - b)).astype(BF16)
        upd = lax.dot_general(v, kd, (((0,), (0,)), ((), ())), preferred_element_type=F32)
        st_ref[hh] = st * jnp.exp(b_last) + upd


def _gla(h, g_low, w_gate2, b_gate2, norm_g, *, width):
    T = h.shape[0]
    qk = width // 2
    dk = qk // GLA_HEADS
    dv = width // GLA_HEADS
    cb = GLA_BLOCK
    w2 = jnp.zeros((LANES, qk), F32).at[:w_gate2.shape[0]].set(w_gate2.astype(F32))
    return pl.pallas_call(
        functools.partial(_gla_kernel, dk=dk, dv=dv),
        name="gla",
        grid=(T // cb,),
        in_specs=[pl.BlockSpec((cb, qk), lambda i: (i, 2)),
                  pl.BlockSpec((cb, qk), lambda i: (i, 3)),
                  pl.BlockSpec((cb, width), lambda i: (i, 2)),
                  pl.BlockSpec((cb, width), lambda i: (i, 3)),
                  pl.BlockSpec((cb, LANES), lambda i: (i, 0)),
                  pl.BlockSpec((LANES, qk), lambda i: (0, 0)),
                  pl.BlockSpec((1, qk), lambda i: (0, 0)),
                  pl.BlockSpec((1, dv), lambda i: (0, 0))],
        out_specs=pl.BlockSpec((cb, width), lambda i: (i, 0)),
        out_shape=jax.ShapeDtypeStruct((T, width), BF16),
        scratch_shapes=[pltpu.VMEM((GLA_HEADS, dv, dk), F32)],
        compiler_params=_cparams(("arbitrary",)),
    )(h, h, h, h, g_low, w2, b_gate2.reshape(1, qk).astype(F32), norm_g.reshape(1, dv).astype(F32))


def _attn_kernel(q_ref, k_ref, v_ref, o_ref, acc_ref, m_ref, l_ref, v1_ref, pat1_ref, pat4_ref, pat16_ref, *, na):
    dh = v_ref.shape[1]
    v1_ref[:, :dh] = v_ref[...]
    v1_ref[:, dh:] = jnp.ones_like(v_ref)
    acc_ref[...] = jnp.zeros_like(acc_ref)
    m_ref[...] = jnp.full_like(m_ref, NEG_BIG)
    l_ref[...] = jnp.zeros_like(l_ref)

    def token_offsets(rows, r_step, n, axis):
        shape = (n, 1) if axis == 0 else (1, n)
        idx = lax.broadcasted_iota(I32, shape, axis)
        c = idx // rows
        return DIL * (idx - c * rows) + r_step * c

    def delta_pattern(n_chunks, q_rows, k_rows, r_step):
        return (token_offsets(q_rows, r_step, n_chunks * q_rows, 0)
                - token_offsets(k_rows, r_step, n_chunks * k_rows, 1))

    def cat(ref, starts, rows):
        return jnp.concatenate([ref[pl.ds(pl.multiple_of(s, 16), rows), :] for s in starts], axis=0)

    def attend(blocks, q_rows, k_rows, window, pat_ref):
        loaded = []
        for q_starts, k_starts, a_q, a_k in blocks:
            loaded.append((cat(q_ref, q_starts, q_rows), cat(k_ref, k_starts, k_rows), cat(v1_ref, k_starts, k_rows),
                           cat(m_ref, q_starts, q_rows), cat(l_ref, q_starts, q_rows),
                           cat(acc_ref, q_starts, q_rows)))
        results = []
        for (q_starts, k_starts, a_q, a_k), (qb, kb, vb, m_old, l_old, acc_old) in zip(blocks, loaded):
            s = lax.dot_general(qb, kb, (((1,), (1,)), ((), ())), preferred_element_type=F32)
            off = DIL * (a_q - a_k)
            pat = pat_ref[...]
            s = jnp.where((pat >= -off) & (pat <= window - off), s, NEG_BIG)
            m_new = jnp.maximum(m_old, jnp.max(s, axis=1, keepdims=True))
            alpha = jnp.exp(m_old - m_new)
            p = jnp.exp(s - m_new[:, :1])
            pv = jnp.dot(p.astype(BF16), vb, preferred_element_type=F32)
            l_new = alpha * l_old + pv[:, dh:]
            acc_new = alpha * acc_old + pv[:, :dh]
            results.append((m_new, l_new, acc_new))
        for (q_starts, *_), (m_new, l_new, acc_new) in zip(blocks, results):
            for c, st in enumerate(q_starts):
                st = pl.multiple_of(st, 16)
                rs = slice(c * q_rows, (c + 1) * q_rows)
                m_ref[pl.ds(st, q_rows), :] = m_new[rs]
                l_ref[pl.ds(st, q_rows), :] = l_new[rs]
                acc_ref[pl.ds(st, q_rows), :] = acc_new[rs]

    w1 = DILATED_GROUPS[0][0]
    u1 = 2
    pat1_ref[...] = delta_pattern(DIL, 16, 32, 1)

    def body1(it, carry):
        blocks = []
        for j in range(u1):
            a0 = (it * u1 + j) * 16
            ak = jnp.maximum(a0 - 16, 0)
            blocks.append(([r * na + a0 for r in range(DIL)], [r * na + ak for r in range(DIL)], a0, ak))
        attend(blocks, 16, 32, w1, pat1_ref)
        return carry

    lax.fori_loop(0, na // (16 * u1), body1, 0)

    w4 = DILATED_GROUPS[1][0]
    pat4_ref[...] = delta_pattern(4, 32, 64, 4)

    def body4(it, carry):
        a0 = it * 32
        ak = jnp.maximum(a0 - 32, 0)
        blocks = [([(rho + 4 * sg) * na + a0 for sg in range(4)], [(rho + 4 * sg) * na + ak for sg in range(4)],
                   a0, ak) for rho in range(4)]
        attend(blocks, 32, 64, w4, pat4_ref)
        return carry

    lax.fori_loop(0, na // 32, body4, 0)

    w16 = DILATED_GROUPS[2][0]
    u16 = 4
    pat16_ref[...] = delta_pattern(1, 128, 256, 0)

    def body16(it, carry):
        rg = it // (na // 128)
        a0 = (it - rg * (na // 128)) * 128
        ak = jnp.maximum(a0 - 128, 0)
        blocks = [([(rg * u16 + j) * na + a0], [(rg * u16 + j) * na + ak], a0, ak) for j in range(u16)]
        attend(blocks, 128, 256, w16, pat16_ref)
        return carry

    lax.fori_loop(0, (DIL // u16) * (na // 128), body16, 0)

    o_ref[...] = (acc_ref[...] / l_ref[...]).astype(o_ref.dtype)


def _dilated_attention(qkv, d_model):
    T = qkv.shape[0]
    dh = d_model // ATT_HEADS
    assert dh == LANES
    na = T // DIL
    return pl.pallas_call(
        functools.partial(_attn_kernel, na=na),
        name="dilated_attn",
        grid=(ATT_HEADS,),
        in_specs=[pl.BlockSpec((T, dh), lambda h: (0, h)),
                  pl.BlockSpec((T, dh), lambda h: (0, ATT_HEADS + h)),
                  pl.BlockSpec((T, dh), lambda h: (0, 2 * ATT_HEADS + h))],
        out_specs=pl.BlockSpec((T, dh), lambda h: (0, h)),
        out_shape=jax.ShapeDtypeStruct((T, d_model), BF16),
        scratch_shapes=[pltpu.VMEM((T, dh), F32), pltpu.VMEM((T, dh), F32), pltpu.VMEM((T, dh), F32),
                        pltpu.VMEM((T, 2 * dh), BF16), pltpu.VMEM((256, 512), I32), pltpu.VMEM((128, 256), I32), pltpu.VMEM((128, 256), I32)],
        compiler_params=_cparams(("parallel",)),
    )(qkv, qkv, qkv)


def _route_kernel(x_ref, w_ref, b_ref, idx_ref, gate_ref, rank_ref, cnt_ref, carry_ref):
    @pl.when(pl.program_id(0) == 0)
    def _():
        carry_ref[...] = jnp.zeros_like(carry_ref)

    tb = x_ref.shape[0]
    x = x_ref[...]
    w = w_ref[...]
    x_hi, w_hi = x.astype(BF16), w.astype(BF16)
    x_lo = (x - x_hi.astype(F32)).astype(BF16)
    w_lo = (w - w_hi.astype(F32)).astype(BF16)
    lg = (jnp.dot(x_hi, w_hi, preferred_element_type=F32) + jnp.dot(x_hi, w_lo, preferred_element_type=F32)
          + jnp.dot(x_lo, w_hi, preferred_element_type=F32)) + b_ref[...]
    lane = lax.broadcasted_iota(I32, lg.shape, 1)
    vals, hots = [], []
    idx_out = jnp.zeros(lg.shape, I32)
    for k in range(TOP_K):
        m = jnp.max(lg, axis=1, keepdims=True)
        sel = jnp.min(jnp.where(lg == m, lane, LANES), axis=1, keepdims=True)
        hot = lane == sel
        vals.append(m)
        hots.append(hot)
        idx_out = jnp.where(lane == k, sel, idx_out)
        lg = jnp.where(hot, -jnp.inf, lg)
    ex = [jnp.exp(v - vals[0]) for v in vals]
    den = sum(ex)
    gate_out = jnp.zeros(lg.shape, F32)
    for k in range(TOP_K):
        gate_out = jnp.where(lane == k, ex[k] / den, gate_out)
    chosen = sum(h.astype(F32) for h in hots)
    ri = lax.broadcasted_iota(I32, (tb, tb), 0)
    ci = lax.broadcasted_iota(I32, (tb, tb), 1)
    before = jnp.dot((ri > ci).astype(BF16), chosen.astype(BF16), preferred_element_type=F32) + carry_ref[...]
    rank_out = jnp.zeros(lg.shape, I32)
    for k in range(TOP_K):
        rk = jnp.sum(jnp.where(hots[k], before, 0.0), axis=1, keepdims=True).astype(I32)
        rank_out = jnp.where(lane == k, rk, rank_out)
    idx_ref[...] = idx_out
    gate_ref[...] = gate_out
    rank_ref[...] = rank_out
    carry_ref[...] = carry_ref[...] + jnp.sum(chosen, axis=0, keepdims=True)
    cnt_ref[...] = carry_ref[...].astype(I32)


def _route(x, w_router, b_router, *, tb=512):
    T, D = x.shape
    E = w_router.shape[1]
    wr = jnp.zeros((D, LANES), F32).at[:, :E].set(w_router.astype(F32))
    br = jnp.full((1, LANES), NEG_BIG, F32).at[0, :E].set(b_router.astype(F32))
    row = lambda dt: jax.ShapeDtypeStruct((T, LANES), dt)
    idx, gate, rank, cnt = pl.pallas_call(
        _route_kernel,
        name="moe_route",
        grid=(T // tb,),
        in_specs=[pl.BlockSpec((tb, D), lambda i: (i, 0)),
                  pl.BlockSpec((D, LANES), lambda i: (0, 0)),
                  pl.BlockSpec((1, LANES), lambda i: (0, 0))],
        out_specs=[pl.BlockSpec((tb, LANES), lambda i: (i, 0)),
                   pl.BlockSpec((tb, LANES), lambda i: (i, 0)),
                   pl.BlockSpec((tb, LANES), lambda i: (i, 0)),
                   pl.BlockSpec((1, LANES), lambda i: (0, 0))],
        out_shape=[row(I32), row(F32), row(I32), jax.ShapeDtypeStruct((1, LANES), I32)],
        scratch_shapes=[pltpu.VMEM((1, LANES), F32)],
        compiler_params=_cparams(("arbitrary",)),
    )(x, wr, br)
    return idx[:, :TOP_K], gate[:, :TOP_K], rank[:, :TOP_K], cnt[0, :E]


def _zero_tail_kernel(row_ref, o_ref, z_ref, sem):
    z_ref[...] = jnp.zeros_like(z_ref)
    n = row_ref.shape[0]

    def copy(e):
        return pltpu.make_async_copy(z_ref, o_ref.at[pl.ds(pl.multiple_of(row_ref[e], MOE_BLOCK), MOE_BLOCK)], sem)

    def start(e, c):
        copy(e).start()
        return c

    def wait(e, c):
        copy(e).wait()
        return c

    lax.fori_loop(0, n, start, 0)
    lax.fori_loop(0, n, wait, 0)


def _zero_tails(tail_rows, n_rows, d):
    return pl.pallas_call(
        _zero_tail_kernel,
        name="moe_zero_tails",
        grid_spec=pltpu.PrefetchScalarGridSpec(
            num_scalar_prefetch=1, grid=(1,),
            in_specs=[],
            out_specs=pl.BlockSpec(memory_space=pl.ANY),
            scratch_shapes=[pltpu.VMEM((MOE_BLOCK, d), F32), pltpu.SemaphoreType.DMA(())]),
        out_shape=jax.ShapeDtypeStruct((n_rows, d), F32),
        compiler_params=_cparams(("arbitrary",)),
    )(tail_rows)


def _dispatch_kernel(dest_ref, x_ref, xs_in_ref, xs_ref, sem):
    del xs_in_ref
    tb = x_ref.shape[0]
    base = pl.program_id(0) * tb * TOP_K

    def copy(i, k):
        return pltpu.make_async_copy(x_ref.at[pl.ds(i, 1)],
                                     xs_ref.at[pl.ds(dest_ref[base + i * TOP_K + k], 1)], sem)

    def start(i, c):
        for k in range(TOP_K):
            copy(i, k).start()
        return c

    lax.fori_loop(0, tb, start, 0)
    for _ in range(TOP_K):
        pltpu.make_async_copy(x_ref, xs_ref.at[pl.ds(0, tb)], sem).wait()


def _dispatch(x, dest_flat, xs_init, *, tb=256):
    T, D = x.shape
    return pl.pallas_call(
        _dispatch_kernel,
        name="moe_dispatch",
        grid_spec=pltpu.PrefetchScalarGridSpec(
            num_scalar_prefetch=1, grid=(T // tb,),
            in_specs=[pl.BlockSpec((tb, D), lambda i, dest: (i, 0)),
                      pl.BlockSpec(memory_space=pl.ANY)],
            out_specs=pl.BlockSpec(memory_space=pl.ANY),
            scratch_shapes=[pltpu.SemaphoreType.DMA(())]),
        out_shape=jax.ShapeDtypeStruct(xs_init.shape, xs_init.dtype),
        input_output_aliases={2: 0},
        compiler_params=_cparams(("arbitrary",)),
    )(dest_flat, x, xs_init)


def _moe_kernel(item_e_ref, item_row_ref, item_nb_ref, xs_ref, wg_ref, wu_ref, bg_ref, bu_ref, wd_ref, bd_ref,
                ys_ref, xbuf_ref, acc_ref, wgb_ref, wub_ref, wdb_ref, in_sem, out_sem, *, n_f):
    it = pl.program_id(0)
    f = pl.program_id(1)
    nb = item_nb_ref[it]
    row0 = item_row_ref[it]
    blk = MOE_BLOCK

    def acc_rows(j):
        return pl.ds(pl.multiple_of(j * blk, blk), blk)

    def in_copy(j):
        src = xs_ref.at[pl.ds(pl.multiple_of(row0 + j * blk, blk), blk)]
        return pltpu.make_async_copy(src, acc_ref.at[acc_rows(j)], in_sem)

    def out_copy(j):
        dst = ys_ref.at[pl.ds(pl.multiple_of(row0 + j * blk, blk), blk)]
        return pltpu.make_async_copy(acc_ref.at[acc_rows(j)], dst, out_sem)

    def for_blocks(fn):
        def body(j, c):
            fn(j)
            return c

        lax.fori_loop(0, nb, body, 0)

    @pl.when((f == 0) & (nb > 0))
    def _():
        for_blocks(lambda j: in_copy(j).start())
        for_blocks(lambda j: in_copy(j).wait())

        def to_bf16(j):
            xbuf_ref[acc_rows(j), :] = acc_ref[acc_rows(j), :].astype(BF16)
            acc_ref[acc_rows(j), :] = jnp.broadcast_to(bd_ref[...], (blk, acc_ref.shape[1]))

        for_blocks(to_bf16)

    @pl.when(nb > 0)
    def _():
        def compute(r0, n_rows, wg, wu, wd):
            rows = pl.ds(pl.multiple_of(r0, blk), n_rows)
            xb = xbuf_ref[rows, :]
            hg = jnp.dot(xb, wg, preferred_element_type=F32) + bg_ref[...]
            hu = jnp.dot(xb, wu, preferred_element_type=F32) + bu_ref[...]
            gate = jnp.minimum(hg, SWIGLU_LIMIT)
            up = jnp.clip(hu, -SWIGLU_LIMIT, SWIGLU_LIMIT)
            act = (up + 1.0) * (gate * jax.nn.sigmoid(SWIGLU_ALPHA * gate))
            acc_ref[rows, :] += jnp.dot(act.astype(BF16), wd, preferred_element_type=F32)

        def cast_weights():
            wg = wg_ref[...].astype(BF16)
            wu = wu_ref[...].astype(BF16)
            wd = wd_ref[...].astype(BF16)
            wgb_ref[...] = wg
            wub_ref[...] = wu
            wdb_ref[...] = wd
            return wg, wu, wd

        def compute_cached(r0, n_rows):
            compute(r0, n_rows, wgb_ref[...], wub_ref[...], wdb_ref[...])

        n4 = nb // 4

        @pl.when(n4 > 0)
        def _():
            compute(0, 4 * blk, *cast_weights())

            def quad(j, c):
                compute_cached(j * (4 * blk), 4 * blk)
                return c

            lax.fori_loop(1, n4, quad, 0)

        @pl.when(n4 == 0)
        def _():
            cast_weights()

        @pl.when((nb & 2) != 0)
        def _():
            compute_cached(n4 * (4 * blk), 2 * blk)

        @pl.when((nb & 1) != 0)
        def _():
            compute_cached((nb - 1) * blk, blk)

    @pl.when((f == n_f - 1) & (nb > 0))
    def _():
        for_blocks(lambda j: out_copy(j).start())
        for_blocks(lambda j: out_copy(j).wait())


def _moe_experts(xs, items, w_gu, b_gu, w_down, b_down, layer, *, tf=512):
    item_e, item_row, item_nb = items
    n_items = item_e.shape[0]
    P, D = xs.shape
    L, E, _, F2 = w_gu.shape
    F = F2 // 2
    n_f = F // tf
    rc = MOE_CHUNK_BLOCKS * MOE_BLOCK

    def fe(f, nb_ref, it):
        return jnp.where(nb_ref[it] > 0, f, n_f - 1)

    in_specs = [
        pl.BlockSpec(memory_space=pl.ANY),
        pl.BlockSpec((None, None, D, tf), lambda it, f, e, r, nb: (layer, e[it], 0, fe(f, nb, it))),
        pl.BlockSpec((None, None, D, tf), lambda it, f, e, r, nb: (layer, e[it], 0, n_f + fe(f, nb, it))),
        pl.BlockSpec((None, None, 1, tf), lambda it, f, e, r, nb: (layer, e[it], 0, fe(f, nb, it))),
        pl.BlockSpec((None, None, 1, tf), lambda it, f, e, r, nb: (layer, e[it], 0, n_f + fe(f, nb, it))),
        pl.BlockSpec((None, None, tf, D), lambda it, f, e, r, nb: (layer, e[it], fe(f, nb, it), 0)),
        pl.BlockSpec((None, None, 1, D), lambda it, f, e, r, nb: (layer, e[it], 0, 0)),
    ]
    return pl.pallas_call(
        functools.partial(_moe_kernel, n_f=n_f),
        name="moe_experts",
        grid_spec=pltpu.PrefetchScalarGridSpec(
            num_scalar_prefetch=3, grid=(n_items, n_f),
            in_specs=in_specs,
            out_specs=pl.BlockSpec(memory_space=pl.ANY),
            scratch_shapes=[pltpu.VMEM((rc, D), BF16),
                            pltpu.VMEM((rc, D), F32),
                            pltpu.VMEM((D, tf), BF16),
                            pltpu.VMEM((D, tf), BF16),
                            pltpu.VMEM((tf, D), BF16),
                            pltpu.SemaphoreType.DMA(()),
                            pltpu.SemaphoreType.DMA(())]),
        out_shape=jax.ShapeDtypeStruct((P, D), F32),
        compiler_params=_cparams(("arbitrary", "arbitrary")),
    )(item_e, item_row, item_nb, xs, w_gu, w_gu, b_gu.reshape(L, E, 1, F2), b_gu.reshape(L, E, 1, F2),
      w_down, b_down.reshape(L, E, 1, D))


def _combine_kernel(dest_ref, ys_ref, gate_ref, x_ref, g_ref, b_ref, o_ref, buf_ref, sem, *, alpha):
    tb = x_ref.shape[0]
    step = pl.program_id(0)
    slot = step % 2

    def copy(blk, sl, i, k):
        src = ys_ref.at[pl.ds(dest_ref[(blk * tb + i) * TOP_K + k], 1)]
        return pltpu.make_async_copy(src, buf_ref.at[sl, k, pl.ds(i, 1)], sem.at[sl])

    def for_rows(fn):
        def body(i, c):
            for k in range(TOP_K):
                fn(i, k)
            return c

        lax.fori_loop(0, tb, body, 0)

    @pl.when(step == 0)
    def _():
        for_rows(lambda i, k: copy(0, 0, i, k).start())

    @pl.when(step + 1 < pl.num_programs(0))
    def _():
        for_rows(lambda i, k: copy(step + 1, 1 - slot, i, k).start())

    for k in range(TOP_K):
        pltpu.make_async_copy(ys_ref.at[pl.ds(0, tb)], buf_ref.at[slot, k], sem.at[slot]).wait()
    gate = gate_ref[...]
    ffn = gate[:, 0:1] * buf_ref[slot, 0]
    for k in range(1, TOP_K):
        ffn = ffn + gate[:, k:k + 1] * buf_ref[slot, k]
    o_ref[...] = _layer_norm_rows(alpha * x_ref[...] + ffn, g_ref[...], b_ref[...])


def _combine_ln(ys, dest_flat, gates, x, g, b, *, alpha, tb=128):
    T, D = x.shape
    gate_pad = jnp.zeros((T, LANES), F32).at[:, :TOP_K].set(gates)
    return pl.pallas_call(
        functools.partial(_combine_kernel, alpha=alpha),
        name="moe_combine",
        grid_spec=pltpu.PrefetchScalarGridSpec(
            num_scalar_prefetch=1, grid=(T // tb,),
            in_specs=[pl.BlockSpec(memory_space=pl.ANY),
                      pl.BlockSpec((tb, LANES), lambda i, dest: (i, 0)),
                      pl.BlockSpec((tb, D), lambda i, dest: (i, 0)),
                      pl.BlockSpec((1, D), lambda i, dest: (0, 0)),
                      pl.BlockSpec((1, D), lambda i, dest: (0, 0))],
            out_specs=pl.BlockSpec((tb, D), lambda i, dest: (i, 0)),
            scratch_shapes=[pltpu.VMEM((2, TOP_K, tb, D), F32), pltpu.SemaphoreType.DMA((2,))]),
        out_shape=jax.ShapeDtypeStruct((T, D), F32),
        compiler_params=_cparams(("arbitrary",)),
    )(dest_flat, ys, gate_pad, x, g.reshape(1, D), b.reshape(1, D))


def _moe_layout(idx, rank, counts, n_assign):
    E = counts.shape[0]
    blk = MOE_BLOCK
    nblk = (counts + blk - 1) // blk
    bend = jnp.cumsum(nblk)
    bstart = bend - nblk
    dest = (bstart * blk)[idx] + rank
    tail = jnp.where(nblk > 0, (bend - 1) * blk, 0).astype(I32)
    cb = MOE_CHUNK_BLOCKS
    n_items_max = E + (n_assign // blk + E) // cb + 1
    per_e = (nblk + cb - 1) // cb
    iend = jnp.cumsum(per_e)
    istart = iend - per_e
    ids = jnp.arange(n_items_max, dtype=I32)
    e_of = jnp.minimum(jnp.searchsorted(iend, ids, side='right'), E - 1).astype(I32)
    valid = ids < iend[-1]
    last_e = jnp.max(jnp.where(per_e > 0, jnp.arange(E, dtype=I32), 0))
    e_of = jnp.where(valid, e_of, last_e)
    local = ids - istart[e_of]
    first_blk = bstart[e_of] + local * cb
    nb = jnp.where(valid, jnp.minimum(cb, nblk[e_of] - local * cb), 0)
    row = jnp.where(valid, first_blk * blk, 0)
    return dest.astype(I32), tail, (e_of, row.astype(I32), nb.astype(I32))


def _moe(x, w_router, b_router, w_gu, b_gu, w_down, b_down, ln_g, ln_b, *, alpha, layer):
    T, D = x.shape
    E = w_router.shape[1]
    idx, gates, rank, counts = _route(x, w_router, b_router)
    n_assign = T * TOP_K
    dest, tail, items = _moe_layout(idx, rank, counts, n_assign)
    dest_flat = dest.reshape(n_assign)
    n_rows = n_assign + E * MOE_BLOCK
    xs = _dispatch(x, dest_flat, _zero_tails(tail, n_rows, D))
    ys = _moe_experts(xs, items, w_gu, b_gu, w_down, b_down, layer)
    return _combine_ln(ys, dest_flat, gates, x, ln_g, ln_b, alpha=alpha)


def _even_mixer_ln(x, w_in, lam_re, lam_im, log_step, b_re, b_im, c_re, c_im, d_skip, w_glu, b_glu,
                   w_gate2, b_gate2, norm_g, w_out, ln_g, ln_b, *, alpha):
    T, D = x.shape
    W = d_skip.shape[0]
    qk = W // 2
    rank = w_gate2.shape[0]
    s4 = W + 2 * qk + W
    w_main = jnp.concatenate([w_in[:, :s4], w_in[:, s4 + rank:]], axis=1).astype(BF16)
    w_gate = jnp.zeros((D, LANES), BF16).at[:, :rank].set(w_in[:, s4:s4 + rank].astype(BF16))
    h, g_low = _proj_in(x, w_main, w_gate)
    tables = _s5_tables(lam_re, lam_im, log_step, b_re, b_im, c_re, c_im, T // S5_CHUNK)
    y = _s5_core(h, W, tables)
    ya = _s5_glu(y, h, d_skip.astype(F32), w_glu.astype(BF16), b_glu.astype(F32))
    yb = _gla(h, g_low, w_gate2, b_gate2, norm_g, width=W)
    w_out_b = w_out.astype(BF16)
    return _proj_ln([ya, yb], [w_out_b[:W], w_out_b[W:]], x, ln_g, ln_b, alpha=alpha, permuted=False)


def _odd_mixer_ln(x, w_qkv, w_o, ln_g, ln_b, *, alpha):
    T, D = x.shape
    qkv = _proj_perm(x, w_qkv.astype(BF16), scaled_cols=D, scale=(D // ATT_HEADS) ** -0.5)
    y = _dilated_attention(qkv, D)
    return _proj_ln([y], [w_o.astype(BF16)], x, ln_g, ln_b, alpha=alpha, permuted=True)


def kernel(x, ab_w_in, s5_lam_re, s5_lam_im, s5_log_step, s5_b_re, s5_b_im, s5_c_re, s5_c_im, s5_d, s5_w_glu, s5_b_glu, gla_w_gate2, gla_b_gate2, gla_norm_g, ab_w_out, c_w_qkv, c_w_o, ln1_g, ln1_b, moe_w_router, moe_b_router, moe_w_gu, moe_b_gu, moe_w_down, moe_b_down, ln2_g, ln2_b):
    bsz, L, D = x.shape
    depth = ln1_g.shape[0]
    alpha = (2 * depth) ** 0.25
    outs = []
    for bi in range(bsz):
        xt = x[bi].astype(F32)
        for layer in range(depth):
            i = layer // 2
            if layer % 2 == 0:
                xt = _even_mixer_ln(xt, ab_w_in[i], s5_lam_re[i], s5_lam_im[i], s5_log_step[i], s5_b_re[i],
                                    s5_b_im[i], s5_c_re[i], s5_c_im[i], s5_d[i], s5_w_glu[i], s5_b_glu[i],
                                    gla_w_gate2[i], gla_b_gate2[i], gla_norm_g[i], ab_w_out[i],
                                    ln1_g[layer], ln1_b[layer], alpha=alpha)
            else:
                xt = _odd_mixer_ln(xt, c_w_qkv[i], c_w_o[i], ln1_g[layer], ln1_b[layer], alpha=alpha)
            xt = _moe(xt, moe_w_router[layer], moe_b_router[layer], moe_w_gu, moe_b_gu,
                      moe_w_down, moe_b_down, ln2_g[layer], ln2_b[layer], alpha=alpha, layer=layer)
        outs.append(xt.astype(x.dtype))
    return outs[0].reshape(1, L, D) if bsz == 1 else jnp.stack(outs)
```

```python
import functools
import math

import jax
import jax.numpy as jnp
from jax import lax
from jax.experimental import pallas as pl
from jax.experimental.pallas import tpu as pltpu

F32 = jnp.float32
BF16 = jnp.bfloat16
I32 = jnp.int32
HIGHEST = lax.Precision.HIGHEST

LANES = 128
VMEM_LIMIT_BYTES = 56 * 1024 * 1024

S5_GROUP = 16
S5_STATE = 64
S5_MAX_RE = -1e-4
S5_CHUNK = 16
GLA_HEADS = 4
GLA_GATE_TEMP = 16.0
GLA_EPS = 1e-6
GLA_BLOCK = 64
GLA_SUB = 16
ATT_HEADS = 16
DIL = 16
DILATED_GROUPS = ((128, 1), (512, 4), (2048, 16))
TOP_K = 4
SWIGLU_LIMIT = 7.0
SWIGLU_ALPHA = 1.702
MOE_BLOCK = 128
MOE_CHUNK_BLOCKS = 10
LN_EPS = 1e-5
NEG_BIG = -1e30


def _cparams(semantics):
    return pltpu.CompilerParams(dimension_semantics=semantics, vmem_limit_bytes=VMEM_LIMIT_BYTES)


def _layer_norm_rows(z, g, b):
    mu = jnp.mean(z, axis=-1, keepdims=True)
    zc = z - mu
    var = jnp.mean(zc * zc, axis=-1, keepdims=True)
    return zc * lax.rsqrt(var + LN_EPS) * g + b


def _proj_in_kernel(x_ref, w_ref, wg_ref, h_ref, g_ref, xb_ref):
    @pl.when(pl.program_id(1) == 0)
    def _():
        xb = x_ref[...].astype(BF16)
        xb_ref[...] = xb
        g_ref[...] = jnp.dot(xb, wg_ref[...], preferred_element_type=F32)

    h_ref[...] = jnp.dot(xb_ref[...], w_ref[...], preferred_element_type=F32)


def _proj_in(x, w_main, w_gate, *, tm=1024, tn=512):
    T, D = x.shape
    N = w_main.shape[1]
    return pl.pallas_call(
        _proj_in_kernel,
        name="proj_in",
        grid=(T // tm, N // tn),
        in_specs=[pl.BlockSpec((tm, D), lambda i, j: (i, 0)),
                  pl.BlockSpec((D, tn), lambda i, j: (0, j)),
                  pl.BlockSpec((D, LANES), lambda i, j: (0, 0))],
        out_specs=[pl.BlockSpec((tm, tn), lambda i, j: (i, j)),
                   pl.BlockSpec((tm, LANES), lambda i, j: (i, 0))],
        out_shape=[jax.ShapeDtypeStruct((T, N), F32), jax.ShapeDtypeStruct((T, LANES), F32)],
        scratch_shapes=[pltpu.VMEM((tm, D), BF16)],
        compiler_params=_cparams(("parallel", "arbitrary")),
    )(x, w_main, w_gate)


def _proj_scaled_kernel(x_ref, w_ref, o_ref, xb_ref, *, n_scaled, scale):
    j = pl.program_id(1)

    @pl.when(j == 0)
    def _():
        xb_ref[...] = x_ref[...].astype(BF16)

    y = jnp.dot(xb_ref[...], w_ref[...], preferred_element_type=F32)
    o_ref[...] = (y * jnp.where(j < n_scaled, scale, 1.0)).astype(o_ref.dtype)


def _proj_scaled(x, w, *, scaled_cols, scale, tm=1024, tn=1024):
    T, D = x.shape
    N = w.shape[1]
    return pl.pallas_call(
        functools.partial(_proj_scaled_kernel, n_scaled=scaled_cols // tn, scale=scale),
        name="proj_qkv",
        grid=(T // tm, N // tn),
        in_specs=[pl.BlockSpec((tm, D), lambda i, j: (i, 0)),
                  pl.BlockSpec((D, tn), lambda i, j: (0, j))],
        out_specs=pl.BlockSpec((tm, tn), lambda i, j: (i, j)),
        out_shape=jax.ShapeDtypeStruct((T, N), BF16),
        scratch_shapes=[pltpu.VMEM((tm, D), BF16)],
        compiler_params=_cparams(("parallel", "arbitrary")),
    )(x, w)


def _proj_ln_kernel(*refs, n_lhs, alpha, n_tiles, to_residue_major):
    lhs_refs = refs[:n_lhs]
    w_refs = refs[n_lhs:2 * n_lhs]
    res_ref, g_ref, b_ref, o_ref, acc_ref = refs[2 * n_lhs:2 * n_lhs + 5]
    j = pl.program_id(1)
    y = jnp.dot(lhs_refs[0][...], w_refs[0][...], preferred_element_type=F32)
    for a_ref, w_ref in zip(lhs_refs[1:], w_refs[1:]):
        y = y + jnp.dot(a_ref[...], w_ref[...], preferred_element_type=F32)
    acc_ref[j] = y

    @pl.when(j == n_tiles - 1)
    def _():
        tn = acc_ref.shape[2]
        z = [alpha * res_ref[:, t * tn:(t + 1) * tn] + acc_ref[t] for t in range(n_tiles)]
        n = float(n_tiles * tn)
        mu = sum(jnp.sum(zt, axis=-1, keepdims=True) for zt in z) / n
        zc = [zt - mu for zt in z]
        var = sum(jnp.sum(zt * zt, axis=-1, keepdims=True) for zt in zc) / n
        rstd = lax.rsqrt(var + LN_EPS)
        out = [zc[t] * rstd * g_ref[:, t * tn:(t + 1) * tn] + b_ref[:, t * tn:(t + 1) * tn] for t in range(n_tiles)]
        if not to_residue_major:
            for t in range(n_tiles):
                o_ref[:, t * tn:(t + 1) * tn] = out[t]
        else:
            rows_ref = refs[-1]
            per_res = rows_ref.shape[1] // DIL
            for c in range(rows_ref.shape[0]):
                t, off = divmod(c * LANES, tn)
                rows_ref[c] = out[t][:, off:off + LANES]
                for r in range(DIL):
                    o_ref[r, :, c * LANES:(c + 1) * LANES] = rows_ref[c, pl.ds(r, per_res, stride=DIL), :]


def _proj_ln(lhs, ws, res, g, b, *, alpha, to_residue_major, tm=512, tn=512):
    T, N = res.shape
    n_lhs = len(lhs)
    n_tiles = N // tn
    scratch = [pltpu.VMEM((n_tiles, tm, tn), F32)]
    if to_residue_major:
        na = T // DIL
        out_spec = pl.BlockSpec((DIL, tm // DIL, N), lambda i, j: (0, i, 0))
        out_shape = jax.ShapeDtypeStruct((DIL, na, N), F32)
        scratch.append(pltpu.VMEM((N // LANES, tm, LANES), F32))
    else:
        out_spec = pl.BlockSpec((tm, N), lambda i, j: (i, 0))
        out_shape = jax.ShapeDtypeStruct((T, N), F32)
    in_specs = ([pl.BlockSpec((tm, a.shape[1]), lambda i, j: (i, 0)) for a in lhs]
                + [pl.BlockSpec((w.shape[0], tn), lambda i, j: (0, j)) for w in ws]
                + [pl.BlockSpec((tm, N), lambda i, j: (i, 0)),
                   pl.BlockSpec((1, N), lambda i, j: (0, 0)),
                   pl.BlockSpec((1, N), lambda i, j: (0, 0))])
    out = pl.pallas_call(
        functools.partial(_proj_ln_kernel, n_lhs=n_lhs, alpha=alpha, n_tiles=n_tiles,
                          to_residue_major=to_residue_major),
        name="proj_ln",
        grid=(T // tm, n_tiles),
        in_specs=in_specs,
        out_specs=out_spec,
        out_shape=out_shape,
        scratch_shapes=scratch,
        compiler_params=_cparams(("parallel", "arbitrary")),
    )(*lhs, *ws, res, g.reshape(1, N), b.reshape(1, N))
    return out.reshape(T, N)


def _s5_tables(lam_re, lam_im, log_step, b_re, b_im, c_re, c_im, n_chunks):
    C = S5_CHUNK
    G, P = lam_re.shape
    H = b_re.shape[-1]
    lr = jnp.minimum(lam_re.astype(F32), S5_MAX_RE)
    li = lam_im.astype(F32)
    dt = jnp.exp(log_step.astype(F32))[:, None]
    kk = jnp.arange(C + 1, dtype=F32)[:, None, None]
    pw_mag = jnp.exp(kk * (lr * dt))
    pw_re = pw_mag * jnp.cos(kk * (li * dt))
    pw_im = pw_mag * jnp.sin(kk * (li * dt))
    a_re, a_im = pw_re[1], pw_im[1]
    den = lr * lr + li * li
    nr = a_re - 1.0
    f_re = (nr * lr + a_im * li) / den
    f_im = (a_im * lr - nr * li) / den
    br = b_re.astype(F32)
    bi = b_im.astype(F32)
    bb_re = f_re[..., None] * br - f_im[..., None] * bi
    bb_im = f_re[..., None] * bi + f_im[..., None] * br
    ab_re = pw_re[:C, :, :, None] * bb_re[None] - pw_im[:C, :, :, None] * bb_im[None]
    ab_im = pw_re[:C, :, :, None] * bb_im[None] + pw_im[:C, :, :, None] * bb_re[None]
    cr = c_re.astype(F32)
    ci = c_im.astype(F32)
    z_re = jnp.transpose(ab_re[::-1], (1, 0, 3, 2)).reshape(G, C * H, P)
    z_im = jnp.transpose(ab_im[::-1], (1, 0, 3, 2)).reshape(G, C * H, P)
    zmat = jnp.concatenate([z_re, z_im], axis=-1)
    kern = (jnp.einsum('gop,kgph->kgoh', cr, ab_re, precision=HIGHEST)
            - jnp.einsum('gop,kgph->kgoh', ci, ab_im, precision=HIGHEST))
    lag = jnp.arange(C)[None, :] - jnp.arange(C)[:, None]
    kl = kern[jnp.clip(lag, 0, C - 1)]
    kl = jnp.where((lag >= 0)[:, :, None, None, None], kl, 0.0)
    mmat = jnp.transpose(kl, (2, 0, 4, 1, 3)).reshape(G, C * H, C * H)
    ca_re = cr[None] * pw_re[1:, :, None, :] - ci[None] * pw_im[1:, :, None, :]
    ca_im = cr[None] * pw_im[1:, :, None, :] + ci[None] * pw_re[1:, :, None, :]
    n_re = jnp.transpose(ca_re, (1, 3, 0, 2)).reshape(G, P, C * H)
    n_im = jnp.transpose(-ca_im, (1, 3, 0, 2)).reshape(G, P, C * H)
    nmat = jnp.concatenate([n_re, n_im], axis=1)
    n_steps = max(1, (n_chunks - 1).bit_length())
    qr, qi = pw_re[C], pw_im[C]
    a1, a2 = [], []
    for _ in range(n_steps):
        a1.append(jnp.concatenate([qr, qr], axis=-1))
        a2.append(jnp.concatenate([-qi, qi], axis=-1))
        qr, qi = qr * qr - qi * qi, 2.0 * qr * qi
    a1 = jnp.stack(a1, axis=1)
    a2 = jnp.stack(a2, axis=1)
    return zmat.astype(BF16), mmat.astype(BF16), nmat.astype(BF16), a1, a2


def _s5_kernel(u_ref, z_ref, m_ref, n_ref, a1_ref, a2_ref, y_ref, ub_ref, *, n_steps):
    C, H = S5_CHUNK, S5_GROUP
    nc = u_ref.shape[0] // C
    gpt = LANES // H
    ch = C * H
    for j in range(C):
        ub_ref[:, j * LANES:(j + 1) * LANES] = u_ref[pl.ds(j, nc, stride=C), :].astype(BF16)
    row = lax.broadcasted_iota(I32, (nc, LANES), 0)
    sr = lax.broadcasted_iota(I32, (gpt * LANES, LANES), 0)
    sc = lax.broadcasted_iota(I32, (gpt * LANES, LANES), 1)
    sel_hit = sc == H * (sr // LANES) + sr % H
    sel_grp = (sr % LANES) // H
    pr = lax.broadcasted_iota(I32, (ch, C * LANES), 0)
    pc = lax.broadcasted_iota(I32, (ch, C * LANES), 1)
    put_tile = pc // LANES == pr // H
    put_lane = pc % LANES - pr % H

    def group(gl, first):
        sel = jnp.where(sel_hit & (sel_grp == gl), 1.0, 0.0).astype(BF16)
        half_w = gpt * LANES
        u = jnp.concatenate(
            [jnp.dot(ub_ref[:, t * half_w:(t + 1) * half_w], sel, preferred_element_type=F32)
             for t in range(C * LANES // half_w)], axis=1).astype(BF16)
        s = jnp.dot(u, z_ref[gl], preferred_element_type=F32)
        half = s.shape[1] // 2
        a1 = a1_ref[gl]
        a2 = a2_ref[gl]
        for k in range(n_steps):
            sh = 1 << k
            prev = jnp.where(row >= sh, pltpu.roll(s, sh, axis=0), 0.0)
            s = s + a1[k:k + 1, :] * prev + a2[k:k + 1, :] * pltpu.roll(prev, half, axis=1)
        s_in = jnp.where(row >= 1, pltpu.roll(s, 1, axis=0), 0.0)
        put = jnp.where(put_tile & (put_lane == H * gl), 1.0, 0.0).astype(BF16)
        m_wide = jnp.dot(m_ref[gl], put, preferred_element_type=F32).astype(BF16)
        n_wide = jnp.dot(n_ref[gl], put, preferred_element_type=F32).astype(BF16)
        y = (jnp.dot(u, m_wide, preferred_element_type=F32)
             + jnp.dot(s_in.astype(BF16), n_wide, preferred_element_type=F32))
        for i in range(C):
            rows = pl.ds(i, nc, stride=C)
            piece = y[:, i * LANES:(i + 1) * LANES]
            y_ref[rows, :] = piece if first else y_ref[rows, :] + piece

    group(0, True)

    def later(gl, c):
        group(gl, False)
        return c

    lax.fori_loop(1, gpt, later, 0)


def _s5_core(h, width, tables):
    zmat, mmat, nmat, a1, a2 = tables
    T = h.shape[0]
    G, CH, P2 = zmat.shape
    C, H = S5_CHUNK, S5_GROUP
    nc = T // C
    n_steps = a1.shape[1]
    gpt = LANES // H
    assert CH == C * H and width == G * H
    return pl.pallas_call(
        functools.partial(_s5_kernel, n_steps=n_steps),
        name="s5_core",
        grid=(G // gpt,),
        in_specs=[pl.BlockSpec((T, LANES), lambda m: (0, m)),
                  pl.BlockSpec((gpt, CH, P2), lambda m: (m, 0, 0)),
                  pl.BlockSpec((gpt, CH, CH), lambda m: (m, 0, 0)),
                  pl.BlockSpec((gpt, P2, CH), lambda m: (m, 0, 0)),
                  pl.BlockSpec((gpt, n_steps, P2), lambda m: (m, 0, 0)),
                  pl.BlockSpec((gpt, n_steps, P2), lambda m: (m, 0, 0))],
        out_specs=pl.BlockSpec((T, LANES), lambda m: (0, m)),
        out_shape=jax.ShapeDtypeStruct((T, width), F32),
        scratch_shapes=[pltpu.VMEM((nc, C * LANES), BF16)],
        compiler_params=_cparams(("parallel",)),
    )(h, zmat, mmat, nmat, a1, a2)


def _s5_glu_kernel(y_ref, u_ref, d_ref, w_ref, b_ref, o_ref):
    y = y_ref[...] + d_ref[...] * u_ref[...]
    c0 = math.sqrt(2.0 / math.pi)
    z = 0.5 * y * (1.0 + jnp.tanh(c0 * (y + 0.044715 * (y * y * y))))
    lin = jnp.dot(z.astype(BF16), w_ref[...], preferred_element_type=F32) + b_ref[...]
    o_ref[...] = (z * jax.nn.sigmoid(lin)).astype(o_ref.dtype)


def _s5_glu(y, h, d_skip, w_glu, b_glu, *, tm=512):
    T, W = y.shape
    return pl.pallas_call(
        _s5_glu_kernel,
        name="s5_glu",
        grid=(T // tm,),
        in_specs=[pl.BlockSpec((tm, W), lambda i: (i, 0)),
                  pl.BlockSpec((tm, W), lambda i: (i, 0)),
                  pl.BlockSpec((1, W), lambda i: (0, 0)),
                  pl.BlockSpec((W, W), lambda i: (0, 0)),
                  pl.BlockSpec((1, W), lambda i: (0, 0))],
        out_specs=pl.BlockSpec((tm, W), lambda i: (i, 0)),
        out_shape=jax.ShapeDtypeStruct((T, W), BF16),
        compiler_params=_cparams(("parallel",)),
    )(y, h, d_skip.reshape(1, W), w_glu, b_glu.reshape(1, W))


def _gla_kernel(q_ref, k_ref, v_ref, r_ref, g_ref, w2_ref, b2_ref, ng_ref, o_ref, st_ref, *, dk, dv):
    @pl.when(pl.program_id(0) == 0)
    def _():
        st_ref[...] = jnp.zeros_like(st_ref)

    cb = q_ref.shape[0]
    n_sub = cb // GLA_SUB
    scale = dk ** -0.5
    logit = jnp.dot(g_ref[...], w2_ref[...], preferred_element_type=F32, precision=HIGHEST) + b2_ref[...]
    log_a = (jnp.minimum(logit, 0.0) - jnp.log(1.0 + jnp.exp(-jnp.abs(logit)))) / GLA_GATE_TEMP
    ri = lax.broadcasted_iota(I32, (cb, cb), 0)
    ci = lax.broadcasted_iota(I32, (cb, cb), 1)
    tri = (ri >= ci).astype(F32)
    bcum = jnp.dot(tri, log_a, preferred_element_type=F32, precision=HIGHEST)
    for hh in range(GLA_HEADS):
        ks = slice(hh * dk, (hh + 1) * dk)
        vs = slice(hh * dv, (hh + 1) * dv)
        b = bcum[:, ks]
        q = q_ref[:, ks] * scale
        k = k_ref[:, ks]
        v = v_ref[:, vs].astype(BF16)
        refs = [jnp.zeros((1, dk), F32)] + [b[a * GLA_SUB - 1:a * GLA_SUB, :] for a in range(1, n_sub)]
        refmat = jnp.concatenate([jnp.broadcast_to(r, (GLA_SUB, dk)) for r in refs], axis=0)
        qe = (q * jnp.exp(b - refmat)).astype(BF16)
        st = st_ref[hh]
        o_inter = lax.dot_general((q * jnp.exp(b)).astype(BF16), st.astype(BF16),
                                  (((1,), (1,)), ((), ())), preferred_element_type=F32)
        o_rows = []
        for a in range(n_sub):
            hi = (a + 1) * GLA_SUB
            ke = (k[:hi] * jnp.exp(refs[a] - b[:hi])).astype(BF16)
            att = lax.dot_general(qe[a * GLA_SUB:hi], ke, (((1,), (1,)), ((), ())),
                                  preferred_element_type=F32)
            row_a = lax.broadcasted_iota(I32, (GLA_SUB, hi), 0) + a * GLA_SUB
            att = jnp.where(lax.broadcasted_iota(I32, (GLA_SUB, hi), 1) <= row_a, att, 0.0)
            o_rows.append(jnp.dot(att.astype(BF16), v[:hi], preferred_element_type=F32))
        o = jnp.concatenate(o_rows, axis=0) + o_inter
        o = o * lax.rsqrt(jnp.mean(o * o, axis=-1, keepdims=True) + GLA_EPS) * ng_ref[...]
        r = r_ref[:, vs]
        o_ref[:, vs] = (o * (r * jax.nn.sigmoid(r))).astype(o_ref.dtype)
        b_last = b[cb - 1:cb, :]
        kd = (k * jnp.exp(b_last - b)).astype(BF16)
        upd = lax.dot_general(v, kd, (((0,), (0,)), ((), ())), preferred_element_type=F32)
        st_ref[hh] = st * jnp.exp(b_last) + upd


def _gla(h, g_low, w_gate2, b_gate2, norm_g, *, width):
    T = h.shape[0]
    qk = width // 2
    dk = qk // GLA_HEADS
    dv = width // GLA_HEADS
    cb = GLA_BLOCK
    w2 = jnp.zeros((LANES, qk), F32).at[:w_gate2.shape[0]].set(w_gate2.astype(F32))
    return pl.pallas_call(
        functools.partial(_gla_kernel, dk=dk, dv=dv),
        name="gla",
        grid=(T // cb,),
        in_specs=[pl.BlockSpec((cb, qk), lambda i: (i, 2)),
                  pl.BlockSpec((cb, qk), lambda i: (i, 3)),
                  pl.BlockSpec((cb, width), lambda i: (i, 2)),
                  pl.BlockSpec((cb, width), lambda i: (i, 3)),
                  pl.BlockSpec((cb, LANES), lambda i: (i, 0)),
                  pl.BlockSpec((LANES, qk), lambda i: (0, 0)),
                  pl.BlockSpec((1, qk), lambda i: (0, 0)),
                  pl.BlockSpec((1, dv), lambda i: (0, 0))],
        out_specs=pl.BlockSpec((cb, width), lambda i: (i, 0)),
        out_shape=jax.ShapeDtypeStruct((T, width), BF16),
        scratch_shapes=[pltpu.VMEM((GLA_HEADS, dv, dk), F32)],
        compiler_params=_cparams(("arbitrary",)),
    )(h, h, h, h, g_low, w2, b_gate2.reshape(1, qk).astype(F32), norm_g.reshape(1, dv).astype(F32))


def _attn_kernel(q_ref, k_ref, v_ref, o_ref, acc_ref, m_ref, l_ref, v1_ref, pat1_ref, pat4_ref, pat16_ref, *, na):
    dh = v_ref.shape[1]
    v1_ref[:, :dh] = v_ref[...]
    v1_ref[:, dh:] = jnp.ones_like(v_ref)
    acc_ref[...] = jnp.zeros_like(acc_ref)
    m_ref[...] = jnp.full_like(m_ref, NEG_BIG)
    l_ref[...] = jnp.zeros_like(l_ref)

    def token_offsets(rows, r_step, n, axis):
        shape = (n, 1) if axis == 0 else (1, n)
        idx = lax.broadcasted_iota(I32, shape, axis)
        c = idx // rows
        return DIL * (idx - c * rows) + r_step * c

    def delta_pattern(n_chunks, q_rows, k_rows, r_step):
        return (token_offsets(q_rows, r_step, n_chunks * q_rows, 0)
                - token_offsets(k_rows, r_step, n_chunks * k_rows, 1))

    def cat(ref, starts, rows):
        return jnp.concatenate([ref[pl.ds(pl.multiple_of(s, 16), rows), :] for s in starts], axis=0)

    def attend(blocks, q_rows, k_rows, window, pat_ref):
        loaded = []
        for q_starts, k_starts, a_q, a_k in blocks:
            loaded.append((cat(q_ref, q_starts, q_rows), cat(k_ref, k_starts, k_rows), cat(v1_ref, k_starts, k_rows),
                           cat(m_ref, q_starts, q_rows), cat(l_ref, q_starts, q_rows),
                           cat(acc_ref, q_starts, q_rows)))
        results = []
        for (q_starts, k_starts, a_q, a_k), (qb, kb, vb, m_old, l_old, acc_old) in zip(blocks, loaded):
            s = lax.dot_general(qb, kb, (((1,), (1,)), ((), ())), preferred_element_type=F32)
            off = DIL * (a_q - a_k)
            pat = pat_ref[...]
            s = jnp.where((pat >= -off) & (pat <= window - off), s, NEG_BIG)
            m_new = jnp.maximum(m_old, jnp.max(s, axis=1, keepdims=True))
            alpha = jnp.exp(m_old - m_new)
            p = jnp.exp(s - m_new[:, :1])
            pv = jnp.dot(p.astype(BF16), vb, preferred_element_type=F32)
            l_new = alpha * l_old + pv[:, dh:]
            acc_new = alpha * acc_old + pv[:, :dh]
            results.append((m_new, l_new, acc_new))
        for (q_starts, *_), (m_new, l_new, acc_new) in zip(blocks, results):
            for c, st in enumerate(q_starts):
                st = pl.multiple_of(st, 16)
                rs = slice(c * q_rows, (c + 1) * q_rows)
                m_ref[pl.ds(st, q_rows), :] = m_new[rs]
                l_ref[pl.ds(st, q_rows), :] = l_new[rs]
                acc_ref[pl.ds(st, q_rows), :] = acc_new[rs]

    w1 = DILATED_GROUPS[0][0]
    u1 = 2
    pat1_ref[...] = delta_pattern(DIL, 16, 32, 1)

    def body1(it, carry):
        blocks = []
        for j in range(u1):
            a0 = (it * u1 + j) * 16
            ak = jnp.maximum(a0 - 16, 0)
            blocks.append(([r * na + a0 for r in range(DIL)], [r * na + ak for r in range(DIL)], a0, ak))
        attend(blocks, 16, 32, w1, pat1_ref)
        return carry

    lax.fori_loop(0, na // (16 * u1), body1, 0)

    w4 = DILATED_GROUPS[1][0]
    pat4_ref[...] = delta_pattern(4, 32, 64, 4)

    def body4(it, carry):
        a0 = it * 32
        ak = jnp.maximum(a0 - 32, 0)
        blocks = [([(rho + 4 * sg) * na + a0 for sg in range(4)], [(rho + 4 * sg) * na + ak for sg in range(4)],
                   a0, ak) for rho in range(4)]
        attend(blocks, 32, 64, w4, pat4_ref)
        return carry

    lax.fori_loop(0, na // 32, body4, 0)

    w16 = DILATED_GROUPS[2][0]
    u16 = 4
    pat16_ref[...] = delta_pattern(1, 128, 256, 0)

    def body16(it, carry):
        rg = it // (na // 128)
        a0 = (it - rg * (na // 128)) * 128
        ak = jnp.maximum(a0 - 128, 0)
        blocks = [([(rg * u16 + j) * na + a0], [(rg * u16 + j) * na + ak], a0, ak) for j in range(u16)]
        attend(blocks, 128, 256, w16, pat16_ref)
        return carry

    lax.fori_loop(0, (DIL // u16) * (na // 128), body16, 0)

    o_ref[...] = (acc_ref[...] / l_ref[...]).astype(o_ref.dtype)


def _dilated_attention(qkv, d_model):
    T = qkv.shape[0]
    dh = d_model // ATT_HEADS
    assert dh == LANES
    na = T // DIL
    return pl.pallas_call(
        functools.partial(_attn_kernel, na=na),
        name="dilated_attn",
        grid=(ATT_HEADS,),
        in_specs=[pl.BlockSpec((T, dh), lambda h: (0, h)),
                  pl.BlockSpec((T, dh), lambda h: (0, ATT_HEADS + h)),
                  pl.BlockSpec((T, dh), lambda h: (0, 2 * ATT_HEADS + h))],
        out_specs=pl.BlockSpec((T, dh), lambda h: (0, h)),
        out_shape=jax.ShapeDtypeStruct((T, d_model), BF16),
        scratch_shapes=[pltpu.VMEM((T, dh), F32), pltpu.VMEM((T, dh), F32), pltpu.VMEM((T, dh), F32),
                        pltpu.VMEM((T, 2 * dh), BF16), pltpu.VMEM((256, 512), I32), pltpu.VMEM((128, 256), I32), pltpu.VMEM((128, 256), I32)],
        compiler_params=_cparams(("parallel",)),
    )(qkv, qkv, qkv)


def _route_kernel(x_ref, w_ref, b_ref, idx_ref, gate_ref, rank_ref, cnt_ref, carry_ref):
    @pl.when(pl.program_id(0) == 0)
    def _():
        carry_ref[...] = jnp.zeros_like(carry_ref)

    tb = x_ref.shape[0]
    x = x_ref[...]
    w = w_ref[...]
    x_hi, w_hi = x.astype(BF16), w.astype(BF16)
    x_lo = (x - x_hi.astype(F32)).astype(BF16)
    w_lo = (w - w_hi.astype(F32)).astype(BF16)
    lg = (jnp.dot(x_hi, w_hi, preferred_element_type=F32) + jnp.dot(x_hi, w_lo, preferred_element_type=F32)
          + jnp.dot(x_lo, w_hi, preferred_element_type=F32)) + b_ref[...]
    lane = lax.broadcasted_iota(I32, lg.shape, 1)
    vals, hots = [], []
    idx_out = jnp.zeros(lg.shape, I32)
    for k in range(TOP_K):
        m = jnp.max(lg, axis=1, keepdims=True)
        sel = jnp.min(jnp.where(lg == m, lane, LANES), axis=1, keepdims=True)
        hot = lane == sel
        vals.append(m)
        hots.append(hot)
        idx_out = jnp.where(lane == k, sel, idx_out)
        lg = jnp.where(hot, -jnp.inf, lg)
    ex = [jnp.exp(v - vals[0]) for v in vals]
    den = sum(ex)
    gate_out = jnp.zeros(lg.shape, F32)
    for k in range(TOP_K):
        gate_out = jnp.where(lane == k, ex[k] / den, gate_out)
    chosen = sum(h.astype(F32) for h in hots)
    ri = lax.broadcasted_iota(I32, (tb, tb), 0)
    ci = lax.broadcasted_iota(I32, (tb, tb), 1)
    before = jnp.dot((ri > ci).astype(BF16), chosen.astype(BF16), preferred_element_type=F32) + carry_ref[...]
    rank_out = jnp.zeros(lg.shape, I32)
    for k in range(TOP_K):
        rk = jnp.sum(jnp.where(hots[k], before, 0.0), axis=1, keepdims=True).astype(I32)
        rank_out = jnp.where(lane == k, rk, rank_out)
    idx_ref[...] = idx_out
    gate_ref[...] = gate_out
    rank_ref[...] = rank_out
    carry_ref[...] = carry_ref[...] + jnp.sum(chosen, axis=0, keepdims=True)
    cnt_ref[...] = carry_ref[...].astype(I32)


def _route(x, w_router, b_router, *, tb=512):
    T, D = x.shape
    E = w_router.shape[1]
    wr = jnp.zeros((D, LANES), F32).at[:, :E].set(w_router.astype(F32))
    br = jnp.full((1, LANES), NEG_BIG, F32).at[0, :E].set(b_router.astype(F32))
    row = lambda dt: jax.ShapeDtypeStruct((T, LANES), dt)
    idx, gate, rank, cnt = pl.pallas_call(
        _route_kernel,
        name="moe_route",
        grid=(T // tb,),
        in_specs=[pl.BlockSpec((tb, D), lambda i: (i, 0)),
                  pl.BlockSpec((D, LANES), lambda i: (0, 0)),
                  pl.BlockSpec((1, LANES), lambda i: (0, 0))],
        out_specs=[pl.BlockSpec((tb, LANES), lambda i: (i, 0)),
                   pl.BlockSpec((tb, LANES), lambda i: (i, 0)),
                   pl.BlockSpec((tb, LANES), lambda i: (i, 0)),
                   pl.BlockSpec((1, LANES), lambda i: (0, 0))],
        out_shape=[row(I32), row(F32), row(I32), jax.ShapeDtypeStruct((1, LANES), I32)],
        scratch_shapes=[pltpu.VMEM((1, LANES), F32)],
        compiler_params=_cparams(("arbitrary",)),
    )(x, wr, br)
    return idx[:, :TOP_K], gate[:, :TOP_K], rank[:, :TOP_K], cnt[0, :E]


def _zero_tail_kernel(row_ref, o_ref, z_ref, sem):
    z_ref[...] = jnp.zeros_like(z_ref)
    n = row_ref.shape[0]

    def copy(e):
        return pltpu.make_async_copy(z_ref, o_ref.at[pl.ds(pl.multiple_of(row_ref[e], MOE_BLOCK), MOE_BLOCK)], sem)

    def start(e, c):
        copy(e).start()
        return c

    def wait(e, c):
        copy(e).wait()
        return c

    lax.fori_loop(0, n, start, 0)
    lax.fori_loop(0, n, wait, 0)


def _zero_tails(tail_rows, n_rows, d):
    return pl.pallas_call(
        _zero_tail_kernel,
        name="moe_zero_tails",
        grid_spec=pltpu.PrefetchScalarGridSpec(
            num_scalar_prefetch=1, grid=(1,),
            in_specs=[],
            out_specs=pl.BlockSpec(memory_space=pl.ANY),
            scratch_shapes=[pltpu.VMEM((MOE_BLOCK, d), F32), pltpu.SemaphoreType.DMA(())]),
        out_shape=jax.ShapeDtypeStruct((n_rows, d), F32),
        compiler_params=_cparams(("arbitrary",)),
    )(tail_rows)


def _dispatch_kernel(dest_ref, x_ref, xs_in_ref, xs_ref, sem):
    del xs_in_ref
    tb = x_ref.shape[0]
    base = pl.program_id(0) * tb * TOP_K

    def copy(i, k):
        return pltpu.make_async_copy(x_ref.at[pl.ds(i, 1)],
                                     xs_ref.at[pl.ds(dest_ref[base + i * TOP_K + k], 1)], sem)

    def start(i, c):
        for k in range(TOP_K):
            copy(i, k).start()
        return c

    lax.fori_loop(0, tb, start, 0)
    for _ in range(TOP_K):
        pltpu.make_async_copy(x_ref, xs_ref.at[pl.ds(0, tb)], sem).wait()


def _dispatch(x, dest_flat, xs_init, *, tb=256):
    T, D = x.shape
    return pl.pallas_call(
        _dispatch_kernel,
        name="moe_dispatch",
        grid_spec=pltpu.PrefetchScalarGridSpec(
            num_scalar_prefetch=1, grid=(T // tb,),
            in_specs=[pl.BlockSpec((tb, D), lambda i, dest: (i, 0)),
                      pl.BlockSpec(memory_space=pl.ANY)],
            out_specs=pl.BlockSpec(memory_space=pl.ANY),
            scratch_shapes=[pltpu.SemaphoreType.DMA(())]),
        out_shape=jax.ShapeDtypeStruct(xs_init.shape, xs_init.dtype),
        input_output_aliases={2: 0},
        compiler_params=_cparams(("arbitrary",)),
    )(dest_flat, x, xs_init)


def _moe_kernel(item_e_ref, item_row_ref, item_nb_ref, xs_ref, wg_ref, wu_ref, bg_ref, bu_ref, wd_ref, bd_ref,
                ys_ref, xbuf_ref, acc_ref, wgb_ref, wub_ref, wdb_ref, in_sem, out_sem, *, n_f):
    it = pl.program_id(0)
    f = pl.program_id(1)
    nb = item_nb_ref[it]
    row0 = item_row_ref[it]
    blk = MOE_BLOCK

    def acc_rows(j):
        return pl.ds(pl.multiple_of(j * blk, blk), blk)

    def in_copy(j):
        src = xs_ref.at[pl.ds(pl.multiple_of(row0 + j * blk, blk), blk)]
        return pltpu.make_async_copy(src, acc_ref.at[acc_rows(j)], in_sem)

    def out_copy(j):
        dst = ys_ref.at[pl.ds(pl.multiple_of(row0 + j * blk, blk), blk)]
        return pltpu.make_async_copy(acc_ref.at[acc_rows(j)], dst, out_sem)

    def for_blocks(fn):
        def body(j, c):
            fn(j)
            return c

        lax.fori_loop(0, nb, body, 0)

    @pl.when((f == 0) & (nb > 0))
    def _():
        for_blocks(lambda j: in_copy(j).start())
        for_blocks(lambda j: in_copy(j).wait())

        def to_bf16(j):
            xbuf_ref[acc_rows(j), :] = acc_ref[acc_rows(j), :].astype(BF16)
            acc_ref[acc_rows(j), :] = jnp.broadcast_to(bd_ref[...], (blk, acc_ref.shape[1]))

        for_blocks(to_bf16)

    @pl.when(nb > 0)
    def _():
        def compute(r0, n_rows, wg, wu, wd):
            rows = pl.ds(pl.multiple_of(r0, blk), n_rows)
            xb = xbuf_ref[rows, :]
            hg = jnp.dot(xb, wg, preferred_element_type=F32) + bg_ref[...]
            hu = jnp.dot(xb, wu, preferred_element_type=F32) + bu_ref[...]
            gate = jnp.minimum(hg, SWIGLU_LIMIT)
            up = jnp.clip(hu, -SWIGLU_LIMIT, SWIGLU_LIMIT)
            act = (up + 1.0) * (gate * jax.nn.sigmoid(SWIGLU_ALPHA * gate))
            acc_ref[rows, :] += jnp.dot(act.astype(BF16), wd, preferred_element_type=F32)

            @pl.when(f == n_f - 1)
            def _():
                for b in range(n_rows // blk):
                    out_copy(r0 // blk + b).start()

        def cast_weights():
            wg = wg_ref[...].astype(BF16)
            wu = wu_ref[...].astype(BF16)
            wd = wd_ref[...].astype(BF16)
            wgb_ref[...] = wg
            wub_ref[...] = wu
            wdb_ref[...] = wd
            return wg, wu, wd

        def compute_cached(r0, n_rows):
            compute(r0, n_rows, wgb_ref[...], wub_ref[...], wdb_ref[...])

        n4 = nb // 4

        @pl.when(n4 > 0)
        def _():
            compute(0, 4 * blk, *cast_weights())

            def quad(j, c):
                compute_cached(j * (4 * blk), 4 * blk)
                return c

            lax.fori_loop(1, n4, quad, 0)

        @pl.when(n4 == 0)
        def _():
            cast_weights()

        @pl.when((nb & 2) != 0)
        def _():
            compute_cached(n4 * (4 * blk), 2 * blk)

        @pl.when((nb & 1) != 0)
        def _():
            compute_cached((nb - 1) * blk, blk)

    @pl.when((f == n_f - 1) & (nb > 0))
    def _():
        for_blocks(lambda j: out_copy(j).wait())


def _moe_experts(xs, items, w_gu, b_gu, w_down, b_down, layer, *, tf=512):
    item_e, item_row, item_nb = items
    n_items = item_e.shape[0]
    P, D = xs.shape
    L, E, _, F2 = w_gu.shape
    F = F2 // 2
    n_f = F // tf
    rc = MOE_CHUNK_BLOCKS * MOE_BLOCK

    def fe(f, nb_ref, it):
        return jnp.where(nb_ref[it] > 0, f, n_f - 1)

    in_specs = [
        pl.BlockSpec(memory_space=pl.ANY),
        pl.BlockSpec((None, None, D, tf), lambda it, f, e, r, nb: (layer, e[it], 0, fe(f, nb, it))),
        pl.BlockSpec((None, None, D, tf), lambda it, f, e, r, nb: (layer, e[it], 0, n_f + fe(f, nb, it))),
        pl.BlockSpec((None, None, 1, tf), lambda it, f, e, r, nb: (layer, e[it], 0, fe(f, nb, it))),
        pl.BlockSpec((None, None, 1, tf), lambda it, f, e, r, nb: (layer, e[it], 0, n_f + fe(f, nb, it))),
        pl.BlockSpec((None, None, tf, D), lambda it, f, e, r, nb: (layer, e[it], fe(f, nb, it), 0)),
        pl.BlockSpec((None, None, 1, D), lambda it, f, e, r, nb: (layer, e[it], 0, 0)),
    ]
    return pl.pallas_call(
        functools.partial(_moe_kernel, n_f=n_f),
        name="moe_experts",
        grid_spec=pltpu.PrefetchScalarGridSpec(
            num_scalar_prefetch=3, grid=(n_items, n_f),
            in_specs=in_specs,
            out_specs=pl.BlockSpec(memory_space=pl.ANY),
            scratch_shapes=[pltpu.VMEM((rc, D), BF16),
                            pltpu.VMEM((rc, D), F32),
                            pltpu.VMEM((D, tf), BF16),
                            pltpu.VMEM((D, tf), BF16),
                            pltpu.VMEM((tf, D), BF16),
                            pltpu.SemaphoreType.DMA(()),
                            pltpu.SemaphoreType.DMA(())]),
        out_shape=jax.ShapeDtypeStruct((P, D), F32),
        compiler_params=_cparams(("arbitrary", "arbitrary")),
    )(item_e, item_row, item_nb, xs, w_gu, w_gu, b_gu.reshape(L, E, 1, F2), b_gu.reshape(L, E, 1, F2),
      w_down, b_down.reshape(L, E, 1, D))


def _combine_kernel(dest_ref, ys_ref, gate_ref, x_ref, g_ref, b_ref, o_ref, buf_ref, sem, *scratch,
                    alpha, to_natural):
    tb = o_ref.shape[0]
    step = pl.program_id(0)
    slot = step % 2
    per_res = tb // DIL

    def token(blk, i):
        if not to_natural:
            return blk * tb + i
        n_a = pl.num_programs(0) * per_res
        return (i % DIL) * n_a + blk * per_res + i // DIL

    def copy(blk, sl, i, k):
        src = ys_ref.at[pl.ds(dest_ref[token(blk, i) * TOP_K + k], 1)]
        return pltpu.make_async_copy(src, buf_ref.at[sl, k, pl.ds(i, 1)], sem.at[sl])

    def for_rows(fn):
        def body(i, c):
            for k in range(TOP_K):
                fn(i, k)
            return c

        lax.fori_loop(0, tb, body, 0)

    @pl.when(step == 0)
    def _():
        for_rows(lambda i, k: copy(0, 0, i, k).start())

    @pl.when(step + 1 < pl.num_programs(0))
    def _():
        for_rows(lambda i, k: copy(step + 1, 1 - slot, i, k).start())

    for k in range(TOP_K):
        pltpu.make_async_copy(ys_ref.at[pl.ds(0, tb)], buf_ref.at[slot, k], sem.at[slot]).wait()
    if to_natural:
        xn_ref, gn_ref = scratch
        for r in range(DIL):
            gn_ref[pl.ds(r, per_res, stride=DIL), :] = gate_ref[r]
            for c in range(xn_ref.shape[0]):
                xn_ref[c, pl.ds(r, per_res, stride=DIL), :] = x_ref[r, :, c * LANES:(c + 1) * LANES]
        x = jnp.concatenate([xn_ref[c] for c in range(xn_ref.shape[0])], axis=1)
        gate = gn_ref[...]
    else:
        x = x_ref[...]
        gate = gate_ref[...]
    ffn = gate[:, 0:1] * buf_ref[slot, 0]
    for k in range(1, TOP_K):
        ffn = ffn + gate[:, k:k + 1] * buf_ref[slot, k]
    o_ref[...] = _layer_norm_rows(alpha * x + ffn, g_ref[...], b_ref[...])


def _combine_ln(ys, dest_flat, gates, x, g, b, *, alpha, to_natural, tb=128):
    T, D = x.shape
    gate_pad = jnp.zeros((T, LANES), F32).at[:, :TOP_K].set(gates)
    scratch = [pltpu.VMEM((2, TOP_K, tb, D), F32), pltpu.SemaphoreType.DMA((2,))]
    if to_natural:
        na = T // DIL
        per_res = tb // DIL
        gate_in = gate_pad.reshape(DIL, na, LANES)
        x_in = x.reshape(DIL, na, D)
        gate_spec = pl.BlockSpec((DIL, per_res, LANES), lambda i, dest: (0, i, 0))
        x_spec = pl.BlockSpec((DIL, per_res, D), lambda i, dest: (0, i, 0))
        scratch += [pltpu.VMEM((D // LANES, tb, LANES), F32), pltpu.VMEM((tb, LANES), F32)]
    else:
        gate_in, x_in = gate_pad, x
        gate_spec = pl.BlockSpec((tb, LANES), lambda i, dest: (i, 0))
        x_spec = pl.BlockSpec((tb, D), lambda i, dest: (i, 0))
    return pl.pallas_call(
        functools.partial(_combine_kernel, alpha=alpha, to_natural=to_natural),
        name="moe_combine",
        grid_spec=pltpu.PrefetchScalarGridSpec(
            num_scalar_prefetch=1, grid=(T // tb,),
            in_specs=[pl.BlockSpec(memory_space=pl.ANY),
                      gate_spec,
                      x_spec,
                      pl.BlockSpec((1, D), lambda i, dest: (0, 0)),
                      pl.BlockSpec((1, D), lambda i, dest: (0, 0))],
            out_specs=pl.BlockSpec((tb, D), lambda i, dest: (i, 0)),
            scratch_shapes=scratch),
        out_shape=jax.ShapeDtypeStruct((T, D), F32),
        compiler_params=_cparams(("arbitrary",)),
    )(dest_flat, ys, gate_in, x_in, g.reshape(1, D), b.reshape(1, D))


def _moe_layout(idx, rank, counts, n_assign):
    E = counts.shape[0]
    blk = MOE_BLOCK
    nblk = (counts + blk - 1) // blk
    bend = jnp.cumsum(nblk)
    bstart = bend - nblk
    dest = (bstart * blk)[idx] + rank
    tail = jnp.where(nblk > 0, (bend - 1) * blk, 0).astype(I32)
    cb = MOE_CHUNK_BLOCKS
    n_items_max = E + (n_assign // blk + E) // cb + 1
    per_e = (nblk + cb - 1) // cb
    iend = jnp.cumsum(per_e)
    istart = iend - per_e
    ids = jnp.arange(n_items_max, dtype=I32)
    e_of = jnp.minimum(jnp.searchsorted(iend, ids, side='right'), E - 1).astype(I32)
    valid = ids < iend[-1]
    last_e = jnp.max(jnp.where(per_e > 0, jnp.arange(E, dtype=I32), 0))
    e_of = jnp.where(valid, e_of, last_e)
    local = ids - istart[e_of]
    first_blk = bstart[e_of] + local * cb
    nb = jnp.where(valid, jnp.minimum(cb, nblk[e_of] - local * cb), 0)
    row = jnp.where(valid, first_blk * blk, 0)
    return dest.astype(I32), tail, (e_of, row.astype(I32), nb.astype(I32))


def _moe(x, w_router, b_router, w_gu, b_gu, w_down, b_down, ln_g, ln_b, *, alpha, layer, to_natural=False):
    T, D = x.shape
    E = w_router.shape[1]
    idx, gates, rank, counts = _route(x, w_router, b_router)
    n_assign = T * TOP_K
    dest, tail, items = _moe_layout(idx, rank, counts, n_assign)
    dest_flat = dest.reshape(n_assign)
    n_rows = n_assign + E * MOE_BLOCK
    xs = _dispatch(x, dest_flat, _zero_tails(tail, n_rows, D))
    ys = _moe_experts(xs, items, w_gu, b_gu, w_down, b_down, layer)
    return _combine_ln(ys, dest_flat, gates, x, ln_g, ln_b, alpha=alpha, to_natural=to_natural)


def _even_mixer_ln(x, w_in, lam_re, lam_im, log_step, b_re, b_im, c_re, c_im, d_skip, w_glu, b_glu,
                   w_gate2, b_gate2, norm_g, w_out, ln_g, ln_b, *, alpha, to_residue_major):
    T, D = x.shape
    W = d_skip.shape[0]
    qk = W // 2
    rank = w_gate2.shape[0]
    s4 = W + 2 * qk + W
    w_main = jnp.concatenate([w_in[:, :s4], w_in[:, s4 + rank:]], axis=1).astype(BF16)
    w_gate = jnp.zeros((D, LANES), BF16).at[:, :rank].set(w_in[:, s4:s4 + rank].astype(BF16))
    h, g_low = _proj_in(x, w_main, w_gate)
    tables = _s5_tables(lam_re, lam_im, log_step, b_re, b_im, c_re, c_im, T // S5_CHUNK)
    y = _s5_core(h, W, tables)
    ya = _s5_glu(y, h, d_skip.astype(F32), w_glu.astype(BF16), b_glu.astype(F32))
    yb = _gla(h, g_low, w_gate2, b_gate2, norm_g, width=W)
    w_out_b = w_out.astype(BF16)
    return _proj_ln([ya, yb], [w_out_b[:W], w_out_b[W:]], x, ln_g, ln_b, alpha=alpha,
                    to_residue_major=to_residue_major)


def _odd_mixer_ln(x, w_qkv, w_o, ln_g, ln_b, *, alpha):
    T, D = x.shape
    qkv = _proj_scaled(x, w_qkv.astype(BF16), scaled_cols=D, scale=(D // ATT_HEADS) ** -0.5)
    y = _dilated_attention(qkv, D)
    return _proj_ln([y], [w_o.astype(BF16)], x, ln_g, ln_b, alpha=alpha, to_residue_major=False)


def kernel(x, ab_w_in, s5_lam_re, s5_lam_im, s5_log_step, s5_b_re, s5_b_im, s5_c_re, s5_c_im, s5_d, s5_w_glu, s5_b_glu, gla_w_gate2, gla_b_gate2, gla_norm_g, ab_w_out, c_w_qkv, c_w_o, ln1_g, ln1_b, moe_w_router, moe_b_router, moe_w_gu, moe_b_gu, moe_w_down, moe_b_down, ln2_g, ln2_b):
    bsz, L, D = x.shape
    depth = ln1_g.shape[0]
    alpha = (2 * depth) ** 0.25
    outs = []
    for bi in range(bsz):
        xt = x[bi].astype(F32)
        for layer in range(depth):
            i = layer // 2
            odd = layer % 2 == 1
            if not odd:
                xt = _even_mixer_ln(xt, ab_w_in[i], s5_lam_re[i], s5_lam_im[i], s5_log_step[i], s5_b_re[i],
                                    s5_b_im[i], s5_c_re[i], s5_c_im[i], s5_d[i], s5_w_glu[i], s5_b_glu[i],
                                    gla_w_gate2[i], gla_b_gate2[i], gla_norm_g[i], ab_w_out[i],
                                    ln1_g[layer], ln1_b[layer], alpha=alpha, to_residue_major=layer + 1 < depth)
            else:
                xt = _odd_mixer_ln(xt, c_w_qkv[i], c_w_o[i], ln1_g[layer], ln1_b[layer], alpha=alpha)
            xt = _moe(xt, moe_w_router[layer], moe_b_router[layer], moe_w_gu, moe_b_gu,
                      moe_w_down, moe_b_down, ln2_g[layer], ln2_b[layer], alpha=alpha, layer=layer,
                      to_natural=odd)
        outs.append(xt.astype(x.dtype))
    return outs[0].reshape(1, L, D) if bsz == 1 else jnp.stack(outs)
```

```python
import functools
import math

import jax
import jax.numpy as jnp
from jax import lax
from jax.experimental import pallas as pl
from jax.experimental.pallas import tpu as pltpu

F32 = jnp.float32
BF16 = jnp.bfloat16
I32 = jnp.int32
HIGHEST = lax.Precision.HIGHEST

LANES = 128
VMEM_LIMIT_BYTES = 56 * 1024 * 1024

S5_GROUP = 16
S5_STATE = 64
S5_MAX_RE = -1e-4
S5_CHUNK = 16
GLA_HEADS = 4
GLA_GATE_TEMP = 16.0
GLA_EPS = 1e-6
GLA_BLOCK = 64
GLA_SUB = 16
ATT_HEADS = 16
DIL = 16
DILATED_GROUPS = ((128, 1), (512, 4), (2048, 16))
TOP_K = 4
SWIGLU_LIMIT = 7.0
SWIGLU_ALPHA = 1.702
MOE_BLOCK = 128
MOE_CHUNK_BLOCKS = 10
LN_EPS = 1e-5
NEG_BIG = -1e30


def _cparams(semantics):
    return pltpu.CompilerParams(dimension_semantics=semantics, vmem_limit_bytes=VMEM_LIMIT_BYTES)


def _layer_norm_rows(z, g, b):
    mu = jnp.mean(z, axis=-1, keepdims=True)
    zc = z - mu
    var = jnp.mean(zc * zc, axis=-1, keepdims=True)
    return zc * lax.rsqrt(var + LN_EPS) * g + b


def _proj_in_kernel(x_ref, w_ref, wg_ref, h_ref, g_ref, xb_ref):
    @pl.when(pl.program_id(1) == 0)
    def _():
        xb = x_ref[...].astype(BF16)
        xb_ref[...] = xb
        g_ref[...] = jnp.dot(xb, wg_ref[...], preferred_element_type=F32)

    h_ref[...] = jnp.dot(xb_ref[...], w_ref[...], preferred_element_type=F32)


def _proj_in(x, w_main, w_gate, *, tm=1024, tn=512):
    T, D = x.shape
    N = w_main.shape[1]
    return pl.pallas_call(
        _proj_in_kernel,
        name="proj_in",
        grid=(T // tm, N // tn),
        in_specs=[pl.BlockSpec((tm, D), lambda i, j: (i, 0)),
                  pl.BlockSpec((D, tn), lambda i, j: (0, j)),
                  pl.BlockSpec((D, LANES), lambda i, j: (0, 0))],
        out_specs=[pl.BlockSpec((tm, tn), lambda i, j: (i, j)),
                   pl.BlockSpec((tm, LANES), lambda i, j: (i, 0))],
        out_shape=[jax.ShapeDtypeStruct((T, N), F32), jax.ShapeDtypeStruct((T, LANES), F32)],
        scratch_shapes=[pltpu.VMEM((tm, D), BF16)],
        compiler_params=_cparams(("parallel", "arbitrary")),
    )(x, w_main, w_gate)


def _proj_scaled_kernel(x_ref, w_ref, o_ref, xb_ref, *, n_scaled, scale):
    j = pl.program_id(1)

    @pl.when(j == 0)
    def _():
        xb_ref[...] = x_ref[...].astype(BF16)

    y = jnp.dot(xb_ref[...], w_ref[...], preferred_element_type=F32)
    o_ref[...] = (y * jnp.where(j < n_scaled, scale, 1.0)).astype(o_ref.dtype)


def _proj_scaled(x, w, *, scaled_cols, scale, tm=1024, tn=1024):
    T, D = x.shape
    N = w.shape[1]
    return pl.pallas_call(
        functools.partial(_proj_scaled_kernel, n_scaled=scaled_cols // tn, scale=scale),
        name="proj_qkv",
        grid=(T // tm, N // tn),
        in_specs=[pl.BlockSpec((tm, D), lambda i, j: (i, 0)),
                  pl.BlockSpec((D, tn), lambda i, j: (0, j))],
        out_specs=pl.BlockSpec((tm, tn), lambda i, j: (i, j)),
        out_shape=jax.ShapeDtypeStruct((T, N), BF16),
        scratch_shapes=[pltpu.VMEM((tm, D), BF16)],
        compiler_params=_cparams(("parallel", "arbitrary")),
    )(x, w)


def _proj_ln_kernel(*refs, n_lhs, alpha, n_tiles, to_residue_major):
    lhs_refs = refs[:n_lhs]
    w_refs = refs[n_lhs:2 * n_lhs]
    res_ref, g_ref, b_ref, o_ref, acc_ref = refs[2 * n_lhs:2 * n_lhs + 5]
    j = pl.program_id(1)
    y = jnp.dot(lhs_refs[0][...], w_refs[0][...], preferred_element_type=F32)
    for a_ref, w_ref in zip(lhs_refs[1:], w_refs[1:]):
        y = y + jnp.dot(a_ref[...], w_ref[...], preferred_element_type=F32)
    acc_ref[j] = y

    @pl.when(j == n_tiles - 1)
    def _():
        tn = acc_ref.shape[2]
        z = [alpha * res_ref[:, t * tn:(t + 1) * tn] + acc_ref[t] for t in range(n_tiles)]
        n = float(n_tiles * tn)
        mu = sum(jnp.sum(zt, axis=-1, keepdims=True) for zt in z) / n
        zc = [zt - mu for zt in z]
        var = sum(jnp.sum(zt * zt, axis=-1, keepdims=True) for zt in zc) / n
        rstd = lax.rsqrt(var + LN_EPS)
        out = [zc[t] * rstd * g_ref[:, t * tn:(t + 1) * tn] + b_ref[:, t * tn:(t + 1) * tn] for t in range(n_tiles)]
        if not to_residue_major:
            for t in range(n_tiles):
                o_ref[:, t * tn:(t + 1) * tn] = out[t]
        else:
            rows_ref = refs[-1]
            per_res = rows_ref.shape[1] // DIL
            for c in range(rows_ref.shape[0]):
                t, off = divmod(c * LANES, tn)
                rows_ref[c] = out[t][:, off:off + LANES]
                for r in range(DIL):
                    o_ref[r, :, c * LANES:(c + 1) * LANES] = rows_ref[c, pl.ds(r, per_res, stride=DIL), :]


def _proj_ln(lhs, ws, res, g, b, *, alpha, to_residue_major, tm=512, tn=512):
    T, N = res.shape
    n_lhs = len(lhs)
    n_tiles = N // tn
    scratch = [pltpu.VMEM((n_tiles, tm, tn), F32)]
    if to_residue_major:
        na = T // DIL
        out_spec = pl.BlockSpec((DIL, tm // DIL, N), lambda i, j: (0, i, 0))
        out_shape = jax.ShapeDtypeStruct((DIL, na, N), F32)
        scratch.append(pltpu.VMEM((N // LANES, tm, LANES), F32))
    else:
        out_spec = pl.BlockSpec((tm, N), lambda i, j: (i, 0))
        out_shape = jax.ShapeDtypeStruct((T, N), F32)
    in_specs = ([pl.BlockSpec((tm, a.shape[1]), lambda i, j: (i, 0)) for a in lhs]
                + [pl.BlockSpec((w.shape[0], tn), lambda i, j: (0, j)) for w in ws]
                + [pl.BlockSpec((tm, N), lambda i, j: (i, 0)),
                   pl.BlockSpec((1, N), lambda i, j: (0, 0)),
                   pl.BlockSpec((1, N), lambda i, j: (0, 0))])
    out = pl.pallas_call(
        functools.partial(_proj_ln_kernel, n_lhs=n_lhs, alpha=alpha, n_tiles=n_tiles,
                          to_residue_major=to_residue_major),
        name="proj_ln",
        grid=(T // tm, n_tiles),
        in_specs=in_specs,
        out_specs=out_spec,
        out_shape=out_shape,
        scratch_shapes=scratch,
        compiler_params=_cparams(("parallel", "arbitrary")),
    )(*lhs, *ws, res, g.reshape(1, N), b.reshape(1, N))
    return out.reshape(T, N)


def _s5_tables(lam_re, lam_im, log_step, b_re, b_im, c_re, c_im, n_chunks):
    C = S5_CHUNK
    G, P = lam_re.shape
    H = b_re.shape[-1]
    lr = jnp.minimum(lam_re.astype(F32), S5_MAX_RE)
    li = lam_im.astype(F32)
    dt = jnp.exp(log_step.astype(F32))[:, None]
    kk = jnp.arange(C + 1, dtype=F32)[:, None, None]
    pw_mag = jnp.exp(kk * (lr * dt))
    pw_re = pw_mag * jnp.cos(kk * (li * dt))
    pw_im = pw_mag * jnp.sin(kk * (li * dt))
    a_re, a_im = pw_re[1], pw_im[1]
    den = lr * lr + li * li
    nr = a_re - 1.0
    f_re = (nr * lr + a_im * li) / den
    f_im = (a_im * lr - nr * li) / den
    br = b_re.astype(F32)
    bi = b_im.astype(F32)
    bb_re = f_re[..., None] * br - f_im[..., None] * bi
    bb_im = f_re[..., None] * bi + f_im[..., None] * br
    ab_re = pw_re[:C, :, :, None] * bb_re[None] - pw_im[:C, :, :, None] * bb_im[None]
    ab_im = pw_re[:C, :, :, None] * bb_im[None] + pw_im[:C, :, :, None] * bb_re[None]
    cr = c_re.astype(F32)
    ci = c_im.astype(F32)
    z_re = jnp.transpose(ab_re[::-1], (1, 0, 3, 2)).reshape(G, C * H, P)
    z_im = jnp.transpose(ab_im[::-1], (1, 0, 3, 2)).reshape(G, C * H, P)
    zmat = jnp.concatenate([z_re, z_im], axis=-1)
    kern = (jnp.einsum('gop,kgph->kgoh', cr, ab_re, precision=HIGHEST)
            - jnp.einsum('gop,kgph->kgoh', ci, ab_im, precision=HIGHEST))
    lag = jnp.arange(C)[None, :] - jnp.arange(C)[:, None]
    kl = kern[jnp.clip(lag, 0, C - 1)]
    kl = jnp.where((lag >= 0)[:, :, None, None, None], kl, 0.0)
    mmat = jnp.transpose(kl, (2, 0, 4, 1, 3)).reshape(G, C * H, C * H)
    ca_re = cr[None] * pw_re[1:, :, None, :] - ci[None] * pw_im[1:, :, None, :]
    ca_im = cr[None] * pw_im[1:, :, None, :] + ci[None] * pw_re[1:, :, None, :]
    n_re = jnp.transpose(ca_re, (1, 3, 0, 2)).reshape(G, P, C * H)
    n_im = jnp.transpose(-ca_im, (1, 3, 0, 2)).reshape(G, P, C * H)
    nmat = jnp.concatenate([n_re, n_im], axis=1)
    n_steps = max(1, (n_chunks - 1).bit_length())
    qr, qi = pw_re[C], pw_im[C]
    a1, a2 = [], []
    for _ in range(n_steps):
        a1.append(jnp.concatenate([qr, qr], axis=-1))
        a2.append(jnp.concatenate([-qi, qi], axis=-1))
        qr, qi = qr * qr - qi * qi, 2.0 * qr * qi
    a1 = jnp.stack(a1, axis=1)
    a2 = jnp.stack(a2, axis=1)
    return zmat.astype(BF16), mmat.astype(BF16), nmat.astype(BF16), a1, a2


def _s5_kernel(u_ref, z_ref, m_ref, n_ref, a1_ref, a2_ref, y_ref, ub_ref, *, n_steps):
    C, H = S5_CHUNK, S5_GROUP
    nc = u_ref.shape[0] // C
    gpt = LANES // H
    ch = C * H
    for j in range(C):
        ub_ref[:, j * LANES:(j + 1) * LANES] = u_ref[pl.ds(j, nc, stride=C), :].astype(BF16)
    row = lax.broadcasted_iota(I32, (nc, LANES), 0)
    sr = lax.broadcasted_iota(I32, (gpt * LANES, LANES), 0)
    sc = lax.broadcasted_iota(I32, (gpt * LANES, LANES), 1)
    sel_hit = sc == H * (sr // LANES) + sr % H
    sel_grp = (sr % LANES) // H
    pr = lax.broadcasted_iota(I32, (ch, C * LANES), 0)
    pc = lax.broadcasted_iota(I32, (ch, C * LANES), 1)
    put_tile = pc // LANES == pr // H
    put_lane = pc % LANES - pr % H

    def group(gl, first):
        sel = jnp.where(sel_hit & (sel_grp == gl), 1.0, 0.0).astype(BF16)
        half_w = gpt * LANES
        u = jnp.concatenate(
            [jnp.dot(ub_ref[:, t * half_w:(t + 1) * half_w], sel, preferred_element_type=F32)
             for t in range(C * LANES // half_w)], axis=1).astype(BF16)
        s = jnp.dot(u, z_ref[gl], preferred_element_type=F32)
        half = s.shape[1] // 2
        a1 = a1_ref[gl]
        a2 = a2_ref[gl]
        for k in range(n_steps):
            sh = 1 << k
            prev = jnp.where(row >= sh, pltpu.roll(s, sh, axis=0), 0.0)
            s = s + a1[k:k + 1, :] * prev + a2[k:k + 1, :] * pltpu.roll(prev, half, axis=1)
        s_in = jnp.where(row >= 1, pltpu.roll(s, 1, axis=0), 0.0)
        put = jnp.where(put_tile & (put_lane == H * gl), 1.0, 0.0).astype(BF16)
        m_wide = jnp.dot(m_ref[gl], put, preferred_element_type=F32).astype(BF16)
        n_wide = jnp.dot(n_ref[gl], put, preferred_element_type=F32).astype(BF16)
        y = (jnp.dot(u, m_wide, preferred_element_type=F32)
             + jnp.dot(s_in.astype(BF16), n_wide, preferred_element_type=F32))
        for i in range(C):
            rows = pl.ds(i, nc, stride=C)
            piece = y[:, i * LANES:(i + 1) * LANES]
            y_ref[rows, :] = piece if first else y_ref[rows, :] + piece

    group(0, True)

    def later(gl, c):
        group(gl, False)
        return c

    lax.fori_loop(1, gpt, later, 0)


def _s5_core(h, width, tables):
    zmat, mmat, nmat, a1, a2 = tables
    T = h.shape[0]
    G, CH, P2 = zmat.shape
    C, H = S5_CHUNK, S5_GROUP
    nc = T // C
    n_steps = a1.shape[1]
    gpt = LANES // H
    assert CH == C * H and width == G * H
    return pl.pallas_call(
        functools.partial(_s5_kernel, n_steps=n_steps),
        name="s5_core",
        grid=(G // gpt,),
        in_specs=[pl.BlockSpec((T, LANES), lambda m: (0, m)),
                  pl.BlockSpec((gpt, CH, P2), lambda m: (m, 0, 0)),
                  pl.BlockSpec((gpt, CH, CH), lambda m: (m, 0, 0)),
                  pl.BlockSpec((gpt, P2, CH), lambda m: (m, 0, 0)),
                  pl.BlockSpec((gpt, n_steps, P2), lambda m: (m, 0, 0)),
                  pl.BlockSpec((gpt, n_steps, P2), lambda m: (m, 0, 0))],
        out_specs=pl.BlockSpec((T, LANES), lambda m: (0, m)),
        out_shape=jax.ShapeDtypeStruct((T, width), F32),
        scratch_shapes=[pltpu.VMEM((nc, C * LANES), BF16)],
        compiler_params=_cparams(("parallel",)),
    )(h, zmat, mmat, nmat, a1, a2)


def _s5_glu_kernel(y_ref, u_ref, d_ref, w_ref, b_ref, o_ref):
    y = y_ref[...] + d_ref[...] * u_ref[...]
    c0 = math.sqrt(2.0 / math.pi)
    z = 0.5 * y * (1.0 + jnp.tanh(c0 * (y + 0.044715 * (y * y * y))))
    lin = jnp.dot(z.astype(BF16), w_ref[...], preferred_element_type=F32) + b_ref[...]
    o_ref[...] = (z * jax.nn.sigmoid(lin)).astype(o_ref.dtype)


def _s5_glu(y, h, d_skip, w_glu, b_glu, *, tm=512):
    T, W = y.shape
    return pl.pallas_call(
        _s5_glu_kernel,
        name="s5_glu",
        grid=(T // tm,),
        in_specs=[pl.BlockSpec((tm, W), lambda i: (i, 0)),
                  pl.BlockSpec((tm, W), lambda i: (i, 0)),
                  pl.BlockSpec((1, W), lambda i: (0, 0)),
                  pl.BlockSpec((W, W), lambda i: (0, 0)),
                  pl.BlockSpec((1, W), lambda i: (0, 0))],
        out_specs=pl.BlockSpec((tm, W), lambda i: (i, 0)),
        out_shape=jax.ShapeDtypeStruct((T, W), BF16),
        compiler_params=_cparams(("parallel",)),
    )(y, h, d_skip.reshape(1, W), w_glu, b_glu.reshape(1, W))


def _gla_kernel(q_ref, k_ref, v_ref, r_ref, g_ref, w2_ref, b2_ref, ng_ref, o_ref, st_ref, *, dk, dv):
    @pl.when(pl.program_id(0) == 0)
    def _():
        st_ref[...] = jnp.zeros_like(st_ref)

    cb = q_ref.shape[0]
    n_sub = cb // GLA_SUB
    scale = dk ** -0.5
    logit = jnp.dot(g_ref[...], w2_ref[...], preferred_element_type=F32, precision=HIGHEST) + b2_ref[...]
    log_a = (jnp.minimum(logit, 0.0) - jnp.log(1.0 + jnp.exp(-jnp.abs(logit)))) / GLA_GATE_TEMP
    ri = lax.broadcasted_iota(I32, (cb, cb), 0)
    ci = lax.broadcasted_iota(I32, (cb, cb), 1)
    tri = (ri >= ci).astype(F32)
    bcum = jnp.dot(tri, log_a, preferred_element_type=F32, precision=HIGHEST)
    for hh in range(GLA_HEADS):
        ks = slice(hh * dk, (hh + 1) * dk)
        vs = slice(hh * dv, (hh + 1) * dv)
        b = bcum[:, ks]
        q = q_ref[:, ks] * scale
        k = k_ref[:, ks]
        v = v_ref[:, vs].astype(BF16)
        refs = [jnp.zeros((1, dk), F32)] + [b[a * GLA_SUB - 1:a * GLA_SUB, :] for a in range(1, n_sub)]
        refmat = jnp.concatenate([jnp.broadcast_to(r, (GLA_SUB, dk)) for r in refs], axis=0)
        qe = (q * jnp.exp(b - refmat)).astype(BF16)
        st = st_ref[hh]
        o_inter = lax.dot_general((q * jnp.exp(b)).astype(BF16), st.astype(BF16),
                                  (((1,), (1,)), ((), ())), preferred_element_type=F32)
        o_rows = []
        for a in range(n_sub):
            hi = (a + 1) * GLA_SUB
            ke = (k[:hi] * jnp.exp(refs[a] - b[:hi])).astype(BF16)
            att = lax.dot_general(qe[a * GLA_SUB:hi], ke, (((1,), (1,)), ((), ())),
                                  preferred_element_type=F32)
            row_a = lax.broadcasted_iota(I32, (GLA_SUB, hi), 0) + a * GLA_SUB
            att = jnp.where(lax.broadcasted_iota(I32, (GLA_SUB, hi), 1) <= row_a, att, 0.0)
            o_rows.append(jnp.dot(att.astype(BF16), v[:hi], preferred_element_type=F32))
        o = jnp.concatenate(o_rows, axis=0) + o_inter
        o = o * lax.rsqrt(jnp.mean(o * o, axis=-1, keepdims=True) + GLA_EPS) * ng_ref[...]
        r = r_ref[:, vs]
        o_ref[:, vs] = (o * (r * jax.nn.sigmoid(r))).astype(o_ref.dtype)
        b_last = b[cb - 1:cb, :]
        kd = (k * jnp.exp(b_last - b)).astype(BF16)
        upd = lax.dot_general(v, kd, (((0,), (0,)), ((), ())), preferred_element_type=F32)
        st_ref[hh] = st * jnp.exp(b_last) + upd


def _gla(h, g_low, w_gate2, b_gate2, norm_g, *, width):
    T = h.shape[0]
    qk = width // 2
    dk = qk // GLA_HEADS
    dv = width // GLA_HEADS
    cb = GLA_BLOCK
    w2 = jnp.zeros((LANES, qk), F32).at[:w_gate2.shape[0]].set(w_gate2.astype(F32))
    return pl.pallas_call(
        functools.partial(_gla_kernel, dk=dk, dv=dv),
        name="gla",
        grid=(T // cb,),
        in_specs=[pl.BlockSpec((cb, qk), lambda i: (i, 2)),
                  pl.BlockSpec((cb, qk), lambda i: (i, 3)),
                  pl.BlockSpec((cb, width), lambda i: (i, 2)),
                  pl.BlockSpec((cb, width), lambda i: (i, 3)),
                  pl.BlockSpec((cb, LANES), lambda i: (i, 0)),
                  pl.BlockSpec((LANES, qk), lambda i: (0, 0)),
                  pl.BlockSpec((1, qk), lambda i: (0, 0)),
                  pl.BlockSpec((1, dv), lambda i: (0, 0))],
        out_specs=pl.BlockSpec((cb, width), lambda i: (i, 0)),
        out_shape=jax.ShapeDtypeStruct((T, width), BF16),
        scratch_shapes=[pltpu.VMEM((GLA_HEADS, dv, dk), F32)],
        compiler_params=_cparams(("arbitrary",)),
    )(h, h, h, h, g_low, w2, b_gate2.reshape(1, qk).astype(F32), norm_g.reshape(1, dv).astype(F32))


def _attn_kernel(q_ref, k_ref, v_ref, o_ref, acc_ref, m_ref, l_ref, v1_ref, pat1_ref, pat4_ref, pat16_ref, *, na):
    dh = v_ref.shape[1]
    v1_ref[:, :dh] = v_ref[...]
    v1_ref[:, dh:] = jnp.ones_like(v_ref)
    acc_ref[...] = jnp.zeros_like(acc_ref)
    m_ref[...] = jnp.full_like(m_ref, NEG_BIG)
    l_ref[...] = jnp.zeros_like(l_ref)

    def token_offsets(rows, r_step, n, axis):
        shape = (n, 1) if axis == 0 else (1, n)
        idx = lax.broadcasted_iota(I32, shape, axis)
        c = idx // rows
        return DIL * (idx - c * rows) + r_step * c

    def delta_pattern(n_chunks, q_rows, k_rows, r_step):
        return (token_offsets(q_rows, r_step, n_chunks * q_rows, 0)
                - token_offsets(k_rows, r_step, n_chunks * k_rows, 1))

    def cat(ref, starts, rows):
        return jnp.concatenate([ref[pl.ds(pl.multiple_of(s, 16), rows), :] for s in starts], axis=0)

    def attend(blocks, q_rows, k_rows, window, pat_ref):
        loaded = []
        for q_starts, k_starts, a_q, a_k in blocks:
            loaded.append((cat(q_ref, q_starts, q_rows), cat(k_ref, k_starts, k_rows), cat(v1_ref, k_starts, k_rows),
                           cat(m_ref, q_starts, q_rows), cat(l_ref, q_starts, q_rows),
                           cat(acc_ref, q_starts, q_rows)))
        results = []
        for (q_starts, k_starts, a_q, a_k), (qb, kb, vb, m_old, l_old, acc_old) in zip(blocks, loaded):
            s = lax.dot_general(qb, kb, (((1,), (1,)), ((), ())), preferred_element_type=F32)
            off = DIL * (a_q - a_k)
            pat = pat_ref[...]
            s = jnp.where((pat >= -off) & (pat <= window - off), s, NEG_BIG)
            m_new = jnp.maximum(m_old, jnp.max(s, axis=1, keepdims=True))
            alpha = jnp.exp(m_old - m_new)
            p = jnp.exp(s - m_new[:, :1])
            pv = jnp.dot(p.astype(BF16), vb, preferred_element_type=F32)
            l_new = alpha * l_old + pv[:, dh:]
            acc_new = alpha * acc_old + pv[:, :dh]
            results.append((m_new, l_new, acc_new))
        for (q_starts, *_), (m_new, l_new, acc_new) in zip(blocks, results):
            for c, st in enumerate(q_starts):
                st = pl.multiple_of(st, 16)
                rs = slice(c * q_rows, (c + 1) * q_rows)
                m_ref[pl.ds(st, q_rows), :] = m_new[rs]
                l_ref[pl.ds(st, q_rows), :] = l_new[rs]
                acc_ref[pl.ds(st, q_rows), :] = acc_new[rs]

    w1 = DILATED_GROUPS[0][0]
    u1 = 2
    pat1_ref[...] = delta_pattern(DIL, 16, 32, 1)

    def body1(it, carry):
        blocks = []
        for j in range(u1):
            a0 = (it * u1 + j) * 16
            ak = jnp.maximum(a0 - 16, 0)
            blocks.append(([r * na + a0 for r in range(DIL)], [r * na + ak for r in range(DIL)], a0, ak))
        attend(blocks, 16, 32, w1, pat1_ref)
        return carry

    lax.fori_loop(0, na // (16 * u1), body1, 0)

    w4 = DILATED_GROUPS[1][0]
    pat4_ref[...] = delta_pattern(4, 32, 64, 4)

    def body4(it, carry):
        a0 = it * 32
        ak = jnp.maximum(a0 - 32, 0)
        blocks = [([(rho + 4 * sg) * na + a0 for sg in range(4)], [(rho + 4 * sg) * na + ak for sg in range(4)],
                   a0, ak) for rho in range(4)]
        attend(blocks, 32, 64, w4, pat4_ref)
        return carry

    lax.fori_loop(0, na // 32, body4, 0)

    w16 = DILATED_GROUPS[2][0]
    u16 = 4
    pat16_ref[...] = delta_pattern(1, 128, 256, 0)

    def body16(it, carry):
        rg = it // (na // 128)
        a0 = (it - rg * (na // 128)) * 128
        ak = jnp.maximum(a0 - 128, 0)
        blocks = [([(rg * u16 + j) * na + a0], [(rg * u16 + j) * na + ak], a0, ak) for j in range(u16)]
        attend(blocks, 128, 256, w16, pat16_ref)
        return carry

    lax.fori_loop(0, (DIL // u16) * (na // 128), body16, 0)

    o_ref[...] = (acc_ref[...] / l_ref[...]).astype(o_ref.dtype)


def _dilated_attention(qkv, d_model):
    T = qkv.shape[0]
    dh = d_model // ATT_HEADS
    assert dh == LANES
    na = T // DIL
    return pl.pallas_call(
        functools.partial(_attn_kernel, na=na),
        name="dilated_attn",
        grid=(ATT_HEADS,),
        in_specs=[pl.BlockSpec((T, dh), lambda h: (0, h)),
                  pl.BlockSpec((T, dh), lambda h: (0, ATT_HEADS + h)),
                  pl.BlockSpec((T, dh), lambda h: (0, 2 * ATT_HEADS + h))],
        out_specs=pl.BlockSpec((T, dh), lambda h: (0, h)),
        out_shape=jax.ShapeDtypeStruct((T, d_model), BF16),
        scratch_shapes=[pltpu.VMEM((T, dh), F32), pltpu.VMEM((T, dh), F32), pltpu.VMEM((T, dh), F32),
                        pltpu.VMEM((T, 2 * dh), BF16), pltpu.VMEM((256, 512), I32), pltpu.VMEM((128, 256), I32), pltpu.VMEM((128, 256), I32)],
        compiler_params=_cparams(("parallel",)),
    )(qkv, qkv, qkv)


def _route_kernel(x_ref, w_ref, b_ref, idx_ref, gate_ref, rank_ref, cnt_ref, carry_ref):
    @pl.when(pl.program_id(0) == 0)
    def _():
        carry_ref[...] = jnp.zeros_like(carry_ref)

    tb = x_ref.shape[0]
    x = x_ref[...]
    w = w_ref[...]
    x_hi, w_hi = x.astype(BF16), w.astype(BF16)
    x_lo = (x - x_hi.astype(F32)).astype(BF16)
    w_lo = (w - w_hi.astype(F32)).astype(BF16)
    lg = (jnp.dot(x_hi, w_hi, preferred_element_type=F32) + jnp.dot(x_hi, w_lo, preferred_element_type=F32)
          + jnp.dot(x_lo, w_hi, preferred_element_type=F32)) + b_ref[...]
    lane = lax.broadcasted_iota(I32, lg.shape, 1)
    vals, hots = [], []
    idx_out = jnp.zeros(lg.shape, I32)
    for k in range(TOP_K):
        m = jnp.max(lg, axis=1, keepdims=True)
        sel = jnp.min(jnp.where(lg == m, lane, LANES), axis=1, keepdims=True)
        hot = lane == sel
        vals.append(m)
        hots.append(hot)
        idx_out = jnp.where(lane == k, sel, idx_out)
        lg = jnp.where(hot, -jnp.inf, lg)
    ex = [jnp.exp(v - vals[0]) for v in vals]
    den = sum(ex)
    gate_out = jnp.zeros(lg.shape, F32)
    for k in range(TOP_K):
        gate_out = jnp.where(lane == k, ex[k] / den, gate_out)
    chosen = sum(h.astype(F32) for h in hots)
    ri = lax.broadcasted_iota(I32, (tb, tb), 0)
    ci = lax.broadcasted_iota(I32, (tb, tb), 1)
    before = jnp.dot((ri > ci).astype(BF16), chosen.astype(BF16), preferred_element_type=F32) + carry_ref[...]
    rank_out = jnp.zeros(lg.shape, I32)
    for k in range(TOP_K):
        rk = jnp.sum(jnp.where(hots[k], before, 0.0), axis=1, keepdims=True).astype(I32)
        rank_out = jnp.where(lane == k, rk, rank_out)
    idx_ref[...] = idx_out
    gate_ref[...] = gate_out
    rank_ref[...] = rank_out
    carry_ref[...] = carry_ref[...] + jnp.sum(chosen, axis=0, keepdims=True)
    cnt_ref[...] = carry_ref[...].astype(I32)


def _route(x, w_router, b_router, *, tb=512):
    T, D = x.shape
    E = w_router.shape[1]
    wr = jnp.zeros((D, LANES), F32).at[:, :E].set(w_router.astype(F32))
    br = jnp.full((1, LANES), NEG_BIG, F32).at[0, :E].set(b_router.astype(F32))
    row = lambda dt: jax.ShapeDtypeStruct((T, LANES), dt)
    idx, gate, rank, cnt = pl.pallas_call(
        _route_kernel,
        name="moe_route",
        grid=(T // tb,),
        in_specs=[pl.BlockSpec((tb, D), lambda i: (i, 0)),
                  pl.BlockSpec((D, LANES), lambda i: (0, 0)),
                  pl.BlockSpec((1, LANES), lambda i: (0, 0))],
        out_specs=[pl.BlockSpec((tb, LANES), lambda i: (i, 0)),
                   pl.BlockSpec((tb, LANES), lambda i: (i, 0)),
                   pl.BlockSpec((tb, LANES), lambda i: (i, 0)),
                   pl.BlockSpec((1, LANES), lambda i: (0, 0))],
        out_shape=[row(I32), row(F32), row(I32), jax.ShapeDtypeStruct((1, LANES), I32)],
        scratch_shapes=[pltpu.VMEM((1, LANES), F32)],
        compiler_params=_cparams(("arbitrary",)),
    )(x, wr, br)
    return idx, gate, rank, cnt[0, :E]


def _zero_tail_kernel(row_ref, o_ref, z_ref, sem):
    z_ref[...] = jnp.zeros_like(z_ref)
    n = row_ref.shape[0]

    def copy(e):
        return pltpu.make_async_copy(z_ref, o_ref.at[pl.ds(pl.multiple_of(row_ref[e], MOE_BLOCK), MOE_BLOCK)], sem)

    def start(e, c):
        copy(e).start()
        return c

    def wait(e, c):
        copy(e).wait()
        return c

    lax.fori_loop(0, n, start, 0)
    lax.fori_loop(0, n, wait, 0)


def _zero_tails(tail_rows, n_rows, d):
    return pl.pallas_call(
        _zero_tail_kernel,
        name="moe_zero_tails",
        grid_spec=pltpu.PrefetchScalarGridSpec(
            num_scalar_prefetch=1, grid=(1,),
            in_specs=[],
            out_specs=pl.BlockSpec(memory_space=pl.ANY),
            scratch_shapes=[pltpu.VMEM((MOE_BLOCK, d), F32), pltpu.SemaphoreType.DMA(())]),
        out_shape=jax.ShapeDtypeStruct((n_rows, d), F32),
        compiler_params=_cparams(("arbitrary",)),
    )(tail_rows)


def _dispatch_kernel(dest_ref, x_ref, xs_in_ref, xs_ref, sem):
    del xs_in_ref
    tb = x_ref.shape[0]
    base = pl.program_id(0) * tb * TOP_K

    def copy(i, k):
        return pltpu.make_async_copy(x_ref.at[pl.ds(i, 1)],
                                     xs_ref.at[pl.ds(dest_ref[base + i * TOP_K + k], 1)], sem)

    def start(i, c):
        for k in range(TOP_K):
            copy(i, k).start()
        return c

    lax.fori_loop(0, tb, start, 0, unroll=4)
    for _ in range(TOP_K):
        pltpu.make_async_copy(x_ref, xs_ref.at[pl.ds(0, tb)], sem).wait()


def _dispatch(x, dest_flat, xs_init, *, tb=256):
    T, D = x.shape
    return pl.pallas_call(
        _dispatch_kernel,
        name="moe_dispatch",
        grid_spec=pltpu.PrefetchScalarGridSpec(
            num_scalar_prefetch=1, grid=(T // tb,),
            in_specs=[pl.BlockSpec((tb, D), lambda i, dest: (i, 0)),
                      pl.BlockSpec(memory_space=pl.ANY)],
            out_specs=pl.BlockSpec(memory_space=pl.ANY),
            scratch_shapes=[pltpu.SemaphoreType.DMA(())]),
        out_shape=jax.ShapeDtypeStruct(xs_init.shape, xs_init.dtype),
        input_output_aliases={2: 0},
        compiler_params=_cparams(("arbitrary",)),
    )(dest_flat, x, xs_init)


def _moe_kernel(item_e_ref, item_row_ref, item_nb_ref, xs_ref, wg_ref, wu_ref, bg_ref, bu_ref, wd_ref, bd_ref,
                ys_ref, xbuf_ref, acc_ref, wgb_ref, wub_ref, wdb_ref, in_sem, out_sem, *, n_f):
    it = pl.program_id(0)
    f = pl.program_id(1)
    nb = item_nb_ref[it]
    row0 = item_row_ref[it]
    blk = MOE_BLOCK

    def acc_rows(j):
        return pl.ds(pl.multiple_of(j * blk, blk), blk)

    def in_copy(j):
        src = xs_ref.at[pl.ds(pl.multiple_of(row0 + j * blk, blk), blk)]
        return pltpu.make_async_copy(src, acc_ref.at[acc_rows(j)], in_sem)

    def out_copy(j):
        dst = ys_ref.at[pl.ds(pl.multiple_of(row0 + j * blk, blk), blk)]
        return pltpu.make_async_copy(acc_ref.at[acc_rows(j)], dst, out_sem)

    def for_blocks(fn):
        def body(j, c):
            fn(j)
            return c

        lax.fori_loop(0, nb, body, 0)

    @pl.when((f == 0) & (nb > 0))
    def _():
        for_blocks(lambda j: in_copy(j).start())
        for_blocks(lambda j: in_copy(j).wait())

        def to_bf16(j):
            xbuf_ref[acc_rows(j), :] = acc_ref[acc_rows(j), :].astype(BF16)
            acc_ref[acc_rows(j), :] = jnp.broadcast_to(bd_ref[...], (blk, acc_ref.shape[1]))

        for_blocks(to_bf16)

    @pl.when(nb > 0)
    def _():
        def compute(r0, n_rows, wg, wu, wd):
            rows = pl.ds(pl.multiple_of(r0, blk), n_rows)
            xb = xbuf_ref[rows, :]
            hg = jnp.dot(xb, wg, preferred_element_type=F32) + bg_ref[...]
            hu = jnp.dot(xb, wu, preferred_element_type=F32) + bu_ref[...]
            gate = jnp.minimum(hg, SWIGLU_LIMIT)
            up = jnp.clip(hu, -SWIGLU_LIMIT, SWIGLU_LIMIT)
            act = (up + 1.0) * (gate * jax.nn.sigmoid(SWIGLU_ALPHA * gate))
            acc_ref[rows, :] += jnp.dot(act.astype(BF16), wd, preferred_element_type=F32)

            @pl.when(f == n_f - 1)
            def _():
                for b in range(n_rows // blk):
                    out_copy(r0 // blk + b).start()

        def cast_weights():
            wg = wg_ref[...].astype(BF16)
            wu = wu_ref[...].astype(BF16)
            wd = wd_ref[...].astype(BF16)
            wgb_ref[...] = wg
            wub_ref[...] = wu
            wdb_ref[...] = wd
            return wg, wu, wd

        def compute_cached(r0, n_rows):
            compute(r0, n_rows, wgb_ref[...], wub_ref[...], wdb_ref[...])

        n4 = nb // 4

        @pl.when(n4 > 0)
        def _():
            compute(0, 4 * blk, *cast_weights())

            def quad(j, c):
                compute_cached(j * (4 * blk), 4 * blk)
                return c

            lax.fori_loop(1, n4, quad, 0)

        @pl.when(n4 == 0)
        def _():
            cast_weights()

        @pl.when((nb & 2) != 0)
        def _():
            compute_cached(n4 * (4 * blk), 2 * blk)

        @pl.when((nb & 1) != 0)
        def _():
            compute_cached((nb - 1) * blk, blk)

    @pl.when((f == n_f - 1) & (nb > 0))
    def _():
        for_blocks(lambda j: out_copy(j).wait())


def _moe_experts(xs, items, w_gu, b_gu, w_down, b_down, layer, *, tf=512):
    item_e, item_row, item_nb = items
    n_items = item_e.shape[0]
    P, D = xs.shape
    L, E, _, F2 = w_gu.shape
    F = F2 // 2
    n_f = F // tf
    rc = MOE_CHUNK_BLOCKS * MOE_BLOCK

    def fe(f, nb_ref, it):
        return jnp.where(nb_ref[it] > 0, f, n_f - 1)

    in_specs = [
        pl.BlockSpec(memory_space=pl.ANY),
        pl.BlockSpec((None, None, D, tf), lambda it, f, e, r, nb: (layer, e[it], 0, fe(f, nb, it))),
        pl.BlockSpec((None, None, D, tf), lambda it, f, e, r, nb: (layer, e[it], 0, n_f + fe(f, nb, it))),
        pl.BlockSpec((None, None, 1, tf), lambda it, f, e, r, nb: (layer, e[it], 0, fe(f, nb, it))),
        pl.BlockSpec((None, None, 1, tf), lambda it, f, e, r, nb: (layer, e[it], 0, n_f + fe(f, nb, it))),
        pl.BlockSpec((None, None, tf, D), lambda it, f, e, r, nb: (layer, e[it], fe(f, nb, it), 0)),
        pl.BlockSpec((None, None, 1, D), lambda it, f, e, r, nb: (layer, e[it], 0, 0)),
    ]
    return pl.pallas_call(
        functools.partial(_moe_kernel, n_f=n_f),
        name="moe_experts",
        grid_spec=pltpu.PrefetchScalarGridSpec(
            num_scalar_prefetch=3, grid=(n_items, n_f),
            in_specs=in_specs,
            out_specs=pl.BlockSpec(memory_space=pl.ANY),
            scratch_shapes=[pltpu.VMEM((rc, D), BF16),
                            pltpu.VMEM((rc, D), F32),
                            pltpu.VMEM((D, tf), BF16),
                            pltpu.VMEM((D, tf), BF16),
                            pltpu.VMEM((tf, D), BF16),
                            pltpu.SemaphoreType.DMA(()),
                            pltpu.SemaphoreType.DMA(())]),
        out_shape=jax.ShapeDtypeStruct((P, D), F32),
        compiler_params=_cparams(("arbitrary", "arbitrary")),
    )(item_e, item_row, item_nb, xs, w_gu, w_gu, b_gu.reshape(L, E, 1, F2), b_gu.reshape(L, E, 1, F2),
      w_down, b_down.reshape(L, E, 1, D))


def _combine_kernel(dest_ref, ys_ref, gate_ref, x_ref, g_ref, b_ref, o_ref, buf_ref, sem, *scratch,
                    alpha, to_natural):
    tb = o_ref.shape[0]
    step = pl.program_id(0)
    slot = step % 2
    per_res = tb // DIL

    def token(blk, i):
        if not to_natural:
            return blk * tb + i
        n_a = pl.num_programs(0) * per_res
        return (i % DIL) * n_a + blk * per_res + i // DIL

    def copy(blk, sl, i, k):
        src = ys_ref.at[pl.ds(dest_ref[token(blk, i) * TOP_K + k], 1)]
        return pltpu.make_async_copy(src, buf_ref.at[sl, k, pl.ds(i, 1)], sem.at[sl])

    def for_rows(fn):
        def body(i, c):
            for k in range(TOP_K):
                fn(i, k)
            return c

        lax.fori_loop(0, tb, body, 0, unroll=4)

    @pl.when(step == 0)
    def _():
        for_rows(lambda i, k: copy(0, 0, i, k).start())

    @pl.when(step + 1 < pl.num_programs(0))
    def _():
        for_rows(lambda i, k: copy(step + 1, 1 - slot, i, k).start())

    for k in range(TOP_K):
        pltpu.make_async_copy(ys_ref.at[pl.ds(0, tb)], buf_ref.at[slot, k], sem.at[slot]).wait()
    if to_natural:
        xn_ref, gn_ref = scratch
        for r in range(DIL):
            gn_ref[pl.ds(r, per_res, stride=DIL), :] = gate_ref[r]
            for c in range(xn_ref.shape[0]):
                xn_ref[c, pl.ds(r, per_res, stride=DIL), :] = x_ref[r, :, c * LANES:(c + 1) * LANES]
        x = jnp.concatenate([xn_ref[c] for c in range(xn_ref.shape[0])], axis=1)
        gate = gn_ref[...]
    else:
        x = x_ref[...]
        gate = gate_ref[...]
    ffn = gate[:, 0:1] * buf_ref[slot, 0]
    for k in range(1, TOP_K):
        ffn = ffn + gate[:, k:k + 1] * buf_ref[slot, k]
    o_ref[...] = _layer_norm_rows(alpha * x + ffn, g_ref[...], b_ref[...])


def _combine_ln(ys, dest_flat, gates, x, g, b, *, alpha, to_natural, tb=128):
    T, D = x.shape
    gate_pad = gates
    scratch = [pltpu.VMEM((2, TOP_K, tb, D), F32), pltpu.SemaphoreType.DMA((2,))]
    if to_natural:
        na = T // DIL
        per_res = tb // DIL
        gate_in = gate_pad.reshape(DIL, na, LANES)
        x_in = x.reshape(DIL, na, D)
        gate_spec = pl.BlockSpec((DIL, per_res, LANES), lambda i, dest: (0, i, 0))
        x_spec = pl.BlockSpec((DIL, per_res, D), lambda i, dest: (0, i, 0))
        scratch += [pltpu.VMEM((D // LANES, tb, LANES), F32), pltpu.VMEM((tb, LANES), F32)]
    else:
        gate_in, x_in = gate_pad, x
        gate_spec = pl.BlockSpec((tb, LANES), lambda i, dest: (i, 0))
        x_spec = pl.BlockSpec((tb, D), lambda i, dest: (i, 0))
    return pl.pallas_call(
        functools.partial(_combine_kernel, alpha=alpha, to_natural=to_natural),
        name="moe_combine",
        grid_spec=pltpu.PrefetchScalarGridSpec(
            num_scalar_prefetch=1, grid=(T // tb,),
            in_specs=[pl.BlockSpec(memory_space=pl.ANY),
                      gate_spec,
                      x_spec,
                      pl.BlockSpec((1, D), lambda i, dest: (0, 0)),
                      pl.BlockSpec((1, D), lambda i, dest: (0, 0))],
            out_specs=pl.BlockSpec((tb, D), lambda i, dest: (i, 0)),
            scratch_shapes=scratch),
        out_shape=jax.ShapeDtypeStruct((T, D), F32),
        compiler_params=_cparams(("arbitrary",)),
    )(dest_flat, ys, gate_in, x_in, g.reshape(1, D), b.reshape(1, D))


def _moe_layout(idx, rank, counts, n_assign):
    E = counts.shape[0]
    blk = MOE_BLOCK
    nblk = (counts + blk - 1) // blk
    bend = jnp.cumsum(nblk)
    bstart = bend - nblk
    first_row = (bstart * blk).astype(I32)
    dest = rank
    for e in range(E):
        dest = dest + jnp.where(idx == e, first_row[e], 0)
    dest = dest[:, :TOP_K]
    tail = jnp.where(nblk > 0, (bend - 1) * blk, 0).astype(I32)
    cb = MOE_CHUNK_BLOCKS
    n_items_max = E + (n_assign // blk + E) // cb + 1
    per_e = (nblk + cb - 1) // cb
    iend = jnp.cumsum(per_e)
    istart = iend - per_e
    ids = jnp.arange(n_items_max, dtype=I32)
    e_of = jnp.minimum(jnp.searchsorted(iend, ids, side='right'), E - 1).astype(I32)
    valid = ids < iend[-1]
    last_e = jnp.max(jnp.where(per_e > 0, jnp.arange(E, dtype=I32), 0))
    e_of = jnp.where(valid, e_of, last_e)
    local = ids - istart[e_of]
    first_blk = bstart[e_of] + local * cb
    nb = jnp.where(valid, jnp.minimum(cb, nblk[e_of] - local * cb), 0)
    row = jnp.where(valid, first_blk * blk, 0)
    return dest.astype(I32), tail, (e_of, row.astype(I32), nb.astype(I32))


def _moe(x, w_router, b_router, w_gu, b_gu, w_down, b_down, ln_g, ln_b, *, alpha, layer, to_natural=False):
    T, D = x.shape
    E = w_router.shape[1]
    idx, gates, rank, counts = _route(x, w_router, b_router)
    n_assign = T * TOP_K
    dest, tail, items = _moe_layout(idx, rank, counts, n_assign)
    dest_flat = dest.reshape(n_assign)
    n_rows = n_assign + E * MOE_BLOCK
    xs = _dispatch(x, dest_flat, _zero_tails(tail, n_rows, D))
    ys = _moe_experts(xs, items, w_gu, b_gu, w_down, b_down, layer)
    return _combine_ln(ys, dest_flat, gates, x, ln_g, ln_b, alpha=alpha, to_natural=to_natural)


def _even_mixer_ln(x, w_in, lam_re, lam_im, log_step, b_re, b_im, c_re, c_im, d_skip, w_glu, b_glu,
                   w_gate2, b_gate2, norm_g, w_out, ln_g, ln_b, *, alpha, to_residue_major):
    T, D = x.shape
    W = d_skip.shape[0]
    qk = W // 2
    rank = w_gate2.shape[0]
    s4 = W + 2 * qk + W
    w_main = jnp.concatenate([w_in[:, :s4], w_in[:, s4 + rank:]], axis=1).astype(BF16)
    w_gate = jnp.zeros((D, LANES), BF16).at[:, :rank].set(w_in[:, s4:s4 + rank].astype(BF16))
    h, g_low = _proj_in(x, w_main, w_gate)
    tables = _s5_tables(lam_re, lam_im, log_step, b_re, b_im, c_re, c_im, T // S5_CHUNK)
    y = _s5_core(h, W, tables)
    ya = _s5_glu(y, h, d_skip.astype(F32), w_glu.astype(BF16), b_glu.astype(F32))
    yb = _gla(h, g_low, w_gate2, b_gate2, norm_g, width=W)
    w_out_b = w_out.astype(BF16)
    return _proj_ln([ya, yb], [w_out_b[:W], w_out_b[W:]], x, ln_g, ln_b, alpha=alpha,
                    to_residue_major=to_residue_major)


def _odd_mixer_ln(x, w_qkv, w_o, ln_g, ln_b, *, alpha):
    T, D = x.shape
    qkv = _proj_scaled(x, w_qkv.astype(BF16), scaled_cols=D, scale=(D // ATT_HEADS) ** -0.5)
    y = _dilated_attention(qkv, D)
    return _proj_ln([y], [w_o.astype(BF16)], x, ln_g, ln_b, alpha=alpha, to_residue_major=False)


def kernel(x, ab_w_in, s5_lam_re, s5_lam_im, s5_log_step, s5_b_re, s5_b_im, s5_c_re, s5_c_im, s5_d, s5_w_glu, s5_b_glu, gla_w_gate2, gla_b_gate2, gla_norm_g, ab_w_out, c_w_qkv, c_w_o, ln1_g, ln1_b, moe_w_router, moe_b_router, moe_w_gu, moe_b_gu, moe_w_down, moe_b_down, ln2_g, ln2_b):
    bsz, L, D = x.shape
    depth = ln1_g.shape[0]
    alpha = (2 * depth) ** 0.25
    outs = []
    for bi in range(bsz):
        xt = x[bi].astype(F32)
        for layer in range(depth):
            i = layer // 2
            odd = layer % 2 == 1
            if not odd:
                xt = _even_mixer_ln(xt, ab_w_in[i], s5_lam_re[i], s5_lam_im[i], s5_log_step[i], s5_b_re[i],
                                    s5_b_im[i], s5_c_re[i], s5_c_im[i], s5_d[i], s5_w_glu[i], s5_b_glu[i],
                                    gla_w_gate2[i], gla_b_gate2[i], gla_norm_g[i], ab_w_out[i],
                                    ln1_g[layer], ln1_b[layer], alpha=alpha, to_residue_major=layer + 1 < depth)
            else:
                xt = _odd_mixer_ln(xt, c_w_qkv[i], c_w_o[i], ln1_g[layer], ln1_b[layer], alpha=alpha)
            xt = _moe(xt, moe_w_router[layer], moe_b_router[layer], moe_w_gu, moe_b_gu,
                      moe_w_down, moe_b_down, ln2_g[layer], ln2_b[layer], alpha=alpha, layer=layer,
                      to_natural=odd)
        outs.append(xt.astype(x.dtype))
    return outs[0].reshape(1, L, D) if bsz == 1 else jnp.stack(outs)
```

```python
import functools
import math

import jax
import jax.numpy as jnp
from jax import lax
from jax.experimental import pallas as pl
from jax.experimental.pallas import tpu as pltpu

F32 = jnp.float32
BF16 = jnp.bfloat16
I32 = jnp.int32
HIGHEST = lax.Precision.HIGHEST

LANES = 128
VMEM_LIMIT_BYTES = 56 * 1024 * 1024

S5_GROUP = 16
S5_STATE = 64
S5_MAX_RE = -1e-4
S5_CHUNK = 16
GLA_HEADS = 4
GLA_GATE_TEMP = 16.0
GLA_EPS = 1e-6
GLA_BLOCK = 64
GLA_SUB = 16
ATT_HEADS = 16
DIL = 16
DILATED_GROUPS = ((128, 1), (512, 4), (2048, 16))
TOP_K = 4
SWIGLU_LIMIT = 7.0
SWIGLU_ALPHA = 1.702
MOE_BLOCK = 128
MOE_CHUNK_BLOCKS = 10
LN_EPS = 1e-5
NEG_BIG = -1e30


def _cparams(semantics):
    return pltpu.CompilerParams(dimension_semantics=semantics, vmem_limit_bytes=VMEM_LIMIT_BYTES)


def _layer_norm_rows(z, g, b):
    mu = jnp.mean(z, axis=-1, keepdims=True)
    zc = z - mu
    var = jnp.mean(zc * zc, axis=-1, keepdims=True)
    return zc * lax.rsqrt(var + LN_EPS) * g + b


def _proj_in_kernel(x_ref, w_ref, wg_ref, h_ref, g_ref, xb_ref):
    @pl.when(pl.program_id(1) == 0)
    def _():
        xb = x_ref[...].astype(BF16)
        xb_ref[...] = xb
        g_ref[...] = jnp.dot(xb, wg_ref[...], preferred_element_type=F32)

    h_ref[...] = jnp.dot(xb_ref[...], w_ref[...], preferred_element_type=F32)


def _proj_in(x, w_main, w_gate, *, tm=1024, tn=512):
    T, D = x.shape
    N = w_main.shape[1]
    return pl.pallas_call(
        _proj_in_kernel,
        name="proj_in",
        grid=(T // tm, N // tn),
        in_specs=[pl.BlockSpec((tm, D), lambda i, j: (i, 0)),
                  pl.BlockSpec((D, tn), lambda i, j: (0, j)),
                  pl.BlockSpec((D, LANES), lambda i, j: (0, 0))],
        out_specs=[pl.BlockSpec((tm, tn), lambda i, j: (i, j)),
                   pl.BlockSpec((tm, LANES), lambda i, j: (i, 0))],
        out_shape=[jax.ShapeDtypeStruct((T, N), F32), jax.ShapeDtypeStruct((T, LANES), F32)],
        scratch_shapes=[pltpu.VMEM((tm, D), BF16)],
        compiler_params=_cparams(("parallel", "arbitrary")),
    )(x, w_main, w_gate)


def _proj_scaled_kernel(x_ref, w_ref, o_ref, xb_ref, *, n_scaled, scale):
    j = pl.program_id(1)

    @pl.when(j == 0)
    def _():
        xb_ref[...] = x_ref[...].astype(BF16)

    y = jnp.dot(xb_ref[...], w_ref[...], preferred_element_type=F32)
    o_ref[...] = (y * jnp.where(j < n_scaled, scale, 1.0)).astype(o_ref.dtype)


def _proj_scaled(x, w, *, scaled_cols, scale, tm=1024, tn=1024):
    T, D = x.shape
    N = w.shape[1]
    return pl.pallas_call(
        functools.partial(_proj_scaled_kernel, n_scaled=scaled_cols // tn, scale=scale),
        name="proj_qkv",
        grid=(T // tm, N // tn),
        in_specs=[pl.BlockSpec((tm, D), lambda i, j: (i, 0)),
                  pl.BlockSpec((D, tn), lambda i, j: (0, j))],
        out_specs=pl.BlockSpec((tm, tn), lambda i, j: (i, j)),
        out_shape=jax.ShapeDtypeStruct((T, N), BF16),
        scratch_shapes=[pltpu.VMEM((tm, D), BF16)],
        compiler_params=_cparams(("parallel", "arbitrary")),
    )(x, w)


def _proj_ln_kernel(*refs, n_lhs, alpha, n_tiles, to_residue_major):
    lhs_refs = refs[:n_lhs]
    w_refs = refs[n_lhs:2 * n_lhs]
    res_ref, g_ref, b_ref, o_ref, acc_ref = refs[2 * n_lhs:2 * n_lhs + 5]
    j = pl.program_id(1)
    y = jnp.dot(lhs_refs[0][...], w_refs[0][...], preferred_element_type=F32)
    for a_ref, w_ref in zip(lhs_refs[1:], w_refs[1:]):
        y = y + jnp.dot(a_ref[...], w_ref[...], preferred_element_type=F32)
    acc_ref[j] = y

    @pl.when(j == n_tiles - 1)
    def _():
        tn = acc_ref.shape[2]
        z = [alpha * res_ref[:, t * tn:(t + 1) * tn] + acc_ref[t] for t in range(n_tiles)]
        n = float(n_tiles * tn)
        mu = sum(jnp.sum(zt, axis=-1, keepdims=True) for zt in z) / n
        zc = [zt - mu for zt in z]
        var = sum(jnp.sum(zt * zt, axis=-1, keepdims=True) for zt in zc) / n
        rstd = lax.rsqrt(var + LN_EPS)
        out = [zc[t] * rstd * g_ref[:, t * tn:(t + 1) * tn] + b_ref[:, t * tn:(t + 1) * tn] for t in range(n_tiles)]
        if not to_residue_major:
            for t in range(n_tiles):
                o_ref[:, t * tn:(t + 1) * tn] = out[t]
        else:
            rows_ref = refs[-1]
            per_res = rows_ref.shape[1] // DIL
            for c in range(rows_ref.shape[0]):
                t, off = divmod(c * LANES, tn)
                rows_ref[c] = out[t][:, off:off + LANES]
                for r in range(DIL):
                    o_ref[r, :, c * LANES:(c + 1) * LANES] = rows_ref[c, pl.ds(r, per_res, stride=DIL), :]


def _proj_ln(lhs, ws, res, g, b, *, alpha, to_residue_major, tm=512, tn=512):
    T, N = res.shape
    n_lhs = len(lhs)
    n_tiles = N // tn
    scratch = [pltpu.VMEM((n_tiles, tm, tn), F32)]
    if to_residue_major:
        na = T // DIL
        out_spec = pl.BlockSpec((DIL, tm // DIL, N), lambda i, j: (0, i, 0))
        out_shape = jax.ShapeDtypeStruct((DIL, na, N), F32)
        scratch.append(pltpu.VMEM((N // LANES, tm, LANES), F32))
    else:
        out_spec = pl.BlockSpec((tm, N), lambda i, j: (i, 0))
        out_shape = jax.ShapeDtypeStruct((T, N), F32)
    in_specs = ([pl.BlockSpec((tm, a.shape[1]), lambda i, j: (i, 0)) for a in lhs]
                + [pl.BlockSpec((w.shape[0], tn), lambda i, j: (0, j)) for w in ws]
                + [pl.BlockSpec((tm, N), lambda i, j: (i, 0)),
                   pl.BlockSpec((1, N), lambda i, j: (0, 0)),
                   pl.BlockSpec((1, N), lambda i, j: (0, 0))])
    out = pl.pallas_call(
        functools.partial(_proj_ln_kernel, n_lhs=n_lhs, alpha=alpha, n_tiles=n_tiles,
                          to_residue_major=to_residue_major),
        name="proj_ln",
        grid=(T // tm, n_tiles),
        in_specs=in_specs,
        out_specs=out_spec,
        out_shape=out_shape,
        scratch_shapes=scratch,
        compiler_params=_cparams(("parallel", "arbitrary")),
    )(*lhs, *ws, res, g.reshape(1, N), b.reshape(1, N))
    return out.reshape(T, N)


def _s5_tables(lam_re, lam_im, log_step, b_re, b_im, c_re, c_im, n_chunks):
    C = S5_CHUNK
    G, P = lam_re.shape
    H = b_re.shape[-1]
    lr = jnp.minimum(lam_re.astype(F32), S5_MAX_RE)
    li = lam_im.astype(F32)
    dt = jnp.exp(log_step.astype(F32))[:, None]
    kk = jnp.arange(C + 1, dtype=F32)[:, None, None]
    pw_mag = jnp.exp(kk * (lr * dt))
    pw_re = pw_mag * jnp.cos(kk * (li * dt))
    pw_im = pw_mag * jnp.sin(kk * (li * dt))
    a_re, a_im = pw_re[1], pw_im[1]
    den = lr * lr + li * li
    nr = a_re - 1.0
    f_re = (nr * lr + a_im * li) / den
    f_im = (a_im * lr - nr * li) / den
    br = b_re.astype(F32)
    bi = b_im.astype(F32)
    bb_re = f_re[..., None] * br - f_im[..., None] * bi
    bb_im = f_re[..., None] * bi + f_im[..., None] * br
    ab_re = pw_re[:C, :, :, None] * bb_re[None] - pw_im[:C, :, :, None] * bb_im[None]
    ab_im = pw_re[:C, :, :, None] * bb_im[None] + pw_im[:C, :, :, None] * bb_re[None]
    cr = c_re.astype(F32)
    ci = c_im.astype(F32)
    z_re = jnp.transpose(ab_re[::-1], (1, 0, 3, 2)).reshape(G, C * H, P)
    z_im = jnp.transpose(ab_im[::-1], (1, 0, 3, 2)).reshape(G, C * H, P)
    zmat = jnp.concatenate([z_re, z_im], axis=-1)
    kern = (jnp.einsum('gop,kgph->kgoh', cr, ab_re, precision=HIGHEST)
            - jnp.einsum('gop,kgph->kgoh', ci, ab_im, precision=HIGHEST))
    lag = jnp.arange(C)[None, :] - jnp.arange(C)[:, None]
    kl = kern[jnp.clip(lag, 0, C - 1)]
    kl = jnp.where((lag >= 0)[:, :, None, None, None], kl, 0.0)
    mmat = jnp.transpose(kl, (2, 0, 4, 1, 3)).reshape(G, C * H, C * H)
    ca_re = cr[None] * pw_re[1:, :, None, :] - ci[None] * pw_im[1:, :, None, :]
    ca_im = cr[None] * pw_im[1:, :, None, :] + ci[None] * pw_re[1:, :, None, :]
    n_re = jnp.transpose(ca_re, (1, 3, 0, 2)).reshape(G, P, C * H)
    n_im = jnp.transpose(-ca_im, (1, 3, 0, 2)).reshape(G, P, C * H)
    nmat = jnp.concatenate([n_re, n_im], axis=1)
    n_steps = max(1, (n_chunks - 1).bit_length())
    qr, qi = pw_re[C], pw_im[C]
    a1, a2 = [], []
    for _ in range(n_steps):
        a1.append(jnp.concatenate([qr, qr], axis=-1))
        a2.append(jnp.concatenate([-qi, qi], axis=-1))
        qr, qi = qr * qr - qi * qi, 2.0 * qr * qi
    a1 = jnp.stack(a1, axis=1)
    a2 = jnp.stack(a2, axis=1)
    return zmat.astype(BF16), mmat.astype(BF16), nmat.astype(BF16), a1, a2


def _s5_kernel(u_ref, z_ref, m_ref, n_ref, a1_ref, a2_ref, y_ref, ub_ref, ug_ref, sg_ref, mw_ref, nw_ref, *,
               n_steps):
    C, H = S5_CHUNK, S5_GROUP
    nc = u_ref.shape[0] // C
    gpt = LANES // H
    ch = C * H
    for j in range(C):
        ub_ref[:, j * LANES:(j + 1) * LANES] = u_ref[pl.ds(j, nc, stride=C), :].astype(BF16)
    row = lax.broadcasted_iota(I32, (nc, LANES), 0)
    sr = lax.broadcasted_iota(I32, (gpt * LANES, LANES), 0)
    sc = lax.broadcasted_iota(I32, (gpt * LANES, LANES), 1)
    sel_hit = sc == H * (sr // LANES) + sr % H
    sel_grp = (sr % LANES) // H
    pr = lax.broadcasted_iota(I32, (ch, C * LANES), 0)
    pc = lax.broadcasted_iota(I32, (ch, C * LANES), 1)
    put_tile = pc // LANES == pr // H
    put_lane = pc % LANES - pr % H

    def group(gl, c):
        sel = jnp.where(sel_hit & (sel_grp == gl), 1.0, 0.0).astype(BF16)
        half_w = gpt * LANES
        u = jnp.concatenate(
            [jnp.dot(ub_ref[:, t * half_w:(t + 1) * half_w], sel, preferred_element_type=F32)
             for t in range(C * LANES // half_w)], axis=1).astype(BF16)
        s = jnp.dot(u, z_ref[gl], preferred_element_type=F32)
        half = s.shape[1] // 2
        a1 = a1_ref[gl]
        a2 = a2_ref[gl]
        for k in range(n_steps):
            sh = 1 << k
            prev = jnp.where(row >= sh, pltpu.roll(s, sh, axis=0), 0.0)
            s = s + a1[k:k + 1, :] * prev + a2[k:k + 1, :] * pltpu.roll(prev, half, axis=1)
        s_in = jnp.where(row >= 1, pltpu.roll(s, 1, axis=0), 0.0)
        put = jnp.where(put_tile & (put_lane == H * gl), 1.0, 0.0).astype(BF16)
        ug_ref[gl] = u
        sg_ref[gl] = s_in.astype(BF16)
        mw_ref[gl] = jnp.dot(m_ref[gl], put, preferred_element_type=F32).astype(BF16)
        nw_ref[gl] = jnp.dot(n_ref[gl], put, preferred_element_type=F32).astype(BF16)
        return c

    lax.fori_loop(0, gpt, group, 0)
    u_all = jnp.concatenate([ug_ref[g] for g in range(gpt)], axis=1)
    s_all = jnp.concatenate([sg_ref[g] for g in range(gpt)], axis=1)
    y = (jnp.dot(u_all, mw_ref[...].reshape(gpt * ch, C * LANES), preferred_element_type=F32)
         + jnp.dot(s_all, nw_ref[...].reshape(gpt * sg_ref.shape[2], C * LANES), preferred_element_type=F32))
    for i in range(C):
        y_ref[pl.ds(i, nc, stride=C), :] = y[:, i * LANES:(i + 1) * LANES]


def _s5_core(h, width, tables):
    zmat, mmat, nmat, a1, a2 = tables
    T = h.shape[0]
    G, CH, P2 = zmat.shape
    C, H = S5_CHUNK, S5_GROUP
    nc = T // C
    n_steps = a1.shape[1]
    gpt = LANES // H
    assert CH == C * H and width == G * H
    return pl.pallas_call(
        functools.partial(_s5_kernel, n_steps=n_steps),
        name="s5_core",
        grid=(G // gpt,),
        in_specs=[pl.BlockSpec((T, LANES), lambda m: (0, m)),
                  pl.BlockSpec((gpt, CH, P2), lambda m: (m, 0, 0)),
                  pl.BlockSpec((gpt, CH, CH), lambda m: (m, 0, 0)),
                  pl.BlockSpec((gpt, P2, CH), lambda m: (m, 0, 0)),
                  pl.BlockSpec((gpt, n_steps, P2), lambda m: (m, 0, 0)),
                  pl.BlockSpec((gpt, n_steps, P2), lambda m: (m, 0, 0))],
        out_specs=pl.BlockSpec((T, LANES), lambda m: (0, m)),
        out_shape=jax.ShapeDtypeStruct((T, width), F32),
        scratch_shapes=[pltpu.VMEM((nc, C * LANES), BF16),
                        pltpu.VMEM((gpt, nc, CH), BF16),
                        pltpu.VMEM((gpt, nc, P2), BF16),
                        pltpu.VMEM((gpt, CH, C * LANES), BF16),
                        pltpu.VMEM((gpt, P2, C * LANES), BF16)],
        compiler_params=_cparams(("parallel",)),
    )(h, zmat, mmat, nmat, a1, a2)


def _s5_glu_kernel(y_ref, u_ref, d_ref, w_ref, b_ref, o_ref):
    y = y_ref[...] + d_ref[...] * u_ref[...]
    c0 = math.sqrt(2.0 / math.pi)
    z = 0.5 * y * (1.0 + jnp.tanh(c0 * (y + 0.044715 * (y * y * y))))
    lin = jnp.dot(z.astype(BF16), w_ref[...], preferred_element_type=F32) + b_ref[...]
    o_ref[...] = (z * jax.nn.sigmoid(lin)).astype(o_ref.dtype)


def _s5_glu(y, h, d_skip, w_glu, b_glu, *, tm=512):
    T, W = y.shape
    return pl.pallas_call(
        _s5_glu_kernel,
        name="s5_glu",
        grid=(T // tm,),
        in_specs=[pl.BlockSpec((tm, W), lambda i: (i, 0)),
                  pl.BlockSpec((tm, W), lambda i: (i, 0)),
                  pl.BlockSpec((1, W), lambda i: (0, 0)),
                  pl.BlockSpec((W, W), lambda i: (0, 0)),
                  pl.BlockSpec((1, W), lambda i: (0, 0))],
        out_specs=pl.BlockSpec((tm, W), lambda i: (i, 0)),
        out_shape=jax.ShapeDtypeStruct((T, W), BF16),
        compiler_params=_cparams(("parallel",)),
    )(y, h, d_skip.reshape(1, W), w_glu, b_glu.reshape(1, W))


def _gla_kernel(q_ref, k_ref, v_ref, r_ref, g_ref, w2_ref, b2_ref, ng_ref, o_ref, st_ref, *, dk, dv):
    @pl.when(pl.program_id(0) == 0)
    def _():
        st_ref[...] = jnp.zeros_like(st_ref)

    cb = q_ref.shape[0]
    n_sub = cb // GLA_SUB
    scale = dk ** -0.5
    logit = jnp.dot(g_ref[...], w2_ref[...], preferred_element_type=F32, precision=HIGHEST) + b2_ref[...]
    log_a = (jnp.minimum(logit, 0.0) - jnp.log(1.0 + jnp.exp(-jnp.abs(logit)))) / GLA_GATE_TEMP
    ri = lax.broadcasted_iota(I32, (cb, cb), 0)
    ci = lax.broadcasted_iota(I32, (cb, cb), 1)
    tri = (ri >= ci).astype(F32)
    bcum = jnp.dot(tri, log_a, preferred_element_type=F32, precision=HIGHEST)
    states = [st_ref[hh] for hh in range(GLA_HEADS)]
    new_states, outs = [], []
    for hh in range(GLA_HEADS):
        ks = slice(hh * dk, (hh + 1) * dk)
        vs = slice(hh * dv, (hh + 1) * dv)
        b = bcum[:, ks]
        q = q_ref[:, ks] * scale
        k = k_ref[:, ks]
        v = v_ref[:, vs].astype(BF16)
        refs = [jnp.zeros((1, dk), F32)] + [b[a * GLA_SUB - 1:a * GLA_SUB, :] for a in range(1, n_sub)]
        refmat = jnp.concatenate([jnp.broadcast_to(r, (GLA_SUB, dk)) for r in refs], axis=0)
        qe = (q * jnp.exp(b - refmat)).astype(BF16)
        st = states[hh]
        o_inter = lax.dot_general((q * jnp.exp(b)).astype(BF16), st.astype(BF16),
                                  (((1,), (1,)), ((), ())), preferred_element_type=F32)
        o_rows = []
        for a in range(n_sub):
            hi = (a + 1) * GLA_SUB
            ke = (k[:hi] * jnp.exp(refs[a] - b[:hi])).astype(BF16)
            att = lax.dot_general(qe[a * GLA_SUB:hi], ke, (((1,), (1,)), ((), ())),
                                  preferred_element_type=F32)
            row_a = lax.broadcasted_iota(I32, (GLA_SUB, hi), 0) + a * GLA_SUB
            att = jnp.where(lax.broadcasted_iota(I32, (GLA_SUB, hi), 1) <= row_a, att, 0.0)
            o_rows.append(jnp.dot(att.astype(BF16), v[:hi], preferred_element_type=F32))
        o = jnp.concatenate(o_rows, axis=0) + o_inter
        o = o * lax.rsqrt(jnp.mean(o * o, axis=-1, keepdims=True) + GLA_EPS) * ng_ref[...]
        r = r_ref[:, vs]
        outs.append((o * (r * jax.nn.sigmoid(r))).astype(o_ref.dtype))
        b_last = b[cb - 1:cb, :]
        kd = (k * jnp.exp(b_last - b)).astype(BF16)
        upd = lax.dot_general(v, kd, (((0,), (0,)), ((), ())), preferred_element_type=F32)
        new_states.append(st * jnp.exp(b_last) + upd)
    for hh in range(GLA_HEADS):
        o_ref[:, hh * dv:(hh + 1) * dv] = outs[hh]
        st_ref[hh] = new_states[hh]


def _gla(h, g_low, w_gate2, b_gate2, norm_g, *, width):
    T = h.shape[0]
    qk = width // 2
    dk = qk // GLA_HEADS
    dv = width // GLA_HEADS
    cb = GLA_BLOCK
    w2 = jnp.zeros((LANES, qk), F32).at[:w_gate2.shape[0]].set(w_gate2.astype(F32))
    return pl.pallas_call(
        functools.partial(_gla_kernel, dk=dk, dv=dv),
        name="gla",
        grid=(T // cb,),
        in_specs=[pl.BlockSpec((cb, qk), lambda i: (i, 2)),
                  pl.BlockSpec((cb, qk), lambda i: (i, 3)),
                  pl.BlockSpec((cb, width), lambda i: (i, 2)),
                  pl.BlockSpec((cb, width), lambda i: (i, 3)),
                  pl.BlockSpec((cb, LANES), lambda i: (i, 0)),
                  pl.BlockSpec((LANES, qk), lambda i: (0, 0)),
                  pl.BlockSpec((1, qk), lambda i: (0, 0)),
                  pl.BlockSpec((1, dv), lambda i: (0, 0))],
        out_specs=pl.BlockSpec((cb, width), lambda i: (i, 0)),
        out_shape=jax.ShapeDtypeStruct((T, width), BF16),
        scratch_shapes=[pltpu.VMEM((GLA_HEADS, dv, dk), F32)],
        compiler_params=_cparams(("arbitrary",)),
    )(h, h, h, h, g_low, w2, b_gate2.reshape(1, qk).astype(F32), norm_g.reshape(1, dv).astype(F32))


def _attn_kernel(q_ref, k_ref, v_ref, o_ref, acc_ref, m_ref, l_ref, v1_ref, pat1_ref, pat4_ref, pat16_ref, *, na):
    dh = v_ref.shape[1]
    v1_ref[:, :dh] = v_ref[...]
    v1_ref[:, dh:] = jnp.ones_like(v_ref)
    acc_ref[...] = jnp.zeros_like(acc_ref)
    m_ref[...] = jnp.full_like(m_ref, NEG_BIG)
    l_ref[...] = jnp.zeros_like(l_ref)

    def token_offsets(rows, r_step, n, axis):
        shape = (n, 1) if axis == 0 else (1, n)
        idx = lax.broadcasted_iota(I32, shape, axis)
        c = idx // rows
        return DIL * (idx - c * rows) + r_step * c

    def delta_pattern(n_chunks, q_rows, k_rows, r_step):
        return (token_offsets(q_rows, r_step, n_chunks * q_rows, 0)
                - token_offsets(k_rows, r_step, n_chunks * k_rows, 1))

    def cat(ref, starts, rows):
        return jnp.concatenate([ref[pl.ds(pl.multiple_of(s, 16), rows), :] for s in starts], axis=0)

    def attend(blocks, q_rows, k_rows, window, pat_ref):
        loaded = []
        for q_starts, k_starts, a_q, a_k in blocks:
            loaded.append((cat(q_ref, q_starts, q_rows), cat(k_ref, k_starts, k_rows), cat(v1_ref, k_starts, k_rows),
                           cat(m_ref, q_starts, q_rows), cat(l_ref, q_starts, q_rows),
                           cat(acc_ref, q_starts, q_rows)))
        results = []
        for (q_starts, k_starts, a_q, a_k), (qb, kb, vb, m_old, l_old, acc_old) in zip(blocks, loaded):
            s = lax.dot_general(qb, kb, (((1,), (1,)), ((), ())), preferred_element_type=F32)
            off = DIL * (a_q - a_k)
            pat = pat_ref[...]
            s = jnp.where((pat >= -off) & (pat <= window - off), s, NEG_BIG)
            m_new = jnp.maximum(m_old, jnp.max(s, axis=1, keepdims=True))
            alpha = jnp.exp(m_old - m_new)
            p = jnp.exp(s - m_new[:, :1])
            pv = jnp.dot(p.astype(BF16), vb, preferred_element_type=F32)
            l_new = alpha * l_old + pv[:, dh:]
            acc_new = alpha * acc_old + pv[:, :dh]
            results.append((m_new, l_new, acc_new))
        for (q_starts, *_), (m_new, l_new, acc_new) in zip(blocks, results):
            for c, st in enumerate(q_starts):
                st = pl.multiple_of(st, 16)
                rs = slice(c * q_rows, (c + 1) * q_rows)
                m_ref[pl.ds(st, q_rows), :] = m_new[rs]
                l_ref[pl.ds(st, q_rows), :] = l_new[rs]
                acc_ref[pl.ds(st, q_rows), :] = acc_new[rs]

    w1 = DILATED_GROUPS[0][0]
    u1 = 2
    pat1_ref[...] = delta_pattern(DIL, 16, 32, 1)

    def body1(it, carry):
        blocks = []
        for j in range(u1):
            a0 = (it * u1 + j) * 16
            ak = jnp.maximum(a0 - 16, 0)
            blocks.append(([r * na + a0 for r in range(DIL)], [r * na + ak for r in range(DIL)], a0, ak))
        attend(blocks, 16, 32, w1, pat1_ref)
        return carry

    lax.fori_loop(0, na // (16 * u1), body1, 0)

    w4 = DILATED_GROUPS[1][0]
    pat4_ref[...] = delta_pattern(4, 32, 64, 4)

    def body4(it, carry):
        a0 = it * 32
        ak = jnp.maximum(a0 - 32, 0)
        blocks = [([(rho + 4 * sg) * na + a0 for sg in range(4)], [(rho + 4 * sg) * na + ak for sg in range(4)],
                   a0, ak) for rho in range(4)]
        attend(blocks, 32, 64, w4, pat4_ref)
        return carry

    lax.fori_loop(0, na // 32, body4, 0)

    w16 = DILATED_GROUPS[2][0]
    u16 = 4
    pat16_ref[...] = delta_pattern(1, 128, 256, 0)

    def body16(it, carry):
        rg = it // (na // 128)
        a0 = (it - rg * (na // 128)) * 128
        ak = jnp.maximum(a0 - 128, 0)
        blocks = [([(rg * u16 + j) * na + a0], [(rg * u16 + j) * na + ak], a0, ak) for j in range(u16)]
        attend(blocks, 128, 256, w16, pat16_ref)
        return carry

    lax.fori_loop(0, (DIL // u16) * (na // 128), body16, 0)

    o_ref[...] = (acc_ref[...] / l_ref[...]).astype(o_ref.dtype)


def _dilated_attention(qkv, d_model):
    T = qkv.shape[0]
    dh = d_model // ATT_HEADS
    assert dh == LANES
    na = T // DIL
    return pl.pallas_call(
        functools.partial(_attn_kernel, na=na),
        name="dilated_attn",
        grid=(ATT_HEADS,),
        in_specs=[pl.BlockSpec((T, dh), lambda h: (0, h)),
                  pl.BlockSpec((T, dh), lambda h: (0, ATT_HEADS + h)),
                  pl.BlockSpec((T, dh), lambda h: (0, 2 * ATT_HEADS + h))],
        out_specs=pl.BlockSpec((T, dh), lambda h: (0, h)),
        out_shape=jax.ShapeDtypeStruct((T, d_model), BF16),
        scratch_shapes=[pltpu.VMEM((T, dh), F32), pltpu.VMEM((T, dh), F32), pltpu.VMEM((T, dh), F32),
                        pltpu.VMEM((T, 2 * dh), BF16), pltpu.VMEM((256, 512), I32), pltpu.VMEM((128, 256), I32), pltpu.VMEM((128, 256), I32)],
        compiler_params=_cparams(("parallel",)),
    )(qkv, qkv, qkv)


def _route_kernel(x_ref, w_ref, b_ref, idx_ref, gate_ref, rank_ref, cnt_ref, carry_ref):
    @pl.when(pl.program_id(0) == 0)
    def _():
        carry_ref[...] = jnp.zeros_like(carry_ref)

    tb = x_ref.shape[0]
    x = x_ref[...]
    w = w_ref[...]
    x_hi, w_hi = x.astype(BF16), w.astype(BF16)
    x_lo = (x - x_hi.astype(F32)).astype(BF16)
    w_lo = (w - w_hi.astype(F32)).astype(BF16)
    lg = (jnp.dot(x_hi, w_hi, preferred_element_type=F32) + jnp.dot(x_hi, w_lo, preferred_element_type=F32)
          + jnp.dot(x_lo, w_hi, preferred_element_type=F32)) + b_ref[...]
    lane = lax.broadcasted_iota(I32, lg.shape, 1)
    vals, hots = [], []
    idx_out = jnp.zeros(lg.shape, I32)
    for k in range(TOP_K):
        m = jnp.max(lg, axis=1, keepdims=True)
        sel = jnp.min(jnp.where(lg == m, lane, LANES), axis=1, keepdims=True)
        hot = lane == sel
        vals.append(m)
        hots.append(hot)
        idx_out = jnp.where(lane == k, sel, idx_out)
        lg = jnp.where(hot, -jnp.inf, lg)
    ex = [jnp.exp(v - vals[0]) for v in vals]
    den = sum(ex)
    gate_out = jnp.zeros(lg.shape, F32)
    for k in range(TOP_K):
        gate_out = jnp.where(lane == k, ex[k] / den, gate_out)
    chosen = sum(h.astype(F32) for h in hots)
    ri = lax.broadcasted_iota(I32, (tb, tb), 0)
    ci = lax.broadcasted_iota(I32, (tb, tb), 1)
    before = jnp.dot((ri > ci).astype(BF16), chosen.astype(BF16), preferred_element_type=F32) + carry_ref[...]
    rank_out = jnp.zeros(lg.shape, I32)
    for k in range(TOP_K):
        rk = jnp.sum(jnp.where(hots[k], before, 0.0), axis=1, keepdims=True).astype(I32)
        rank_out = jnp.where(lane == k, rk, rank_out)
    idx_ref[...] = idx_out
    gate_ref[...] = gate_out
    rank_ref[...] = rank_out
    carry_ref[...] = carry_ref[...] + jnp.sum(chosen, axis=0, keepdims=True)
    cnt_ref[...] = carry_ref[...].astype(I32)


def _route(x, w_router, b_router, *, tb=512):
    T, D = x.shape
    E = w_router.shape[1]
    wr = jnp.zeros((D, LANES), F32).at[:, :E].set(w_router.astype(F32))
    br = jnp.full((1, LANES), NEG_BIG, F32).at[0, :E].set(b_router.astype(F32))
    row = lambda dt: jax.ShapeDtypeStruct((T, LANES), dt)
    idx, gate, rank, cnt = pl.pallas_call(
        _route_kernel,
        name="moe_route",
        grid=(T // tb,),
        in_specs=[pl.BlockSpec((tb, D), lambda i: (i, 0)),
                  pl.BlockSpec((D, LANES), lambda i: (0, 0)),
                  pl.BlockSpec((1, LANES), lambda i: (0, 0))],
        out_specs=[pl.BlockSpec((tb, LANES), lambda i: (i, 0)),
                   pl.BlockSpec((tb, LANES), lambda i: (i, 0)),
                   pl.BlockSpec((tb, LANES), lambda i: (i, 0)),
                   pl.BlockSpec((1, LANES), lambda i: (0, 0))],
        out_shape=[row(I32), row(F32), row(I32), jax.ShapeDtypeStruct((1, LANES), I32)],
        scratch_shapes=[pltpu.VMEM((1, LANES), F32)],
        compiler_params=_cparams(("arbitrary",)),
    )(x, wr, br)
    return idx, gate, rank, cnt[0, :E]


def _zero_tail_kernel(row_ref, o_ref, z_ref, sem):
    z_ref[...] = jnp.zeros_like(z_ref)
    n = row_ref.shape[0]

    def copy(e):
        return pltpu.make_async_copy(z_ref, o_ref.at[pl.ds(pl.multiple_of(row_ref[e], MOE_BLOCK), MOE_BLOCK)], sem)

    def start(e, c):
        copy(e).start()
        return c

    def wait(e, c):
        copy(e).wait()
        return c

    lax.fori_loop(0, n, start, 0)
    lax.fori_loop(0, n, wait, 0)


def _zero_tails(tail_rows, n_rows, d):
    return pl.pallas_call(
        _zero_tail_kernel,
        name="moe_zero_tails",
        grid_spec=pltpu.PrefetchScalarGridSpec(
            num_scalar_prefetch=1, grid=(1,),
            in_specs=[],
            out_specs=pl.BlockSpec(memory_space=pl.ANY),
            scratch_shapes=[pltpu.VMEM((MOE_BLOCK, d), F32), pltpu.SemaphoreType.DMA(())]),
        out_shape=jax.ShapeDtypeStruct((n_rows, d), F32),
        compiler_params=_cparams(("arbitrary",)),
    )(tail_rows)


def _dispatch_kernel(dest_ref, x_ref, xs_in_ref, xs_ref, sem):
    del xs_in_ref
    tb = x_ref.shape[0]
    base = pl.program_id(0) * tb * TOP_K

    def copy(i, k):
        return pltpu.make_async_copy(x_ref.at[pl.ds(i, 1)],
                                     xs_ref.at[pl.ds(dest_ref[base + i * TOP_K + k], 1)], sem)

    def start(i, c):
        for k in range(TOP_K):
            copy(i, k).start()
        return c

    lax.fori_loop(0, tb, start, 0, unroll=4)
    for _ in range(TOP_K):
        pltpu.make_async_copy(x_ref, xs_ref.at[pl.ds(0, tb)], sem).wait()


def _dispatch(x, dest_flat, xs_init, *, tb=256):
    T, D = x.shape
    return pl.pallas_call(
        _dispatch_kernel,
        name="moe_dispatch",
        grid_spec=pltpu.PrefetchScalarGridSpec(
            num_scalar_prefetch=1, grid=(T // tb,),
            in_specs=[pl.BlockSpec((tb, D), lambda i, dest: (i, 0)),
                      pl.BlockSpec(memory_space=pl.ANY)],
            out_specs=pl.BlockSpec(memory_space=pl.ANY),
            scratch_shapes=[pltpu.SemaphoreType.DMA(())]),
        out_shape=jax.ShapeDtypeStruct(xs_init.shape, xs_init.dtype),
        input_output_aliases={2: 0},
        compiler_params=_cparams(("arbitrary",)),
    )(dest_flat, x, xs_init)


def _moe_kernel(item_e_ref, item_row_ref, item_nb_ref, xs_ref, wg_ref, wu_ref, bg_ref, bu_ref, wd_ref, bd_ref,
                ys_ref, xbuf_ref, acc_ref, wgb_ref, wub_ref, wdb_ref, in_sem, out_sem, *, n_f):
    it = pl.program_id(0)
    f = pl.program_id(1)
    nb = item_nb_ref[it]
    row0 = item_row_ref[it]
    blk = MOE_BLOCK

    def acc_rows(j):
        return pl.ds(pl.multiple_of(j * blk, blk), blk)

    def in_copy(j):
        src = xs_ref.at[pl.ds(pl.multiple_of(row0 + j * blk, blk), blk)]
        return pltpu.make_async_copy(src, acc_ref.at[acc_rows(j)], in_sem)

    def out_copy(j):
        dst = ys_ref.at[pl.ds(pl.multiple_of(row0 + j * blk, blk), blk)]
        return pltpu.make_async_copy(acc_ref.at[acc_rows(j)], dst, out_sem)

    def for_blocks(fn):
        def body(j, c):
            fn(j)
            return c

        lax.fori_loop(0, nb, body, 0)

    @pl.when((f == 0) & (nb > 0))
    def _():
        for_blocks(lambda j: in_copy(j).start())
        for_blocks(lambda j: in_copy(j).wait())

        def to_bf16(j):
            xbuf_ref[acc_rows(j), :] = acc_ref[acc_rows(j), :].astype(BF16)
            acc_ref[acc_rows(j), :] = jnp.broadcast_to(bd_ref[...], (blk, acc_ref.shape[1]))

        for_blocks(to_bf16)

    @pl.when(nb > 0)
    def _():
        def compute(r0, n_rows, wg, wu, wd):
            rows = pl.ds(pl.multiple_of(r0, blk), n_rows)
            xb = xbuf_ref[rows, :]
            hg = jnp.dot(xb, wg, preferred_element_type=F32) + bg_ref[...]
            hu = jnp.dot(xb, wu, preferred_element_type=F32) + bu_ref[...]
            gate = jnp.minimum(hg, SWIGLU_LIMIT)
            up = jnp.clip(hu, -SWIGLU_LIMIT, SWIGLU_LIMIT)
            act = (up + 1.0) * (gate * jax.nn.sigmoid(SWIGLU_ALPHA * gate))
            acc_ref[rows, :] += jnp.dot(act.astype(BF16), wd, preferred_element_type=F32)

            @pl.when(f == n_f - 1)
            def _():
                for b in range(n_rows // blk):
                    out_copy(r0 // blk + b).start()

        def cast_weights():
            wg = wg_ref[...].astype(BF16)
            wu = wu_ref[...].astype(BF16)
            wd = wd_ref[...].astype(BF16)
            wgb_ref[...] = wg
            wub_ref[...] = wu
            wdb_ref[...] = wd
            return wg, wu, wd

        def compute_cached(r0, n_rows):
            compute(r0, n_rows, wgb_ref[...], wub_ref[...], wdb_ref[...])

        n4 = nb // 4

        @pl.when(n4 > 0)
        def _():
            compute(0, 4 * blk, *cast_weights())

            def quad(j, c):
                compute_cached(j * (4 * blk), 4 * blk)
                return c

            lax.fori_loop(1, n4, quad, 0)

        @pl.when(n4 == 0)
        def _():
            cast_weights()

        @pl.when((nb & 2) != 0)
        def _():
            compute_cached(n4 * (4 * blk), 2 * blk)

        @pl.when((nb & 1) != 0)
        def _():
            compute_cached((nb - 1) * blk, blk)

    @pl.when((f == n_f - 1) & (nb > 0))
    def _():
        for_blocks(lambda j: out_copy(j).wait())


def _moe_experts(xs, items, w_gu, b_gu, w_down, b_down, layer, *, tf=512):
    item_e, item_row, item_nb = items
    n_items = item_e.shape[0]
    P, D = xs.shape
    L, E, _, F2 = w_gu.shape
    F = F2 // 2
    n_f = F // tf
    rc = MOE_CHUNK_BLOCKS * MOE_BLOCK

    def fe(f, nb_ref, it):
        return jnp.where(nb_ref[it] > 0, f, n_f - 1)

    in_specs = [
        pl.BlockSpec(memory_space=pl.ANY),
        pl.BlockSpec((None, None, D, tf), lambda it, f, e, r, nb: (layer, e[it], 0, fe(f, nb, it))),
        pl.BlockSpec((None, None, D, tf), lambda it, f, e, r, nb: (layer, e[it], 0, n_f + fe(f, nb, it))),
        pl.BlockSpec((None, None, 1, tf), lambda it, f, e, r, nb: (layer, e[it], 0, fe(f, nb, it))),
        pl.BlockSpec((None, None, 1, tf), lambda it, f, e, r, nb: (layer, e[it], 0, n_f + fe(f, nb, it))),
        pl.BlockSpec((None, None, tf, D), lambda it, f, e, r, nb: (layer, e[it], fe(f, nb, it), 0)),
        pl.BlockSpec((None, None, 1, D), lambda it, f, e, r, nb: (layer, e[it], 0, 0)),
    ]
    return pl.pallas_call(
        functools.partial(_moe_kernel, n_f=n_f),
        name="moe_experts",
        grid_spec=pltpu.PrefetchScalarGridSpec(
            num_scalar_prefetch=3, grid=(n_items, n_f),
            in_specs=in_specs,
            out_specs=pl.BlockSpec(memory_space=pl.ANY),
            scratch_shapes=[pltpu.VMEM((rc, D), BF16),
                            pltpu.VMEM((rc, D), F32),
                            pltpu.VMEM((D, tf), BF16),
                            pltpu.VMEM((D, tf), BF16),
                            pltpu.VMEM((tf, D), BF16),
                            pltpu.SemaphoreType.DMA(()),
                            pltpu.SemaphoreType.DMA(())]),
        out_shape=jax.ShapeDtypeStruct((P, D), F32),
        compiler_params=_cparams(("arbitrary", "arbitrary")),
    )(item_e, item_row, item_nb, xs, w_gu, w_gu, b_gu.reshape(L, E, 1, F2), b_gu.reshape(L, E, 1, F2),
      w_down, b_down.reshape(L, E, 1, D))


def _combine_kernel(dest_ref, ys_ref, gate_ref, x_ref, g_ref, b_ref, o_ref, buf_ref, sem, *scratch,
                    alpha, to_natural):
    tb = o_ref.shape[0]
    step = pl.program_id(0)
    slot = step % 2
    per_res = tb // DIL

    def token(blk, i):
        if not to_natural:
            return blk * tb + i
        n_a = pl.num_programs(0) * per_res
        return (i % DIL) * n_a + blk * per_res + i // DIL

    def copy(blk, sl, i, k):
        src = ys_ref.at[pl.ds(dest_ref[token(blk, i) * TOP_K + k], 1)]
        return pltpu.make_async_copy(src, buf_ref.at[sl, k, pl.ds(i, 1)], sem.at[sl])

    def for_rows(fn):
        def body(i, c):
            for k in range(TOP_K):
                fn(i, k)
            return c

        lax.fori_loop(0, tb, body, 0, unroll=4)

    @pl.when(step == 0)
    def _():
        for_rows(lambda i, k: copy(0, 0, i, k).start())

    @pl.when(step + 1 < pl.num_programs(0))
    def _():
        for_rows(lambda i, k: copy(step + 1, 1 - slot, i, k).start())

    for k in range(TOP_K):
        pltpu.make_async_copy(ys_ref.at[pl.ds(0, tb)], buf_ref.at[slot, k], sem.at[slot]).wait()
    if to_natural:
        xn_ref, gn_ref = scratch
        for r in range(DIL):
            gn_ref[pl.ds(r, per_res, stride=DIL), :] = gate_ref[r]
            for c in range(xn_ref.shape[0]):
                xn_ref[c, pl.ds(r, per_res, stride=DIL), :] = x_ref[r, :, c * LANES:(c + 1) * LANES]
        x = jnp.concatenate([xn_ref[c] for c in range(xn_ref.shape[0])], axis=1)
        gate = gn_ref[...]
    else:
        x = x_ref[...]
        gate = gate_ref[...]
    ffn = gate[:, 0:1] * buf_ref[slot, 0]
    for k in range(1, TOP_K):
        ffn = ffn + gate[:, k:k + 1] * buf_ref[slot, k]
    o_ref[...] = _layer_norm_rows(alpha * x + ffn, g_ref[...], b_ref[...])


def _combine_ln(ys, dest_flat, gates, x, g, b, *, alpha, to_natural, tb=128):
    T, D = x.shape
    gate_pad = gates
    scratch = [pltpu.VMEM((2, TOP_K, tb, D), F32), pltpu.SemaphoreType.DMA((2,))]
    if to_natural:
        na = T // DIL
        per_res = tb // DIL
        gate_in = gate_pad.reshape(DIL, na, LANES)
        x_in = x.reshape(DIL, na, D)
        gate_spec = pl.BlockSpec((DIL, per_res, LANES), lambda i, dest: (0, i, 0))
        x_spec = pl.BlockSpec((DIL, per_res, D), lambda i, dest: (0, i, 0))
        scratch += [pltpu.VMEM((D // LANES, tb, LANES), F32), pltpu.VMEM((tb, LANES), F32)]
    else:
        gate_in, x_in = gate_pad, x
        gate_spec = pl.BlockSpec((tb, LANES), lambda i, dest: (i, 0))
        x_spec = pl.BlockSpec((tb, D), lambda i, dest: (i, 0))
    return pl.pallas_call(
        functools.partial(_combine_kernel, alpha=alpha, to_natural=to_natural),
        name="moe_combine",
        grid_spec=pltpu.PrefetchScalarGridSpec(
            num_scalar_prefetch=1, grid=(T // tb,),
            in_specs=[pl.BlockSpec(memory_space=pl.ANY),
                      gate_spec,
                      x_spec,
                      pl.BlockSpec((1, D), lambda i, dest: (0, 0)),
                      pl.BlockSpec((1, D), lambda i, dest: (0, 0))],
            out_specs=pl.BlockSpec((tb, D), lambda i, dest: (i, 0)),
            scratch_shapes=scratch),
        out_shape=jax.ShapeDtypeStruct((T, D), F32),
        compiler_params=_cparams(("arbitrary",)),
    )(dest_flat, ys, gate_in, x_in, g.reshape(1, D), b.reshape(1, D))


def _moe_layout(idx, rank, counts, n_assign):
    E = counts.shape[0]
    blk = MOE_BLOCK
    nblk = (counts + blk - 1) // blk
    bend = jnp.cumsum(nblk)
    bstart = bend - nblk
    first_row = (bstart * blk).astype(I32)
    dest = rank
    for e in range(E):
        dest = dest + jnp.where(idx == e, first_row[e], 0)
    dest = dest[:, :TOP_K]
    tail = jnp.where(nblk > 0, (bend - 1) * blk, 0).astype(I32)
    cb = MOE_CHUNK_BLOCKS
    n_items_max = E + (n_assign // blk + E) // cb + 1
    per_e = (nblk + cb - 1) // cb
    iend = jnp.cumsum(per_e)
    istart = iend - per_e
    ids = jnp.arange(n_items_max, dtype=I32)
    e_of = jnp.minimum(jnp.searchsorted(iend, ids, side='right'), E - 1).astype(I32)
    valid = ids < iend[-1]
    last_e = jnp.max(jnp.where(per_e > 0, jnp.arange(E, dtype=I32), 0))
    e_of = jnp.where(valid, e_of, last_e)
    local = ids - istart[e_of]
    first_blk = bstart[e_of] + local * cb
    nb = jnp.where(valid, jnp.minimum(cb, nblk[e_of] - local * cb), 0)
    row = jnp.where(valid, first_blk * blk, 0)
    return dest.astype(I32), tail, (e_of, row.astype(I32), nb.astype(I32))


def _moe(x, w_router, b_router, w_gu, b_gu, w_down, b_down, ln_g, ln_b, *, alpha, layer, to_natural=False):
    T, D = x.shape
    E = w_router.shape[1]
    idx, gates, rank, counts = _route(x, w_router, b_router)
    n_assign = T * TOP_K
    dest, tail, items = _moe_layout(idx, rank, counts, n_assign)
    dest_flat = dest.reshape(n_assign)
    n_rows = n_assign + E * MOE_BLOCK
    xs = _dispatch(x, dest_flat, _zero_tails(tail, n_rows, D))
    ys = _moe_experts(xs, items, w_gu, b_gu, w_down, b_down, layer)
    return _combine_ln(ys, dest_flat, gates, x, ln_g, ln_b, alpha=alpha, to_natural=to_natural)


def _even_mixer_ln(x, w_in, lam_re, lam_im, log_step, b_re, b_im, c_re, c_im, d_skip, w_glu, b_glu,
                   w_gate2, b_gate2, norm_g, w_out, ln_g, ln_b, *, alpha, to_residue_major):
    T, D = x.shape
    W = d_skip.shape[0]
    qk = W // 2
    rank = w_gate2.shape[0]
    s4 = W + 2 * qk + W
    w_main = jnp.concatenate([w_in[:, :s4], w_in[:, s4 + rank:]], axis=1).astype(BF16)
    w_gate = jnp.zeros((D, LANES), BF16).at[:, :rank].set(w_in[:, s4:s4 + rank].astype(BF16))
    h, g_low = _proj_in(x, w_main, w_gate)
    tables = _s5_tables(lam_re, lam_im, log_step, b_re, b_im, c_re, c_im, T // S5_CHUNK)
    y = _s5_core(h, W, tables)
    ya = _s5_glu(y, h, d_skip.astype(F32), w_glu.astype(BF16), b_glu.astype(F32))
    yb = _gla(h, g_low, w_gate2, b_gate2, norm_g, width=W)
    w_out_b = w_out.astype(BF16)
    return _proj_ln([ya, yb], [w_out_b[:W], w_out_b[W:]], x, ln_g, ln_b, alpha=alpha,
                    to_residue_major=to_residue_major)


def _odd_mixer_ln(x, w_qkv, w_o, ln_g, ln_b, *, alpha):
    T, D = x.shape
    qkv = _proj_scaled(x, w_qkv.astype(BF16), scaled_cols=D, scale=(D // ATT_HEADS) ** -0.5)
    y = _dilated_attention(qkv, D)
    return _proj_ln([y], [w_o.astype(BF16)], x, ln_g, ln_b, alpha=alpha, to_residue_major=False)


def kernel(x, ab_w_in, s5_lam_re, s5_lam_im, s5_log_step, s5_b_re, s5_b_im, s5_c_re, s5_c_im, s5_d, s5_w_glu, s5_b_glu, gla_w_gate2, gla_b_gate2, gla_norm_g, ab_w_out, c_w_qkv, c_w_o, ln1_g, ln1_b, moe_w_router, moe_b_router, moe_w_gu, moe_b_gu, moe_w_down, moe_b_down, ln2_g, ln2_b):
    bsz, L, D = x.shape
    depth = ln1_g.shape[0]
    alpha = (2 * depth) ** 0.25
    outs = []
    for bi in range(bsz):
        xt = x[bi].astype(F32)
        for layer in range(depth):
            i = layer // 2
            odd = layer % 2 == 1
            if not odd:
                xt = _even_mixer_ln(xt, ab_w_in[i], s5_lam_re[i], s5_lam_im[i], s5_log_step[i], s5_b_re[i],
                                    s5_b_im[i], s5_c_re[i], s5_c_im[i], s5_d[i], s5_w_glu[i], s5_b_glu[i],
                                    gla_w_gate2[i], gla_b_gate2[i], gla_norm_g[i], ab_w_out[i],
                                    ln1_g[layer], ln1_b[layer], alpha=alpha, to_residue_major=layer + 1 < depth)
            else:
                xt = _odd_mixer_ln(xt, c_w_qkv[i], c_w_o[i], ln1_g[layer], ln1_b[layer], alpha=alpha)
            xt = _moe(xt, moe_w_router[layer], moe_b_router[layer], moe_w_gu, moe_b_gu,
                      moe_w_down, moe_b_down, ln2_g[layer], ln2_b[layer], alpha=alpha, layer=layer,
                      to_natural=odd)
        outs.append(xt.astype(x.dtype))
    return outs[0].reshape(1, L, D) if bsz == 1 else jnp.stack(outs)
```

```python
import functools
import math

import jax
import jax.numpy as jnp
from jax import lax
from jax.experimental import pallas as pl
from jax.experimental.pallas import tpu as pltpu

F32 = jnp.float32
BF16 = jnp.bfloat16
I32 = jnp.int32
HIGHEST = lax.Precision.HIGHEST

LANES = 128
VMEM_LIMIT_BYTES = 56 * 1024 * 1024

S5_GROUP = 16
S5_STATE = 64
S5_MAX_RE = -1e-4
S5_CHUNK = 16
GLA_HEADS = 4
GLA_GATE_TEMP = 16.0
GLA_EPS = 1e-6
GLA_BLOCK = 64
GLA_SUB = 16
ATT_HEADS = 16
DIL = 16
DILATED_GROUPS = ((128, 1), (512, 4), (2048, 16))
TOP_K = 4
SWIGLU_LIMIT = 7.0
SWIGLU_ALPHA = 1.702
MOE_BLOCK = 128
MOE_CHUNK_BLOCKS = 9
LN_EPS = 1e-5
NEG_BIG = -1e30


def _cparams(semantics):
    return pltpu.CompilerParams(dimension_semantics=semantics, vmem_limit_bytes=VMEM_LIMIT_BYTES)


def _layer_norm_rows(z, g, b):
    mu = jnp.mean(z, axis=-1, keepdims=True)
    zc = z - mu
    var = jnp.mean(zc * zc, axis=-1, keepdims=True)
    return zc * lax.rsqrt(var + LN_EPS) * g + b


def _proj_in_kernel(x_ref, w_ref, wg_ref, h_ref, g_ref, xb_ref):
    @pl.when(pl.program_id(1) == 0)
    def _():
        xb = x_ref[...].astype(BF16)
        xb_ref[...] = xb
        g_ref[...] = jnp.dot(xb, wg_ref[...], preferred_element_type=F32)

    h_ref[...] = jnp.dot(xb_ref[...], w_ref[...], preferred_element_type=F32)


def _proj_in(x, w_main, w_gate, *, tm=1024, tn=512):
    T, D = x.shape
    N = w_main.shape[1]
    return pl.pallas_call(
        _proj_in_kernel,
        name="proj_in",
        grid=(T // tm, N // tn),
        in_specs=[pl.BlockSpec((tm, D), lambda i, j: (i, 0)),
                  pl.BlockSpec((D, tn), lambda i, j: (0, j)),
                  pl.BlockSpec((D, LANES), lambda i, j: (0, 0))],
        out_specs=[pl.BlockSpec((tm, tn), lambda i, j: (i, j)),
                   pl.BlockSpec((tm, LANES), lambda i, j: (i, 0))],
        out_shape=[jax.ShapeDtypeStruct((T, N), F32), jax.ShapeDtypeStruct((T, LANES), F32)],
        scratch_shapes=[pltpu.VMEM((tm, D), BF16)],
        compiler_params=_cparams(("parallel", "arbitrary")),
    )(x, w_main, w_gate)


def _proj_scaled_kernel(x_ref, w_ref, o_ref, xb_ref, *, n_scaled, scale):
    j = pl.program_id(1)

    @pl.when(j == 0)
    def _():
        xb_ref[...] = x_ref[...].astype(BF16)

    y = jnp.dot(xb_ref[...], w_ref[...], preferred_element_type=F32)
    o_ref[...] = (y * jnp.where(j < n_scaled, scale, 1.0)).astype(o_ref.dtype)


def _proj_scaled(x, w, *, scaled_cols, scale, tm=1024, tn=1024):
    T, D = x.shape
    N = w.shape[1]
    return pl.pallas_call(
        functools.partial(_proj_scaled_kernel, n_scaled=scaled_cols // tn, scale=scale),
        name="proj_qkv",
        grid=(T // tm, N // tn),
        in_specs=[pl.BlockSpec((tm, D), lambda i, j: (i, 0)),
                  pl.BlockSpec((D, tn), lambda i, j: (0, j))],
        out_specs=pl.BlockSpec((tm, tn), lambda i, j: (i, j)),
        out_shape=jax.ShapeDtypeStruct((T, N), BF16),
        scratch_shapes=[pltpu.VMEM((tm, D), BF16)],
        compiler_params=_cparams(("parallel", "arbitrary")),
    )(x, w)


def _proj_ln_kernel(*refs, n_lhs, alpha, n_tiles, to_residue_major):
    lhs_refs = refs[:n_lhs]
    w_refs = refs[n_lhs:2 * n_lhs]
    res_ref, g_ref, b_ref, o_ref, acc_ref = refs[2 * n_lhs:2 * n_lhs + 5]
    j = pl.program_id(1)
    y = jnp.dot(lhs_refs[0][...], w_refs[0][...], preferred_element_type=F32)
    for a_ref, w_ref in zip(lhs_refs[1:], w_refs[1:]):
        y = y + jnp.dot(a_ref[...], w_ref[...], preferred_element_type=F32)
    acc_ref[j] = y

    @pl.when(j == n_tiles - 1)
    def _():
        tn = acc_ref.shape[2]
        z = [alpha * res_ref[:, t * tn:(t + 1) * tn] + acc_ref[t] for t in range(n_tiles)]
        n = float(n_tiles * tn)
        mu = sum(jnp.sum(zt, axis=-1, keepdims=True) for zt in z) / n
        zc = [zt - mu for zt in z]
        var = sum(jnp.sum(zt * zt, axis=-1, keepdims=True) for zt in zc) / n
        rstd = lax.rsqrt(var + LN_EPS)
        out = [zc[t] * rstd * g_ref[:, t * tn:(t + 1) * tn] + b_ref[:, t * tn:(t + 1) * tn] for t in range(n_tiles)]
        if not to_residue_major:
            for t in range(n_tiles):
                o_ref[:, t * tn:(t + 1) * tn] = out[t]
        else:
            rows_ref = refs[-1]
            per_res = rows_ref.shape[1] // DIL
            for c in range(rows_ref.shape[0]):
                t, off = divmod(c * LANES, tn)
                rows_ref[c] = out[t][:, off:off + LANES]
                for r in range(DIL):
                    o_ref[r, :, c * LANES:(c + 1) * LANES] = rows_ref[c, pl.ds(r, per_res, stride=DIL), :]


def _proj_ln(lhs, ws, res, g, b, *, alpha, to_residue_major, tm=512, tn=512):
    T, N = res.shape
    n_lhs = len(lhs)
    n_tiles = N // tn
    scratch = [pltpu.VMEM((n_tiles, tm, tn), F32)]
    if to_residue_major:
        na = T // DIL
        out_spec = pl.BlockSpec((DIL, tm // DIL, N), lambda i, j: (0, i, 0))
        out_shape = jax.ShapeDtypeStruct((DIL, na, N), F32)
        scratch.append(pltpu.VMEM((N // LANES, tm, LANES), F32))
    else:
        out_spec = pl.BlockSpec((tm, N), lambda i, j: (i, 0))
        out_shape = jax.ShapeDtypeStruct((T, N), F32)
    in_specs = ([pl.BlockSpec((tm, a.shape[1]), lambda i, j: (i, 0)) for a in lhs]
                + [pl.BlockSpec((w.shape[0], tn), lambda i, j: (0, j)) for w in ws]
                + [pl.BlockSpec((tm, N), lambda i, j: (i, 0)),
                   pl.BlockSpec((1, N), lambda i, j: (0, 0)),
                   pl.BlockSpec((1, N), lambda i, j: (0, 0))])
    out = pl.pallas_call(
        functools.partial(_proj_ln_kernel, n_lhs=n_lhs, alpha=alpha, n_tiles=n_tiles,
                          to_residue_major=to_residue_major),
        name="proj_ln",
        grid=(T // tm, n_tiles),
        in_specs=in_specs,
        out_specs=out_spec,
        out_shape=out_shape,
        scratch_shapes=scratch,
        compiler_params=_cparams(("parallel", "arbitrary")),
    )(*lhs, *ws, res, g.reshape(1, N), b.reshape(1, N))
    return out.reshape(T, N)


def _s5_tables(lam_re, lam_im, log_step, b_re, b_im, c_re, c_im, n_chunks):
    C = S5_CHUNK
    G, P = lam_re.shape
    H = b_re.shape[-1]
    lr = jnp.minimum(lam_re.astype(F32), S5_MAX_RE)
    li = lam_im.astype(F32)
    dt = jnp.exp(log_step.astype(F32))[:, None]
    kk = jnp.arange(C + 1, dtype=F32)[:, None, None]
    pw_mag = jnp.exp(kk * (lr * dt))
    pw_re = pw_mag * jnp.cos(kk * (li * dt))
    pw_im = pw_mag * jnp.sin(kk * (li * dt))
    a_re, a_im = pw_re[1], pw_im[1]
    den = lr * lr + li * li
    nr = a_re - 1.0
    f_re = (nr * lr + a_im * li) / den
    f_im = (a_im * lr - nr * li) / den
    br = b_re.astype(F32)
    bi = b_im.astype(F32)
    bb_re = f_re[..., None] * br - f_im[..., None] * bi
    bb_im = f_re[..., None] * bi + f_im[..., None] * br
    ab_re = pw_re[:C, :, :, None] * bb_re[None] - pw_im[:C, :, :, None] * bb_im[None]
    ab_im = pw_re[:C, :, :, None] * bb_im[None] + pw_im[:C, :, :, None] * bb_re[None]
    cr = c_re.astype(F32)
    ci = c_im.astype(F32)
    z_re = jnp.transpose(ab_re[::-1], (1, 0, 3, 2)).reshape(G, C * H, P)
    z_im = jnp.transpose(ab_im[::-1], (1, 0, 3, 2)).reshape(G, C * H, P)
    zmat = jnp.concatenate([z_re, z_im], axis=-1)
    kern = (jnp.einsum('gop,kgph->kgoh', cr, ab_re, precision=HIGHEST)
            - jnp.einsum('gop,kgph->kgoh', ci, ab_im, precision=HIGHEST))
    lag = jnp.arange(C)[None, :] - jnp.arange(C)[:, None]
    kl = kern[jnp.clip(lag, 0, C - 1)]
    kl = jnp.where((lag >= 0)[:, :, None, None, None], kl, 0.0)
    mmat = jnp.transpose(kl, (2, 0, 4, 1, 3)).reshape(G, C * H, C * H)
    ca_re = cr[None] * pw_re[1:, :, None, :] - ci[None] * pw_im[1:, :, None, :]
    ca_im = cr[None] * pw_im[1:, :, None, :] + ci[None] * pw_re[1:, :, None, :]
    n_re = jnp.transpose(ca_re, (1, 3, 0, 2)).reshape(G, P, C * H)
    n_im = jnp.transpose(-ca_im, (1, 3, 0, 2)).reshape(G, P, C * H)
    nmat = jnp.concatenate([n_re, n_im], axis=1)
    n_steps = max(1, (n_chunks - 1).bit_length())
    qr, qi = pw_re[C], pw_im[C]
    a1, a2 = [], []
    for _ in range(n_steps):
        a1.append(jnp.concatenate([qr, qr], axis=-1))
        a2.append(jnp.concatenate([-qi, qi], axis=-1))
        qr, qi = qr * qr - qi * qi, 2.0 * qr * qi
    a1 = jnp.stack(a1, axis=1)
    a2 = jnp.stack(a2, axis=1)
    return zmat.astype(BF16), mmat.astype(BF16), nmat.astype(BF16), a1, a2


def _s5_kernel(u_ref, z_ref, m_ref, n_ref, a1_ref, a2_ref, y_ref, ub_ref, ug_ref, sg_ref, mw_ref, nw_ref, *,
               n_steps):
    C, H = S5_CHUNK, S5_GROUP
    nc = u_ref.shape[0] // C
    gpt = LANES // H
    ch = C * H
    for j in range(C):
        ub_ref[:, j * LANES:(j + 1) * LANES] = u_ref[pl.ds(j, nc, stride=C), :].astype(BF16)
    row = lax.broadcasted_iota(I32, (nc, LANES), 0)
    sr = lax.broadcasted_iota(I32, (gpt * LANES, LANES), 0)
    sc = lax.broadcasted_iota(I32, (gpt * LANES, LANES), 1)
    sel_hit = sc == H * (sr // LANES) + sr % H
    sel_grp = (sr % LANES) // H
    pr = lax.broadcasted_iota(I32, (ch, C * LANES), 0)
    pc = lax.broadcasted_iota(I32, (ch, C * LANES), 1)
    put_tile = pc // LANES == pr // H
    put_lane = pc % LANES - pr % H

    def group(gl, c):
        sel = jnp.where(sel_hit & (sel_grp == gl), 1.0, 0.0).astype(BF16)
        half_w = gpt * LANES
        u = jnp.concatenate(
            [jnp.dot(ub_ref[:, t * half_w:(t + 1) * half_w], sel, preferred_element_type=F32)
             for t in range(C * LANES // half_w)], axis=1).astype(BF16)
        s = jnp.dot(u, z_ref[gl], preferred_element_type=F32)
        half = s.shape[1] // 2
        a1 = a1_ref[gl]
        a2 = a2_ref[gl]
        for k in range(n_steps):
            sh = 1 << k
            prev = jnp.where(row >= sh, pltpu.roll(s, sh, axis=0), 0.0)
            s = s + a1[k:k + 1, :] * prev + a2[k:k + 1, :] * pltpu.roll(prev, half, axis=1)
        s_in = jnp.where(row >= 1, pltpu.roll(s, 1, axis=0), 0.0)
        put = jnp.where(put_tile & (put_lane == H * gl), 1.0, 0.0).astype(BF16)
        ug_ref[gl] = u
        sg_ref[gl] = s_in.astype(BF16)
        mw_ref[gl] = jnp.dot(m_ref[gl], put, preferred_element_type=F32).astype(BF16)
        nw_ref[gl] = jnp.dot(n_ref[gl], put, preferred_element_type=F32).astype(BF16)
        return c

    lax.fori_loop(0, gpt, group, 0)
    u_all = jnp.concatenate([ug_ref[g] for g in range(gpt)], axis=1)
    s_all = jnp.concatenate([sg_ref[g] for g in range(gpt)], axis=1)
    y = (jnp.dot(u_all, mw_ref[...].reshape(gpt * ch, C * LANES), preferred_element_type=F32)
         + jnp.dot(s_all, nw_ref[...].reshape(gpt * sg_ref.shape[2], C * LANES), preferred_element_type=F32))
    for i in range(C):
        y_ref[pl.ds(i, nc, stride=C), :] = y[:, i * LANES:(i + 1) * LANES]


def _s5_core(h, width, tables):
    zmat, mmat, nmat, a1, a2 = tables
    T = h.shape[0]
    G, CH, P2 = zmat.shape
    C, H = S5_CHUNK, S5_GROUP
    nc = T // C
    n_steps = a1.shape[1]
    gpt = LANES // H
    assert CH == C * H and width == G * H
    return pl.pallas_call(
        functools.partial(_s5_kernel, n_steps=n_steps),
        name="s5_core",
        grid=(G // gpt,),
        in_specs=[pl.BlockSpec((T, LANES), lambda m: (0, m)),
                  pl.BlockSpec((gpt, CH, P2), lambda m: (m, 0, 0)),
                  pl.BlockSpec((gpt, CH, CH), lambda m: (m, 0, 0)),
                  pl.BlockSpec((gpt, P2, CH), lambda m: (m, 0, 0)),
                  pl.BlockSpec((gpt, n_steps, P2), lambda m: (m, 0, 0)),
                  pl.BlockSpec((gpt, n_steps, P2), lambda m: (m, 0, 0))],
        out_specs=pl.BlockSpec((T, LANES), lambda m: (0, m)),
        out_shape=jax.ShapeDtypeStruct((T, width), F32),
        scratch_shapes=[pltpu.VMEM((nc, C * LANES), BF16),
                        pltpu.VMEM((gpt, nc, CH), BF16),
                        pltpu.VMEM((gpt, nc, P2), BF16),
                        pltpu.VMEM((gpt, CH, C * LANES), BF16),
                        pltpu.VMEM((gpt, P2, C * LANES), BF16)],
        compiler_params=_cparams(("parallel",)),
    )(h, zmat, mmat, nmat, a1, a2)


def _s5_glu_kernel(y_ref, u_ref, d_ref, w_ref, b_ref, o_ref):
    y = y_ref[...] + d_ref[...] * u_ref[...]
    c0 = math.sqrt(2.0 / math.pi)
    z = 0.5 * y * (1.0 + jnp.tanh(c0 * (y + 0.044715 * (y * y * y))))
    lin = jnp.dot(z.astype(BF16), w_ref[...], preferred_element_type=F32) + b_ref[...]
    o_ref[...] = (z * jax.nn.sigmoid(lin)).astype(o_ref.dtype)


def _s5_glu(y, h, d_skip, w_glu, b_glu, *, tm=512):
    T, W = y.shape
    return pl.pallas_call(
        _s5_glu_kernel,
        name="s5_glu",
        grid=(T // tm,),
        in_specs=[pl.BlockSpec((tm, W), lambda i: (i, 0)),
                  pl.BlockSpec((tm, W), lambda i: (i, 0)),
                  pl.BlockSpec((1, W), lambda i: (0, 0)),
                  pl.BlockSpec((W, W), lambda i: (0, 0)),
                  pl.BlockSpec((1, W), lambda i: (0, 0))],
        out_specs=pl.BlockSpec((tm, W), lambda i: (i, 0)),
        out_shape=jax.ShapeDtypeStruct((T, W), BF16),
        compiler_params=_cparams(("parallel",)),
    )(y, h, d_skip.reshape(1, W), w_glu, b_glu.reshape(1, W))


def _gla_kernel(q_ref, k_ref, v_ref, r_ref, g_ref, w2_ref, b2_ref, ng_ref, o_ref, st_ref, *, dk, dv):
    @pl.when(pl.program_id(0) == 0)
    def _():
        st_ref[...] = jnp.zeros_like(st_ref)

    cb = q_ref.shape[0]
    n_sub = cb // GLA_SUB
    scale = dk ** -0.5
    logit = jnp.dot(g_ref[...], w2_ref[...], preferred_element_type=F32, precision=HIGHEST) + b2_ref[...]
    log_a = (jnp.minimum(logit, 0.0) - jnp.log(1.0 + jnp.exp(-jnp.abs(logit)))) / GLA_GATE_TEMP
    ri = lax.broadcasted_iota(I32, (cb, cb), 0)
    ci = lax.broadcasted_iota(I32, (cb, cb), 1)
    tri = (ri >= ci).astype(F32)
    bcum = jnp.dot(tri, log_a, preferred_element_type=F32, precision=HIGHEST)
    states = [st_ref[hh] for hh in range(GLA_HEADS)]
    new_states, outs = [], []
    for hh in range(GLA_HEADS):
        ks = slice(hh * dk, (hh + 1) * dk)
        vs = slice(hh * dv, (hh + 1) * dv)
        b = bcum[:, ks]
        q = q_ref[:, ks] * scale
        k = k_ref[:, ks]
        v = v_ref[:, vs].astype(BF16)
        refs = [jnp.zeros((1, dk), F32)] + [b[a * GLA_SUB - 1:a * GLA_SUB, :] for a in range(1, n_sub)]
        refmat = jnp.concatenate([jnp.broadcast_to(r, (GLA_SUB, dk)) for r in refs], axis=0)
        qe = (q * jnp.exp(b - refmat)).astype(BF16)
        st = states[hh]
        o_inter = lax.dot_general((q * jnp.exp(b)).astype(BF16), st.astype(BF16),
                                  (((1,), (1,)), ((), ())), preferred_element_type=F32)
        o_rows = []
        for a in range(n_sub):
            hi = (a + 1) * GLA_SUB
            ke = (k[:hi] * jnp.exp(refs[a] - b[:hi])).astype(BF16)
            att = lax.dot_general(qe[a * GLA_SUB:hi], ke, (((1,), (1,)), ((), ())),
                                  preferred_element_type=F32)
            row_a = lax.broadcasted_iota(I32, (GLA_SUB, hi), 0) + a * GLA_SUB
            att = jnp.where(lax.broadcasted_iota(I32, (GLA_SUB, hi), 1) <= row_a, att, 0.0)
            o_rows.append(jnp.dot(att.astype(BF16), v[:hi], preferred_element_type=F32))
        o = jnp.concatenate(o_rows, axis=0) + o_inter
        o = o * lax.rsqrt(jnp.mean(o * o, axis=-1, keepdims=True) + GLA_EPS) * ng_ref[...]
        r = r_ref[:, vs]
        outs.append((o * (r * jax.nn.sigmoid(r))).astype(o_ref.dtype))
        b_last = b[cb - 1:cb, :]
        kd = (k * jnp.exp(b_last - b)).astype(BF16)
        upd = lax.dot_general(v, kd, (((0,), (0,)), ((), ())), preferred_element_type=F32)
        new_states.append(st * jnp.exp(b_last) + upd)
    for hh in range(GLA_HEADS):
        o_ref[:, hh * dv:(hh + 1) * dv] = outs[hh]
        st_ref[hh] = new_states[hh]


def _gla(h, g_low, w_gate2, b_gate2, norm_g, *, width):
    T = h.shape[0]
    qk = width // 2
    dk = qk // GLA_HEADS
    dv = width // GLA_HEADS
    cb = GLA_BLOCK
    w2 = jnp.zeros((LANES, qk), F32).at[:w_gate2.shape[0]].set(w_gate2.astype(F32))
    return pl.pallas_call(
        functools.partial(_gla_kernel, dk=dk, dv=dv),
        name="gla",
        grid=(T // cb,),
        in_specs=[pl.BlockSpec((cb, qk), lambda i: (i, 2)),
                  pl.BlockSpec((cb, qk), lambda i: (i, 3)),
                  pl.BlockSpec((cb, width), lambda i: (i, 2)),
                  pl.BlockSpec((cb, width), lambda i: (i, 3)),
                  pl.BlockSpec((cb, LANES), lambda i: (i, 0)),
                  pl.BlockSpec((LANES, qk), lambda i: (0, 0)),
                  pl.BlockSpec((1, qk), lambda i: (0, 0)),
                  pl.BlockSpec((1, dv), lambda i: (0, 0))],
        out_specs=pl.BlockSpec((cb, width), lambda i: (i, 0)),
        out_shape=jax.ShapeDtypeStruct((T, width), BF16),
        scratch_shapes=[pltpu.VMEM((GLA_HEADS, dv, dk), F32)],
        compiler_params=_cparams(("arbitrary",)),
    )(h, h, h, h, g_low, w2, b_gate2.reshape(1, qk).astype(F32), norm_g.reshape(1, dv).astype(F32))


def _attn_kernel(q_ref, k_ref, v_ref, o_ref, acc_ref, m_ref, l_ref, v1_ref, pat1_ref, pat4_ref, pat16_ref, *, na):
    dh = v_ref.shape[1]
    v1_ref[:, :dh] = v_ref[...]
    v1_ref[:, dh:] = jnp.ones_like(v_ref)
    acc_ref[...] = jnp.zeros_like(acc_ref)
    m_ref[...] = jnp.full_like(m_ref, NEG_BIG)
    l_ref[...] = jnp.zeros_like(l_ref)

    def token_offsets(rows, r_step, n, axis):
        shape = (n, 1) if axis == 0 else (1, n)
        idx = lax.broadcasted_iota(I32, shape, axis)
        c = idx // rows
        return DIL * (idx - c * rows) + r_step * c

    def delta_pattern(n_chunks, q_rows, k_rows, r_step):
        return (token_offsets(q_rows, r_step, n_chunks * q_rows, 0)
                - token_offsets(k_rows, r_step, n_chunks * k_rows, 1))

    def cat(ref, starts, rows):
        return jnp.concatenate([ref[pl.ds(pl.multiple_of(s, 16), rows), :] for s in starts], axis=0)

    def attend(blocks, q_rows, k_rows, window, pat_ref):
        loaded = []
        for q_starts, k_starts, a_q, a_k in blocks:
            loaded.append((cat(q_ref, q_starts, q_rows), cat(k_ref, k_starts, k_rows), cat(v1_ref, k_starts, k_rows),
                           cat(m_ref, q_starts, q_rows), cat(l_ref, q_starts, q_rows),
                           cat(acc_ref, q_starts, q_rows)))
        results = []
        for (q_starts, k_starts, a_q, a_k), (qb, kb, vb, m_old, l_old, acc_old) in zip(blocks, loaded):
            s = lax.dot_general(qb, kb, (((1,), (1,)), ((), ())), preferred_element_type=F32)
            off = DIL * (a_q - a_k)
            pat = pat_ref[...]
            s = jnp.where((pat >= -off) & (pat <= window - off), s, NEG_BIG)
            m_new = jnp.maximum(m_old, jnp.max(s, axis=1, keepdims=True))
            alpha = jnp.exp(m_old - m_new)
            p = jnp.exp(s - m_new[:, :1])
            pv = jnp.dot(p.astype(BF16), vb, preferred_element_type=F32)
            l_new = alpha * l_old + pv[:, dh:]
            acc_new = alpha * acc_old + pv[:, :dh]
            results.append((m_new, l_new, acc_new))
        for (q_starts, *_), (m_new, l_new, acc_new) in zip(blocks, results):
            for c, st in enumerate(q_starts):
                st = pl.multiple_of(st, 16)
                rs = slice(c * q_rows, (c + 1) * q_rows)
                m_ref[pl.ds(st, q_rows), :] = m_new[rs]
                l_ref[pl.ds(st, q_rows), :] = l_new[rs]
                acc_ref[pl.ds(st, q_rows), :] = acc_new[rs]

    w1 = DILATED_GROUPS[0][0]
    u1 = 2
    pat1_ref[...] = delta_pattern(DIL, 16, 32, 1)

    def body1(it, carry):
        blocks = []
        for j in range(u1):
            a0 = (it * u1 + j) * 16
            ak = jnp.maximum(a0 - 16, 0)
            blocks.append(([r * na + a0 for r in range(DIL)], [r * na + ak for r in range(DIL)], a0, ak))
        attend(blocks, 16, 32, w1, pat1_ref)
        return carry

    lax.fori_loop(0, na // (16 * u1), body1, 0)

    w4 = DILATED_GROUPS[1][0]
    pat4_ref[...] = delta_pattern(4, 32, 64, 4)

    def body4(it, carry):
        a0 = it * 32
        ak = jnp.maximum(a0 - 32, 0)
        blocks = [([(rho + 4 * sg) * na + a0 for sg in range(4)], [(rho + 4 * sg) * na + ak for sg in range(4)],
                   a0, ak) for rho in range(4)]
        attend(blocks, 32, 64, w4, pat4_ref)
        return carry

    lax.fori_loop(0, na // 32, body4, 0)

    w16 = DILATED_GROUPS[2][0]
    u16 = 4
    pat16_ref[...] = delta_pattern(1, 128, 256, 0)

    def body16(it, carry):
        rg = it // (na // 128)
        a0 = (it - rg * (na // 128)) * 128
        ak = jnp.maximum(a0 - 128, 0)
        blocks = [([(rg * u16 + j) * na + a0], [(rg * u16 + j) * na + ak], a0, ak) for j in range(u16)]
        attend(blocks, 128, 256, w16, pat16_ref)
        return carry

    lax.fori_loop(0, (DIL // u16) * (na // 128), body16, 0)

    o_ref[...] = (acc_ref[...] / l_ref[...]).astype(o_ref.dtype)


def _dilated_attention(qkv, d_model):
    T = qkv.shape[0]
    dh = d_model // ATT_HEADS
    assert dh == LANES
    na = T // DIL
    return pl.pallas_call(
        functools.partial(_attn_kernel, na=na),
        name="dilated_attn",
        grid=(ATT_HEADS,),
        in_specs=[pl.BlockSpec((T, dh), lambda h: (0, h)),
                  pl.BlockSpec((T, dh), lambda h: (0, ATT_HEADS + h)),
                  pl.BlockSpec((T, dh), lambda h: (0, 2 * ATT_HEADS + h))],
        out_specs=pl.BlockSpec((T, dh), lambda h: (0, h)),
        out_shape=jax.ShapeDtypeStruct((T, d_model), BF16),
        scratch_shapes=[pltpu.VMEM((T, dh), F32), pltpu.VMEM((T, dh), F32), pltpu.VMEM((T, dh), F32),
                        pltpu.VMEM((T, 2 * dh), BF16), pltpu.VMEM((256, 512), I32), pltpu.VMEM((128, 256), I32), pltpu.VMEM((128, 256), I32)],
        compiler_params=_cparams(("parallel",)),
    )(qkv, qkv, qkv)


def _route_kernel(x_ref, w_ref, b_ref, idx_ref, gate_ref, rank_ref, cnt_ref, carry_ref):
    @pl.when(pl.program_id(0) == 0)
    def _():
        carry_ref[...] = jnp.zeros_like(carry_ref)

    tb = x_ref.shape[0]
    x = x_ref[...]
    w = w_ref[...]
    x_hi, w_hi = x.astype(BF16), w.astype(BF16)
    x_lo = (x - x_hi.astype(F32)).astype(BF16)
    w_lo = (w - w_hi.astype(F32)).astype(BF16)
    lg = (jnp.dot(x_hi, w_hi, preferred_element_type=F32) + jnp.dot(x_hi, w_lo, preferred_element_type=F32)
          + jnp.dot(x_lo, w_hi, preferred_element_type=F32)) + b_ref[...]
    lane = lax.broadcasted_iota(I32, lg.shape, 1)
    vals, hots = [], []
    idx_out = jnp.zeros(lg.shape, I32)
    for k in range(TOP_K):
        m = jnp.max(lg, axis=1, keepdims=True)
        sel = jnp.min(jnp.where(lg == m, lane, LANES), axis=1, keepdims=True)
        hot = lane == sel
        vals.append(m)
        hots.append(hot)
        idx_out = jnp.where(lane == k, sel, idx_out)
        lg = jnp.where(hot, -jnp.inf, lg)
    ex = [jnp.exp(v - vals[0]) for v in vals]
    den = sum(ex)
    gate_out = jnp.zeros(lg.shape, F32)
    for k in range(TOP_K):
        gate_out = jnp.where(lane == k, ex[k] / den, gate_out)
    chosen = sum(h.astype(F32) for h in hots)
    ri = lax.broadcasted_iota(I32, (tb, tb), 0)
    ci = lax.broadcasted_iota(I32, (tb, tb), 1)
    before = jnp.dot((ri > ci).astype(BF16), chosen.astype(BF16), preferred_element_type=F32) + carry_ref[...]
    rank_out = jnp.zeros(lg.shape, I32)
    for k in range(TOP_K):
        rk = jnp.sum(jnp.where(hots[k], before, 0.0), axis=1, keepdims=True).astype(I32)
        rank_out = jnp.where(lane == k, rk, rank_out)
    idx_ref[...] = idx_out
    gate_ref[...] = gate_out
    rank_ref[...] = rank_out
    carry_ref[...] = carry_ref[...] + jnp.sum(chosen, axis=0, keepdims=True)
    cnt_ref[...] = carry_ref[...].astype(I32)


def _route(x, w_router, b_router, *, tb=512):
    T, D = x.shape
    E = w_router.shape[1]
    wr = jnp.zeros((D, LANES), F32).at[:, :E].set(w_router.astype(F32))
    br = jnp.full((1, LANES), NEG_BIG, F32).at[0, :E].set(b_router.astype(F32))
    row = lambda dt: jax.ShapeDtypeStruct((T, LANES), dt)
    idx, gate, rank, cnt = pl.pallas_call(
        _route_kernel,
        name="moe_route",
        grid=(T // tb,),
        in_specs=[pl.BlockSpec((tb, D), lambda i: (i, 0)),
                  pl.BlockSpec((D, LANES), lambda i: (0, 0)),
                  pl.BlockSpec((1, LANES), lambda i: (0, 0))],
        out_specs=[pl.BlockSpec((tb, LANES), lambda i: (i, 0)),
                   pl.BlockSpec((tb, LANES), lambda i: (i, 0)),
                   pl.BlockSpec((tb, LANES), lambda i: (i, 0)),
                   pl.BlockSpec((1, LANES), lambda i: (0, 0))],
        out_shape=[row(I32), row(F32), row(I32), jax.ShapeDtypeStruct((1, LANES), I32)],
        scratch_shapes=[pltpu.VMEM((1, LANES), F32)],
        compiler_params=_cparams(("arbitrary",)),
    )(x, wr, br)
    return idx, gate, rank, cnt[0, :E]


def _zero_tail_kernel(row_ref, o_ref, z_ref, sem):
    z_ref[...] = jnp.zeros_like(z_ref)
    n = row_ref.shape[0]

    def copy(e):
        return pltpu.make_async_copy(z_ref, o_ref.at[pl.ds(pl.multiple_of(row_ref[e], MOE_BLOCK), MOE_BLOCK)], sem)

    def start(e, c):
        copy(e).start()
        return c

    def wait(e, c):
        copy(e).wait()
        return c

    lax.fori_loop(0, n, start, 0)
    lax.fori_loop(0, n, wait, 0)


def _zero_tails(tail_rows, n_rows, d):
    return pl.pallas_call(
        _zero_tail_kernel,
        name="moe_zero_tails",
        grid_spec=pltpu.PrefetchScalarGridSpec(
            num_scalar_prefetch=1, grid=(1,),
            in_specs=[],
            out_specs=pl.BlockSpec(memory_space=pl.ANY),
            scratch_shapes=[pltpu.VMEM((MOE_BLOCK, d), F32), pltpu.SemaphoreType.DMA(())]),
        out_shape=jax.ShapeDtypeStruct((n_rows, d), F32),
        compiler_params=_cparams(("arbitrary",)),
    )(tail_rows)


def _dispatch_kernel(dest_ref, x_ref, xs_in_ref, xs_ref, sem):
    del xs_in_ref
    tb = x_ref.shape[0]
    base = pl.program_id(0) * tb * TOP_K

    def copy(i, k):
        return pltpu.make_async_copy(x_ref.at[pl.ds(i, 1)],
                                     xs_ref.at[pl.ds(dest_ref[base + i * TOP_K + k], 1)], sem)

    def start(i, c):
        for k in range(TOP_K):
            copy(i, k).start()
        return c

    lax.fori_loop(0, tb, start, 0, unroll=4)
    for _ in range(TOP_K):
        pltpu.make_async_copy(x_ref, xs_ref.at[pl.ds(0, tb)], sem).wait()


def _dispatch(x, dest_flat, xs_init, *, tb=256):
    T, D = x.shape
    return pl.pallas_call(
        _dispatch_kernel,
        name="moe_dispatch",
        grid_spec=pltpu.PrefetchScalarGridSpec(
            num_scalar_prefetch=1, grid=(T // tb,),
            in_specs=[pl.BlockSpec((tb, D), lambda i, dest: (i, 0)),
                      pl.BlockSpec(memory_space=pl.ANY)],
            out_specs=pl.BlockSpec(memory_space=pl.ANY),
            scratch_shapes=[pltpu.SemaphoreType.DMA(())]),
        out_shape=jax.ShapeDtypeStruct(xs_init.shape, xs_init.dtype),
        input_output_aliases={2: 0},
        compiler_params=_cparams(("arbitrary",)),
    )(dest_flat, x, xs_init)


def _moe_kernel(item_e_ref, item_row_ref, item_nb_ref, xs_ref, wg_ref, wu_ref, bg_ref, bu_ref, wd_ref, bd_ref,
                ys_ref, xbuf_ref, land_ref, acc_ref, in_sem, out_sem, *, n_f):
    it = pl.program_id(0)
    f = pl.program_id(1)
    nb = item_nb_ref[it]
    row0 = item_row_ref[it]
    blk = MOE_BLOCK

    def acc_rows(j):
        return pl.ds(pl.multiple_of(j * blk, blk), blk)

    def in_copy(item, j):
        src = xs_ref.at[pl.ds(pl.multiple_of(item_row_ref[item] + j * blk, blk), blk)]
        return pltpu.make_async_copy(src, land_ref.at[acc_rows(j)], in_sem)

    def out_copy(j):
        dst = ys_ref.at[pl.ds(pl.multiple_of(row0 + j * blk, blk), blk)]
        return pltpu.make_async_copy(acc_ref.at[acc_rows(j)], dst, out_sem)

    def for_blocks(fn, n=nb):
        def body(j, c):
            fn(j)
            return c

        lax.fori_loop(0, n, body, 0)

    @pl.when((f == 0) & (nb > 0))
    def _():
        @pl.when(it == 0)
        def _():
            for_blocks(lambda j: in_copy(it, j).start())

        for_blocks(lambda j: in_copy(it, j).wait())

        def to_bf16(j):
            xbuf_ref[acc_rows(j), :] = land_ref[acc_rows(j), :].astype(BF16)
            acc_ref[acc_rows(j), :] = jnp.broadcast_to(bd_ref[...], (blk, acc_ref.shape[1]))

        for_blocks(to_bf16)

    @pl.when((f == 1) & (it + 1 < pl.num_programs(0)))
    def _():
        nxt = it + 1
        for_blocks(lambda j: in_copy(nxt, j).start(), item_nb_ref[nxt])

    @pl.when(nb > 0)
    def _():
        def compute(r0, n_rows, wg, wu, wd):
            rows = pl.ds(pl.multiple_of(r0, blk), n_rows)
            xb = xbuf_ref[rows, :]
            hg = jnp.dot(xb, wg, preferred_element_type=F32) + bg_ref[...]
            hu = jnp.dot(xb, wu, preferred_element_type=F32) + bu_ref[...]
            gate = jnp.minimum(hg, SWIGLU_LIMIT)
            up = jnp.clip(hu, -SWIGLU_LIMIT, SWIGLU_LIMIT)
            act = (up + 1.0) * (gate * jax.nn.sigmoid(SWIGLU_ALPHA * gate))
            acc_ref[rows, :] += jnp.dot(act.astype(BF16), wd, preferred_element_type=F32)

            @pl.when(f == n_f - 1)
            def _():
                for b in range(n_rows // blk):
                    out_copy(r0 // blk + b).start()

        def piece(r0, n_rows):
            compute(r0, n_rows, wg_ref[...].astype(BF16), wu_ref[...].astype(BF16), wd_ref[...].astype(BF16))

        n4 = nb // 4

        def quad(j, c):
            piece(j * (4 * blk), 4 * blk)
            return c

        lax.fori_loop(0, n4, quad, 0)

        @pl.when((nb & 2) != 0)
        def _():
            piece(n4 * (4 * blk), 2 * blk)

        @pl.when((nb & 1) != 0)
        def _():
            piece((nb - 1) * blk, blk)

    @pl.when((f == n_f - 1) & (nb > 0))
    def _():
        for_blocks(lambda j: out_copy(j).wait())


def _moe_experts(xs, items, w_gu, b_gu, w_down, b_down, layer, *, tf=512):
    item_e, item_row, item_nb = items
    n_items = item_e.shape[0]
    P, D = xs.shape
    L, E, _, F2 = w_gu.shape
    F = F2 // 2
    n_f = F // tf
    assert n_f >= 2
    rc = MOE_CHUNK_BLOCKS * MOE_BLOCK

    def fe(f, nb_ref, it):
        return jnp.where(nb_ref[it] > 0, f, n_f - 1)

    in_specs = [
        pl.BlockSpec(memory_space=pl.ANY),
        pl.BlockSpec((None, None, D, tf), lambda it, f, e, r, nb: (layer, e[it], 0, fe(f, nb, it))),
        pl.BlockSpec((None, None, D, tf), lambda it, f, e, r, nb: (layer, e[it], 0, n_f + fe(f, nb, it))),
        pl.BlockSpec((None, None, 1, tf), lambda it, f, e, r, nb: (layer, e[it], 0, fe(f, nb, it))),
        pl.BlockSpec((None, None, 1, tf), lambda it, f, e, r, nb: (layer, e[it], 0, n_f + fe(f, nb, it))),
        pl.BlockSpec((None, None, tf, D), lambda it, f, e, r, nb: (layer, e[it], fe(f, nb, it), 0)),
        pl.BlockSpec((None, None, 1, D), lambda it, f, e, r, nb: (layer, e[it], 0, 0)),
    ]
    return pl.pallas_call(
        functools.partial(_moe_kernel, n_f=n_f),
        name="moe_experts",
        grid_spec=pltpu.PrefetchScalarGridSpec(
            num_scalar_prefetch=3, grid=(n_items, n_f),
            in_specs=in_specs,
            out_specs=pl.BlockSpec(memory_space=pl.ANY),
            scratch_shapes=[pltpu.VMEM((rc, D), BF16),
                            pltpu.VMEM((rc, D), F32),
                            pltpu.VMEM((rc, D), F32),
                            pltpu.SemaphoreType.DMA(()),
                            pltpu.SemaphoreType.DMA(())]),
        out_shape=jax.ShapeDtypeStruct((P, D), F32),
        compiler_params=_cparams(("arbitrary", "arbitrary")),
    )(item_e, item_row, item_nb, xs, w_gu, w_gu, b_gu.reshape(L, E, 1, F2), b_gu.reshape(L, E, 1, F2),
      w_down, b_down.reshape(L, E, 1, D))


def _combine_kernel(dest_ref, ys_ref, gate_ref, x_ref, g_ref, b_ref, o_ref, buf_ref, sem, *scratch,
                    alpha, to_natural):
    tb = o_ref.shape[0]
    step = pl.program_id(0)
    slot = step % 2
    per_res = tb // DIL

    def token(blk, i):
        if not to_natural:
            return blk * tb + i
        n_a = pl.num_programs(0) * per_res
        return (i % DIL) * n_a + blk * per_res + i // DIL

    def copy(blk, sl, i, k):
        src = ys_ref.at[pl.ds(dest_ref[token(blk, i) * TOP_K + k], 1)]
        return pltpu.make_async_copy(src, buf_ref.at[sl, k, pl.ds(i, 1)], sem.at[sl])

    def for_rows(fn):
        def body(i, c):
            for k in range(TOP_K):
                fn(i, k)
            return c

        lax.fori_loop(0, tb, body, 0, unroll=4)

    @pl.when(step == 0)
    def _():
        for_rows(lambda i, k: copy(0, 0, i, k).start())

    @pl.when(step + 1 < pl.num_programs(0))
    def _():
        for_rows(lambda i, k: copy(step + 1, 1 - slot, i, k).start())

    for k in range(TOP_K):
        pltpu.make_async_copy(ys_ref.at[pl.ds(0, tb)], buf_ref.at[slot, k], sem.at[slot]).wait()
    if to_natural:
        xn_ref, gn_ref = scratch
        for r in range(DIL):
            gn_ref[pl.ds(r, per_res, stride=DIL), :] = gate_ref[r]
            for c in range(xn_ref.shape[0]):
                xn_ref[c, pl.ds(r, per_res, stride=DIL), :] = x_ref[r, :, c * LANES:(c + 1) * LANES]
        x = jnp.concatenate([xn_ref[c] for c in range(xn_ref.shape[0])], axis=1)
        gate = gn_ref[...]
    else:
        x = x_ref[...]
        gate = gate_ref[...]
    ffn = gate[:, 0:1] * buf_ref[slot, 0]
    for k in range(1, TOP_K):
        ffn = ffn + gate[:, k:k + 1] * buf_ref[slot, k]
    o_ref[...] = _layer_norm_rows(alpha * x + ffn, g_ref[...], b_ref[...])


def _combine_ln(ys, dest_flat, gates, x, g, b, *, alpha, to_natural, tb=128):
    T, D = x.shape
    gate_pad = gates
    scratch = [pltpu.VMEM((2, TOP_K, tb, D), F32), pltpu.SemaphoreType.DMA((2,))]
    if to_natural:
        na = T // DIL
        per_res = tb // DIL
        gate_in = gate_pad.reshape(DIL, na, LANES)
        x_in = x.reshape(DIL, na, D)
        gate_spec = pl.BlockSpec((DIL, per_res, LANES), lambda i, dest: (0, i, 0))
        x_spec = pl.BlockSpec((DIL, per_res, D), lambda i, dest: (0, i, 0))
        scratch += [pltpu.VMEM((D // LANES, tb, LANES), F32), pltpu.VMEM((tb, LANES), F32)]
    else:
        gate_in, x_in = gate_pad, x
        gate_spec = pl.BlockSpec((tb, LANES), lambda i, dest: (i, 0))
        x_spec = pl.BlockSpec((tb, D), lambda i, dest: (i, 0))
    return pl.pallas_call(
        functools.partial(_combine_kernel, alpha=alpha, to_natural=to_natural),
        name="moe_combine",
        grid_spec=pltpu.PrefetchScalarGridSpec(
            num_scalar_prefetch=1, grid=(T // tb,),
            in_specs=[pl.BlockSpec(memory_space=pl.ANY),
                      gate_spec,
                      x_spec,
                      pl.BlockSpec((1, D), lambda i, dest: (0, 0)),
                      pl.BlockSpec((1, D), lambda i, dest: (0, 0))],
            out_specs=pl.BlockSpec((tb, D), lambda i, dest: (i, 0)),
            scratch_shapes=scratch),
        out_shape=jax.ShapeDtypeStruct((T, D), F32),
        compiler_params=_cparams(("arbitrary",)),
    )(dest_flat, ys, gate_in, x_in, g.reshape(1, D), b.reshape(1, D))


def _moe_layout(idx, rank, counts, n_assign):
    E = counts.shape[0]
    blk = MOE_BLOCK
    nblk = (counts + blk - 1) // blk
    bend = jnp.cumsum(nblk)
    bstart = bend - nblk
    first_row = (bstart * blk).astype(I32)
    dest = rank
    for e in range(E):
        dest = dest + jnp.where(idx == e, first_row[e], 0)
    dest = dest[:, :TOP_K]
    tail = jnp.where(nblk > 0, (bend - 1) * blk, 0).astype(I32)
    cb = MOE_CHUNK_BLOCKS
    n_items_max = E + (n_assign // blk + E) // cb + 1
    per_e = (nblk + cb - 1) // cb
    iend = jnp.cumsum(per_e)
    istart = iend - per_e
    ids = jnp.arange(n_items_max, dtype=I32)
    e_of = jnp.minimum(jnp.searchsorted(iend, ids, side='right'), E - 1).astype(I32)
    valid = ids < iend[-1]
    last_e = jnp.max(jnp.where(per_e > 0, jnp.arange(E, dtype=I32), 0))
    e_of = jnp.where(valid, e_of, last_e)
    local = ids - istart[e_of]
    first_blk = bstart[e_of] + local * cb
    nb = jnp.where(valid, jnp.minimum(cb, nblk[e_of] - local * cb), 0)
    row = jnp.where(valid, first_blk * blk, 0)
    return dest.astype(I32), tail, (e_of, row.astype(I32), nb.astype(I32))


def _moe(x, w_router, b_router, w_gu, b_gu, w_down, b_down, ln_g, ln_b, *, alpha, layer, to_natural=False):
    T, D = x.shape
    E = w_router.shape[1]
    idx, gates, rank, counts = _route(x, w_router, b_router)
    n_assign = T * TOP_K
    dest, tail, items = _moe_layout(idx, rank, counts, n_assign)
    dest_flat = dest.reshape(n_assign)
    n_rows = n_assign + E * MOE_BLOCK
    xs = _dispatch(x, dest_flat, _zero_tails(tail, n_rows, D))
    ys = _moe_experts(xs, items, w_gu, b_gu, w_down, b_down, layer)
    return _combine_ln(ys, dest_flat, gates, x, ln_g, ln_b, alpha=alpha, to_natural=to_natural)


def _even_mixer_ln(x, w_in, lam_re, lam_im, log_step, b_re, b_im, c_re, c_im, d_skip, w_glu, b_glu,
                   w_gate2, b_gate2, norm_g, w_out, ln_g, ln_b, *, alpha, to_residue_major):
    T, D = x.shape
    W = d_skip.shape[0]
    qk = W // 2
    rank = w_gate2.shape[0]
    s4 = W + 2 * qk + W
    w_main = jnp.concatenate([w_in[:, :s4], w_in[:, s4 + rank:]], axis=1).astype(BF16)
    w_gate = jnp.zeros((D, LANES), BF16).at[:, :rank].set(w_in[:, s4:s4 + rank].astype(BF16))
    h, g_low = _proj_in(x, w_main, w_gate)
    tables = _s5_tables(lam_re, lam_im, log_step, b_re, b_im, c_re, c_im, T // S5_CHUNK)
    y = _s5_core(h, W, tables)
    ya = _s5_glu(y, h, d_skip.astype(F32), w_glu.astype(BF16), b_glu.astype(F32))
    yb = _gla(h, g_low, w_gate2, b_gate2, norm_g, width=W)
    w_out_b = w_out.astype(BF16)
    return _proj_ln([ya, yb], [w_out_b[:W], w_out_b[W:]], x, ln_g, ln_b, alpha=alpha,
                    to_residue_major=to_residue_major)


def _odd_mixer_ln(x, w_qkv, w_o, ln_g, ln_b, *, alpha):
    T, D = x.shape
    qkv = _proj_scaled(x, w_qkv.astype(BF16), scaled_cols=D, scale=(D // ATT_HEADS) ** -0.5)
    y = _dilated_attention(qkv, D)
    return _proj_ln([y], [w_o.astype(BF16)], x, ln_g, ln_b, alpha=alpha, to_residue_major=False)


def kernel(x, ab_w_in, s5_lam_re, s5_lam_im, s5_log_step, s5_b_re, s5_b_im, s5_c_re, s5_c_im, s5_d, s5_w_glu, s5_b_glu, gla_w_gate2, gla_b_gate2, gla_norm_g, ab_w_out, c_w_qkv, c_w_o, ln1_g, ln1_b, moe_w_router, moe_b_router, moe_w_gu, moe_b_gu, moe_w_down, moe_b_down, ln2_g, ln2_b):
    bsz, L, D = x.shape
    depth = ln1_g.shape[0]
    alpha = (2 * depth) ** 0.25
    outs = []
    for bi in range(bsz):
        xt = x[bi].astype(F32)
        for layer in range(depth):
            i = layer // 2
            odd = layer % 2 == 1
            if not odd:
                xt = _even_mixer_ln(xt, ab_w_in[i], s5_lam_re[i], s5_lam_im[i], s5_log_step[i], s5_b_re[i],
                                    s5_b_im[i], s5_c_re[i], s5_c_im[i], s5_d[i], s5_w_glu[i], s5_b_glu[i],
                                    gla_w_gate2[i], gla_b_gate2[i], gla_norm_g[i], ab_w_out[i],
                                    ln1_g[layer], ln1_b[layer], alpha=alpha, to_residue_major=layer + 1 < depth)
            else:
                xt = _odd_mixer_ln(xt, c_w_qkv[i], c_w_o[i], ln1_g[layer], ln1_b[layer], alpha=alpha)
            xt = _moe(xt, moe_w_router[layer], moe_b_router[layer], moe_w_gu, moe_b_gu,
                      moe_w_down, moe_b_down, ln2_g[layer], ln2_b[layer], alpha=alpha, layer=layer,
                      to_natural=odd)
        outs.append(xt.astype(x.dtype))
    return outs[0].reshape(1, L, D) if bsz == 1 else jnp.stack(outs)
```

```python
import functools
import math

import jax
import jax.numpy as jnp
from jax import lax
from jax.experimental import pallas as pl
from jax.experimental.pallas import tpu as pltpu

F32 = jnp.float32
BF16 = jnp.bfloat16
I32 = jnp.int32
HIGHEST = lax.Precision.HIGHEST

LANES = 128
VMEM_LIMIT_BYTES = 56 * 1024 * 1024

S5_GROUP = 16
S5_STATE = 64
S5_MAX_RE = -1e-4
S5_CHUNK = 16
GLA_HEADS = 4
GLA_GATE_TEMP = 16.0
GLA_EPS = 1e-6
GLA_BLOCK = 64
GLA_BLOCKS_PER_STEP = 2
GLA_SUB = 16
ATT_HEADS = 16
DIL = 16
DILATED_GROUPS = ((128, 1), (512, 4), (2048, 16))
TOP_K = 4
SWIGLU_LIMIT = 7.0
SWIGLU_ALPHA = 1.702
MOE_BLOCK = 128
MOE_CHUNK_BLOCKS = 9
LN_EPS = 1e-5
NEG_BIG = -1e30


def _cparams(semantics):
    return pltpu.CompilerParams(dimension_semantics=semantics, vmem_limit_bytes=VMEM_LIMIT_BYTES)


def _layer_norm_rows(z, g, b):
    mu = jnp.mean(z, axis=-1, keepdims=True)
    zc = z - mu
    var = jnp.mean(zc * zc, axis=-1, keepdims=True)
    return zc * lax.rsqrt(var + LN_EPS) * g + b


def _proj_in_kernel(x_ref, w_ref, wg_ref, h_ref, g_ref, xb_ref):
    @pl.when(pl.program_id(1) == 0)
    def _():
        xb = x_ref[...].astype(BF16)
        xb_ref[...] = xb
        g_ref[...] = jnp.dot(xb, wg_ref[...], preferred_element_type=F32)

    h_ref[...] = jnp.dot(xb_ref[...], w_ref[...], preferred_element_type=F32)


def _proj_in(x, w_main, w_gate, *, tm=1024, tn=512):
    T, D = x.shape
    N = w_main.shape[1]
    return pl.pallas_call(
        _proj_in_kernel,
        name="proj_in",
        grid=(T // tm, N // tn),
        in_specs=[pl.BlockSpec((tm, D), lambda i, j: (i, 0)),
                  pl.BlockSpec((D, tn), lambda i, j: (0, j)),
                  pl.BlockSpec((D, LANES), lambda i, j: (0, 0))],
        out_specs=[pl.BlockSpec((tm, tn), lambda i, j: (i, j)),
                   pl.BlockSpec((tm, LANES), lambda i, j: (i, 0))],
        out_shape=[jax.ShapeDtypeStruct((T, N), F32), jax.ShapeDtypeStruct((T, LANES), F32)],
        scratch_shapes=[pltpu.VMEM((tm, D), BF16)],
        compiler_params=_cparams(("parallel", "arbitrary")),
    )(x, w_main, w_gate)


def _proj_scaled_kernel(x_ref, w_ref, o_ref, xb_ref, *, n_scaled, scale):
    j = pl.program_id(1)

    @pl.when(j == 0)
    def _():
        xb_ref[...] = x_ref[...].astype(BF16)

    y = jnp.dot(xb_ref[...], w_ref[...], preferred_element_type=F32)
    o_ref[...] = (y * jnp.where(j < n_scaled, scale, 1.0)).astype(o_ref.dtype)


def _proj_scaled(x, w, *, scaled_cols, scale, tm=1024, tn=1024):
    T, D = x.shape
    N = w.shape[1]
    return pl.pallas_call(
        functools.partial(_proj_scaled_kernel, n_scaled=scaled_cols // tn, scale=scale),
        name="proj_qkv",
        grid=(T // tm, N // tn),
        in_specs=[pl.BlockSpec((tm, D), lambda i, j: (i, 0)),
                  pl.BlockSpec((D, tn), lambda i, j: (0, j))],
        out_specs=pl.BlockSpec((tm, tn), lambda i, j: (i, j)),
        out_shape=jax.ShapeDtypeStruct((T, N), BF16),
        scratch_shapes=[pltpu.VMEM((tm, D), BF16)],
        compiler_params=_cparams(("parallel", "arbitrary")),
    )(x, w)


def _proj_ln_kernel(*refs, n_lhs, alpha, n_tiles, to_residue_major):
    lhs_refs = refs[:n_lhs]
    w_refs = refs[n_lhs:2 * n_lhs]
    res_ref, g_ref, b_ref, o_ref, acc_ref = refs[2 * n_lhs:2 * n_lhs + 5]
    j = pl.program_id(1)
    y = jnp.dot(lhs_refs[0][...], w_refs[0][...], preferred_element_type=F32)
    for a_ref, w_ref in zip(lhs_refs[1:], w_refs[1:]):
        y = y + jnp.dot(a_ref[...], w_ref[...], preferred_element_type=F32)
    acc_ref[j] = y

    @pl.when(j == n_tiles - 1)
    def _():
        tn = acc_ref.shape[2]
        z = [alpha * res_ref[:, t * tn:(t + 1) * tn] + acc_ref[t] for t in range(n_tiles)]
        n = float(n_tiles * tn)
        mu = sum(jnp.sum(zt, axis=-1, keepdims=True) for zt in z) / n
        zc = [zt - mu for zt in z]
        var = sum(jnp.sum(zt * zt, axis=-1, keepdims=True) for zt in zc) / n
        rstd = lax.rsqrt(var + LN_EPS)
        out = [zc[t] * rstd * g_ref[:, t * tn:(t + 1) * tn] + b_ref[:, t * tn:(t + 1) * tn] for t in range(n_tiles)]
        if not to_residue_major:
            for t in range(n_tiles):
                o_ref[:, t * tn:(t + 1) * tn] = out[t]
        else:
            rows_ref = refs[-1]
            per_res = rows_ref.shape[1] // DIL
            for c in range(rows_ref.shape[0]):
                t, off = divmod(c * LANES, tn)
                rows_ref[c] = out[t][:, off:off + LANES]
                for r in range(DIL):
                    o_ref[r, :, c * LANES:(c + 1) * LANES] = rows_ref[c, pl.ds(r, per_res, stride=DIL), :]


def _proj_ln(lhs, ws, res, g, b, *, alpha, to_residue_major, tm=512, tn=512):
    T, N = res.shape
    n_lhs = len(lhs)
    n_tiles = N // tn
    scratch = [pltpu.VMEM((n_tiles, tm, tn), F32)]
    if to_residue_major:
        na = T // DIL
        out_spec = pl.BlockSpec((DIL, tm // DIL, N), lambda i, j: (0, i, 0))
        out_shape = jax.ShapeDtypeStruct((DIL, na, N), F32)
        scratch.append(pltpu.VMEM((N // LANES, tm, LANES), F32))
    else:
        out_spec = pl.BlockSpec((tm, N), lambda i, j: (i, 0))
        out_shape = jax.ShapeDtypeStruct((T, N), F32)
    in_specs = ([pl.BlockSpec((tm, a.shape[1]), lambda i, j: (i, 0)) for a in lhs]
                + [pl.BlockSpec((w.shape[0], tn), lambda i, j: (0, j)) for w in ws]
                + [pl.BlockSpec((tm, N), lambda i, j: (i, 0)),
                   pl.BlockSpec((1, N), lambda i, j: (0, 0)),
                   pl.BlockSpec((1, N), lambda i, j: (0, 0))])
    out = pl.pallas_call(
        functools.partial(_proj_ln_kernel, n_lhs=n_lhs, alpha=alpha, n_tiles=n_tiles,
                          to_residue_major=to_residue_major),
        name="proj_ln",
        grid=(T // tm, n_tiles),
        in_specs=in_specs,
        out_specs=out_spec,
        out_shape=out_shape,
        scratch_shapes=scratch,
        compiler_params=_cparams(("parallel", "arbitrary")),
    )(*lhs, *ws, res, g.reshape(1, N), b.reshape(1, N))
    return out.reshape(T, N)


def _s5_tables(lam_re, lam_im, log_step, b_re, b_im, c_re, c_im, n_chunks):
    C = S5_CHUNK
    G, P = lam_re.shape
    H = b_re.shape[-1]
    lr = jnp.minimum(lam_re.astype(F32), S5_MAX_RE)
    li = lam_im.astype(F32)
    dt = jnp.exp(log_step.astype(F32))[:, None]
    kk = jnp.arange(C + 1, dtype=F32)[:, None, None]
    pw_mag = jnp.exp(kk * (lr * dt))
    pw_re = pw_mag * jnp.cos(kk * (li * dt))
    pw_im = pw_mag * jnp.sin(kk * (li * dt))
    a_re, a_im = pw_re[1], pw_im[1]
    den = lr * lr + li * li
    nr = a_re - 1.0
    f_re = (nr * lr + a_im * li) / den
    f_im = (a_im * lr - nr * li) / den
    br = b_re.astype(F32)
    bi = b_im.astype(F32)
    bb_re = f_re[..., None] * br - f_im[..., None] * bi
    bb_im = f_re[..., None] * bi + f_im[..., None] * br
    ab_re = pw_re[:C, :, :, None] * bb_re[None] - pw_im[:C, :, :, None] * bb_im[None]
    ab_im = pw_re[:C, :, :, None] * bb_im[None] + pw_im[:C, :, :, None] * bb_re[None]
    cr = c_re.astype(F32)
    ci = c_im.astype(F32)
    z_re = jnp.transpose(ab_re[::-1], (1, 0, 3, 2)).reshape(G, C * H, P)
    z_im = jnp.transpose(ab_im[::-1], (1, 0, 3, 2)).reshape(G, C * H, P)
    zmat = jnp.concatenate([z_re, z_im], axis=-1)
    kern = (jnp.einsum('gop,kgph->kgoh', cr, ab_re, precision=HIGHEST)
            - jnp.einsum('gop,kgph->kgoh', ci, ab_im, precision=HIGHEST))
    lag = jnp.arange(C)[None, :] - jnp.arange(C)[:, None]
    kl = kern[jnp.clip(lag, 0, C - 1)]
    kl = jnp.where((lag >= 0)[:, :, None, None, None], kl, 0.0)
    mmat = jnp.transpose(kl, (2, 0, 4, 1, 3)).reshape(G, C * H, C * H)
    ca_re = cr[None] * pw_re[1:, :, None, :] - ci[None] * pw_im[1:, :, None, :]
    ca_im = cr[None] * pw_im[1:, :, None, :] + ci[None] * pw_re[1:, :, None, :]
    n_re = jnp.transpose(ca_re, (1, 3, 0, 2)).reshape(G, P, C * H)
    n_im = jnp.transpose(-ca_im, (1, 3, 0, 2)).reshape(G, P, C * H)
    nmat = jnp.concatenate([n_re, n_im], axis=1)
    n_steps = max(1, (n_chunks - 1).bit_length())
    qr, qi = pw_re[C], pw_im[C]
    a1, a2 = [], []
    for _ in range(n_steps):
        a1.append(jnp.concatenate([qr, qr], axis=-1))
        a2.append(jnp.concatenate([-qi, qi], axis=-1))
        qr, qi = qr * qr - qi * qi, 2.0 * qr * qi
    a1 = jnp.stack(a1, axis=1)
    a2 = jnp.stack(a2, axis=1)
    return zmat.astype(BF16), mmat.astype(BF16), nmat.astype(BF16), a1, a2


def _s5_kernel(u_ref, z_ref, m_ref, n_ref, a1_ref, a2_ref, y_ref, ub_ref, ug_ref, sg_ref, mw_ref, nw_ref, *,
               n_steps):
    C, H = S5_CHUNK, S5_GROUP
    nc = u_ref.shape[0] // C
    gpt = LANES // H
    ch = C * H
    for j in range(C):
        ub_ref[:, j * LANES:(j + 1) * LANES] = u_ref[pl.ds(j, nc, stride=C), :].astype(BF16)
    row = lax.broadcasted_iota(I32, (nc, LANES), 0)
    sr = lax.broadcasted_iota(I32, (gpt * LANES, LANES), 0)
    sc = lax.broadcasted_iota(I32, (gpt * LANES, LANES), 1)
    sel_hit = sc == H * (sr // LANES) + sr % H
    sel_grp = (sr % LANES) // H
    pr = lax.broadcasted_iota(I32, (ch, C * LANES), 0)
    pc = lax.broadcasted_iota(I32, (ch, C * LANES), 1)
    put_tile = pc // LANES == pr // H
    put_lane = pc % LANES - pr % H

    def group(gl, c):
        sel = jnp.where(sel_hit & (sel_grp == gl), 1.0, 0.0).astype(BF16)
        half_w = gpt * LANES
        u = jnp.concatenate(
            [jnp.dot(ub_ref[:, t * half_w:(t + 1) * half_w], sel, preferred_element_type=F32)
             for t in range(C * LANES // half_w)], axis=1).astype(BF16)
        s = jnp.dot(u, z_ref[gl], preferred_element_type=F32)
        half = s.shape[1] // 2
        a1 = a1_ref[gl]
        a2 = a2_ref[gl]
        for k in range(n_steps):
            sh = 1 << k
            prev = jnp.where(row >= sh, pltpu.roll(s, sh, axis=0), 0.0)
            s = s + a1[k:k + 1, :] * prev + a2[k:k + 1, :] * pltpu.roll(prev, half, axis=1)
        s_in = jnp.where(row >= 1, pltpu.roll(s, 1, axis=0), 0.0)
        put = jnp.where(put_tile & (put_lane == H * gl), 1.0, 0.0).astype(BF16)
        ug_ref[gl] = u
        sg_ref[gl] = s_in.astype(BF16)
        mw_ref[gl] = jnp.dot(m_ref[gl], put, preferred_element_type=F32).astype(BF16)
        nw_ref[gl] = jnp.dot(n_ref[gl], put, preferred_element_type=F32).astype(BF16)
        return c

    lax.fori_loop(0, gpt, group, 0)
    u_all = jnp.concatenate([ug_ref[g] for g in range(gpt)], axis=1)
    s_all = jnp.concatenate([sg_ref[g] for g in range(gpt)], axis=1)
    y = (jnp.dot(u_all, mw_ref[...].reshape(gpt * ch, C * LANES), preferred_element_type=F32)
         + jnp.dot(s_all, nw_ref[...].reshape(gpt * sg_ref.shape[2], C * LANES), preferred_element_type=F32))
    for i in range(C):
        y_ref[pl.ds(i, nc, stride=C), :] = y[:, i * LANES:(i + 1) * LANES]


def _s5_core(h, width, tables):
    zmat, mmat, nmat, a1, a2 = tables
    T = h.shape[0]
    G, CH, P2 = zmat.shape
    C, H = S5_CHUNK, S5_GROUP
    nc = T // C
    n_steps = a1.shape[1]
    gpt = LANES // H
    assert CH == C * H and width == G * H
    return pl.pallas_call(
        functools.partial(_s5_kernel, n_steps=n_steps),
        name="s5_core",
        grid=(G // gpt,),
        in_specs=[pl.BlockSpec((T, LANES), lambda m: (0, m)),
                  pl.BlockSpec((gpt, CH, P2), lambda m: (m, 0, 0)),
                  pl.BlockSpec((gpt, CH, CH), lambda m: (m, 0, 0)),
                  pl.BlockSpec((gpt, P2, CH), lambda m: (m, 0, 0)),
                  pl.BlockSpec((gpt, n_steps, P2), lambda m: (m, 0, 0)),
                  pl.BlockSpec((gpt, n_steps, P2), lambda m: (m, 0, 0))],
        out_specs=pl.BlockSpec((T, LANES), lambda m: (0, m)),
        out_shape=jax.ShapeDtypeStruct((T, width), F32),
        scratch_shapes=[pltpu.VMEM((nc, C * LANES), BF16),
                        pltpu.VMEM((gpt, nc, CH), BF16),
                        pltpu.VMEM((gpt, nc, P2), BF16),
                        pltpu.VMEM((gpt, CH, C * LANES), BF16),
                        pltpu.VMEM((gpt, P2, C * LANES), BF16)],
        compiler_params=_cparams(("parallel",)),
    )(h, zmat, mmat, nmat, a1, a2)


def _s5_glu_kernel(y_ref, u_ref, d_ref, w_ref, b_ref, o_ref):
    y = y_ref[...] + d_ref[...] * u_ref[...]
    c0 = math.sqrt(2.0 / math.pi)
    z = 0.5 * y * (1.0 + jnp.tanh(c0 * (y + 0.044715 * (y * y * y))))
    lin = jnp.dot(z.astype(BF16), w_ref[...], preferred_element_type=F32) + b_ref[...]
    o_ref[...] = (z * jax.nn.sigmoid(lin)).astype(o_ref.dtype)


def _s5_glu(y, h, d_skip, w_glu, b_glu, *, tm=512):
    T, W = y.shape
    return pl.pallas_call(
        _s5_glu_kernel,
        name="s5_glu",
        grid=(T // tm,),
        in_specs=[pl.BlockSpec((tm, W), lambda i: (i, 0)),
                  pl.BlockSpec((tm, W), lambda i: (i, 0)),
                  pl.BlockSpec((1, W), lambda i: (0, 0)),
                  pl.BlockSpec((W, W), lambda i: (0, 0)),
                  pl.BlockSpec((1, W), lambda i: (0, 0))],
        out_specs=pl.BlockSpec((tm, W), lambda i: (i, 0)),
        out_shape=jax.ShapeDtypeStruct((T, W), BF16),
        compiler_params=_cparams(("parallel",)),
    )(y, h, d_skip.reshape(1, W), w_glu, b_glu.reshape(1, W))


def _gla_kernel(q_ref, k_ref, v_ref, r_ref, g_ref, w2_ref, b2_ref, ng_ref, o_ref, st_ref, *, dk, dv):
    @pl.when(pl.program_id(0) == 0)
    def _():
        st_ref[...] = jnp.zeros_like(st_ref)

    cb = GLA_BLOCK
    n_sub = cb // GLA_SUB
    scale = dk ** -0.5
    ri = lax.broadcasted_iota(I32, (cb, cb), 0)
    ci = lax.broadcasted_iota(I32, (cb, cb), 1)
    tri = (ri >= ci).astype(F32)
    states = [st_ref[hh] for hh in range(GLA_HEADS)]
    for blk in range(q_ref.shape[0] // cb):
        rows = slice(blk * cb, (blk + 1) * cb)
        states = _gla_block(q_ref, k_ref, v_ref, r_ref, g_ref, w2_ref, b2_ref, ng_ref, o_ref, rows, states, tri,
                            dk=dk, dv=dv, n_sub=n_sub, scale=scale)
    for hh in range(GLA_HEADS):
        st_ref[hh] = states[hh]


def _gla_block(q_ref, k_ref, v_ref, r_ref, g_ref, w2_ref, b2_ref, ng_ref, o_ref, rows, states, tri, *,
               dk, dv, n_sub, scale):
    cb = GLA_BLOCK
    logit = jnp.dot(g_ref[rows, :], w2_ref[...], preferred_element_type=F32, precision=HIGHEST) + b2_ref[...]
    log_a = (jnp.minimum(logit, 0.0) - jnp.log(1.0 + jnp.exp(-jnp.abs(logit)))) / GLA_GATE_TEMP
    bcum = jnp.dot(tri, log_a, preferred_element_type=F32, precision=HIGHEST)
    new_states = []
    for hh in range(GLA_HEADS):
        ks = slice(hh * dk, (hh + 1) * dk)
        vs = slice(hh * dv, (hh + 1) * dv)
        b = bcum[:, ks]
        q = q_ref[rows, ks] * scale
        k = k_ref[rows, ks]
        v = v_ref[rows, vs].astype(BF16)
        refs = [jnp.zeros((1, dk), F32)] + [b[a * GLA_SUB - 1:a * GLA_SUB, :] for a in range(1, n_sub)]
        refmat = jnp.concatenate([jnp.broadcast_to(r, (GLA_SUB, dk)) for r in refs], axis=0)
        qe = (q * jnp.exp(b - refmat)).astype(BF16)
        st = states[hh]
        o_inter = lax.dot_general((q * jnp.exp(b)).astype(BF16), st.astype(BF16),
                                  (((1,), (1,)), ((), ())), preferred_element_type=F32)
        o_rows = []
        for a in range(n_sub):
            lo, hi = a * GLA_SUB, (a + 1) * GLA_SUB
            row_i = lax.broadcasted_iota(I32, (GLA_SUB, hi), 0)
            col_j = lax.broadcasted_iota(I32, (GLA_SUB, hi), 1)
            key_row = lax.broadcasted_iota(I32, (hi, dk), 0)
            ke = jnp.where(key_row < lo, k[:hi] * jnp.exp(jnp.minimum(refs[a] - b[:hi], 0.0)), 0.0).astype(BF16)
            att = lax.dot_general(qe[lo:hi], ke, (((1,), (1,)), ((), ())), preferred_element_type=F32)
            b_a, q_a, k_a = b[lo:hi], q[lo:hi], k[lo:hi]
            terms = [q_a * jnp.exp(jnp.minimum(b_a - b_a[j:j + 1, :], 0.0)) * k_a[j:j + 1, :]
                     for j in range(GLA_SUB)]
            sums = jnp.dot(jnp.concatenate(terms, axis=0).astype(BF16), jnp.ones((dk, hi), BF16),
                           preferred_element_type=F32)
            for j in range(GLA_SUB):
                att = jnp.where((col_j == lo + j) & (row_i >= j), sums[j * GLA_SUB:(j + 1) * GLA_SUB], att)
            o_rows.append(jnp.dot(att.astype(BF16), v[:hi], preferred_element_type=F32))
        o = jnp.concatenate(o_rows, axis=0) + o_inter
        o = o * lax.rsqrt(jnp.mean(o * o, axis=-1, keepdims=True) + GLA_EPS) * ng_ref[...]
        r = r_ref[rows, vs]
        o_ref[rows, vs] = (o * (r * jax.nn.sigmoid(r))).astype(o_ref.dtype)
        b_last = b[cb - 1:cb, :]
        kd = (k * jnp.exp(b_last - b)).astype(BF16)
        upd = lax.dot_general(v, kd, (((0,), (0,)), ((), ())), preferred_element_type=F32)
        new_states.append(st * jnp.exp(b_last) + upd)
    return new_states


def _gla(h, g_low, w_gate2, b_gate2, norm_g, *, width):
    T = h.shape[0]
    qk = width // 2
    dk = qk // GLA_HEADS
    dv = width // GLA_HEADS
    cb = GLA_BLOCK * GLA_BLOCKS_PER_STEP
    w2 = jnp.zeros((LANES, qk), F32).at[:w_gate2.shape[0]].set(w_gate2.astype(F32))
    return pl.pallas_call(
        functools.partial(_gla_kernel, dk=dk, dv=dv),
        name="gla",
        grid=(T // cb,),
        in_specs=[pl.BlockSpec((cb, qk), lambda i: (i, 2)),
                  pl.BlockSpec((cb, qk), lambda i: (i, 3)),
                  pl.BlockSpec((cb, width), lambda i: (i, 2)),
                  pl.BlockSpec((cb, width), lambda i: (i, 3)),
                  pl.BlockSpec((cb, LANES), lambda i: (i, 0)),
                  pl.BlockSpec((LANES, qk), lambda i: (0, 0)),
                  pl.BlockSpec((1, qk), lambda i: (0, 0)),
                  pl.BlockSpec((1, dv), lambda i: (0, 0))],
        out_specs=pl.BlockSpec((cb, width), lambda i: (i, 0)),
        out_shape=jax.ShapeDtypeStruct((T, width), BF16),
        scratch_shapes=[pltpu.VMEM((GLA_HEADS, dv, dk), F32)],
        compiler_params=_cparams(("arbitrary",)),
    )(h, h, h, h, g_low, w2, b_gate2.reshape(1, qk).astype(F32), norm_g.reshape(1, dv).astype(F32))


def _attn_kernel(q_ref, k_ref, v_ref, o_ref, acc_ref, m_ref, l_ref, v1_ref, pat1_ref, pat4_ref, pat16_ref, *, na):
    dh = v_ref.shape[1]
    v1_ref[:, :dh] = v_ref[...]
    v1_ref[:, dh:] = jnp.ones_like(v_ref)
    acc_ref[...] = jnp.zeros_like(acc_ref)
    m_ref[...] = jnp.full_like(m_ref, NEG_BIG)
    l_ref[...] = jnp.zeros_like(l_ref)

    def token_offsets(rows, r_step, n, axis):
        shape = (n, 1) if axis == 0 else (1, n)
        idx = lax.broadcasted_iota(I32, shape, axis)
        c = idx // rows
        return DIL * (idx - c * rows) + r_step * c

    def delta_pattern(n_chunks, q_rows, k_rows, r_step):
        return (token_offsets(q_rows, r_step, n_chunks * q_rows, 0)
                - token_offsets(k_rows, r_step, n_chunks * k_rows, 1))

    def cat(ref, starts, rows):
        return jnp.concatenate([ref[pl.ds(pl.multiple_of(s, 16), rows), :] for s in starts], axis=0)

    def window_bias(n_chunks, q_rows, k_rows, r_step, back, window):
        pat = delta_pattern(n_chunks, q_rows, k_rows, r_step)
        inside = lambda off: jnp.where((pat + off >= 0) & (pat + off <= window), 0.0, NEG_BIG)
        return jnp.stack([inside(0), inside(DIL * back)])

    def attend(blocks, q_rows, k_rows, bias_ref):
        loaded = []
        for q_starts, k_starts, a_q, a_k in blocks:
            loaded.append((cat(q_ref, q_starts, q_rows), cat(k_ref, k_starts, k_rows), cat(v1_ref, k_starts, k_rows),
                           cat(m_ref, q_starts, q_rows), cat(l_ref, q_starts, q_rows),
                           cat(acc_ref, q_starts, q_rows)))
        results = []
        for (q_starts, k_starts, a_q, a_k), (qb, kb, vb, m_old, l_old, acc_old) in zip(blocks, loaded):
            s = lax.dot_general(qb, kb, (((1,), (1,)), ((), ())), preferred_element_type=F32)
            s = s + bias_ref[jnp.where(a_q == a_k, 0, 1)]
            m_new = jnp.maximum(m_old, jnp.max(s, axis=1, keepdims=True))
            alpha = jnp.exp(m_old - m_new)
            p = jnp.exp(s - m_new[:, :1])
            pv = jnp.dot(p.astype(BF16), vb, preferred_element_type=F32)
            l_new = alpha * l_old + pv[:, dh:]
            acc_new = alpha * acc_old + pv[:, :dh]
            results.append((m_new, l_new, acc_new))
        for (q_starts, *_), (m_new, l_new, acc_new) in zip(blocks, results):
            for c, st in enumerate(q_starts):
                st = pl.multiple_of(st, 16)
                rs = slice(c * q_rows, (c + 1) * q_rows)
                m_ref[pl.ds(st, q_rows), :] = m_new[rs]
                l_ref[pl.ds(st, q_rows), :] = l_new[rs]
                acc_ref[pl.ds(st, q_rows), :] = acc_new[rs]

    w1 = DILATED_GROUPS[0][0]
    u1 = 2
    pat1_ref[...] = window_bias(DIL, 16, 32, 1, 16, w1)

    def body1(it, carry):
        blocks = []
        for j in range(u1):
            a0 = (it * u1 + j) * 16
            ak = jnp.maximum(a0 - 16, 0)
            blocks.append(([r * na + a0 for r in range(DIL)], [r * na + ak for r in range(DIL)], a0, ak))
        attend(blocks, 16, 32, pat1_ref)
        return carry

    lax.fori_loop(0, na // (16 * u1), body1, 0)

    w4 = DILATED_GROUPS[1][0]
    pat4_ref[...] = window_bias(4, 32, 64, 4, 32, w4)

    def body4(it, carry):
        a0 = it * 32
        ak = jnp.maximum(a0 - 32, 0)
        blocks = [([(rho + 4 * sg) * na + a0 for sg in range(4)], [(rho + 4 * sg) * na + ak for sg in range(4)],
                   a0, ak) for rho in range(4)]
        attend(blocks, 32, 64, pat4_ref)
        return carry

    lax.fori_loop(0, na // 32, body4, 0)

    w16 = DILATED_GROUPS[2][0]
    u16 = 4
    pat16_ref[...] = window_bias(1, 128, 256, 0, 128, w16)

    def body16(it, carry):
        rg = it // (na // 128)
        a0 = (it - rg * (na // 128)) * 128
        ak = jnp.maximum(a0 - 128, 0)
        blocks = [([(rg * u16 + j) * na + a0], [(rg * u16 + j) * na + ak], a0, ak) for j in range(u16)]
        attend(blocks, 128, 256, pat16_ref)
        return carry

    lax.fori_loop(0, (DIL // u16) * (na // 128), body16, 0)

    o_ref[...] = (acc_ref[...] / l_ref[...]).astype(o_ref.dtype)


def _dilated_attention(qkv, d_model):
    T = qkv.shape[0]
    dh = d_model // ATT_HEADS
    assert dh == LANES
    na = T // DIL
    return pl.pallas_call(
        functools.partial(_attn_kernel, na=na),
        name="dilated_attn",
        grid=(ATT_HEADS,),
        in_specs=[pl.BlockSpec((T, dh), lambda h: (0, h)),
                  pl.BlockSpec((T, dh), lambda h: (0, ATT_HEADS + h)),
                  pl.BlockSpec((T, dh), lambda h: (0, 2 * ATT_HEADS + h))],
        out_specs=pl.BlockSpec((T, dh), lambda h: (0, h)),
        out_shape=jax.ShapeDtypeStruct((T, d_model), BF16),
        scratch_shapes=[pltpu.VMEM((T, dh), F32), pltpu.VMEM((T, dh), F32), pltpu.VMEM((T, dh), F32),
                        pltpu.VMEM((T, 2 * dh), BF16), pltpu.VMEM((2, 256, 512), F32),
                        pltpu.VMEM((2, 128, 256), F32), pltpu.VMEM((2, 128, 256), F32)],
        compiler_params=_cparams(("parallel",)),
    )(qkv, qkv, qkv)


def _route_kernel(x_ref, w_ref, b_ref, idx_ref, gate_ref, rank_ref, cnt_ref, carry_ref):
    @pl.when(pl.program_id(0) == 0)
    def _():
        carry_ref[...] = jnp.zeros_like(carry_ref)

    tb = x_ref.shape[0]
    x = x_ref[...]
    w = w_ref[...]
    x_hi, w_hi = x.astype(BF16), w.astype(BF16)
    x_lo = (x - x_hi.astype(F32)).astype(BF16)
    w_lo = (w - w_hi.astype(F32)).astype(BF16)
    lg = (jnp.dot(x_hi, w_hi, preferred_element_type=F32) + jnp.dot(x_hi, w_lo, preferred_element_type=F32)
          + jnp.dot(x_lo, w_hi, preferred_element_type=F32)) + b_ref[...]
    lane = lax.broadcasted_iota(I32, lg.shape, 1)
    vals, hots = [], []
    idx_out = jnp.zeros(lg.shape, I32)
    for k in range(TOP_K):
        m = jnp.max(lg, axis=1, keepdims=True)
        sel = jnp.min(jnp.where(lg == m, lane, LANES), axis=1, keepdims=True)
        hot = lane == sel
        vals.append(m)
        hots.append(hot)
        idx_out = jnp.where(lane == k, sel, idx_out)
        lg = jnp.where(hot, -jnp.inf, lg)
    ex = [jnp.exp(v - vals[0]) for v in vals]
    den = sum(ex)
    gate_out = jnp.zeros(lg.shape, F32)
    for k in range(TOP_K):
        gate_out = jnp.where(lane == k, ex[k] / den, gate_out)
    chosen = sum(h.astype(F32) for h in hots)
    ri = lax.broadcasted_iota(I32, (tb, tb), 0)
    ci = lax.broadcasted_iota(I32, (tb, tb), 1)
    before = jnp.dot((ri > ci).astype(BF16), chosen.astype(BF16), preferred_element_type=F32) + carry_ref[...]
    rank_out = jnp.zeros(lg.shape, I32)
    for k in range(TOP_K):
        rk = jnp.sum(jnp.where(hots[k], before, 0.0), axis=1, keepdims=True).astype(I32)
        rank_out = jnp.where(lane == k, rk, rank_out)
    idx_ref[...] = idx_out
    gate_ref[...] = gate_out
    rank_ref[...] = rank_out
    carry_ref[...] = carry_ref[...] + jnp.sum(chosen, axis=0, keepdims=True)
    cnt_ref[...] = carry_ref[...].astype(I32)


def _route(x, w_router, b_router, *, tb=512):
    T, D = x.shape
    E = w_router.shape[1]
    wr = jnp.zeros((D, LANES), F32).at[:, :E].set(w_router.astype(F32))
    br = jnp.full((1, LANES), NEG_BIG, F32).at[0, :E].set(b_router.astype(F32))
    row = lambda dt: jax.ShapeDtypeStruct((T, LANES), dt)
    idx, gate, rank, cnt = pl.pallas_call(
        _route_kernel,
        name="moe_route",
        grid=(T // tb,),
        in_specs=[pl.BlockSpec((tb, D), lambda i: (i, 0)),
                  pl.BlockSpec((D, LANES), lambda i: (0, 0)),
                  pl.BlockSpec((1, LANES), lambda i: (0, 0))],
        out_specs=[pl.BlockSpec((tb, LANES), lambda i: (i, 0)),
                   pl.BlockSpec((tb, LANES), lambda i: (i, 0)),
                   pl.BlockSpec((tb, LANES), lambda i: (i, 0)),
                   pl.BlockSpec((1, LANES), lambda i: (0, 0))],
        out_shape=[row(I32), row(F32), row(I32), jax.ShapeDtypeStruct((1, LANES), I32)],
        scratch_shapes=[pltpu.VMEM((1, LANES), F32)],
        compiler_params=_cparams(("arbitrary",)),
    )(x, wr, br)
    return idx, gate, rank, cnt[0, :E]


def _zero_tail_kernel(row_ref, o_ref, z_ref, sem):
    z_ref[...] = jnp.zeros_like(z_ref)
    n = row_ref.shape[0]

    def copy(e):
        return pltpu.make_async_copy(z_ref, o_ref.at[pl.ds(pl.multiple_of(row_ref[e], MOE_BLOCK), MOE_BLOCK)], sem)

    def start(e, c):
        copy(e).start()
        return c

    def wait(e, c):
        copy(e).wait()
        return c

    lax.fori_loop(0, n, start, 0)
    lax.fori_loop(0, n, wait, 0)


def _zero_tails(tail_rows, n_rows, d):
    return pl.pallas_call(
        _zero_tail_kernel,
        name="moe_zero_tails",
        grid_spec=pltpu.PrefetchScalarGridSpec(
            num_scalar_prefetch=1, grid=(1,),
            in_specs=[],
            out_specs=pl.BlockSpec(memory_space=pl.ANY),
            scratch_shapes=[pltpu.VMEM((MOE_BLOCK, d), F32), pltpu.SemaphoreType.DMA(())]),
        out_shape=jax.ShapeDtypeStruct((n_rows, d), F32),
        compiler_params=_cparams(("arbitrary",)),
    )(tail_rows)


def _dispatch_kernel(dest_ref, x_ref, xs_in_ref, xs_ref, sem):
    del xs_in_ref
    tb = x_ref.shape[0]
    base = pl.program_id(0) * tb * TOP_K

    def copy(i, k):
        return pltpu.make_async_copy(x_ref.at[pl.ds(i, 1)],
                                     xs_ref.at[pl.ds(dest_ref[base + i * TOP_K + k], 1)], sem)

    def start(i, c):
        for k in range(TOP_K):
            copy(i, k).start()
        return c

    lax.fori_loop(0, tb, start, 0, unroll=4)
    for _ in range(TOP_K):
        pltpu.make_async_copy(x_ref, xs_ref.at[pl.ds(0, tb)], sem).wait()


def _dispatch(x, dest_flat, xs_init, *, tb=256):
    T, D = x.shape
    return pl.pallas_call(
        _dispatch_kernel,
        name="moe_dispatch",
        grid_spec=pltpu.PrefetchScalarGridSpec(
            num_scalar_prefetch=1, grid=(T // tb,),
            in_specs=[pl.BlockSpec((tb, D), lambda i, dest: (i, 0)),
                      pl.BlockSpec(memory_space=pl.ANY)],
            out_specs=pl.BlockSpec(memory_space=pl.ANY),
            scratch_shapes=[pltpu.SemaphoreType.DMA(())]),
        out_shape=jax.ShapeDtypeStruct(xs_init.shape, xs_init.dtype),
        input_output_aliases={2: 0},
        compiler_params=_cparams(("arbitrary",)),
    )(dest_flat, x, xs_init)


def _moe_kernel(item_e_ref, item_row_ref, item_nb_ref, xs_ref, wg_ref, wu_ref, bg_ref, bu_ref, wd_ref, bd_ref,
                ys_ref, xbuf_ref, land_ref, acc_ref, in_sem, out_sem, *, n_f):
    it = pl.program_id(0)
    f = pl.program_id(1)
    nb = item_nb_ref[it]
    row0 = item_row_ref[it]
    blk = MOE_BLOCK

    def acc_rows(j):
        return pl.ds(pl.multiple_of(j * blk, blk), blk)

    def in_copy(item, j):
        src = xs_ref.at[pl.ds(pl.multiple_of(item_row_ref[item] + j * blk, blk), blk)]
        return pltpu.make_async_copy(src, land_ref.at[acc_rows(j)], in_sem)

    def out_copy(j):
        dst = ys_ref.at[pl.ds(pl.multiple_of(row0 + j * blk, blk), blk)]
        return pltpu.make_async_copy(acc_ref.at[acc_rows(j)], dst, out_sem)

    def for_blocks(fn, n=nb):
        def body(j, c):
            fn(j)
            return c

        lax.fori_loop(0, n, body, 0)

    @pl.when((f == 0) & (nb > 0))
    def _():
        @pl.when(it == 0)
        def _():
            for_blocks(lambda j: in_copy(it, j).start())

        for_blocks(lambda j: in_copy(it, j).wait())

        def to_bf16(j):
            xbuf_ref[acc_rows(j), :] = land_ref[acc_rows(j), :].astype(BF16)
            acc_ref[acc_rows(j), :] = jnp.broadcast_to(bd_ref[...], (blk, acc_ref.shape[1]))

        for_blocks(to_bf16)

    @pl.when((f == 1) & (it + 1 < pl.num_programs(0)))
    def _():
        nxt = it + 1
        for_blocks(lambda j: in_copy(nxt, j).start(), item_nb_ref[nxt])

    @pl.when(nb > 0)
    def _():
        def compute(r0, n_rows, wg, wu, wd):
            rows = pl.ds(pl.multiple_of(r0, blk), n_rows)
            xb = xbuf_ref[rows, :]
            hg = jnp.dot(xb, wg, preferred_element_type=F32) + bg_ref[...]
            hu = jnp.dot(xb, wu, preferred_element_type=F32) + bu_ref[...]
            gate = jnp.minimum(hg, SWIGLU_LIMIT)
            up = jnp.clip(hu, -SWIGLU_LIMIT, SWIGLU_LIMIT)
            act = (up + 1.0) * (gate * jax.nn.sigmoid(SWIGLU_ALPHA * gate))
            acc_ref[rows, :] += jnp.dot(act.astype(BF16), wd, preferred_element_type=F32)

            @pl.when(f == n_f - 1)
            def _():
                for b in range(n_rows // blk):
                    out_copy(r0 // blk + b).start()

        def piece(r0, n_rows):
            compute(r0, n_rows, wg_ref[...].astype(BF16), wu_ref[...].astype(BF16), wd_ref[...].astype(BF16))

        n4 = nb // 4

        def quad(j, c):
            piece(j * (4 * blk), 4 * blk)
            return c

        lax.fori_loop(0, n4, quad, 0)

        @pl.when((nb & 2) != 0)
        def _():
            piece(n4 * (4 * blk), 2 * blk)

        @pl.when((nb & 1) != 0)
        def _():
            piece((nb - 1) * blk, blk)

    @pl.when((f == n_f - 1) & (nb > 0))
    def _():
        for_blocks(lambda j: out_copy(j).wait())


def _moe_experts(xs, items, w_gu, b_gu, w_down, b_down, layer, *, tf=512):
    item_e, item_row, item_nb = items
    n_items = item_e.shape[0]
    P, D = xs.shape
    L, E, _, F2 = w_gu.shape
    F = F2 // 2
    n_f = F // tf
    assert n_f >= 2
    rc = MOE_CHUNK_BLOCKS * MOE_BLOCK

    def fe(f, nb_ref, it):
        return jnp.where(nb_ref[it] > 0, f, n_f - 1)

    in_specs = [
        pl.BlockSpec(memory_space=pl.ANY),
        pl.BlockSpec((None, None, D, tf), lambda it, f, e, r, nb: (layer, e[it], 0, fe(f, nb, it))),
        pl.BlockSpec((None, None, D, tf), lambda it, f, e, r, nb: (layer, e[it], 0, n_f + fe(f, nb, it))),
        pl.BlockSpec((None, None, 1, tf), lambda it, f, e, r, nb: (layer, e[it], 0, fe(f, nb, it))),
        pl.BlockSpec((None, None, 1, tf), lambda it, f, e, r, nb: (layer, e[it], 0, n_f + fe(f, nb, it))),
        pl.BlockSpec((None, None, tf, D), lambda it, f, e, r, nb: (layer, e[it], fe(f, nb, it), 0)),
        pl.BlockSpec((None, None, 1, D), lambda it, f, e, r, nb: (layer, e[it], 0, 0)),
    ]
    return pl.pallas_call(
        functools.partial(_moe_kernel, n_f=n_f),
        name="moe_experts",
        grid_spec=pltpu.PrefetchScalarGridSpec(
            num_scalar_prefetch=3, grid=(n_items, n_f),
            in_specs=in_specs,
            out_specs=pl.BlockSpec(memory_space=pl.ANY),
            scratch_shapes=[pltpu.VMEM((rc, D), BF16),
                            pltpu.VMEM((rc, D), F32),
                            pltpu.VMEM((rc, D), F32),
                            pltpu.SemaphoreType.DMA(()),
                            pltpu.SemaphoreType.DMA(())]),
        out_shape=jax.ShapeDtypeStruct((P, D), F32),
        compiler_params=_cparams(("arbitrary", "arbitrary")),
    )(item_e, item_row, item_nb, xs, w_gu, w_gu, b_gu.reshape(L, E, 1, F2), b_gu.reshape(L, E, 1, F2),
      w_down, b_down.reshape(L, E, 1, D))


def _combine_kernel(dest_ref, ys_ref, gate_ref, x_ref, g_ref, b_ref, o_ref, buf_ref, sem, *scratch,
                    alpha, to_natural):
    tb = o_ref.shape[0]
    step = pl.program_id(0)
    slot = step % 2
    per_res = tb // DIL

    def token(blk, i):
        if not to_natural:
            return blk * tb + i
        n_a = pl.num_programs(0) * per_res
        return (i % DIL) * n_a + blk * per_res + i // DIL

    def copy(blk, sl, i, k):
        src = ys_ref.at[pl.ds(dest_ref[token(blk, i) * TOP_K + k], 1)]
        return pltpu.make_async_copy(src, buf_ref.at[sl, k, pl.ds(i, 1)], sem.at[sl])

    def for_rows(fn):
        def body(i, c):
            for k in range(TOP_K):
                fn(i, k)
            return c

        lax.fori_loop(0, tb, body, 0, unroll=4)

    @pl.when(step == 0)
    def _():
        for_rows(lambda i, k: copy(0, 0, i, k).start())

    @pl.when(step + 1 < pl.num_programs(0))
    def _():
        for_rows(lambda i, k: copy(step + 1, 1 - slot, i, k).start())

    for k in range(TOP_K):
        pltpu.make_async_copy(ys_ref.at[pl.ds(0, tb)], buf_ref.at[slot, k], sem.at[slot]).wait()
    if to_natural:
        xn_ref, gn_ref = scratch
        for r in range(DIL):
            gn_ref[pl.ds(r, per_res, stride=DIL), :] = gate_ref[r]
            for c in range(xn_ref.shape[0]):
                xn_ref[c, pl.ds(r, per_res, stride=DIL), :] = x_ref[r, :, c * LANES:(c + 1) * LANES]
        x = jnp.concatenate([xn_ref[c] for c in range(xn_ref.shape[0])], axis=1)
        gate = gn_ref[...]
    else:
        x = x_ref[...]
        gate = gate_ref[...]
    ffn = gate[:, 0:1] * buf_ref[slot, 0]
    for k in range(1, TOP_K):
        ffn = ffn + gate[:, k:k + 1] * buf_ref[slot, k]
    o_ref[...] = _layer_norm_rows(alpha * x + ffn, g_ref[...], b_ref[...])


def _combine_ln(ys, dest_flat, gates, x, g, b, *, alpha, to_natural, tb=128):
    T, D = x.shape
    gate_pad = gates
    scratch = [pltpu.VMEM((2, TOP_K, tb, D), F32), pltpu.SemaphoreType.DMA((2,))]
    if to_natural:
        na = T // DIL
        per_res = tb // DIL
        gate_in = gate_pad.reshape(DIL, na, LANES)
        x_in = x.reshape(DIL, na, D)
        gate_spec = pl.BlockSpec((DIL, per_res, LANES), lambda i, dest: (0, i, 0))
        x_spec = pl.BlockSpec((DIL, per_res, D), lambda i, dest: (0, i, 0))
        scratch += [pltpu.VMEM((D // LANES, tb, LANES), F32), pltpu.VMEM((tb, LANES), F32)]
    else:
        gate_in, x_in = gate_pad, x
        gate_spec = pl.BlockSpec((tb, LANES), lambda i, dest: (i, 0))
        x_spec = pl.BlockSpec((tb, D), lambda i, dest: (i, 0))
    return pl.pallas_call(
        functools.partial(_combine_kernel, alpha=alpha, to_natural=to_natural),
        name="moe_combine",
        grid_spec=pltpu.PrefetchScalarGridSpec(
            num_scalar_prefetch=1, grid=(T // tb,),
            in_specs=[pl.BlockSpec(memory_space=pl.ANY),
                      gate_spec,
                      x_spec,
                      pl.BlockSpec((1, D), lambda i, dest: (0, 0)),
                      pl.BlockSpec((1, D), lambda i, dest: (0, 0))],
            out_specs=pl.BlockSpec((tb, D), lambda i, dest: (i, 0)),
            scratch_shapes=scratch),
        out_shape=jax.ShapeDtypeStruct((T, D), F32),
        compiler_params=_cparams(("arbitrary",)),
    )(dest_flat, ys, gate_in, x_in, g.reshape(1, D), b.reshape(1, D))


def _moe_layout(idx, rank, counts, n_assign):
    E = counts.shape[0]
    blk = MOE_BLOCK
    nblk = (counts + blk - 1) // blk
    bend = jnp.cumsum(nblk)
    bstart = bend - nblk
    first_row = (bstart * blk).astype(I32)
    dest = rank
    for e in range(E):
        dest = dest + jnp.where(idx == e, first_row[e], 0)
    dest = dest[:, :TOP_K]
    tail = jnp.where(nblk > 0, (bend - 1) * blk, 0).astype(I32)
    cb = MOE_CHUNK_BLOCKS
    n_items_max = E + (n_assign // blk + E) // cb + 1
    per_e = (nblk + cb - 1) // cb
    iend = jnp.cumsum(per_e)
    istart = iend - per_e
    ids = jnp.arange(n_items_max, dtype=I32)
    e_of = jnp.minimum(jnp.searchsorted(iend, ids, side='right'), E - 1).astype(I32)
    valid = ids < iend[-1]
    last_e = jnp.max(jnp.where(per_e > 0, jnp.arange(E, dtype=I32), 0))
    e_of = jnp.where(valid, e_of, last_e)
    local = ids - istart[e_of]
    first_blk = bstart[e_of] + local * cb
    nb = jnp.where(valid, jnp.minimum(cb, nblk[e_of] - local * cb), 0)
    row = jnp.where(valid, first_blk * blk, 0)
    return dest.astype(I32), tail, (e_of, row.astype(I32), nb.astype(I32))


def _moe(x, w_router, b_router, w_gu, b_gu, w_down, b_down, ln_g, ln_b, *, alpha, layer, to_natural=False):
    T, D = x.shape
    E = w_router.shape[1]
    idx, gates, rank, counts = _route(x, w_router, b_router)
    n_assign = T * TOP_K
    dest, tail, items = _moe_layout(idx, rank, counts, n_assign)
    dest_flat = dest.reshape(n_assign)
    n_rows = n_assign + E * MOE_BLOCK
    xs = _dispatch(x, dest_flat, _zero_tails(tail, n_rows, D))
    ys = _moe_experts(xs, items, w_gu, b_gu, w_down, b_down, layer)
    return _combine_ln(ys, dest_flat, gates, x, ln_g, ln_b, alpha=alpha, to_natural=to_natural)


def _even_mixer_ln(x, w_in, lam_re, lam_im, log_step, b_re, b_im, c_re, c_im, d_skip, w_glu, b_glu,
                   w_gate2, b_gate2, norm_g, w_out, ln_g, ln_b, *, alpha, to_residue_major):
    T, D = x.shape
    W = d_skip.shape[0]
    qk = W // 2
    rank = w_gate2.shape[0]
    s4 = W + 2 * qk + W
    w_main = jnp.concatenate([w_in[:, :s4], w_in[:, s4 + rank:]], axis=1).astype(BF16)
    w_gate = jnp.zeros((D, LANES), BF16).at[:, :rank].set(w_in[:, s4:s4 + rank].astype(BF16))
    h, g_low = _proj_in(x, w_main, w_gate)
    tables = _s5_tables(lam_re, lam_im, log_step, b_re, b_im, c_re, c_im, T // S5_CHUNK)
    y = _s5_core(h, W, tables)
    ya = _s5_glu(y, h, d_skip.astype(F32), w_glu.astype(BF16), b_glu.astype(F32))
    yb = _gla(h, g_low, w_gate2, b_gate2, norm_g, width=W)
    w_out_b = w_out.astype(BF16)
    return _proj_ln([ya, yb], [w_out_b[:W], w_out_b[W:]], x, ln_g, ln_b, alpha=alpha,
                    to_residue_major=to_residue_major)


def _odd_mixer_ln(x, w_qkv, w_o, ln_g, ln_b, *, alpha):
    T, D = x.shape
    qkv = _proj_scaled(x, w_qkv.astype(BF16), scaled_cols=D, scale=(D // ATT_HEADS) ** -0.5)
    y = _dilated_attention(qkv, D)
    return _proj_ln([y], [w_o.astype(BF16)], x, ln_g, ln_b, alpha=alpha, to_residue_major=False)


def kernel(x, ab_w_in, s5_lam_re, s5_lam_im, s5_log_step, s5_b_re, s5_b_im, s5_c_re, s5_c_im, s5_d, s5_w_glu, s5_b_glu, gla_w_gate2, gla_b_gate2, gla_norm_g, ab_w_out, c_w_qkv, c_w_o, ln1_g, ln1_b, moe_w_router, moe_b_router, moe_w_gu, moe_b_gu, moe_w_down, moe_b_down, ln2_g, ln2_b):
    bsz, L, D = x.shape
    depth = ln1_g.shape[0]
    alpha = (2 * depth) ** 0.25
    outs = []
    for bi in range(bsz):
        xt = x[bi].astype(F32)
        for layer in range(depth):
            i = layer // 2
            odd = layer % 2 == 1
            if not odd:
                xt = _even_mixer_ln(xt, ab_w_in[i], s5_lam_re[i], s5_lam_im[i], s5_log_step[i], s5_b_re[i],
                                    s5_b_im[i], s5_c_re[i], s5_c_im[i], s5_d[i], s5_w_glu[i], s5_b_glu[i],
                                    gla_w_gate2[i], gla_b_gate2[i], gla_norm_g[i], ab_w_out[i],
                                    ln1_g[layer], ln1_b[layer], alpha=alpha, to_residue_major=layer + 1 < depth)
            else:
                xt = _odd_mixer_ln(xt, c_w_qkv[i], c_w_o[i], ln1_g[layer], ln1_b[layer], alpha=alpha)
            xt = _moe(xt, moe_w_router[layer], moe_b_router[layer], moe_w_gu, moe_b_gu,
                      moe_w_down, moe_b_down, ln2_g[layer], ln2_b[layer], alpha=alpha, layer=layer,
                      to_natural=odd)
        outs.append(xt.astype(x.dtype))
    return outs[0].reshape(1, L, D) if bsz == 1 else jnp.stack(outs)
```

```python
import functools
import math

import jax
import jax.numpy as jnp
from jax import lax
from jax.experimental import pallas as pl
from jax.experimental.pallas import tpu as pltpu

F32 = jnp.float32
BF16 = jnp.bfloat16
I32 = jnp.int32
HIGHEST = lax.Precision.HIGHEST

LANES = 128
VMEM_LIMIT_BYTES = 56 * 1024 * 1024

S5_GROUP = 16
S5_STATE = 64
S5_MAX_RE = -1e-4
S5_CHUNK = 16
GLA_HEADS = 4
GLA_GATE_TEMP = 16.0
GLA_EPS = 1e-6
GLA_BLOCK = 64
GLA_BLOCKS_PER_STEP = 2
GLA_SAFE_LOG_DECAY = 60.0
GLA_SUB = 16
ATT_HEADS = 16
DIL = 16
DILATED_GROUPS = ((128, 1), (512, 4), (2048, 16))
TOP_K = 4
SWIGLU_LIMIT = 7.0
SWIGLU_ALPHA = 1.702
MOE_BLOCK = 128
MOE_CHUNK_BLOCKS = 9
LN_EPS = 1e-5
NEG_BIG = -1e30


def _cparams(semantics):
    return pltpu.CompilerParams(dimension_semantics=semantics, vmem_limit_bytes=VMEM_LIMIT_BYTES)


def _layer_norm_rows(z, g, b):
    mu = jnp.mean(z, axis=-1, keepdims=True)
    zc = z - mu
    var = jnp.mean(zc * zc, axis=-1, keepdims=True)
    return zc * lax.rsqrt(var + LN_EPS) * g + b


def _proj_in_kernel(x_ref, w_ref, wg_ref, h_ref, g_ref, xb_ref):
    @pl.when(pl.program_id(1) == 0)
    def _():
        xb = x_ref[...].astype(BF16)
        xb_ref[...] = xb
        g_ref[...] = jnp.dot(xb, wg_ref[...], preferred_element_type=F32)

    h_ref[...] = jnp.dot(xb_ref[...], w_ref[...], preferred_element_type=F32)


def _proj_in(x, w_main, w_gate, *, tm=1024, tn=512):
    T, D = x.shape
    N = w_main.shape[1]
    return pl.pallas_call(
        _proj_in_kernel,
        name="proj_in",
        grid=(T // tm, N // tn),
        in_specs=[pl.BlockSpec((tm, D), lambda i, j: (i, 0)),
                  pl.BlockSpec((D, tn), lambda i, j: (0, j)),
                  pl.BlockSpec((D, LANES), lambda i, j: (0, 0))],
        out_specs=[pl.BlockSpec((tm, tn), lambda i, j: (i, j)),
                   pl.BlockSpec((tm, LANES), lambda i, j: (i, 0))],
        out_shape=[jax.ShapeDtypeStruct((T, N), F32), jax.ShapeDtypeStruct((T, LANES), F32)],
        scratch_shapes=[pltpu.VMEM((tm, D), BF16)],
        compiler_params=_cparams(("parallel", "arbitrary")),
    )(x, w_main, w_gate)


def _proj_scaled_kernel(x_ref, w_ref, o_ref, xb_ref, *, n_scaled, scale):
    j = pl.program_id(1)

    @pl.when(j == 0)
    def _():
        xb_ref[...] = x_ref[...].astype(BF16)

    y = jnp.dot(xb_ref[...], w_ref[...], preferred_element_type=F32)
    o_ref[...] = (y * jnp.where(j < n_scaled, scale, 1.0)).astype(o_ref.dtype)


def _proj_scaled(x, w, *, scaled_cols, scale, tm=1024, tn=1024):
    T, D = x.shape
    N = w.shape[1]
    return pl.pallas_call(
        functools.partial(_proj_scaled_kernel, n_scaled=scaled_cols // tn, scale=scale),
        name="proj_qkv",
        grid=(T // tm, N // tn),
        in_specs=[pl.BlockSpec((tm, D), lambda i, j: (i, 0)),
                  pl.BlockSpec((D, tn), lambda i, j: (0, j))],
        out_specs=pl.BlockSpec((tm, tn), lambda i, j: (i, j)),
        out_shape=jax.ShapeDtypeStruct((T, N), BF16),
        scratch_shapes=[pltpu.VMEM((tm, D), BF16)],
        compiler_params=_cparams(("parallel", "arbitrary")),
    )(x, w)


def _proj_ln_kernel(*refs, n_lhs, alpha, n_tiles, to_residue_major):
    lhs_refs = refs[:n_lhs]
    w_refs = refs[n_lhs:2 * n_lhs]
    res_ref, g_ref, b_ref, o_ref, acc_ref = refs[2 * n_lhs:2 * n_lhs + 5]
    j = pl.program_id(1)
    y = jnp.dot(lhs_refs[0][...], w_refs[0][...], preferred_element_type=F32)
    for a_ref, w_ref in zip(lhs_refs[1:], w_refs[1:]):
        y = y + jnp.dot(a_ref[...], w_ref[...], preferred_element_type=F32)
    acc_ref[j] = y

    @pl.when(j == n_tiles - 1)
    def _():
        tn = acc_ref.shape[2]
        z = [alpha * res_ref[:, t * tn:(t + 1) * tn] + acc_ref[t] for t in range(n_tiles)]
        n = float(n_tiles * tn)
        mu = sum(jnp.sum(zt, axis=-1, keepdims=True) for zt in z) / n
        zc = [zt - mu for zt in z]
        var = sum(jnp.sum(zt * zt, axis=-1, keepdims=True) for zt in zc) / n
        rstd = lax.rsqrt(var + LN_EPS)
        out = [zc[t] * rstd * g_ref[:, t * tn:(t + 1) * tn] + b_ref[:, t * tn:(t + 1) * tn] for t in range(n_tiles)]
        if not to_residue_major:
            for t in range(n_tiles):
                o_ref[:, t * tn:(t + 1) * tn] = out[t]
        else:
            rows_ref = refs[-1]
            per_res = rows_ref.shape[1] // DIL
            for c in range(rows_ref.shape[0]):
                t, off = divmod(c * LANES, tn)
                rows_ref[c] = out[t][:, off:off + LANES]
                for r in range(DIL):
                    o_ref[r, :, c * LANES:(c + 1) * LANES] = rows_ref[c, pl.ds(r, per_res, stride=DIL), :]


def _proj_ln(lhs, ws, res, g, b, *, alpha, to_residue_major, tm=512, tn=512):
    T, N = res.shape
    n_lhs = len(lhs)
    n_tiles = N // tn
    scratch = [pltpu.VMEM((n_tiles, tm, tn), F32)]
    if to_residue_major:
        na = T // DIL
        out_spec = pl.BlockSpec((DIL, tm // DIL, N), lambda i, j: (0, i, 0))
        out_shape = jax.ShapeDtypeStruct((DIL, na, N), F32)
        scratch.append(pltpu.VMEM((N // LANES, tm, LANES), F32))
    else:
        out_spec = pl.BlockSpec((tm, N), lambda i, j: (i, 0))
        out_shape = jax.ShapeDtypeStruct((T, N), F32)
    in_specs = ([pl.BlockSpec((tm, a.shape[1]), lambda i, j: (i, 0)) for a in lhs]
                + [pl.BlockSpec((w.shape[0], tn), lambda i, j: (0, j)) for w in ws]
                + [pl.BlockSpec((tm, N), lambda i, j: (i, 0)),
                   pl.BlockSpec((1, N), lambda i, j: (0, 0)),
                   pl.BlockSpec((1, N), lambda i, j: (0, 0))])
    out = pl.pallas_call(
        functools.partial(_proj_ln_kernel, n_lhs=n_lhs, alpha=alpha, n_tiles=n_tiles,
                          to_residue_major=to_residue_major),
        name="proj_ln",
        grid=(T // tm, n_tiles),
        in_specs=in_specs,
        out_specs=out_spec,
        out_shape=out_shape,
        scratch_shapes=scratch,
        compiler_params=_cparams(("parallel", "arbitrary")),
    )(*lhs, *ws, res, g.reshape(1, N), b.reshape(1, N))
    return out.reshape(T, N)


def _s5_tables(lam_re, lam_im, log_step, b_re, b_im, c_re, c_im, n_chunks):
    C = S5_CHUNK
    G, P = lam_re.shape
    H = b_re.shape[-1]
    lr = jnp.minimum(lam_re.astype(F32), S5_MAX_RE)
    li = lam_im.astype(F32)
    dt = jnp.exp(log_step.astype(F32))[:, None]
    kk = jnp.arange(C + 1, dtype=F32)[:, None, None]
    pw_mag = jnp.exp(kk * (lr * dt))
    pw_re = pw_mag * jnp.cos(kk * (li * dt))
    pw_im = pw_mag * jnp.sin(kk * (li * dt))
    a_re, a_im = pw_re[1], pw_im[1]
    den = lr * lr + li * li
    nr = a_re - 1.0
    f_re = (nr * lr + a_im * li) / den
    f_im = (a_im * lr - nr * li) / den
    br = b_re.astype(F32)
    bi = b_im.astype(F32)
    bb_re = f_re[..., None] * br - f_im[..., None] * bi
    bb_im = f_re[..., None] * bi + f_im[..., None] * br
    ab_re = pw_re[:C, :, :, None] * bb_re[None] - pw_im[:C, :, :, None] * bb_im[None]
    ab_im = pw_re[:C, :, :, None] * bb_im[None] + pw_im[:C, :, :, None] * bb_re[None]
    cr = c_re.astype(F32)
    ci = c_im.astype(F32)
    z_re = jnp.transpose(ab_re[::-1], (1, 0, 3, 2)).reshape(G, C * H, P)
    z_im = jnp.transpose(ab_im[::-1], (1, 0, 3, 2)).reshape(G, C * H, P)
    zmat = jnp.concatenate([z_re, z_im], axis=-1)
    kern = (jnp.einsum('gop,kgph->kgoh', cr, ab_re, precision=HIGHEST)
            - jnp.einsum('gop,kgph->kgoh', ci, ab_im, precision=HIGHEST))
    lag = jnp.arange(C)[None, :] - jnp.arange(C)[:, None]
    kl = kern[jnp.clip(lag, 0, C - 1)]
    kl = jnp.where((lag >= 0)[:, :, None, None, None], kl, 0.0)
    mmat = jnp.transpose(kl, (2, 0, 4, 1, 3)).reshape(G, C * H, C * H)
    ca_re = cr[None] * pw_re[1:, :, None, :] - ci[None] * pw_im[1:, :, None, :]
    ca_im = cr[None] * pw_im[1:, :, None, :] + ci[None] * pw_re[1:, :, None, :]
    n_re = jnp.transpose(ca_re, (1, 3, 0, 2)).reshape(G, P, C * H)
    n_im = jnp.transpose(-ca_im, (1, 3, 0, 2)).reshape(G, P, C * H)
    nmat = jnp.concatenate([n_re, n_im], axis=1)
    n_steps = max(1, (n_chunks - 1).bit_length())
    qr, qi = pw_re[C], pw_im[C]
    a1, a2 = [], []
    for _ in range(n_steps):
        a1.append(jnp.concatenate([qr, qr], axis=-1))
        a2.append(jnp.concatenate([-qi, qi], axis=-1))
        qr, qi = qr * qr - qi * qi, 2.0 * qr * qi
    a1 = jnp.stack(a1, axis=1)
    a2 = jnp.stack(a2, axis=1)
    return zmat.astype(BF16), mmat.astype(BF16), nmat.astype(BF16), a1, a2


def _s5_kernel(u_ref, z_ref, m_ref, n_ref, a1_ref, a2_ref, y_ref, ub_ref, ug_ref, sg_ref, mw_ref, nw_ref, *,
               n_steps):
    C, H = S5_CHUNK, S5_GROUP
    nc = u_ref.shape[0] // C
    gpt = LANES // H
    ch = C * H
    for j in range(C):
        ub_ref[:, j * LANES:(j + 1) * LANES] = u_ref[pl.ds(j, nc, stride=C), :].astype(BF16)
    row = lax.broadcasted_iota(I32, (nc, LANES), 0)
    sr = lax.broadcasted_iota(I32, (gpt * LANES, LANES), 0)
    sc = lax.broadcasted_iota(I32, (gpt * LANES, LANES), 1)
    sel_hit = sc == H * (sr // LANES) + sr % H
    sel_grp = (sr % LANES) // H
    pr = lax.broadcasted_iota(I32, (ch, C * LANES), 0)
    pc = lax.broadcasted_iota(I32, (ch, C * LANES), 1)
    put_tile = pc // LANES == pr // H
    put_lane = pc % LANES - pr % H

    def group(gl, c):
        sel = jnp.where(sel_hit & (sel_grp == gl), 1.0, 0.0).astype(BF16)
        half_w = gpt * LANES
        u = jnp.concatenate(
            [jnp.dot(ub_ref[:, t * half_w:(t + 1) * half_w], sel, preferred_element_type=F32)
             for t in range(C * LANES // half_w)], axis=1).astype(BF16)
        s = jnp.dot(u, z_ref[gl], preferred_element_type=F32)
        half = s.shape[1] // 2
        a1 = a1_ref[gl]
        a2 = a2_ref[gl]
        for k in range(n_steps):
            sh = 1 << k
            prev = jnp.where(row >= sh, pltpu.roll(s, sh, axis=0), 0.0)
            s = s + a1[k:k + 1, :] * prev + a2[k:k + 1, :] * pltpu.roll(prev, half, axis=1)
        s_in = jnp.where(row >= 1, pltpu.roll(s, 1, axis=0), 0.0)
        put = jnp.where(put_tile & (put_lane == H * gl), 1.0, 0.0).astype(BF16)
        ug_ref[gl] = u
        sg_ref[gl] = s_in.astype(BF16)
        mw_ref[gl] = jnp.dot(m_ref[gl], put, preferred_element_type=F32).astype(BF16)
        nw_ref[gl] = jnp.dot(n_ref[gl], put, preferred_element_type=F32).astype(BF16)
        return c

    lax.fori_loop(0, gpt, group, 0)
    u_all = jnp.concatenate([ug_ref[g] for g in range(gpt)], axis=1)
    s_all = jnp.concatenate([sg_ref[g] for g in range(gpt)], axis=1)
    y = (jnp.dot(u_all, mw_ref[...].reshape(gpt * ch, C * LANES), preferred_element_type=F32)
         + jnp.dot(s_all, nw_ref[...].reshape(gpt * sg_ref.shape[2], C * LANES), preferred_element_type=F32))
    for i in range(C):
        y_ref[pl.ds(i, nc, stride=C), :] = y[:, i * LANES:(i + 1) * LANES]


def _s5_core(h, width, tables):
    zmat, mmat, nmat, a1, a2 = tables
    T = h.shape[0]
    G, CH, P2 = zmat.shape
    C, H = S5_CHUNK, S5_GROUP
    nc = T // C
    n_steps = a1.shape[1]
    gpt = LANES // H
    assert CH == C * H and width == G * H
    return pl.pallas_call(
        functools.partial(_s5_kernel, n_steps=n_steps),
        name="s5_core",
        grid=(G // gpt,),
        in_specs=[pl.BlockSpec((T, LANES), lambda m: (0, m)),
                  pl.BlockSpec((gpt, CH, P2), lambda m: (m, 0, 0)),
                  pl.BlockSpec((gpt, CH, CH), lambda m: (m, 0, 0)),
                  pl.BlockSpec((gpt, P2, CH), lambda m: (m, 0, 0)),
                  pl.BlockSpec((gpt, n_steps, P2), lambda m: (m, 0, 0)),
                  pl.BlockSpec((gpt, n_steps, P2), lambda m: (m, 0, 0))],
        out_specs=pl.BlockSpec((T, LANES), lambda m: (0, m)),
        out_shape=jax.ShapeDtypeStruct((T, width), F32),
        scratch_shapes=[pltpu.VMEM((nc, C * LANES), BF16),
                        pltpu.VMEM((gpt, nc, CH), BF16),
                        pltpu.VMEM((gpt, nc, P2), BF16),
                        pltpu.VMEM((gpt, CH, C * LANES), BF16),
                        pltpu.VMEM((gpt, P2, C * LANES), BF16)],
        compiler_params=_cparams(("parallel",)),
    )(h, zmat, mmat, nmat, a1, a2)


def _s5_glu_kernel(y_ref, u_ref, d_ref, w_ref, b_ref, o_ref):
    y = y_ref[...] + d_ref[...] * u_ref[...]
    c0 = math.sqrt(2.0 / math.pi)
    z = 0.5 * y * (1.0 + jnp.tanh(c0 * (y + 0.044715 * (y * y * y))))
    lin = jnp.dot(z.astype(BF16), w_ref[...], preferred_element_type=F32) + b_ref[...]
    o_ref[...] = (z * jax.nn.sigmoid(lin)).astype(o_ref.dtype)


def _s5_glu(y, h, d_skip, w_glu, b_glu, *, tm=512):
    T, W = y.shape
    return pl.pallas_call(
        _s5_glu_kernel,
        name="s5_glu",
        grid=(T // tm,),
        in_specs=[pl.BlockSpec((tm, W), lambda i: (i, 0)),
                  pl.BlockSpec((tm, W), lambda i: (i, 0)),
                  pl.BlockSpec((1, W), lambda i: (0, 0)),
                  pl.BlockSpec((W, W), lambda i: (0, 0)),
                  pl.BlockSpec((1, W), lambda i: (0, 0))],
        out_specs=pl.BlockSpec((tm, W), lambda i: (i, 0)),
        out_shape=jax.ShapeDtypeStruct((T, W), BF16),
        compiler_params=_cparams(("parallel",)),
    )(y, h, d_skip.reshape(1, W), w_glu, b_glu.reshape(1, W))


def _gla_kernel(q_ref, k_ref, v_ref, r_ref, g_ref, w2_ref, b2_ref, ng_ref, o_ref, st_ref, *, dk, dv):
    @pl.when(pl.program_id(0) == 0)
    def _():
        st_ref[...] = jnp.zeros_like(st_ref)

    cb = GLA_BLOCK
    n_sub = cb // GLA_SUB
    scale = dk ** -0.5
    ri = lax.broadcasted_iota(I32, (cb, cb), 0)
    ci = lax.broadcasted_iota(I32, (cb, cb), 1)
    tri = (ri >= ci).astype(F32)
    n_blk = q_ref.shape[0] // cb
    rows = [slice(blk * cb, (blk + 1) * cb) for blk in range(n_blk)]
    bcums, spans = [], []
    for blk in range(n_blk):
        logit = jnp.dot(g_ref[rows[blk], :], w2_ref[...], preferred_element_type=F32, precision=HIGHEST) + b2_ref[...]
        log_a = (jnp.minimum(logit, 0.0) - jnp.log(1.0 + jnp.exp(-jnp.abs(logit)))) / GLA_GATE_TEMP
        bcum = jnp.dot(tri, log_a, preferred_element_type=F32, precision=HIGHEST)
        ends = [bcum[(a + 1) * GLA_SUB - 1:(a + 1) * GLA_SUB, :] for a in range(n_sub)]
        spans += [-ends[0]] + [ends[a - 1] - ends[a] for a in range(1, n_sub)]
        bcums.append(bcum)
    factorised_ok = jnp.max(jnp.concatenate(spans, axis=0)) < GLA_SAFE_LOG_DECAY

    def run(bcs, states, pairwise):
        for blk in range(n_blk):
            states = _gla_heads(q_ref, k_ref, v_ref, r_ref, ng_ref, o_ref, rows[blk], bcs[blk], states,
                                dk=dk, dv=dv, n_sub=n_sub, scale=scale, pairwise=pairwise)
        return states

    states = lax.cond(factorised_ok,
                      lambda bcs, st: run(bcs, st, False),
                      lambda bcs, st: run(bcs, st, True),
                      tuple(bcums), tuple(st_ref[hh] for hh in range(GLA_HEADS)))
    for hh in range(GLA_HEADS):
        st_ref[hh] = states[hh]


def _gla_heads(q_ref, k_ref, v_ref, r_ref, ng_ref, o_ref, rows, bcum, states, *, dk, dv, n_sub, scale, pairwise):
    cb = GLA_BLOCK
    new_states = []
    for hh in range(GLA_HEADS):
        ks = slice(hh * dk, (hh + 1) * dk)
        vs = slice(hh * dv, (hh + 1) * dv)
        b = bcum[:, ks]
        q = q_ref[rows, ks] * scale
        k = k_ref[rows, ks]
        v = v_ref[rows, vs].astype(BF16)
        refs = [jnp.zeros((1, dk), F32)] + [b[a * GLA_SUB - 1:a * GLA_SUB, :] for a in range(1, n_sub)]
        refmat = jnp.concatenate([jnp.broadcast_to(r, (GLA_SUB, dk)) for r in refs], axis=0)
        qe = (q * jnp.exp(b - refmat)).astype(BF16)
        st = states[hh]
        o_inter = lax.dot_general((q * jnp.exp(b)).astype(BF16), st.astype(BF16),
                                  (((1,), (1,)), ((), ())), preferred_element_type=F32)
        o_rows = []
        for a in range(n_sub):
            lo, hi = a * GLA_SUB, (a + 1) * GLA_SUB
            row_i = lax.broadcasted_iota(I32, (GLA_SUB, hi), 0)
            col_j = lax.broadcasted_iota(I32, (GLA_SUB, hi), 1)
            if not pairwise:
                ke = (k[:hi] * jnp.exp(refs[a] - b[:hi])).astype(BF16)
                att = lax.dot_general(qe[lo:hi], ke, (((1,), (1,)), ((), ())), preferred_element_type=F32)
                att = jnp.where(col_j <= row_i + lo, att, 0.0)
            else:
                key_row = lax.broadcasted_iota(I32, (hi, dk), 0)
                ke = jnp.where(key_row < lo, k[:hi] * jnp.exp(jnp.minimum(refs[a] - b[:hi], 0.0)), 0.0)
                att = lax.dot_general(qe[lo:hi], ke.astype(BF16), (((1,), (1,)), ((), ())),
                                      preferred_element_type=F32)
                b_a, q_a, k_a = b[lo:hi], q[lo:hi], k[lo:hi]
                terms = [q_a * jnp.exp(jnp.minimum(b_a - b_a[j:j + 1, :], 0.0)) * k_a[j:j + 1, :]
                         for j in range(GLA_SUB)]
                sums = jnp.dot(jnp.concatenate(terms, axis=0).astype(BF16), jnp.ones((dk, hi), BF16),
                               preferred_element_type=F32)
                for j in range(GLA_SUB):
                    att = jnp.where((col_j == lo + j) & (row_i >= j), sums[j * GLA_SUB:(j + 1) * GLA_SUB], att)
            o_rows.append(jnp.dot(att.astype(BF16), v[:hi], preferred_element_type=F32))
        o = jnp.concatenate(o_rows, axis=0) + o_inter
        o = o * lax.rsqrt(jnp.mean(o * o, axis=-1, keepdims=True) + GLA_EPS) * ng_ref[...]
        r = r_ref[rows, vs]
        o_ref[rows, vs] = (o * (r * jax.nn.sigmoid(r))).astype(o_ref.dtype)
        b_last = b[cb - 1:cb, :]
        kd = (k * jnp.exp(b_last - b)).astype(BF16)
        upd = lax.dot_general(v, kd, (((0,), (0,)), ((), ())), preferred_element_type=F32)
        new_states.append(st * jnp.exp(b_last) + upd)
    return tuple(new_states)


def _gla(h, g_low, w_gate2, b_gate2, norm_g, *, width):
    T = h.shape[0]
    qk = width // 2
    dk = qk // GLA_HEADS
    dv = width // GLA_HEADS
    cb = GLA_BLOCK * GLA_BLOCKS_PER_STEP
    w2 = jnp.zeros((LANES, qk), F32).at[:w_gate2.shape[0]].set(w_gate2.astype(F32))
    return pl.pallas_call(
        functools.partial(_gla_kernel, dk=dk, dv=dv),
        name="gla",
        grid=(T // cb,),
        in_specs=[pl.BlockSpec((cb, qk), lambda i: (i, 2)),
                  pl.BlockSpec((cb, qk), lambda i: (i, 3)),
                  pl.BlockSpec((cb, width), lambda i: (i, 2)),
                  pl.BlockSpec((cb, width), lambda i: (i, 3)),
                  pl.BlockSpec((cb, LANES), lambda i: (i, 0)),
                  pl.BlockSpec((LANES, qk), lambda i: (0, 0)),
                  pl.BlockSpec((1, qk), lambda i: (0, 0)),
                  pl.BlockSpec((1, dv), lambda i: (0, 0))],
        out_specs=pl.BlockSpec((cb, width), lambda i: (i, 0)),
        out_shape=jax.ShapeDtypeStruct((T, width), BF16),
        scratch_shapes=[pltpu.VMEM((GLA_HEADS, dv, dk), F32)],
        compiler_params=_cparams(("arbitrary",)),
    )(h, h, h, h, g_low, w2, b_gate2.reshape(1, qk).astype(F32), norm_g.reshape(1, dv).astype(F32))


def _attn_kernel(q_ref, k_ref, v_ref, o_ref, acc_ref, m_ref, l_ref, v1_ref, pat1_ref, pat4_ref, pat16_ref, *, na):
    dh = v_ref.shape[1]
    v1_ref[:, :dh] = v_ref[...]
    v1_ref[:, dh:] = jnp.ones_like(v_ref)
    acc_ref[...] = jnp.zeros_like(acc_ref)
    m_ref[...] = jnp.full_like(m_ref, NEG_BIG)
    l_ref[...] = jnp.zeros_like(l_ref)

    def token_offsets(rows, r_step, n, axis):
        shape = (n, 1) if axis == 0 else (1, n)
        idx = lax.broadcasted_iota(I32, shape, axis)
        c = idx // rows
        return DIL * (idx - c * rows) + r_step * c

    def delta_pattern(n_chunks, q_rows, k_rows, r_step):
        return (token_offsets(q_rows, r_step, n_chunks * q_rows, 0)
                - token_offsets(k_rows, r_step, n_chunks * k_rows, 1))

    def cat(ref, starts, rows):
        return jnp.concatenate([ref[pl.ds(pl.multiple_of(s, 16), rows), :] for s in starts], axis=0)

    def window_bias(n_chunks, q_rows, k_rows, r_step, back, window):
        pat = delta_pattern(n_chunks, q_rows, k_rows, r_step)
        inside = lambda off: jnp.where((pat + off >= 0) & (pat + off <= window), 0.0, NEG_BIG)
        return jnp.stack([inside(0), inside(DIL * back)])

    def attend(blocks, q_rows, k_rows, bias_ref):
        loaded = []
        for q_starts, k_starts, a_q, a_k in blocks:
            loaded.append((cat(q_ref, q_starts, q_rows), cat(k_ref, k_starts, k_rows), cat(v1_ref, k_starts, k_rows),
                           cat(m_ref, q_starts, q_rows), cat(l_ref, q_starts, q_rows),
                           cat(acc_ref, q_starts, q_rows)))
        results = []
        for (q_starts, k_starts, a_q, a_k), (qb, kb, vb, m_old, l_old, acc_old) in zip(blocks, loaded):
            s = lax.dot_general(qb, kb, (((1,), (1,)), ((), ())), preferred_element_type=F32)
            s = s + bias_ref[jnp.where(a_q == a_k, 0, 1)]
            m_new = jnp.maximum(m_old, jnp.max(s, axis=1, keepdims=True))
            alpha = jnp.exp(m_old - m_new)
            p = jnp.exp(s - m_new[:, :1])
            pv = jnp.dot(p.astype(BF16), vb, preferred_element_type=F32)
            l_new = alpha * l_old + pv[:, dh:]
            acc_new = alpha * acc_old + pv[:, :dh]
            results.append((m_new, l_new, acc_new))
        for (q_starts, *_), (m_new, l_new, acc_new) in zip(blocks, results):
            for c, st in enumerate(q_starts):
                st = pl.multiple_of(st, 16)
                rs = slice(c * q_rows, (c + 1) * q_rows)
                m_ref[pl.ds(st, q_rows), :] = m_new[rs]
                l_ref[pl.ds(st, q_rows), :] = l_new[rs]
                acc_ref[pl.ds(st, q_rows), :] = acc_new[rs]

    w1 = DILATED_GROUPS[0][0]
    u1 = 2
    pat1_ref[...] = window_bias(DIL, 16, 32, 1, 16, w1)

    def body1(it, carry):
        blocks = []
        for j in range(u1):
            a0 = (it * u1 + j) * 16
            ak = jnp.maximum(a0 - 16, 0)
            blocks.append(([r * na + a0 for r in range(DIL)], [r * na + ak for r in range(DIL)], a0, ak))
        attend(blocks, 16, 32, pat1_ref)
        return carry

    lax.fori_loop(0, na // (16 * u1), body1, 0)

    w4 = DILATED_GROUPS[1][0]
    pat4_ref[...] = window_bias(4, 32, 64, 4, 32, w4)

    def body4(it, carry):
        a0 = it * 32
        ak = jnp.maximum(a0 - 32, 0)
        blocks = [([(rho + 4 * sg) * na + a0 for sg in range(4)], [(rho + 4 * sg) * na + ak for sg in range(4)],
                   a0, ak) for rho in range(4)]
        attend(blocks, 32, 64, pat4_ref)
        return carry

    lax.fori_loop(0, na // 32, body4, 0)

    w16 = DILATED_GROUPS[2][0]
    u16 = 4
    pat16_ref[...] = window_bias(1, 128, 256, 0, 128, w16)

    def body16(it, carry):
        rg = it // (na // 128)
        a0 = (it - rg * (na // 128)) * 128
        ak = jnp.maximum(a0 - 128, 0)
        blocks = [([(rg * u16 + j) * na + a0], [(rg * u16 + j) * na + ak], a0, ak) for j in range(u16)]
        attend(blocks, 128, 256, pat16_ref)
        return carry

    lax.fori_loop(0, (DIL // u16) * (na // 128), body16, 0)

    o_ref[...] = (acc_ref[...] / l_ref[...]).astype(o_ref.dtype)


def _dilated_attention(qkv, d_model):
    T = qkv.shape[0]
    dh = d_model // ATT_HEADS
    assert dh == LANES
    na = T // DIL
    return pl.pallas_call(
        functools.partial(_attn_kernel, na=na),
        name="dilated_attn",
        grid=(ATT_HEADS,),
        in_specs=[pl.BlockSpec((T, dh), lambda h: (0, h)),
                  pl.BlockSpec((T, dh), lambda h: (0, ATT_HEADS + h)),
                  pl.BlockSpec((T, dh), lambda h: (0, 2 * ATT_HEADS + h))],
        out_specs=pl.BlockSpec((T, dh), lambda h: (0, h)),
        out_shape=jax.ShapeDtypeStruct((T, d_model), BF16),
        scratch_shapes=[pltpu.VMEM((T, dh), F32), pltpu.VMEM((T, dh), F32), pltpu.VMEM((T, dh), F32),
                        pltpu.VMEM((T, 2 * dh), BF16), pltpu.VMEM((2, 256, 512), F32),
                        pltpu.VMEM((2, 128, 256), F32), pltpu.VMEM((2, 128, 256), F32)],
        compiler_params=_cparams(("parallel",)),
    )(qkv, qkv, qkv)


def _route_kernel(x_ref, w_ref, b_ref, idx_ref, gate_ref, rank_ref, cnt_ref, carry_ref):
    @pl.when(pl.program_id(0) == 0)
    def _():
        carry_ref[...] = jnp.zeros_like(carry_ref)

    tb = x_ref.shape[0]
    x = x_ref[...]
    w = w_ref[...]
    x_hi, w_hi = x.astype(BF16), w.astype(BF16)
    x_lo = (x - x_hi.astype(F32)).astype(BF16)
    w_lo = (w - w_hi.astype(F32)).astype(BF16)
    lg = (jnp.dot(x_hi, w_hi, preferred_element_type=F32) + jnp.dot(x_hi, w_lo, preferred_element_type=F32)
          + jnp.dot(x_lo, w_hi, preferred_element_type=F32)) + b_ref[...]
    lane = lax.broadcasted_iota(I32, lg.shape, 1)
    vals, hots = [], []
    idx_out = jnp.zeros(lg.shape, I32)
    for k in range(TOP_K):
        m = jnp.max(lg, axis=1, keepdims=True)
        sel = jnp.min(jnp.where(lg == m, lane, LANES), axis=1, keepdims=True)
        hot = lane == sel
        vals.append(m)
        hots.append(hot)
        idx_out = jnp.where(lane == k, sel, idx_out)
        lg = jnp.where(hot, -jnp.inf, lg)
    ex = [jnp.exp(v - vals[0]) for v in vals]
    den = sum(ex)
    gate_out = jnp.zeros(lg.shape, F32)
    for k in range(TOP_K):
        gate_out = jnp.where(lane == k, ex[k] / den, gate_out)
    chosen = sum(h.astype(F32) for h in hots)
    ri = lax.broadcasted_iota(I32, (tb, tb), 0)
    ci = lax.broadcasted_iota(I32, (tb, tb), 1)
    before = jnp.dot((ri > ci).astype(BF16), chosen.astype(BF16), preferred_element_type=F32) + carry_ref[...]
    rank_out = jnp.zeros(lg.shape, I32)
    for k in range(TOP_K):
        rk = jnp.sum(jnp.where(hots[k], before, 0.0), axis=1, keepdims=True).astype(I32)
        rank_out = jnp.where(lane == k, rk, rank_out)
    idx_ref[...] = idx_out
    gate_ref[...] = gate_out
    rank_ref[...] = rank_out
    carry_ref[...] = carry_ref[...] + jnp.sum(chosen, axis=0, keepdims=True)
    cnt_ref[...] = carry_ref[...].astype(I32)


def _route(x, w_router, b_router, *, tb=512):
    T, D = x.shape
    E = w_router.shape[1]
    wr = jnp.zeros((D, LANES), F32).at[:, :E].set(w_router.astype(F32))
    br = jnp.full((1, LANES), NEG_BIG, F32).at[0, :E].set(b_router.astype(F32))
    row = lambda dt: jax.ShapeDtypeStruct((T, LANES), dt)
    idx, gate, rank, cnt = pl.pallas_call(
        _route_kernel,
        name="moe_route",
        grid=(T // tb,),
        in_specs=[pl.BlockSpec((tb, D), lambda i: (i, 0)),
                  pl.BlockSpec((D, LANES), lambda i: (0, 0)),
                  pl.BlockSpec((1, LANES), lambda i: (0, 0))],
        out_specs=[pl.BlockSpec((tb, LANES), lambda i: (i, 0)),
                   pl.BlockSpec((tb, LANES), lambda i: (i, 0)),
                   pl.BlockSpec((tb, LANES), lambda i: (i, 0)),
                   pl.BlockSpec((1, LANES), lambda i: (0, 0))],
        out_shape=[row(I32), row(F32), row(I32), jax.ShapeDtypeStruct((1, LANES), I32)],
        scratch_shapes=[pltpu.VMEM((1, LANES), F32)],
        compiler_params=_cparams(("arbitrary",)),
    )(x, wr, br)
    return idx, gate, rank, cnt[0, :E]


def _zero_tail_kernel(row_ref, o_ref, z_ref, sem):
    z_ref[...] = jnp.zeros_like(z_ref)
    n = row_ref.shape[0]

    def copy(e):
        return pltpu.make_async_copy(z_ref, o_ref.at[pl.ds(pl.multiple_of(row_ref[e], MOE_BLOCK), MOE_BLOCK)], sem)

    def start(e, c):
        copy(e).start()
        return c

    def wait(e, c):
        copy(e).wait()
        return c

    lax.fori_loop(0, n, start, 0)
    lax.fori_loop(0, n, wait, 0)


def _zero_tails(tail_rows, n_rows, d):
    return pl.pallas_call(
        _zero_tail_kernel,
        name="moe_zero_tails",
        grid_spec=pltpu.PrefetchScalarGridSpec(
            num_scalar_prefetch=1, grid=(1,),
            in_specs=[],
            out_specs=pl.BlockSpec(memory_space=pl.ANY),
            scratch_shapes=[pltpu.VMEM((MOE_BLOCK, d), F32), pltpu.SemaphoreType.DMA(())]),
        out_shape=jax.ShapeDtypeStruct((n_rows, d), F32),
        compiler_params=_cparams(("arbitrary",)),
    )(tail_rows)


def _dispatch_kernel(dest_ref, x_ref, xs_in_ref, xs_ref, sem):
    del xs_in_ref
    tb = x_ref.shape[0]
    base = pl.program_id(0) * tb * TOP_K

    def copy(i, k):
        return pltpu.make_async_copy(x_ref.at[pl.ds(i, 1)],
                                     xs_ref.at[pl.ds(dest_ref[base + i * TOP_K + k], 1)], sem)

    def start(i, c):
        for k in range(TOP_K):
            copy(i, k).start()
        return c

    lax.fori_loop(0, tb, start, 0, unroll=4)
    for _ in range(TOP_K):
        pltpu.make_async_copy(x_ref, xs_ref.at[pl.ds(0, tb)], sem).wait()


def _dispatch(x, dest_flat, xs_init, *, tb=256):
    T, D = x.shape
    return pl.pallas_call(
        _dispatch_kernel,
        name="moe_dispatch",
        grid_spec=pltpu.PrefetchScalarGridSpec(
            num_scalar_prefetch=1, grid=(T // tb,),
            in_specs=[pl.BlockSpec((tb, D), lambda i, dest: (i, 0)),
                      pl.BlockSpec(memory_space=pl.ANY)],
            out_specs=pl.BlockSpec(memory_space=pl.ANY),
            scratch_shapes=[pltpu.SemaphoreType.DMA(())]),
        out_shape=jax.ShapeDtypeStruct(xs_init.shape, xs_init.dtype),
        input_output_aliases={2: 0},
        compiler_params=_cparams(("arbitrary",)),
    )(dest_flat, x, xs_init)


def _moe_kernel(item_e_ref, item_row_ref, item_nb_ref, xs_ref, wg_ref, wu_ref, bg_ref, bu_ref, wd_ref, bd_ref,
                ys_ref, xbuf_ref, land_ref, acc_ref, in_sem, out_sem, *, n_f):
    it = pl.program_id(0)
    f = pl.program_id(1)
    nb = item_nb_ref[it]
    row0 = item_row_ref[it]
    blk = MOE_BLOCK

    def acc_rows(j):
        return pl.ds(pl.multiple_of(j * blk, blk), blk)

    def in_copy(item, j):
        src = xs_ref.at[pl.ds(pl.multiple_of(item_row_ref[item] + j * blk, blk), blk)]
        return pltpu.make_async_copy(src, land_ref.at[acc_rows(j)], in_sem)

    def out_copy(j):
        dst = ys_ref.at[pl.ds(pl.multiple_of(row0 + j * blk, blk), blk)]
        return pltpu.make_async_copy(acc_ref.at[acc_rows(j)], dst, out_sem)

    def for_blocks(fn, n=nb):
        def body(j, c):
            fn(j)
            return c

        lax.fori_loop(0, n, body, 0)

    @pl.when((f == 0) & (nb > 0))
    def _():
        @pl.when(it == 0)
        def _():
            for_blocks(lambda j: in_copy(it, j).start())

        for_blocks(lambda j: in_copy(it, j).wait())

        def to_bf16(j):
            xbuf_ref[acc_rows(j), :] = land_ref[acc_rows(j), :].astype(BF16)
            acc_ref[acc_rows(j), :] = jnp.broadcast_to(bd_ref[...], (blk, acc_ref.shape[1]))

        for_blocks(to_bf16)

    @pl.when((f == 1) & (it + 1 < pl.num_programs(0)))
    def _():
        nxt = it + 1
        for_blocks(lambda j: in_copy(nxt, j).start(), item_nb_ref[nxt])

    @pl.when(nb > 0)
    def _():
        def compute(r0, n_rows, wg, wu, wd):
            rows = pl.ds(pl.multiple_of(r0, blk), n_rows)
            xb = xbuf_ref[rows, :]
            hg = jnp.dot(xb, wg, preferred_element_type=F32) + bg_ref[...]
            hu = jnp.dot(xb, wu, preferred_element_type=F32) + bu_ref[...]
            gate = jnp.minimum(hg, SWIGLU_LIMIT)
            up = jnp.clip(hu, -SWIGLU_LIMIT, SWIGLU_LIMIT)
            act = (up + 1.0) * (gate * jax.nn.sigmoid(SWIGLU_ALPHA * gate))
            acc_ref[rows, :] += jnp.dot(act.astype(BF16), wd, preferred_element_type=F32)

            @pl.when(f == n_f - 1)
            def _():
                for b in range(n_rows // blk):
                    out_copy(r0 // blk + b).start()

        def piece(r0, n_rows):
            compute(r0, n_rows, wg_ref[...].astype(BF16), wu_ref[...].astype(BF16), wd_ref[...].astype(BF16))

        n4 = nb // 4

        def quad(j, c):
            piece(j * (4 * blk), 4 * blk)
            return c

        lax.fori_loop(0, n4, quad, 0)

        @pl.when((nb & 2) != 0)
        def _():
            piece(n4 * (4 * blk), 2 * blk)

        @pl.when((nb & 1) != 0)
        def _():
            piece((nb - 1) * blk, blk)

    @pl.when((f == n_f - 1) & (nb > 0))
    def _():
        for_blocks(lambda j: out_copy(j).wait())


def _moe_experts(xs, items, w_gu, b_gu, w_down, b_down, layer, *, tf=512):
    item_e, item_row, item_nb = items
    n_items = item_e.shape[0]
    P, D = xs.shape
    L, E, _, F2 = w_gu.shape
    F = F2 // 2
    n_f = F // tf
    assert n_f >= 2
    rc = MOE_CHUNK_BLOCKS * MOE_BLOCK

    def fe(f, nb_ref, it):
        return jnp.where(nb_ref[it] > 0, f, n_f - 1)

    in_specs = [
        pl.BlockSpec(memory_space=pl.ANY),
        pl.BlockSpec((None, None, D, tf), lambda it, f, e, r, nb: (layer, e[it], 0, fe(f, nb, it))),
        pl.BlockSpec((None, None, D, tf), lambda it, f, e, r, nb: (layer, e[it], 0, n_f + fe(f, nb, it))),
        pl.BlockSpec((None, None, 1, tf), lambda it, f, e, r, nb: (layer, e[it], 0, fe(f, nb, it))),
        pl.BlockSpec((None, None, 1, tf), lambda it, f, e, r, nb: (layer, e[it], 0, n_f + fe(f, nb, it))),
        pl.BlockSpec((None, None, tf, D), lambda it, f, e, r, nb: (layer, e[it], fe(f, nb, it), 0)),
        pl.BlockSpec((None, None, 1, D), lambda it, f, e, r, nb: (layer, e[it], 0, 0)),
    ]
    return pl.pallas_call(
        functools.partial(_moe_kernel, n_f=n_f),
        name="moe_experts",
        grid_spec=pltpu.PrefetchScalarGridSpec(
            num_scalar_prefetch=3, grid=(n_items, n_f),
            in_specs=in_specs,
            out_specs=pl.BlockSpec(memory_space=pl.ANY),
            scratch_shapes=[pltpu.VMEM((rc, D), BF16),
                            pltpu.VMEM((rc, D), F32),
                            pltpu.VMEM((rc, D), F32),
                            pltpu.SemaphoreType.DMA(()),
                            pltpu.SemaphoreType.DMA(())]),
        out_shape=jax.ShapeDtypeStruct((P, D), F32),
        compiler_params=_cparams(("arbitrary", "arbitrary")),
    )(item_e, item_row, item_nb, xs, w_gu, w_gu, b_gu.reshape(L, E, 1, F2), b_gu.reshape(L, E, 1, F2),
      w_down, b_down.reshape(L, E, 1, D))


def _combine_kernel(dest_ref, ys_ref, gate_ref, x_ref, g_ref, b_ref, o_ref, buf_ref, sem, *scratch,
                    alpha, to_natural):
    tb = o_ref.shape[0]
    step = pl.program_id(0)
    slot = step % 2
    per_res = tb // DIL

    def token(blk, i):
        if not to_natural:
            return blk * tb + i
        n_a = pl.num_programs(0) * per_res
        return (i % DIL) * n_a + blk * per_res + i // DIL

    def copy(blk, sl, i, k):
        src = ys_ref.at[pl.ds(dest_ref[token(blk, i) * TOP_K + k], 1)]
        return pltpu.make_async_copy(src, buf_ref.at[sl, k, pl.ds(i, 1)], sem.at[sl])

    def for_rows(fn):
        def body(i, c):
            for k in range(TOP_K):
                fn(i, k)
            return c

        lax.fori_loop(0, tb, body, 0, unroll=4)

    @pl.when(step == 0)
    def _():
        for_rows(lambda i, k: copy(0, 0, i, k).start())

    @pl.when(step + 1 < pl.num_programs(0))
    def _():
        for_rows(lambda i, k: copy(step + 1, 1 - slot, i, k).start())

    for k in range(TOP_K):
        pltpu.make_async_copy(ys_ref.at[pl.ds(0, tb)], buf_ref.at[slot, k], sem.at[slot]).wait()
    if to_natural:
        xn_ref, gn_ref = scratch
        for r in range(DIL):
            gn_ref[pl.ds(r, per_res, stride=DIL), :] = gate_ref[r]
            for c in range(xn_ref.shape[0]):
                xn_ref[c, pl.ds(r, per_res, stride=DIL), :] = x_ref[r, :, c * LANES:(c + 1) * LANES]
        x = jnp.concatenate([xn_ref[c] for c in range(xn_ref.shape[0])], axis=1)
        gate = gn_ref[...]
    else:
        x = x_ref[...]
        gate = gate_ref[...]
    ffn = gate[:, 0:1] * buf_ref[slot, 0]
    for k in range(1, TOP_K):
        ffn = ffn + gate[:, k:k + 1] * buf_ref[slot, k]
    o_ref[...] = _layer_norm_rows(alpha * x + ffn, g_ref[...], b_ref[...])


def _combine_ln(ys, dest_flat, gates, x, g, b, *, alpha, to_natural, tb=128):
    T, D = x.shape
    gate_pad = gates
    scratch = [pltpu.VMEM((2, TOP_K, tb, D), F32), pltpu.SemaphoreType.DMA((2,))]
    if to_natural:
        na = T // DIL
        per_res = tb // DIL
        gate_in = gate_pad.reshape(DIL, na, LANES)
        x_in = x.reshape(DIL, na, D)
        gate_spec = pl.BlockSpec((DIL, per_res, LANES), lambda i, dest: (0, i, 0))
        x_spec = pl.BlockSpec((DIL, per_res, D), lambda i, dest: (0, i, 0))
        scratch += [pltpu.VMEM((D // LANES, tb, LANES), F32), pltpu.VMEM((tb, LANES), F32)]
    else:
        gate_in, x_in = gate_pad, x
        gate_spec = pl.BlockSpec((tb, LANES), lambda i, dest: (i, 0))
        x_spec = pl.BlockSpec((tb, D), lambda i, dest: (i, 0))
    return pl.pallas_call(
        functools.partial(_combine_kernel, alpha=alpha, to_natural=to_natural),
        name="moe_combine",
        grid_spec=pltpu.PrefetchScalarGridSpec(
            num_scalar_prefetch=1, grid=(T // tb,),
            in_specs=[pl.BlockSpec(memory_space=pl.ANY),
                      gate_spec,
                      x_spec,
                      pl.BlockSpec((1, D), lambda i, dest: (0, 0)),
                      pl.BlockSpec((1, D), lambda i, dest: (0, 0))],
            out_specs=pl.BlockSpec((tb, D), lambda i, dest: (i, 0)),
            scratch_shapes=scratch),
        out_shape=jax.ShapeDtypeStruct((T, D), F32),
        compiler_params=_cparams(("arbitrary",)),
    )(dest_flat, ys, gate_in, x_in, g.reshape(1, D), b.reshape(1, D))


def _moe_layout(idx, rank, counts, n_assign):
    E = counts.shape[0]
    blk = MOE_BLOCK
    nblk = (counts + blk - 1) // blk
    bend = jnp.cumsum(nblk)
    bstart = bend - nblk
    first_row = (bstart * blk).astype(I32)
    dest = rank
    for e in range(E):
        dest = dest + jnp.where(idx == e, first_row[e], 0)
    dest = dest[:, :TOP_K]
    tail = jnp.where(nblk > 0, (bend - 1) * blk, 0).astype(I32)
    cb = MOE_CHUNK_BLOCKS
    n_items_max = E + (n_assign // blk + E) // cb + 1
    per_e = (nblk + cb - 1) // cb
    iend = jnp.cumsum(per_e)
    istart = iend - per_e
    ids = jnp.arange(n_items_max, dtype=I32)
    e_of = jnp.minimum(jnp.searchsorted(iend, ids, side='right'), E - 1).astype(I32)
    valid = ids < iend[-1]
    last_e = jnp.max(jnp.where(per_e > 0, jnp.arange(E, dtype=I32), 0))
    e_of = jnp.where(valid, e_of, last_e)
    local = ids - istart[e_of]
    first_blk = bstart[e_of] + local * cb
    nb = jnp.where(valid, jnp.minimum(cb, nblk[e_of] - local * cb), 0)
    row = jnp.where(valid, first_blk * blk, 0)
    return dest.astype(I32), tail, (e_of, row.astype(I32), nb.astype(I32))


def _moe(x, w_router, b_router, w_gu, b_gu, w_down, b_down, ln_g, ln_b, *, alpha, layer, to_natural=False):
    T, D = x.shape
    E = w_router.shape[1]
    idx, gates, rank, counts = _route(x, w_router, b_router)
    n_assign = T * TOP_K
    dest, tail, items = _moe_layout(idx, rank, counts, n_assign)
    dest_flat = dest.reshape(n_assign)
    n_rows = n_assign + E * MOE_BLOCK
    xs = _dispatch(x, dest_flat, _zero_tails(tail, n_rows, D))
    ys = _moe_experts(xs, items, w_gu, b_gu, w_down, b_down, layer)
    return _combine_ln(ys, dest_flat, gates, x, ln_g, ln_b, alpha=alpha, to_natural=to_natural)


def _even_mixer_ln(x, w_in, lam_re, lam_im, log_step, b_re, b_im, c_re, c_im, d_skip, w_glu, b_glu,
                   w_gate2, b_gate2, norm_g, w_out, ln_g, ln_b, *, alpha, to_residue_major):
    T, D = x.shape
    W = d_skip.shape[0]
    qk = W // 2
    rank = w_gate2.shape[0]
    s4 = W + 2 * qk + W
    w_main = jnp.concatenate([w_in[:, :s4], w_in[:, s4 + rank:]], axis=1).astype(BF16)
    w_gate = jnp.zeros((D, LANES), BF16).at[:, :rank].set(w_in[:, s4:s4 + rank].astype(BF16))
    h, g_low = _proj_in(x, w_main, w_gate)
    tables = _s5_tables(lam_re, lam_im, log_step, b_re, b_im, c_re, c_im, T // S5_CHUNK)
    y = _s5_core(h, W, tables)
    ya = _s5_glu(y, h, d_skip.astype(F32), w_glu.astype(BF16), b_glu.astype(F32))
    yb = _gla(h, g_low, w_gate2, b_gate2, norm_g, width=W)
    w_out_b = w_out.astype(BF16)
    return _proj_ln([ya, yb], [w_out_b[:W], w_out_b[W:]], x, ln_g, ln_b, alpha=alpha,
                    to_residue_major=to_residue_major)


def _odd_mixer_ln(x, w_qkv, w_o, ln_g, ln_b, *, alpha):
    T, D = x.shape
    qkv = _proj_scaled(x, w_qkv.astype(BF16), scaled_cols=D, scale=(D // ATT_HEADS) ** -0.5)
    y = _dilated_attention(qkv, D)
    return _proj_ln([y], [w_o.astype(BF16)], x, ln_g, ln_b, alpha=alpha, to_residue_major=False)


def kernel(x, ab_w_in, s5_lam_re, s5_lam_im, s5_log_step, s5_b_re, s5_b_im, s5_c_re, s5_c_im, s5_d, s5_w_glu, s5_b_glu, gla_w_gate2, gla_b_gate2, gla_norm_g, ab_w_out, c_w_qkv, c_w_o, ln1_g, ln1_b, moe_w_router, moe_b_router, moe_w_gu, moe_b_gu, moe_w_down, moe_b_down, ln2_g, ln2_b):
    bsz, L, D = x.shape
    depth = ln1_g.shape[0]
    alpha = (2 * depth) ** 0.25
    outs = []
    for bi in range(bsz):
        xt = x[bi].astype(F32)
        for layer in range(depth):
            i = layer // 2
            odd = layer % 2 == 1
            if not odd:
                xt = _even_mixer_ln(xt, ab_w_in[i], s5_lam_re[i], s5_lam_im[i], s5_log_step[i], s5_b_re[i],
                                    s5_b_im[i], s5_c_re[i], s5_c_im[i], s5_d[i], s5_w_glu[i], s5_b_glu[i],
                                    gla_w_gate2[i], gla_b_gate2[i], gla_norm_g[i], ab_w_out[i],
                                    ln1_g[layer], ln1_b[layer], alpha=alpha, to_residue_major=layer + 1 < depth)
            else:
                xt = _odd_mixer_ln(xt, c_w_qkv[i], c_w_o[i], ln1_g[layer], ln1_b[layer], alpha=alpha)
            xt = _moe(xt, moe_w_router[layer], moe_b_router[layer], moe_w_gu, moe_b_gu,
                      moe_w_down, moe_b_down, ln2_g[layer], ln2_b[layer], alpha=alpha, layer=layer,
                      to_natural=odd)
        outs.append(xt.astype(x.dtype))
    return outs[0].reshape(1, L, D) if bsz == 1 else jnp.stack(outs)
```

```python
import functools
import math

import jax
import jax.numpy as jnp
from jax import lax
from jax.experimental import pallas as pl
from jax.experimental.pallas import tpu as pltpu

F32 = jnp.float32
BF16 = jnp.bfloat16
I32 = jnp.int32
HIGHEST = lax.Precision.HIGHEST

LANES = 128
VMEM_LIMIT_BYTES = 56 * 1024 * 1024

S5_GROUP = 16
S5_STATE = 64
S5_MAX_RE = -1e-4
S5_CHUNK = 16
GLA_HEADS = 4
GLA_GATE_TEMP = 16.0
GLA_EPS = 1e-6
GLA_BLOCK = 64
GLA_BLOCKS_PER_STEP = 2
GLA_SAFE_LOG_DECAY = 60.0
GLA_SUB = 16
ATT_HEADS = 16
DIL = 16
DILATED_GROUPS = ((128, 1), (512, 4), (2048, 16))
TOP_K = 4
SWIGLU_LIMIT = 7.0
SWIGLU_ALPHA = 1.702
MOE_BLOCK = 128
MOE_CHUNK_BLOCKS = 9
LN_EPS = 1e-5
NEG_BIG = -1e30


def _cparams(semantics):
    return pltpu.CompilerParams(dimension_semantics=semantics, vmem_limit_bytes=VMEM_LIMIT_BYTES)


def _layer_norm_rows(z, g, b):
    mu = jnp.mean(z, axis=-1, keepdims=True)
    zc = z - mu
    var = jnp.mean(zc * zc, axis=-1, keepdims=True)
    return zc * lax.rsqrt(var + LN_EPS) * g + b


def _proj_in_kernel(x_ref, w_ref, wg_ref, h_ref, g_ref, xb_ref):
    @pl.when(pl.program_id(1) == 0)
    def _():
        xb = x_ref[...].astype(BF16)
        xb_ref[...] = xb
        g_ref[...] = jnp.dot(xb, wg_ref[...], preferred_element_type=F32)

    h_ref[...] = jnp.dot(xb_ref[...], w_ref[...], preferred_element_type=F32)


def _proj_in(x, w_main, w_gate, *, tm=1024, tn=1024):
    T, D = x.shape
    N = w_main.shape[1]
    return pl.pallas_call(
        _proj_in_kernel,
        name="proj_in",
        grid=(T // tm, N // tn),
        in_specs=[pl.BlockSpec((tm, D), lambda i, j: (i, 0)),
                  pl.BlockSpec((D, tn), lambda i, j: (0, j)),
                  pl.BlockSpec((D, LANES), lambda i, j: (0, 0))],
        out_specs=[pl.BlockSpec((tm, tn), lambda i, j: (i, j)),
                   pl.BlockSpec((tm, LANES), lambda i, j: (i, 0))],
        out_shape=[jax.ShapeDtypeStruct((T, N), F32), jax.ShapeDtypeStruct((T, LANES), F32)],
        scratch_shapes=[pltpu.VMEM((tm, D), BF16)],
        compiler_params=_cparams(("parallel", "arbitrary")),
    )(x, w_main, w_gate)


def _proj_scaled_kernel(x_ref, w_ref, o_ref, xb_ref, *, n_scaled, scale):
    j = pl.program_id(1)

    @pl.when(j == 0)
    def _():
        xb_ref[...] = x_ref[...].astype(BF16)

    y = jnp.dot(xb_ref[...], w_ref[...], preferred_element_type=F32)
    o_ref[...] = (y * jnp.where(j < n_scaled, scale, 1.0)).astype(o_ref.dtype)


def _proj_scaled(x, w, *, scaled_cols, scale, tm=1024, tn=1024):
    T, D = x.shape
    N = w.shape[1]
    return pl.pallas_call(
        functools.partial(_proj_scaled_kernel, n_scaled=scaled_cols // tn, scale=scale),
        name="proj_qkv",
        grid=(T // tm, N // tn),
        in_specs=[pl.BlockSpec((tm, D), lambda i, j: (i, 0)),
                  pl.BlockSpec((D, tn), lambda i, j: (0, j))],
        out_specs=pl.BlockSpec((tm, tn), lambda i, j: (i, j)),
        out_shape=jax.ShapeDtypeStruct((T, N), BF16),
        scratch_shapes=[pltpu.VMEM((tm, D), BF16)],
        compiler_params=_cparams(("parallel", "arbitrary")),
    )(x, w)


def _proj_ln_kernel(*refs, n_lhs, alpha, n_tiles, to_residue_major):
    lhs_refs = refs[:n_lhs]
    w_refs = refs[n_lhs:2 * n_lhs]
    res_ref, g_ref, b_ref, o_ref, acc_ref = refs[2 * n_lhs:2 * n_lhs + 5]
    j = pl.program_id(1)
    y = jnp.dot(lhs_refs[0][...], w_refs[0][...], preferred_element_type=F32)
    for a_ref, w_ref in zip(lhs_refs[1:], w_refs[1:]):
        y = y + jnp.dot(a_ref[...], w_ref[...], preferred_element_type=F32)
    acc_ref[j] = y

    @pl.when(j == n_tiles - 1)
    def _():
        tn = acc_ref.shape[2]
        z = [alpha * res_ref[:, t * tn:(t + 1) * tn] + acc_ref[t] for t in range(n_tiles)]
        n = float(n_tiles * tn)
        mu = sum(jnp.sum(zt, axis=-1, keepdims=True) for zt in z) / n
        zc = [zt - mu for zt in z]
        var = sum(jnp.sum(zt * zt, axis=-1, keepdims=True) for zt in zc) / n
        rstd = lax.rsqrt(var + LN_EPS)
        out = [zc[t] * rstd * g_ref[:, t * tn:(t + 1) * tn] + b_ref[:, t * tn:(t + 1) * tn] for t in range(n_tiles)]
        if not to_residue_major:
            for t in range(n_tiles):
                o_ref[:, t * tn:(t + 1) * tn] = out[t]
        else:
            rows_ref = refs[-1]
            per_res = rows_ref.shape[1] // DIL
            for c in range(rows_ref.shape[0]):
                t, off = divmod(c * LANES, tn)
                rows_ref[c] = out[t][:, off:off + LANES]
                for r in range(DIL):
                    o_ref[r, :, c * LANES:(c + 1) * LANES] = rows_ref[c, pl.ds(r, per_res, stride=DIL), :]


def _proj_ln(lhs, ws, res, g, b, *, alpha, to_residue_major, tm=512, tn=1024):
    T, N = res.shape
    n_lhs = len(lhs)
    n_tiles = N // tn
    scratch = [pltpu.VMEM((n_tiles, tm, tn), F32)]
    if to_residue_major:
        na = T // DIL
        out_spec = pl.BlockSpec((DIL, tm // DIL, N), lambda i, j: (0, i, 0))
        out_shape = jax.ShapeDtypeStruct((DIL, na, N), F32)
        scratch.append(pltpu.VMEM((N // LANES, tm, LANES), F32))
    else:
        out_spec = pl.BlockSpec((tm, N), lambda i, j: (i, 0))
        out_shape = jax.ShapeDtypeStruct((T, N), F32)
    in_specs = ([pl.BlockSpec((tm, a.shape[1]), lambda i, j: (i, 0)) for a in lhs]
                + [pl.BlockSpec((w.shape[0], tn), lambda i, j: (0, j)) for w in ws]
                + [pl.BlockSpec((tm, N), lambda i, j: (i, 0)),
                   pl.BlockSpec((1, N), lambda i, j: (0, 0)),
                   pl.BlockSpec((1, N), lambda i, j: (0, 0))])
    out = pl.pallas_call(
        functools.partial(_proj_ln_kernel, n_lhs=n_lhs, alpha=alpha, n_tiles=n_tiles,
                          to_residue_major=to_residue_major),
        name="proj_ln",
        grid=(T // tm, n_tiles),
        in_specs=in_specs,
        out_specs=out_spec,
        out_shape=out_shape,
        scratch_shapes=scratch,
        compiler_params=_cparams(("parallel", "arbitrary")),
    )(*lhs, *ws, res, g.reshape(1, N), b.reshape(1, N))
    return out.reshape(T, N)


def _s5_tables(lam_re, lam_im, log_step, b_re, b_im, c_re, c_im, n_chunks):
    C = S5_CHUNK
    G, P = lam_re.shape
    H = b_re.shape[-1]
    lr = jnp.minimum(lam_re.astype(F32), S5_MAX_RE)
    li = lam_im.astype(F32)
    dt = jnp.exp(log_step.astype(F32))[:, None]
    kk = jnp.arange(C + 1, dtype=F32)[:, None, None]
    pw_mag = jnp.exp(kk * (lr * dt))
    pw_re = pw_mag * jnp.cos(kk * (li * dt))
    pw_im = pw_mag * jnp.sin(kk * (li * dt))
    a_re, a_im = pw_re[1], pw_im[1]
    den = lr * lr + li * li
    nr = a_re - 1.0
    f_re = (nr * lr + a_im * li) / den
    f_im = (a_im * lr - nr * li) / den
    br = b_re.astype(F32)
    bi = b_im.astype(F32)
    bb_re = f_re[..., None] * br - f_im[..., None] * bi
    bb_im = f_re[..., None] * bi + f_im[..., None] * br
    ab_re = pw_re[:C, :, :, None] * bb_re[None] - pw_im[:C, :, :, None] * bb_im[None]
    ab_im = pw_re[:C, :, :, None] * bb_im[None] + pw_im[:C, :, :, None] * bb_re[None]
    cr = c_re.astype(F32)
    ci = c_im.astype(F32)
    z_re = jnp.transpose(ab_re[::-1], (1, 0, 3, 2)).reshape(G, C * H, P)
    z_im = jnp.transpose(ab_im[::-1], (1, 0, 3, 2)).reshape(G, C * H, P)
    zmat = jnp.concatenate([z_re, z_im], axis=-1)
    kern = (jnp.einsum('gop,kgph->kgoh', cr, ab_re, precision=HIGHEST)
            - jnp.einsum('gop,kgph->kgoh', ci, ab_im, precision=HIGHEST))
    lag = jnp.arange(C)[None, :] - jnp.arange(C)[:, None]
    kl = kern[jnp.clip(lag, 0, C - 1)]
    kl = jnp.where((lag >= 0)[:, :, None, None, None], kl, 0.0)
    mmat = jnp.transpose(kl, (2, 0, 4, 1, 3)).reshape(G, C * H, C * H)
    ca_re = cr[None] * pw_re[1:, :, None, :] - ci[None] * pw_im[1:, :, None, :]
    ca_im = cr[None] * pw_im[1:, :, None, :] + ci[None] * pw_re[1:, :, None, :]
    n_re = jnp.transpose(ca_re, (1, 3, 0, 2)).reshape(G, P, C * H)
    n_im = jnp.transpose(-ca_im, (1, 3, 0, 2)).reshape(G, P, C * H)
    nmat = jnp.concatenate([n_re, n_im], axis=1)
    n_steps = max(1, (n_chunks - 1).bit_length())
    qr, qi = pw_re[C], pw_im[C]
    a1, a2 = [], []
    for _ in range(n_steps):
        a1.append(jnp.concatenate([qr, qr], axis=-1))
        a2.append(jnp.concatenate([-qi, qi], axis=-1))
        qr, qi = qr * qr - qi * qi, 2.0 * qr * qi
    a1 = jnp.stack(a1, axis=1)
    a2 = jnp.stack(a2, axis=1)
    return zmat.astype(BF16), mmat.astype(BF16), nmat.astype(BF16), a1, a2


def _s5_kernel(u_ref, z_ref, m_ref, n_ref, a1_ref, a2_ref, y_ref, ub_ref, ug_ref, sg_ref, mw_ref, nw_ref, *,
               n_steps):
    C, H = S5_CHUNK, S5_GROUP
    nc = u_ref.shape[0] // C
    gpt = LANES // H
    ch = C * H
    for j in range(C):
        ub_ref[:, j * LANES:(j + 1) * LANES] = u_ref[pl.ds(j, nc, stride=C), :].astype(BF16)
    row = lax.broadcasted_iota(I32, (nc, LANES), 0)
    sr = lax.broadcasted_iota(I32, (gpt * LANES, LANES), 0)
    sc = lax.broadcasted_iota(I32, (gpt * LANES, LANES), 1)
    sel_hit = sc == H * (sr // LANES) + sr % H
    sel_grp = (sr % LANES) // H
    pr = lax.broadcasted_iota(I32, (ch, C * LANES), 0)
    pc = lax.broadcasted_iota(I32, (ch, C * LANES), 1)
    put_tile = pc // LANES == pr // H
    put_lane = pc % LANES - pr % H

    def group(gl, c):
        sel = jnp.where(sel_hit & (sel_grp == gl), 1.0, 0.0).astype(BF16)
        half_w = gpt * LANES
        u = jnp.concatenate(
            [jnp.dot(ub_ref[:, t * half_w:(t + 1) * half_w], sel, preferred_element_type=F32)
             for t in range(C * LANES // half_w)], axis=1).astype(BF16)
        s = jnp.dot(u, z_ref[gl], preferred_element_type=F32)
        half = s.shape[1] // 2
        a1 = a1_ref[gl]
        a2 = a2_ref[gl]
        for k in range(n_steps):
            sh = 1 << k
            prev = jnp.where(row >= sh, pltpu.roll(s, sh, axis=0), 0.0)
            s = s + a1[k:k + 1, :] * prev + a2[k:k + 1, :] * pltpu.roll(prev, half, axis=1)
        s_in = jnp.where(row >= 1, pltpu.roll(s, 1, axis=0), 0.0)
        put = jnp.where(put_tile & (put_lane == H * gl), 1.0, 0.0).astype(BF16)
        ug_ref[gl] = u
        sg_ref[gl] = s_in.astype(BF16)
        mw_ref[gl] = jnp.dot(m_ref[gl], put, preferred_element_type=F32).astype(BF16)
        nw_ref[gl] = jnp.dot(n_ref[gl], put, preferred_element_type=F32).astype(BF16)
        return c

    lax.fori_loop(0, gpt, group, 0)
    u_all = jnp.concatenate([ug_ref[g] for g in range(gpt)], axis=1)
    s_all = jnp.concatenate([sg_ref[g] for g in range(gpt)], axis=1)
    y = (jnp.dot(u_all, mw_ref[...].reshape(gpt * ch, C * LANES), preferred_element_type=F32)
         + jnp.dot(s_all, nw_ref[...].reshape(gpt * sg_ref.shape[2], C * LANES), preferred_element_type=F32))
    for i in range(C):
        y_ref[pl.ds(i, nc, stride=C), :] = y[:, i * LANES:(i + 1) * LANES]


def _s5_core(h, width, tables):
    zmat, mmat, nmat, a1, a2 = tables
    T = h.shape[0]
    G, CH, P2 = zmat.shape
    C, H = S5_CHUNK, S5_GROUP
    nc = T // C
    n_steps = a1.shape[1]
    gpt = LANES // H
    assert CH == C * H and width == G * H
    return pl.pallas_call(
        functools.partial(_s5_kernel, n_steps=n_steps),
        name="s5_core",
        grid=(G // gpt,),
        in_specs=[pl.BlockSpec((T, LANES), lambda m: (0, m)),
                  pl.BlockSpec((gpt, CH, P2), lambda m: (m, 0, 0)),
                  pl.BlockSpec((gpt, CH, CH), lambda m: (m, 0, 0)),
                  pl.BlockSpec((gpt, P2, CH), lambda m: (m, 0, 0)),
                  pl.BlockSpec((gpt, n_steps, P2), lambda m: (m, 0, 0)),
                  pl.BlockSpec((gpt, n_steps, P2), lambda m: (m, 0, 0))],
        out_specs=pl.BlockSpec((T, LANES), lambda m: (0, m)),
        out_shape=jax.ShapeDtypeStruct((T, width), F32),
        scratch_shapes=[pltpu.VMEM((nc, C * LANES), BF16),
                        pltpu.VMEM((gpt, nc, CH), BF16),
                        pltpu.VMEM((gpt, nc, P2), BF16),
                        pltpu.VMEM((gpt, CH, C * LANES), BF16),
                        pltpu.VMEM((gpt, P2, C * LANES), BF16)],
        compiler_params=_cparams(("parallel",)),
    )(h, zmat, mmat, nmat, a1, a2)


def _s5_glu_kernel(y_ref, u_ref, d_ref, w_ref, b_ref, o_ref):
    y = y_ref[...] + d_ref[...] * u_ref[...]
    c0 = math.sqrt(2.0 / math.pi)
    z = 0.5 * y * (1.0 + jnp.tanh(c0 * (y + 0.044715 * (y * y * y))))
    lin = jnp.dot(z.astype(BF16), w_ref[...], preferred_element_type=F32) + b_ref[...]
    o_ref[...] = (z * jax.nn.sigmoid(lin)).astype(o_ref.dtype)


def _s5_glu(y, h, d_skip, w_glu, b_glu, *, tm=512):
    T, W = y.shape
    return pl.pallas_call(
        _s5_glu_kernel,
        name="s5_glu",
        grid=(T // tm,),
        in_specs=[pl.BlockSpec((tm, W), lambda i: (i, 0)),
                  pl.BlockSpec((tm, W), lambda i: (i, 0)),
                  pl.BlockSpec((1, W), lambda i: (0, 0)),
                  pl.BlockSpec((W, W), lambda i: (0, 0)),
                  pl.BlockSpec((1, W), lambda i: (0, 0))],
        out_specs=pl.BlockSpec((tm, W), lambda i: (i, 0)),
        out_shape=jax.ShapeDtypeStruct((T, W), BF16),
        compiler_params=_cparams(("parallel",)),
    )(y, h, d_skip.reshape(1, W), w_glu, b_glu.reshape(1, W))


def _gla_kernel(q_ref, k_ref, v_ref, r_ref, g_ref, w2_ref, b2_ref, ng_ref, o_ref, st_ref, *, dk, dv):
    @pl.when(pl.program_id(0) == 0)
    def _():
        st_ref[...] = jnp.zeros_like(st_ref)

    cb = GLA_BLOCK
    n_sub = cb // GLA_SUB
    scale = dk ** -0.5
    ri = lax.broadcasted_iota(I32, (cb, cb), 0)
    ci = lax.broadcasted_iota(I32, (cb, cb), 1)
    tri = (ri >= ci).astype(F32)
    n_blk = q_ref.shape[0] // cb
    rows = [slice(blk * cb, (blk + 1) * cb) for blk in range(n_blk)]
    bcums, spans = [], []
    for blk in range(n_blk):
        logit = jnp.dot(g_ref[rows[blk], :], w2_ref[...], preferred_element_type=F32, precision=HIGHEST) + b2_ref[...]
        log_a = (jnp.minimum(logit, 0.0) - jnp.log(1.0 + jnp.exp(-jnp.abs(logit)))) / GLA_GATE_TEMP
        bcum = jnp.dot(tri, log_a, preferred_element_type=F32, precision=HIGHEST)
        ends = [bcum[(a + 1) * GLA_SUB - 1:(a + 1) * GLA_SUB, :] for a in range(n_sub)]
        spans += [-ends[0]] + [ends[a - 1] - ends[a] for a in range(1, n_sub)]
        bcums.append(bcum)
    factorised_ok = jnp.max(jnp.concatenate(spans, axis=0)) < GLA_SAFE_LOG_DECAY

    def run(bcs, states, pairwise):
        for blk in range(n_blk):
            states = _gla_heads(q_ref, k_ref, v_ref, r_ref, ng_ref, o_ref, rows[blk], bcs[blk], states,
                                dk=dk, dv=dv, n_sub=n_sub, scale=scale, pairwise=pairwise)
        return states

    states = lax.cond(factorised_ok,
                      lambda bcs, st: run(bcs, st, False),
                      lambda bcs, st: run(bcs, st, True),
                      tuple(bcums), tuple(st_ref[hh] for hh in range(GLA_HEADS)))
    for hh in range(GLA_HEADS):
        st_ref[hh] = states[hh]


def _gla_heads(q_ref, k_ref, v_ref, r_ref, ng_ref, o_ref, rows, bcum, states, *, dk, dv, n_sub, scale, pairwise):
    cb = GLA_BLOCK
    new_states = []
    for hh in range(GLA_HEADS):
        ks = slice(hh * dk, (hh + 1) * dk)
        vs = slice(hh * dv, (hh + 1) * dv)
        b = bcum[:, ks]
        q = q_ref[rows, ks] * scale
        k = k_ref[rows, ks]
        v = v_ref[rows, vs].astype(BF16)
        refs = [jnp.zeros((1, dk), F32)] + [b[a * GLA_SUB - 1:a * GLA_SUB, :] for a in range(1, n_sub)]
        refmat = jnp.concatenate([jnp.broadcast_to(r, (GLA_SUB, dk)) for r in refs], axis=0)
        qe = (q * jnp.exp(b - refmat)).astype(BF16)
        st = states[hh]
        o_inter = lax.dot_general((q * jnp.exp(b)).astype(BF16), st.astype(BF16),
                                  (((1,), (1,)), ((), ())), preferred_element_type=F32)
        o_rows = []
        for a in range(n_sub):
            lo, hi = a * GLA_SUB, (a + 1) * GLA_SUB
            row_i = lax.broadcasted_iota(I32, (GLA_SUB, hi), 0)
            col_j = lax.broadcasted_iota(I32, (GLA_SUB, hi), 1)
            if not pairwise:
                ke = (k[:hi] * jnp.exp(refs[a] - b[:hi])).astype(BF16)
                att = lax.dot_general(qe[lo:hi], ke, (((1,), (1,)), ((), ())), preferred_element_type=F32)
                att = jnp.where(col_j <= row_i + lo, att, 0.0)
            else:
                key_row = lax.broadcasted_iota(I32, (hi, dk), 0)
                ke = jnp.where(key_row < lo, k[:hi] * jnp.exp(jnp.minimum(refs[a] - b[:hi], 0.0)), 0.0)
                att = lax.dot_general(qe[lo:hi], ke.astype(BF16), (((1,), (1,)), ((), ())),
                                      preferred_element_type=F32)
                b_a, q_a, k_a = b[lo:hi], q[lo:hi], k[lo:hi]
                terms = [q_a * jnp.exp(jnp.minimum(b_a - b_a[j:j + 1, :], 0.0)) * k_a[j:j + 1, :]
                         for j in range(GLA_SUB)]
                sums = jnp.dot(jnp.concatenate(terms, axis=0).astype(BF16), jnp.ones((dk, hi), BF16),
                               preferred_element_type=F32)
                for j in range(GLA_SUB):
                    att = jnp.where((col_j == lo + j) & (row_i >= j), sums[j * GLA_SUB:(j + 1) * GLA_SUB], att)
            o_rows.append(jnp.dot(att.astype(BF16), v[:hi], preferred_element_type=F32))
        o = jnp.concatenate(o_rows, axis=0) + o_inter
        o = o * lax.rsqrt(jnp.mean(o * o, axis=-1, keepdims=True) + GLA_EPS) * ng_ref[...]
        r = r_ref[rows, vs]
        o_ref[rows, vs] = (o * (r * jax.nn.sigmoid(r))).astype(o_ref.dtype)
        b_last = b[cb - 1:cb, :]
        kd = (k * jnp.exp(b_last - b)).astype(BF16)
        upd = lax.dot_general(v, kd, (((0,), (0,)), ((), ())), preferred_element_type=F32)
        new_states.append(st * jnp.exp(b_last) + upd)
    return tuple(new_states)


def _gla(h, g_low, w_gate2, b_gate2, norm_g, *, width):
    T = h.shape[0]
    qk = width // 2
    dk = qk // GLA_HEADS
    dv = width // GLA_HEADS
    cb = GLA_BLOCK * GLA_BLOCKS_PER_STEP
    w2 = jnp.zeros((LANES, qk), F32).at[:w_gate2.shape[0]].set(w_gate2.astype(F32))
    return pl.pallas_call(
        functools.partial(_gla_kernel, dk=dk, dv=dv),
        name="gla",
        grid=(T // cb,),
        in_specs=[pl.BlockSpec((cb, qk), lambda i: (i, 2)),
                  pl.BlockSpec((cb, qk), lambda i: (i, 3)),
                  pl.BlockSpec((cb, width), lambda i: (i, 2)),
                  pl.BlockSpec((cb, width), lambda i: (i, 3)),
                  pl.BlockSpec((cb, LANES), lambda i: (i, 0)),
                  pl.BlockSpec((LANES, qk), lambda i: (0, 0)),
                  pl.BlockSpec((1, qk), lambda i: (0, 0)),
                  pl.BlockSpec((1, dv), lambda i: (0, 0))],
        out_specs=pl.BlockSpec((cb, width), lambda i: (i, 0)),
        out_shape=jax.ShapeDtypeStruct((T, width), BF16),
        scratch_shapes=[pltpu.VMEM((GLA_HEADS, dv, dk), F32)],
        compiler_params=_cparams(("arbitrary",)),
    )(h, h, h, h, g_low, w2, b_gate2.reshape(1, qk).astype(F32), norm_g.reshape(1, dv).astype(F32))


def _attn_kernel(q_ref, k_ref, v_ref, o_ref, acc_ref, m_ref, l_ref, v1_ref, pat1_ref, pat4_ref, pat16_ref, *, na):
    dh = v_ref.shape[1]
    v1_ref[:, :dh] = v_ref[...]
    v1_ref[:, dh:] = jnp.ones_like(v_ref)
    acc_ref[...] = jnp.zeros_like(acc_ref)
    m_ref[...] = jnp.full_like(m_ref, NEG_BIG)
    l_ref[...] = jnp.zeros_like(l_ref)

    def token_offsets(rows, r_step, n, axis):
        shape = (n, 1) if axis == 0 else (1, n)
        idx = lax.broadcasted_iota(I32, shape, axis)
        c = idx // rows
        return DIL * (idx - c * rows) + r_step * c

    def delta_pattern(n_chunks, q_rows, k_rows, r_step):
        return (token_offsets(q_rows, r_step, n_chunks * q_rows, 0)
                - token_offsets(k_rows, r_step, n_chunks * k_rows, 1))

    def cat(ref, starts, rows):
        return jnp.concatenate([ref[pl.ds(pl.multiple_of(s, 16), rows), :] for s in starts], axis=0)

    def window_bias(n_chunks, q_rows, k_rows, r_step, back, window):
        pat = delta_pattern(n_chunks, q_rows, k_rows, r_step)
        inside = lambda off: jnp.where((pat + off >= 0) & (pat + off <= window), 0.0, NEG_BIG)
        return jnp.stack([inside(0), inside(DIL * back)])

    def attend(blocks, q_rows, k_rows, bias_ref):
        loaded = []
        for q_starts, k_starts, a_q, a_k in blocks:
            loaded.append((cat(q_ref, q_starts, q_rows), cat(k_ref, k_starts, k_rows), cat(v1_ref, k_starts, k_rows),
                           cat(m_ref, q_starts, q_rows), cat(l_ref, q_starts, q_rows),
                           cat(acc_ref, q_starts, q_rows)))
        results = []
        for (q_starts, k_starts, a_q, a_k), (qb, kb, vb, m_old, l_old, acc_old) in zip(blocks, loaded):
            s = lax.dot_general(qb, kb, (((1,), (1,)), ((), ())), preferred_element_type=F32)
            s = s + bias_ref[jnp.where(a_q == a_k, 0, 1)]
            m_new = jnp.maximum(m_old, jnp.max(s, axis=1, keepdims=True))
            alpha = jnp.exp(m_old - m_new)
            p = jnp.exp(s - m_new[:, :1])
            pv = jnp.dot(p.astype(BF16), vb, preferred_element_type=F32)
            l_new = alpha * l_old + pv[:, dh:]
            acc_new = alpha * acc_old + pv[:, :dh]
            results.append((m_new, l_new, acc_new))
        for (q_starts, *_), (m_new, l_new, acc_new) in zip(blocks, results):
            for c, st in enumerate(q_starts):
                st = pl.multiple_of(st, 16)
                rs = slice(c * q_rows, (c + 1) * q_rows)
                m_ref[pl.ds(st, q_rows), :] = m_new[rs]
                l_ref[pl.ds(st, q_rows), :] = l_new[rs]
                acc_ref[pl.ds(st, q_rows), :] = acc_new[rs]

    w1 = DILATED_GROUPS[0][0]
    u1 = 2
    pat1_ref[...] = window_bias(DIL, 16, 32, 1, 16, w1)

    def body1(it, carry):
        blocks = []
        for j in range(u1):
            a0 = (it * u1 + j) * 16
            ak = jnp.maximum(a0 - 16, 0)
            blocks.append(([r * na + a0 for r in range(DIL)], [r * na + ak for r in range(DIL)], a0, ak))
        attend(blocks, 16, 32, pat1_ref)
        return carry

    lax.fori_loop(0, na // (16 * u1), body1, 0)

    w4 = DILATED_GROUPS[1][0]
    pat4_ref[...] = window_bias(4, 32, 64, 4, 32, w4)

    def body4(it, carry):
        a0 = it * 32
        ak = jnp.maximum(a0 - 32, 0)
        blocks = [([(rho + 4 * sg) * na + a0 for sg in range(4)], [(rho + 4 * sg) * na + ak for sg in range(4)],
                   a0, ak) for rho in range(4)]
        attend(blocks, 32, 64, pat4_ref)
        return carry

    lax.fori_loop(0, na // 32, body4, 0)

    w16 = DILATED_GROUPS[2][0]
    u16 = 4
    pat16_ref[...] = window_bias(1, 128, 256, 0, 128, w16)

    def body16(it, carry):
        rg = it // (na // 128)
        a0 = (it - rg * (na // 128)) * 128
        ak = jnp.maximum(a0 - 128, 0)
        blocks = [([(rg * u16 + j) * na + a0], [(rg * u16 + j) * na + ak], a0, ak) for j in range(u16)]
        attend(blocks, 128, 256, pat16_ref)
        return carry

    lax.fori_loop(0, (DIL // u16) * (na // 128), body16, 0)

    o_ref[...] = (acc_ref[...] / l_ref[...]).astype(o_ref.dtype)


def _dilated_attention(qkv, d_model):
    T = qkv.shape[0]
    dh = d_model // ATT_HEADS
    assert dh == LANES
    na = T // DIL
    return pl.pallas_call(
        functools.partial(_attn_kernel, na=na),
        name="dilated_attn",
        grid=(ATT_HEADS,),
        in_specs=[pl.BlockSpec((T, dh), lambda h: (0, h)),
                  pl.BlockSpec((T, dh), lambda h: (0, ATT_HEADS + h)),
                  pl.BlockSpec((T, dh), lambda h: (0, 2 * ATT_HEADS + h))],
        out_specs=pl.BlockSpec((T, dh), lambda h: (0, h)),
        out_shape=jax.ShapeDtypeStruct((T, d_model), BF16),
        scratch_shapes=[pltpu.VMEM((T, dh), F32), pltpu.VMEM((T, dh), F32), pltpu.VMEM((T, dh), F32),
                        pltpu.VMEM((T, 2 * dh), BF16), pltpu.VMEM((2, 256, 512), F32),
                        pltpu.VMEM((2, 128, 256), F32), pltpu.VMEM((2, 128, 256), F32)],
        compiler_params=_cparams(("parallel",)),
    )(qkv, qkv, qkv)


def _route_kernel(x_ref, w_ref, b_ref, idx_ref, gate_ref, rank_ref, cnt_ref, carry_ref):
    @pl.when(pl.program_id(0) == 0)
    def _():
        carry_ref[...] = jnp.zeros_like(carry_ref)

    tb = x_ref.shape[0]
    x = x_ref[...]
    w = w_ref[...]
    x_hi, w_hi = x.astype(BF16), w.astype(BF16)
    x_lo = (x - x_hi.astype(F32)).astype(BF16)
    w_lo = (w - w_hi.astype(F32)).astype(BF16)
    lg = (jnp.dot(x_hi, w_hi, preferred_element_type=F32) + jnp.dot(x_hi, w_lo, preferred_element_type=F32)
          + jnp.dot(x_lo, w_hi, preferred_element_type=F32)) + b_ref[...]
    lane = lax.broadcasted_iota(I32, lg.shape, 1)
    vals, hots = [], []
    idx_out = jnp.zeros(lg.shape, I32)
    for k in range(TOP_K):
        m = jnp.max(lg, axis=1, keepdims=True)
        sel = jnp.min(jnp.where(lg == m, lane, LANES), axis=1, keepdims=True)
        hot = lane == sel
        vals.append(m)
        hots.append(hot)
        idx_out = jnp.where(lane == k, sel, idx_out)
        lg = jnp.where(hot, -jnp.inf, lg)
    ex = [jnp.exp(v - vals[0]) for v in vals]
    den = sum(ex)
    gate_out = jnp.zeros(lg.shape, F32)
    for k in range(TOP_K):
        gate_out = jnp.where(lane == k, ex[k] / den, gate_out)
    chosen = sum(h.astype(F32) for h in hots)
    ri = lax.broadcasted_iota(I32, (tb, tb), 0)
    ci = lax.broadcasted_iota(I32, (tb, tb), 1)
    before = jnp.dot((ri > ci).astype(BF16), chosen.astype(BF16), preferred_element_type=F32) + carry_ref[...]
    rank_out = jnp.zeros(lg.shape, I32)
    for k in range(TOP_K):
        rk = jnp.sum(jnp.where(hots[k], before, 0.0), axis=1, keepdims=True).astype(I32)
        rank_out = jnp.where(lane == k, rk, rank_out)
    idx_ref[...] = idx_out
    gate_ref[...] = gate_out
    rank_ref[...] = rank_out
    carry_ref[...] = carry_ref[...] + jnp.sum(chosen, axis=0, keepdims=True)
    cnt_ref[...] = carry_ref[...].astype(I32)


def _route(x, w_router, b_router, *, tb=512):
    T, D = x.shape
    E = w_router.shape[1]
    wr = jnp.zeros((D, LANES), F32).at[:, :E].set(w_router.astype(F32))
    br = jnp.full((1, LANES), NEG_BIG, F32).at[0, :E].set(b_router.astype(F32))
    row = lambda dt: jax.ShapeDtypeStruct((T, LANES), dt)
    idx, gate, rank, cnt = pl.pallas_call(
        _route_kernel,
        name="moe_route",
        grid=(T // tb,),
        in_specs=[pl.BlockSpec((tb, D), lambda i: (i, 0)),
                  pl.BlockSpec((D, LANES), lambda i: (0, 0)),
                  pl.BlockSpec((1, LANES), lambda i: (0, 0))],
        out_specs=[pl.BlockSpec((tb, LANES), lambda i: (i, 0)),
                   pl.BlockSpec((tb, LANES), lambda i: (i, 0)),
                   pl.BlockSpec((tb, LANES), lambda i: (i, 0)),
                   pl.BlockSpec((1, LANES), lambda i: (0, 0))],
        out_shape=[row(I32), row(F32), row(I32), jax.ShapeDtypeStruct((1, LANES), I32)],
        scratch_shapes=[pltpu.VMEM((1, LANES), F32)],
        compiler_params=_cparams(("arbitrary",)),
    )(x, wr, br)
    return idx, gate, rank, cnt[0, :E]


def _zero_tail_kernel(row_ref, o_ref, z_ref, sem):
    z_ref[...] = jnp.zeros_like(z_ref)
    n = row_ref.shape[0]

    def copy(e):
        return pltpu.make_async_copy(z_ref, o_ref.at[pl.ds(pl.multiple_of(row_ref[e], MOE_BLOCK), MOE_BLOCK)], sem)

    def start(e, c):
        copy(e).start()
        return c

    def wait(e, c):
        copy(e).wait()
        return c

    lax.fori_loop(0, n, start, 0)
    lax.fori_loop(0, n, wait, 0)


def _zero_tails(tail_rows, n_rows, d):
    return pl.pallas_call(
        _zero_tail_kernel,
        name="moe_zero_tails",
        grid_spec=pltpu.PrefetchScalarGridSpec(
            num_scalar_prefetch=1, grid=(1,),
            in_specs=[],
            out_specs=pl.BlockSpec(memory_space=pl.ANY),
            scratch_shapes=[pltpu.VMEM((MOE_BLOCK, d), F32), pltpu.SemaphoreType.DMA(())]),
        out_shape=jax.ShapeDtypeStruct((n_rows, d), F32),
        compiler_params=_cparams(("arbitrary",)),
    )(tail_rows)


def _dispatch_kernel(dest_ref, x_ref, xs_in_ref, xs_ref, sem):
    del xs_in_ref
    tb = x_ref.shape[0]
    base = pl.program_id(0) * tb * TOP_K

    def copy(i, k):
        return pltpu.make_async_copy(x_ref.at[pl.ds(i, 1)],
                                     xs_ref.at[pl.ds(dest_ref[base + i * TOP_K + k], 1)], sem)

    def start(i, c):
        for k in range(TOP_K):
            copy(i, k).start()
        return c

    lax.fori_loop(0, tb, start, 0, unroll=4)
    for _ in range(TOP_K):
        pltpu.make_async_copy(x_ref, xs_ref.at[pl.ds(0, tb)], sem).wait()


def _dispatch(x, dest_flat, xs_init, *, tb=256):
    T, D = x.shape
    return pl.pallas_call(
        _dispatch_kernel,
        name="moe_dispatch",
        grid_spec=pltpu.PrefetchScalarGridSpec(
            num_scalar_prefetch=1, grid=(T // tb,),
            in_specs=[pl.BlockSpec((tb, D), lambda i, dest: (i, 0)),
                      pl.BlockSpec(memory_space=pl.ANY)],
            out_specs=pl.BlockSpec(memory_space=pl.ANY),
            scratch_shapes=[pltpu.SemaphoreType.DMA(())]),
        out_shape=jax.ShapeDtypeStruct(xs_init.shape, xs_init.dtype),
        input_output_aliases={2: 0},
        compiler_params=_cparams(("arbitrary",)),
    )(dest_flat, x, xs_init)


def _moe_kernel(item_e_ref, item_row_ref, item_nb_ref, xs_ref, wg_ref, wu_ref, bg_ref, bu_ref, wd_ref, bd_ref,
                ys_ref, xbuf_ref, land_ref, acc_ref, in_sem, out_sem, *, n_f):
    it = pl.program_id(0)
    f = pl.program_id(1)
    nb = item_nb_ref[it]
    row0 = item_row_ref[it]
    blk = MOE_BLOCK

    def acc_rows(j):
        return pl.ds(pl.multiple_of(j * blk, blk), blk)

    def in_copy(item, j):
        src = xs_ref.at[pl.ds(pl.multiple_of(item_row_ref[item] + j * blk, blk), blk)]
        return pltpu.make_async_copy(src, land_ref.at[acc_rows(j)], in_sem)

    def out_copy(j):
        dst = ys_ref.at[pl.ds(pl.multiple_of(row0 + j * blk, blk), blk)]
        return pltpu.make_async_copy(acc_ref.at[acc_rows(j)], dst, out_sem)

    def for_blocks(fn, n=nb):
        def body(j, c):
            fn(j)
            return c

        lax.fori_loop(0, n, body, 0)

    @pl.when((f == 0) & (nb > 0))
    def _():
        @pl.when(it == 0)
        def _():
            for_blocks(lambda j: in_copy(it, j).start())

        for_blocks(lambda j: in_copy(it, j).wait())

        def to_bf16(j):
            xbuf_ref[acc_rows(j), :] = land_ref[acc_rows(j), :].astype(BF16)
            acc_ref[acc_rows(j), :] = jnp.broadcast_to(bd_ref[...], (blk, acc_ref.shape[1]))

        for_blocks(to_bf16)

    @pl.when((f == 1) & (it + 1 < pl.num_programs(0)))
    def _():
        nxt = it + 1
        for_blocks(lambda j: in_copy(nxt, j).start(), item_nb_ref[nxt])

    @pl.when(nb > 0)
    def _():
        def compute(r0, n_rows, wg, wu, wd):
            rows = pl.ds(pl.multiple_of(r0, blk), n_rows)
            xb = xbuf_ref[rows, :]
            hg = jnp.dot(xb, wg, preferred_element_type=F32) + bg_ref[...]
            hu = jnp.dot(xb, wu, preferred_element_type=F32) + bu_ref[...]
            gate = jnp.minimum(hg, SWIGLU_LIMIT)
            up = jnp.clip(hu, -SWIGLU_LIMIT, SWIGLU_LIMIT)
            act = (up + 1.0) * (gate * jax.nn.sigmoid(SWIGLU_ALPHA * gate))
            acc_ref[rows, :] += jnp.dot(act.astype(BF16), wd, preferred_element_type=F32)

            @pl.when(f == n_f - 1)
            def _():
                for b in range(n_rows // blk):
                    out_copy(r0 // blk + b).start()

        def piece(r0, n_rows):
            compute(r0, n_rows, wg_ref[...].astype(BF16), wu_ref[...].astype(BF16), wd_ref[...].astype(BF16))

        n4 = nb // 4

        def quad(j, c):
            piece(j * (4 * blk), 4 * blk)
            return c

        lax.fori_loop(0, n4, quad, 0)

        @pl.when((nb & 2) != 0)
        def _():
            piece(n4 * (4 * blk), 2 * blk)

        @pl.when((nb & 1) != 0)
        def _():
            piece((nb - 1) * blk, blk)

    @pl.when((f == n_f - 1) & (nb > 0))
    def _():
        for_blocks(lambda j: out_copy(j).wait())


def _moe_experts(xs, items, w_gu, b_gu, w_down, b_down, layer, *, tf=512):
    item_e, item_row, item_nb = items
    n_items = item_e.shape[0]
    P, D = xs.shape
    L, E, _, F2 = w_gu.shape
    F = F2 // 2
    n_f = F // tf
    assert n_f >= 2
    rc = MOE_CHUNK_BLOCKS * MOE_BLOCK

    def fe(f, nb_ref, it):
        return jnp.where(nb_ref[it] > 0, f, n_f - 1)

    in_specs = [
        pl.BlockSpec(memory_space=pl.ANY),
        pl.BlockSpec((None, None, D, tf), lambda it, f, e, r, nb: (layer, e[it], 0, fe(f, nb, it))),
        pl.BlockSpec((None, None, D, tf), lambda it, f, e, r, nb: (layer, e[it], 0, n_f + fe(f, nb, it))),
        pl.BlockSpec((None, None, 1, tf), lambda it, f, e, r, nb: (layer, e[it], 0, fe(f, nb, it))),
        pl.BlockSpec((None, None, 1, tf), lambda it, f, e, r, nb: (layer, e[it], 0, n_f + fe(f, nb, it))),
        pl.BlockSpec((None, None, tf, D), lambda it, f, e, r, nb: (layer, e[it], fe(f, nb, it), 0)),
        pl.BlockSpec((None, None, 1, D), lambda it, f, e, r, nb: (layer, e[it], 0, 0)),
    ]
    return pl.pallas_call(
        functools.partial(_moe_kernel, n_f=n_f),
        name="moe_experts",
        grid_spec=pltpu.PrefetchScalarGridSpec(
            num_scalar_prefetch=3, grid=(n_items, n_f),
            in_specs=in_specs,
            out_specs=pl.BlockSpec(memory_space=pl.ANY),
            scratch_shapes=[pltpu.VMEM((rc, D), BF16),
                            pltpu.VMEM((rc, D), F32),
                            pltpu.VMEM((rc, D), F32),
                            pltpu.SemaphoreType.DMA(()),
                            pltpu.SemaphoreType.DMA(())]),
        out_shape=jax.ShapeDtypeStruct((P, D), F32),
        compiler_params=_cparams(("arbitrary", "arbitrary")),
    )(item_e, item_row, item_nb, xs, w_gu, w_gu, b_gu.reshape(L, E, 1, F2), b_gu.reshape(L, E, 1, F2),
      w_down, b_down.reshape(L, E, 1, D))


def _combine_kernel(dest_ref, ys_ref, gate_ref, x_ref, g_ref, b_ref, o_ref, buf_ref, sem, *scratch,
                    alpha, to_natural):
    tb = o_ref.shape[0]
    step = pl.program_id(0)
    slot = step % 2
    per_res = tb // DIL

    def token(blk, i):
        if not to_natural:
            return blk * tb + i
        n_a = pl.num_programs(0) * per_res
        return (i % DIL) * n_a + blk * per_res + i // DIL

    def copy(blk, sl, i, k):
        src = ys_ref.at[pl.ds(dest_ref[token(blk, i) * TOP_K + k], 1)]
        return pltpu.make_async_copy(src, buf_ref.at[sl, k, pl.ds(i, 1)], sem.at[sl])

    def for_rows(fn):
        def body(i, c):
            for k in range(TOP_K):
                fn(i, k)
            return c

        lax.fori_loop(0, tb, body, 0, unroll=4)

    @pl.when(step == 0)
    def _():
        for_rows(lambda i, k: copy(0, 0, i, k).start())

    @pl.when(step + 1 < pl.num_programs(0))
    def _():
        for_rows(lambda i, k: copy(step + 1, 1 - slot, i, k).start())

    for k in range(TOP_K):
        pltpu.make_async_copy(ys_ref.at[pl.ds(0, tb)], buf_ref.at[slot, k], sem.at[slot]).wait()
    if to_natural:
        xn_ref, gn_ref = scratch
        for r in range(DIL):
            gn_ref[pl.ds(r, per_res, stride=DIL), :] = gate_ref[r]
            for c in range(xn_ref.shape[0]):
                xn_ref[c, pl.ds(r, per_res, stride=DIL), :] = x_ref[r, :, c * LANES:(c + 1) * LANES]
        x = jnp.concatenate([xn_ref[c] for c in range(xn_ref.shape[0])], axis=1)
        gate = gn_ref[...]
    else:
        x = x_ref[...]
        gate = gate_ref[...]
    ffn = gate[:, 0:1] * buf_ref[slot, 0]
    for k in range(1, TOP_K):
        ffn = ffn + gate[:, k:k + 1] * buf_ref[slot, k]
    o_ref[...] = _layer_norm_rows(alpha * x + ffn, g_ref[...], b_ref[...])


def _combine_ln(ys, dest_flat, gates, x, g, b, *, alpha, to_natural, tb=128):
    T, D = x.shape
    gate_pad = gates
    scratch = [pltpu.VMEM((2, TOP_K, tb, D), F32), pltpu.SemaphoreType.DMA((2,))]
    if to_natural:
        na = T // DIL
        per_res = tb // DIL
        gate_in = gate_pad.reshape(DIL, na, LANES)
        x_in = x.reshape(DIL, na, D)
        gate_spec = pl.BlockSpec((DIL, per_res, LANES), lambda i, dest: (0, i, 0))
        x_spec = pl.BlockSpec((DIL, per_res, D), lambda i, dest: (0, i, 0))
        scratch += [pltpu.VMEM((D // LANES, tb, LANES), F32), pltpu.VMEM((tb, LANES), F32)]
    else:
        gate_in, x_in = gate_pad, x
        gate_spec = pl.BlockSpec((tb, LANES), lambda i, dest: (i, 0))
        x_spec = pl.BlockSpec((tb, D), lambda i, dest: (i, 0))
    return pl.pallas_call(
        functools.partial(_combine_kernel, alpha=alpha, to_natural=to_natural),
        name="moe_combine",
        grid_spec=pltpu.PrefetchScalarGridSpec(
            num_scalar_prefetch=1, grid=(T // tb,),
            in_specs=[pl.BlockSpec(memory_space=pl.ANY),
                      gate_spec,
                      x_spec,
                      pl.BlockSpec((1, D), lambda i, dest: (0, 0)),
                      pl.BlockSpec((1, D), lambda i, dest: (0, 0))],
            out_specs=pl.BlockSpec((tb, D), lambda i, dest: (i, 0)),
            scratch_shapes=scratch),
        out_shape=jax.ShapeDtypeStruct((T, D), F32),
        compiler_params=_cparams(("arbitrary",)),
    )(dest_flat, ys, gate_in, x_in, g.reshape(1, D), b.reshape(1, D))


def _moe_layout(idx, rank, counts, n_assign):
    E = counts.shape[0]
    blk = MOE_BLOCK
    nblk = (counts + blk - 1) // blk
    bend = jnp.cumsum(nblk)
    bstart = bend - nblk
    first_row = (bstart * blk).astype(I32)
    dest = rank
    for e in range(E):
        dest = dest + jnp.where(idx == e, first_row[e], 0)
    dest = dest[:, :TOP_K]
    tail = jnp.where(nblk > 0, (bend - 1) * blk, 0).astype(I32)
    cb = MOE_CHUNK_BLOCKS
    n_items_max = E + (n_assign // blk + E) // cb + 1
    per_e = (nblk + cb - 1) // cb
    iend = jnp.cumsum(per_e)
    istart = iend - per_e
    ids = jnp.arange(n_items_max, dtype=I32)
    e_of = jnp.minimum(jnp.searchsorted(iend, ids, side='right'), E - 1).astype(I32)
    valid = ids < iend[-1]
    last_e = jnp.max(jnp.where(per_e > 0, jnp.arange(E, dtype=I32), 0))
    e_of = jnp.where(valid, e_of, last_e)
    local = ids - istart[e_of]
    first_blk = bstart[e_of] + local * cb
    nb = jnp.where(valid, jnp.minimum(cb, nblk[e_of] - local * cb), 0)
    row = jnp.where(valid, first_blk * blk, 0)
    return dest.astype(I32), tail, (e_of, row.astype(I32), nb.astype(I32))


def _moe(x, w_router, b_router, w_gu, b_gu, w_down, b_down, ln_g, ln_b, *, alpha, layer, to_natural=False):
    T, D = x.shape
    E = w_router.shape[1]
    idx, gates, rank, counts = _route(x, w_router, b_router)
    n_assign = T * TOP_K
    dest, tail, items = _moe_layout(idx, rank, counts, n_assign)
    dest_flat = dest.reshape(n_assign)
    n_rows = n_assign + E * MOE_BLOCK
    xs = _dispatch(x, dest_flat, _zero_tails(tail, n_rows, D))
    ys = _moe_experts(xs, items, w_gu, b_gu, w_down, b_down, layer)
    return _combine_ln(ys, dest_flat, gates, x, ln_g, ln_b, alpha=alpha, to_natural=to_natural)


def _even_mixer_ln(x, w_in, lam_re, lam_im, log_step, b_re, b_im, c_re, c_im, d_skip, w_glu, b_glu,
                   w_gate2, b_gate2, norm_g, w_out, ln_g, ln_b, *, alpha, to_residue_major):
    T, D = x.shape
    W = d_skip.shape[0]
    qk = W // 2
    rank = w_gate2.shape[0]
    s4 = W + 2 * qk + W
    w_main = jnp.concatenate([w_in[:, :s4], w_in[:, s4 + rank:]], axis=1).astype(BF16)
    w_gate = jnp.zeros((D, LANES), BF16).at[:, :rank].set(w_in[:, s4:s4 + rank].astype(BF16))
    h, g_low = _proj_in(x, w_main, w_gate)
    tables = _s5_tables(lam_re, lam_im, log_step, b_re, b_im, c_re, c_im, T // S5_CHUNK)
    y = _s5_core(h, W, tables)
    ya = _s5_glu(y, h, d_skip.astype(F32), w_glu.astype(BF16), b_glu.astype(F32))
    yb = _gla(h, g_low, w_gate2, b_gate2, norm_g, width=W)
    w_out_b = w_out.astype(BF16)
    return _proj_ln([ya, yb], [w_out_b[:W], w_out_b[W:]], x, ln_g, ln_b, alpha=alpha,
                    to_residue_major=to_residue_major)


def _odd_mixer_ln(x, w_qkv, w_o, ln_g, ln_b, *, alpha):
    T, D = x.shape
    qkv = _proj_scaled(x, w_qkv.astype(BF16), scaled_cols=D, scale=(D // ATT_HEADS) ** -0.5)
    y = _dilated_attention(qkv, D)
    return _proj_ln([y], [w_o.astype(BF16)], x, ln_g, ln_b, alpha=alpha, to_residue_major=False)


def kernel(x, ab_w_in, s5_lam_re, s5_lam_im, s5_log_step, s5_b_re, s5_b_im, s5_c_re, s5_c_im, s5_d, s5_w_glu, s5_b_glu, gla_w_gate2, gla_b_gate2, gla_norm_g, ab_w_out, c_w_qkv, c_w_o, ln1_g, ln1_b, moe_w_router, moe_b_router, moe_w_gu, moe_b_gu, moe_w_down, moe_b_down, ln2_g, ln2_b):
    bsz, L, D = x.shape
    depth = ln1_g.shape[0]
    alpha = (2 * depth) ** 0.25
    outs = []
    for bi in range(bsz):
        xt = x[bi].astype(F32)
        for layer in range(depth):
            i = layer // 2
            odd = layer % 2 == 1
            if not odd:
                xt = _even_mixer_ln(xt, ab_w_in[i], s5_lam_re[i], s5_lam_im[i], s5_log_step[i], s5_b_re[i],
                                    s5_b_im[i], s5_c_re[i], s5_c_im[i], s5_d[i], s5_w_glu[i], s5_b_glu[i],
                                    gla_w_gate2[i], gla_b_gate2[i], gla_norm_g[i], ab_w_out[i],
                                    ln1_g[layer], ln1_b[layer], alpha=alpha, to_residue_major=layer + 1 < depth)
            else:
                xt = _odd_mixer_ln(xt, c_w_qkv[i], c_w_o[i], ln1_g[layer], ln1_b[layer], alpha=alpha)
            xt = _moe(xt, moe_w_router[layer], moe_b_router[layer], moe_w_gu, moe_b_gu,
                      moe_w_down, moe_b_down, ln2_g[layer], ln2_b[layer], alpha=alpha, layer=layer,
                      to_natural=odd)
        outs.append(xt.astype(x.dtype))
    return outs[0].reshape(1, L, D) if bsz == 1 else jnp.stack(outs)
```

```python
import functools
import math

import jax
import jax.numpy as jnp
from jax import lax
from jax.experimental import pallas as pl
from jax.experimental.pallas import tpu as pltpu

F32 = jnp.float32
BF16 = jnp.bfloat16
I32 = jnp.int32
HIGHEST = lax.Precision.HIGHEST

LANES = 128
VMEM_LIMIT_BYTES = 56 * 1024 * 1024

S5_GROUP = 16
S5_STATE = 64
S5_MAX_RE = -1e-4
S5_CHUNK = 16
GLA_HEADS = 4
GLA_GATE_TEMP = 16.0
GLA_EPS = 1e-6
GLA_BLOCK = 64
GLA_BLOCKS_PER_STEP = 2
GLA_SAFE_LOG_DECAY = 60.0
GLA_SUB = 16
ATT_HEADS = 16
DIL = 16
DILATED_GROUPS = ((128, 1), (512, 4), (2048, 16))
TOP_K = 4
SWIGLU_LIMIT = 7.0
SWIGLU_ALPHA = 1.702
MOE_BLOCK = 128
MOE_CHUNK_BLOCKS = 9
LN_EPS = 1e-5
NEG_BIG = -1e30


def _cparams(semantics):
    return pltpu.CompilerParams(dimension_semantics=semantics, vmem_limit_bytes=VMEM_LIMIT_BYTES)


def _layer_norm_rows(z, g, b):
    mu = jnp.mean(z, axis=-1, keepdims=True)
    zc = z - mu
    var = jnp.mean(zc * zc, axis=-1, keepdims=True)
    return zc * lax.rsqrt(var + LN_EPS) * g + b


def _proj_in_kernel(x_ref, w_ref, wg_ref, h_ref, g_ref, xb_ref):
    @pl.when(pl.program_id(1) == 0)
    def _():
        xb = x_ref[...].astype(BF16)
        xb_ref[...] = xb
        g_ref[...] = jnp.dot(xb, wg_ref[...], preferred_element_type=F32)

    h_ref[...] = jnp.dot(xb_ref[...], w_ref[...], preferred_element_type=F32)


def _proj_in(x, w_main, w_gate, *, tm=1024, tn=1024):
    T, D = x.shape
    N = w_main.shape[1]
    return pl.pallas_call(
        _proj_in_kernel,
        name="proj_in",
        grid=(T // tm, N // tn),
        in_specs=[pl.BlockSpec((tm, D), lambda i, j: (i, 0)),
                  pl.BlockSpec((D, tn), lambda i, j: (0, j)),
                  pl.BlockSpec((D, LANES), lambda i, j: (0, 0))],
        out_specs=[pl.BlockSpec((tm, tn), lambda i, j: (i, j)),
                   pl.BlockSpec((tm, LANES), lambda i, j: (i, 0))],
        out_shape=[jax.ShapeDtypeStruct((T, N), F32), jax.ShapeDtypeStruct((T, LANES), F32)],
        scratch_shapes=[pltpu.VMEM((tm, D), BF16)],
        compiler_params=_cparams(("parallel", "arbitrary")),
    )(x, w_main, w_gate)


def _proj_scaled_kernel(x_ref, w_ref, o_ref, xb_ref, *, n_scaled, scale):
    j = pl.program_id(1)

    @pl.when(j == 0)
    def _():
        xb_ref[...] = x_ref[...].astype(BF16)

    y = jnp.dot(xb_ref[...], w_ref[...], preferred_element_type=F32)
    o_ref[...] = (y * jnp.where(j < n_scaled, scale, 1.0)).astype(o_ref.dtype)


def _proj_scaled(x, w, *, scaled_cols, scale, tm=1024, tn=2048):
    T, D = x.shape
    N = w.shape[1]
    return pl.pallas_call(
        functools.partial(_proj_scaled_kernel, n_scaled=scaled_cols // tn, scale=scale),
        name="proj_qkv",
        grid=(T // tm, N // tn),
        in_specs=[pl.BlockSpec((tm, D), lambda i, j: (i, 0)),
                  pl.BlockSpec((D, tn), lambda i, j: (0, j))],
        out_specs=pl.BlockSpec((tm, tn), lambda i, j: (i, j)),
        out_shape=jax.ShapeDtypeStruct((T, N), BF16),
        scratch_shapes=[pltpu.VMEM((tm, D), BF16)],
        compiler_params=_cparams(("parallel", "arbitrary")),
    )(x, w)


def _proj_ln_kernel(*refs, n_lhs, alpha, n_tiles, to_residue_major):
    lhs_refs = refs[:n_lhs]
    w_refs = refs[n_lhs:2 * n_lhs]
    res_ref, g_ref, b_ref, o_ref, acc_ref = refs[2 * n_lhs:2 * n_lhs + 5]
    j = pl.program_id(1)
    y = jnp.dot(lhs_refs[0][...], w_refs[0][...], preferred_element_type=F32)
    for a_ref, w_ref in zip(lhs_refs[1:], w_refs[1:]):
        y = y + jnp.dot(a_ref[...], w_ref[...], preferred_element_type=F32)
    acc_ref[j] = y

    @pl.when(j == n_tiles - 1)
    def _():
        tn = acc_ref.shape[2]
        z = [alpha * res_ref[:, t * tn:(t + 1) * tn] + acc_ref[t] for t in range(n_tiles)]
        n = float(n_tiles * tn)
        mu = sum(jnp.sum(zt, axis=-1, keepdims=True) for zt in z) / n
        zc = [zt - mu for zt in z]
        var = sum(jnp.sum(zt * zt, axis=-1, keepdims=True) for zt in zc) / n
        rstd = lax.rsqrt(var + LN_EPS)
        out = [zc[t] * rstd * g_ref[:, t * tn:(t + 1) * tn] + b_ref[:, t * tn:(t + 1) * tn] for t in range(n_tiles)]
        if not to_residue_major:
            for t in range(n_tiles):
                o_ref[:, t * tn:(t + 1) * tn] = out[t]
        else:
            rows_ref = refs[-1]
            per_res = rows_ref.shape[1] // DIL
            for c in range(rows_ref.shape[0]):
                t, off = divmod(c * LANES, tn)
                rows_ref[c] = out[t][:, off:off + LANES]
                for r in range(DIL):
                    o_ref[r, :, c * LANES:(c + 1) * LANES] = rows_ref[c, pl.ds(r, per_res, stride=DIL), :]


def _proj_ln(lhs, ws, res, g, b, *, alpha, to_residue_major, tm=512, tn=1024):
    T, N = res.shape
    n_lhs = len(lhs)
    n_tiles = N // tn
    scratch = [pltpu.VMEM((n_tiles, tm, tn), F32)]
    if to_residue_major:
        na = T // DIL
        out_spec = pl.BlockSpec((DIL, tm // DIL, N), lambda i, j: (0, i, 0))
        out_shape = jax.ShapeDtypeStruct((DIL, na, N), F32)
        scratch.append(pltpu.VMEM((N // LANES, tm, LANES), F32))
    else:
        out_spec = pl.BlockSpec((tm, N), lambda i, j: (i, 0))
        out_shape = jax.ShapeDtypeStruct((T, N), F32)
    in_specs = ([pl.BlockSpec((tm, a.shape[1]), lambda i, j: (i, 0)) for a in lhs]
                + [pl.BlockSpec((w.shape[0], tn), lambda i, j: (0, j)) for w in ws]
                + [pl.BlockSpec((tm, N), lambda i, j: (i, 0)),
                   pl.BlockSpec((1, N), lambda i, j: (0, 0)),
                   pl.BlockSpec((1, N), lambda i, j: (0, 0))])
    out = pl.pallas_call(
        functools.partial(_proj_ln_kernel, n_lhs=n_lhs, alpha=alpha, n_tiles=n_tiles,
                          to_residue_major=to_residue_major),
        name="proj_ln",
        grid=(T // tm, n_tiles),
        in_specs=in_specs,
        out_specs=out_spec,
        out_shape=out_shape,
        scratch_shapes=scratch,
        compiler_params=_cparams(("parallel", "arbitrary")),
    )(*lhs, *ws, res, g.reshape(1, N), b.reshape(1, N))
    return out.reshape(T, N)


def _s5_tables(lam_re, lam_im, log_step, b_re, b_im, c_re, c_im, n_chunks):
    C = S5_CHUNK
    G, P = lam_re.shape
    H = b_re.shape[-1]
    lr = jnp.minimum(lam_re.astype(F32), S5_MAX_RE)
    li = lam_im.astype(F32)
    dt = jnp.exp(log_step.astype(F32))[:, None]
    kk = jnp.arange(C + 1, dtype=F32)[:, None, None]
    pw_mag = jnp.exp(kk * (lr * dt))
    pw_re = pw_mag * jnp.cos(kk * (li * dt))
    pw_im = pw_mag * jnp.sin(kk * (li * dt))
    a_re, a_im = pw_re[1], pw_im[1]
    den = lr * lr + li * li
    nr = a_re - 1.0
    f_re = (nr * lr + a_im * li) / den
    f_im = (a_im * lr - nr * li) / den
    br = b_re.astype(F32)
    bi = b_im.astype(F32)
    bb_re = f_re[..., None] * br - f_im[..., None] * bi
    bb_im = f_re[..., None] * bi + f_im[..., None] * br
    ab_re = pw_re[:C, :, :, None] * bb_re[None] - pw_im[:C, :, :, None] * bb_im[None]
    ab_im = pw_re[:C, :, :, None] * bb_im[None] + pw_im[:C, :, :, None] * bb_re[None]
    cr = c_re.astype(F32)
    ci = c_im.astype(F32)
    z_re = jnp.transpose(ab_re[::-1], (1, 0, 3, 2)).reshape(G, C * H, P)
    z_im = jnp.transpose(ab_im[::-1], (1, 0, 3, 2)).reshape(G, C * H, P)
    zmat = jnp.concatenate([z_re, z_im], axis=-1)
    kern = (jnp.einsum('gop,kgph->kgoh', cr, ab_re, precision=HIGHEST)
            - jnp.einsum('gop,kgph->kgoh', ci, ab_im, precision=HIGHEST))
    lag = jnp.arange(C)[None, :] - jnp.arange(C)[:, None]
    kl = kern[jnp.clip(lag, 0, C - 1)]
    kl = jnp.where((lag >= 0)[:, :, None, None, None], kl, 0.0)
    mmat = jnp.transpose(kl, (2, 0, 4, 1, 3)).reshape(G, C * H, C * H)
    ca_re = cr[None] * pw_re[1:, :, None, :] - ci[None] * pw_im[1:, :, None, :]
    ca_im = cr[None] * pw_im[1:, :, None, :] + ci[None] * pw_re[1:, :, None, :]
    n_re = jnp.transpose(ca_re, (1, 3, 0, 2)).reshape(G, P, C * H)
    n_im = jnp.transpose(-ca_im, (1, 3, 0, 2)).reshape(G, P, C * H)
    nmat = jnp.concatenate([n_re, n_im], axis=1)
    n_steps = max(1, (n_chunks - 1).bit_length())
    qr, qi = pw_re[C], pw_im[C]
    a1, a2 = [], []
    for _ in range(n_steps):
        a1.append(jnp.concatenate([qr, qr], axis=-1))
        a2.append(jnp.concatenate([-qi, qi], axis=-1))
        qr, qi = qr * qr - qi * qi, 2.0 * qr * qi
    a1 = jnp.stack(a1, axis=1)
    a2 = jnp.stack(a2, axis=1)
    return zmat.astype(BF16), mmat.astype(BF16), nmat.astype(BF16), a1, a2


def _s5_kernel(u_ref, z_ref, m_ref, n_ref, a1_ref, a2_ref, y_ref, ub_ref, ug_ref, sg_ref, mw_ref, nw_ref, *,
               n_steps):
    C, H = S5_CHUNK, S5_GROUP
    nc = u_ref.shape[0] // C
    gpt = LANES // H
    ch = C * H
    for j in range(C):
        ub_ref[:, j * LANES:(j + 1) * LANES] = u_ref[pl.ds(j, nc, stride=C), :].astype(BF16)
    row = lax.broadcasted_iota(I32, (nc, LANES), 0)
    sr = lax.broadcasted_iota(I32, (gpt * LANES, LANES), 0)
    sc = lax.broadcasted_iota(I32, (gpt * LANES, LANES), 1)
    sel_hit = sc == H * (sr // LANES) + sr % H
    sel_grp = (sr % LANES) // H
    pr = lax.broadcasted_iota(I32, (ch, C * LANES), 0)
    pc = lax.broadcasted_iota(I32, (ch, C * LANES), 1)
    put_tile = pc // LANES == pr // H
    put_lane = pc % LANES - pr % H

    def group(gl, c):
        sel = jnp.where(sel_hit & (sel_grp == gl), 1.0, 0.0).astype(BF16)
        half_w = gpt * LANES
        u = jnp.concatenate(
            [jnp.dot(ub_ref[:, t * half_w:(t + 1) * half_w], sel, preferred_element_type=F32)
             for t in range(C * LANES // half_w)], axis=1).astype(BF16)
        s = jnp.dot(u, z_ref[gl], preferred_element_type=F32)
        half = s.shape[1] // 2
        a1 = a1_ref[gl]
        a2 = a2_ref[gl]
        for k in range(n_steps):
            sh = 1 << k
            prev = jnp.where(row >= sh, pltpu.roll(s, sh, axis=0), 0.0)
            s = s + a1[k:k + 1, :] * prev + a2[k:k + 1, :] * pltpu.roll(prev, half, axis=1)
        s_in = jnp.where(row >= 1, pltpu.roll(s, 1, axis=0), 0.0)
        put = jnp.where(put_tile & (put_lane == H * gl), 1.0, 0.0).astype(BF16)
        ug_ref[gl] = u
        sg_ref[gl] = s_in.astype(BF16)
        mw_ref[gl] = jnp.dot(m_ref[gl], put, preferred_element_type=F32).astype(BF16)
        nw_ref[gl] = jnp.dot(n_ref[gl], put, preferred_element_type=F32).astype(BF16)
        return c

    lax.fori_loop(0, gpt, group, 0)
    u_all = jnp.concatenate([ug_ref[g] for g in range(gpt)], axis=1)
    s_all = jnp.concatenate([sg_ref[g] for g in range(gpt)], axis=1)
    y = (jnp.dot(u_all, mw_ref[...].reshape(gpt * ch, C * LANES), preferred_element_type=F32)
         + jnp.dot(s_all, nw_ref[...].reshape(gpt * sg_ref.shape[2], C * LANES), preferred_element_type=F32))
    for i in range(C):
        y_ref[pl.ds(i, nc, stride=C), :] = y[:, i * LANES:(i + 1) * LANES]


def _s5_core(h, width, tables):
    zmat, mmat, nmat, a1, a2 = tables
    T = h.shape[0]
    G, CH, P2 = zmat.shape
    C, H = S5_CHUNK, S5_GROUP
    nc = T // C
    n_steps = a1.shape[1]
    gpt = LANES // H
    assert CH == C * H and width == G * H
    return pl.pallas_call(
        functools.partial(_s5_kernel, n_steps=n_steps),
        name="s5_core",
        grid=(G // gpt,),
        in_specs=[pl.BlockSpec((T, LANES), lambda m: (0, m)),
                  pl.BlockSpec((gpt, CH, P2), lambda m: (m, 0, 0)),
                  pl.BlockSpec((gpt, CH, CH), lambda m: (m, 0, 0)),
                  pl.BlockSpec((gpt, P2, CH), lambda m: (m, 0, 0)),
                  pl.BlockSpec((gpt, n_steps, P2), lambda m: (m, 0, 0)),
                  pl.BlockSpec((gpt, n_steps, P2), lambda m: (m, 0, 0))],
        out_specs=pl.BlockSpec((T, LANES), lambda m: (0, m)),
        out_shape=jax.ShapeDtypeStruct((T, width), F32),
        scratch_shapes=[pltpu.VMEM((nc, C * LANES), BF16),
                        pltpu.VMEM((gpt, nc, CH), BF16),
                        pltpu.VMEM((gpt, nc, P2), BF16),
                        pltpu.VMEM((gpt, CH, C * LANES), BF16),
                        pltpu.VMEM((gpt, P2, C * LANES), BF16)],
        compiler_params=_cparams(("parallel",)),
    )(h, zmat, mmat, nmat, a1, a2)


def _s5_glu_kernel(y_ref, u_ref, d_ref, w_ref, b_ref, o_ref):
    y = y_ref[...] + d_ref[...] * u_ref[...]
    c0 = math.sqrt(2.0 / math.pi)
    z = 0.5 * y * (1.0 + jnp.tanh(c0 * (y + 0.044715 * (y * y * y))))
    lin = jnp.dot(z.astype(BF16), w_ref[...], preferred_element_type=F32) + b_ref[...]
    o_ref[...] = (z * jax.nn.sigmoid(lin)).astype(o_ref.dtype)


def _s5_glu(y, h, d_skip, w_glu, b_glu, *, tm=512):
    T, W = y.shape
    return pl.pallas_call(
        _s5_glu_kernel,
        name="s5_glu",
        grid=(T // tm,),
        in_specs=[pl.BlockSpec((tm, W), lambda i: (i, 0)),
                  pl.BlockSpec((tm, W), lambda i: (i, 0)),
                  pl.BlockSpec((1, W), lambda i: (0, 0)),
                  pl.BlockSpec((W, W), lambda i: (0, 0)),
                  pl.BlockSpec((1, W), lambda i: (0, 0))],
        out_specs=pl.BlockSpec((tm, W), lambda i: (i, 0)),
        out_shape=jax.ShapeDtypeStruct((T, W), BF16),
        compiler_params=_cparams(("parallel",)),
    )(y, h, d_skip.reshape(1, W), w_glu, b_glu.reshape(1, W))


def _gla_kernel(q_ref, k_ref, v_ref, r_ref, g_ref, w2_ref, b2_ref, ng_ref, o_ref, st_ref, *, dk, dv):
    @pl.when(pl.program_id(0) == 0)
    def _():
        st_ref[...] = jnp.zeros_like(st_ref)

    cb = GLA_BLOCK
    n_sub = cb // GLA_SUB
    scale = dk ** -0.5
    ri = lax.broadcasted_iota(I32, (cb, cb), 0)
    ci = lax.broadcasted_iota(I32, (cb, cb), 1)
    tri = (ri >= ci).astype(F32)
    n_blk = q_ref.shape[0] // cb
    rows = [slice(blk * cb, (blk + 1) * cb) for blk in range(n_blk)]
    bcums, spans = [], []
    for blk in range(n_blk):
        logit = jnp.dot(g_ref[rows[blk], :], w2_ref[...], preferred_element_type=F32, precision=HIGHEST) + b2_ref[...]
        log_a = (jnp.minimum(logit, 0.0) - jnp.log(1.0 + jnp.exp(-jnp.abs(logit)))) / GLA_GATE_TEMP
        bcum = jnp.dot(tri, log_a, preferred_element_type=F32, precision=HIGHEST)
        ends = [bcum[(a + 1) * GLA_SUB - 1:(a + 1) * GLA_SUB, :] for a in range(n_sub)]
        spans += [-ends[0]] + [ends[a - 1] - ends[a] for a in range(1, n_sub)]
        bcums.append(bcum)
    factorised_ok = jnp.max(jnp.concatenate(spans, axis=0)) < GLA_SAFE_LOG_DECAY

    def run(bcs, states, pairwise):
        for blk in range(n_blk):
            states = _gla_heads(q_ref, k_ref, v_ref, r_ref, ng_ref, o_ref, rows[blk], bcs[blk], states,
                                dk=dk, dv=dv, n_sub=n_sub, scale=scale, pairwise=pairwise)
        return states

    states = lax.cond(factorised_ok,
                      lambda bcs, st: run(bcs, st, False),
                      lambda bcs, st: run(bcs, st, True),
                      tuple(bcums), tuple(st_ref[hh] for hh in range(GLA_HEADS)))
    for hh in range(GLA_HEADS):
        st_ref[hh] = states[hh]


def _gla_heads(q_ref, k_ref, v_ref, r_ref, ng_ref, o_ref, rows, bcum, states, *, dk, dv, n_sub, scale, pairwise):
    cb = GLA_BLOCK
    new_states = []
    for hh in range(GLA_HEADS):
        ks = slice(hh * dk, (hh + 1) * dk)
        vs = slice(hh * dv, (hh + 1) * dv)
        b = bcum[:, ks]
        q = q_ref[rows, ks] * scale
        k = k_ref[rows, ks]
        v = v_ref[rows, vs].astype(BF16)
        refs = [jnp.zeros((1, dk), F32)] + [b[a * GLA_SUB - 1:a * GLA_SUB, :] for a in range(1, n_sub)]
        refmat = jnp.concatenate([jnp.broadcast_to(r, (GLA_SUB, dk)) for r in refs], axis=0)
        qe = (q * jnp.exp(b - refmat)).astype(BF16)
        st = states[hh]
        o_inter = lax.dot_general((q * jnp.exp(b)).astype(BF16), st.astype(BF16),
                                  (((1,), (1,)), ((), ())), preferred_element_type=F32)
        o_rows = []
        for a in range(n_sub):
            lo, hi = a * GLA_SUB, (a + 1) * GLA_SUB
            row_i = lax.broadcasted_iota(I32, (GLA_SUB, hi), 0)
            col_j = lax.broadcasted_iota(I32, (GLA_SUB, hi), 1)
            if not pairwise:
                ke = (k[:hi] * jnp.exp(refs[a] - b[:hi])).astype(BF16)
                att = lax.dot_general(qe[lo:hi], ke, (((1,), (1,)), ((), ())), preferred_element_type=F32)
                att = jnp.where(col_j <= row_i + lo, att, 0.0)
            else:
                key_row = lax.broadcasted_iota(I32, (hi, dk), 0)
                ke = jnp.where(key_row < lo, k[:hi] * jnp.exp(jnp.minimum(refs[a] - b[:hi], 0.0)), 0.0)
                att = lax.dot_general(qe[lo:hi], ke.astype(BF16), (((1,), (1,)), ((), ())),
                                      preferred_element_type=F32)
                b_a, q_a, k_a = b[lo:hi], q[lo:hi], k[lo:hi]
                terms = [q_a * jnp.exp(jnp.minimum(b_a - b_a[j:j + 1, :], 0.0)) * k_a[j:j + 1, :]
                         for j in range(GLA_SUB)]
                sums = jnp.dot(jnp.concatenate(terms, axis=0).astype(BF16), jnp.ones((dk, hi), BF16),
                               preferred_element_type=F32)
                for j in range(GLA_SUB):
                    att = jnp.where((col_j == lo + j) & (row_i >= j), sums[j * GLA_SUB:(j + 1) * GLA_SUB], att)
            o_rows.append(jnp.dot(att.astype(BF16), v[:hi], preferred_element_type=F32))
        o = jnp.concatenate(o_rows, axis=0) + o_inter
        o = o * lax.rsqrt(jnp.mean(o * o, axis=-1, keepdims=True) + GLA_EPS) * ng_ref[...]
        r = r_ref[rows, vs]
        o_ref[rows, vs] = (o * (r * jax.nn.sigmoid(r))).astype(o_ref.dtype)
        b_last = b[cb - 1:cb, :]
        kd = (k * jnp.exp(b_last - b)).astype(BF16)
        upd = lax.dot_general(v, kd, (((0,), (0,)), ((), ())), preferred_element_type=F32)
        new_states.append(st * jnp.exp(b_last) + upd)
    return tuple(new_states)


def _gla(h, g_low, w_gate2, b_gate2, norm_g, *, width):
    T = h.shape[0]
    qk = width // 2
    dk = qk // GLA_HEADS
    dv = width // GLA_HEADS
    cb = GLA_BLOCK * GLA_BLOCKS_PER_STEP
    w2 = jnp.zeros((LANES, qk), F32).at[:w_gate2.shape[0]].set(w_gate2.astype(F32))
    return pl.pallas_call(
        functools.partial(_gla_kernel, dk=dk, dv=dv),
        name="gla",
        grid=(T // cb,),
        in_specs=[pl.BlockSpec((cb, qk), lambda i: (i, 2)),
                  pl.BlockSpec((cb, qk), lambda i: (i, 3)),
                  pl.BlockSpec((cb, width), lambda i: (i, 2)),
                  pl.BlockSpec((cb, width), lambda i: (i, 3)),
                  pl.BlockSpec((cb, LANES), lambda i: (i, 0)),
                  pl.BlockSpec((LANES, qk), lambda i: (0, 0)),
                  pl.BlockSpec((1, qk), lambda i: (0, 0)),
                  pl.BlockSpec((1, dv), lambda i: (0, 0))],
        out_specs=pl.BlockSpec((cb, width), lambda i: (i, 0)),
        out_shape=jax.ShapeDtypeStruct((T, width), BF16),
        scratch_shapes=[pltpu.VMEM((GLA_HEADS, dv, dk), F32)],
        compiler_params=_cparams(("arbitrary",)),
    )(h, h, h, h, g_low, w2, b_gate2.reshape(1, qk).astype(F32), norm_g.reshape(1, dv).astype(F32))


def _attn_kernel(q_ref, k_ref, v_ref, o_ref, acc_ref, m_ref, l_ref, v1_ref, pat1_ref, pat4_ref, pat16_ref, *, na):
    dh = v_ref.shape[1]
    v1_ref[:, :dh] = v_ref[...]
    v1_ref[:, dh:] = jnp.ones_like(v_ref)
    acc_ref[...] = jnp.zeros_like(acc_ref)
    m_ref[...] = jnp.full_like(m_ref, NEG_BIG)
    l_ref[...] = jnp.zeros_like(l_ref)

    def token_offsets(rows, r_step, n, axis):
        shape = (n, 1) if axis == 0 else (1, n)
        idx = lax.broadcasted_iota(I32, shape, axis)
        c = idx // rows
        return DIL * (idx - c * rows) + r_step * c

    def delta_pattern(n_chunks, q_rows, k_rows, r_step):
        return (token_offsets(q_rows, r_step, n_chunks * q_rows, 0)
                - token_offsets(k_rows, r_step, n_chunks * k_rows, 1))

    def cat(ref, starts, rows):
        return jnp.concatenate([ref[pl.ds(pl.multiple_of(s, 16), rows), :] for s in starts], axis=0)

    def window_bias(n_chunks, q_rows, k_rows, r_step, back, window):
        pat = delta_pattern(n_chunks, q_rows, k_rows, r_step)
        inside = lambda off: jnp.where((pat + off >= 0) & (pat + off <= window), 0.0, NEG_BIG)
        return jnp.stack([inside(0), inside(DIL * back)])

    def attend(blocks, q_rows, k_rows, bias_ref):
        loaded = []
        for q_starts, k_starts, a_q, a_k in blocks:
            loaded.append((cat(q_ref, q_starts, q_rows), cat(k_ref, k_starts, k_rows), cat(v1_ref, k_starts, k_rows),
                           cat(m_ref, q_starts, q_rows), cat(l_ref, q_starts, q_rows),
                           cat(acc_ref, q_starts, q_rows)))
        results = []
        for (q_starts, k_starts, a_q, a_k), (qb, kb, vb, m_old, l_old, acc_old) in zip(blocks, loaded):
            s = lax.dot_general(qb, kb, (((1,), (1,)), ((), ())), preferred_element_type=F32)
            s = s + bias_ref[jnp.where(a_q == a_k, 0, 1)]
            m_new = jnp.maximum(m_old, jnp.max(s, axis=1, keepdims=True))
            alpha = jnp.exp(m_old - m_new)
            p = jnp.exp(s - m_new[:, :1])
            pv = jnp.dot(p.astype(BF16), vb, preferred_element_type=F32)
            l_new = alpha * l_old + pv[:, dh:]
            acc_new = alpha * acc_old + pv[:, :dh]
            results.append((m_new, l_new, acc_new))
        for (q_starts, *_), (m_new, l_new, acc_new) in zip(blocks, results):
            for c, st in enumerate(q_starts):
                st = pl.multiple_of(st, 16)
                rs = slice(c * q_rows, (c + 1) * q_rows)
                m_ref[pl.ds(st, q_rows), :] = m_new[rs]
                l_ref[pl.ds(st, q_rows), :] = l_new[rs]
                acc_ref[pl.ds(st, q_rows), :] = acc_new[rs]

    w1 = DILATED_GROUPS[0][0]
    u1 = 2
    pat1_ref[...] = window_bias(DIL, 16, 32, 1, 16, w1)

    def body1(it, carry):
        blocks = []
        for j in range(u1):
            a0 = (it * u1 + j) * 16
            ak = jnp.maximum(a0 - 16, 0)
            blocks.append(([r * na + a0 for r in range(DIL)], [r * na + ak for r in range(DIL)], a0, ak))
        attend(blocks, 16, 32, pat1_ref)
        return carry

    lax.fori_loop(0, na // (16 * u1), body1, 0)

    w4 = DILATED_GROUPS[1][0]
    pat4_ref[...] = window_bias(4, 32, 64, 4, 32, w4)

    def body4(it, carry):
        a0 = it * 32
        ak = jnp.maximum(a0 - 32, 0)
        blocks = [([(rho + 4 * sg) * na + a0 for sg in range(4)], [(rho + 4 * sg) * na + ak for sg in range(4)],
                   a0, ak) for rho in range(4)]
        attend(blocks, 32, 64, pat4_ref)
        return carry

    lax.fori_loop(0, na // 32, body4, 0)

    w16 = DILATED_GROUPS[2][0]
    u16 = 4
    pat16_ref[...] = window_bias(1, 128, 256, 0, 128, w16)

    def body16(it, carry):
        rg = it // (na // 128)
        a0 = (it - rg * (na // 128)) * 128
        ak = jnp.maximum(a0 - 128, 0)
        blocks = [([(rg * u16 + j) * na + a0], [(rg * u16 + j) * na + ak], a0, ak) for j in range(u16)]
        attend(blocks, 128, 256, pat16_ref)
        return carry

    lax.fori_loop(0, (DIL // u16) * (na // 128), body16, 0)

    o_ref[...] = (acc_ref[...] / l_ref[...]).astype(o_ref.dtype)


def _dilated_attention(qkv, d_model):
    T = qkv.shape[0]
    dh = d_model // ATT_HEADS
    assert dh == LANES
    na = T // DIL
    return pl.pallas_call(
        functools.partial(_attn_kernel, na=na),
        name="dilated_attn",
        grid=(ATT_HEADS,),
        in_specs=[pl.BlockSpec((T, dh), lambda h: (0, h)),
                  pl.BlockSpec((T, dh), lambda h: (0, ATT_HEADS + h)),
                  pl.BlockSpec((T, dh), lambda h: (0, 2 * ATT_HEADS + h))],
        out_specs=pl.BlockSpec((T, dh), lambda h: (0, h)),
        out_shape=jax.ShapeDtypeStruct((T, d_model), BF16),
        scratch_shapes=[pltpu.VMEM((T, dh), F32), pltpu.VMEM((T, dh), F32), pltpu.VMEM((T, dh), F32),
                        pltpu.VMEM((T, 2 * dh), BF16), pltpu.VMEM((2, 256, 512), F32),
                        pltpu.VMEM((2, 128, 256), F32), pltpu.VMEM((2, 128, 256), F32)],
        compiler_params=_cparams(("parallel",)),
    )(qkv, qkv, qkv)


def _route_kernel(x_ref, w_ref, b_ref, idx_ref, gate_ref, rank_ref, cnt_ref, carry_ref):
    @pl.when(pl.program_id(0) == 0)
    def _():
        carry_ref[...] = jnp.zeros_like(carry_ref)

    tb = x_ref.shape[0]
    x = x_ref[...]
    w = w_ref[...]
    x_hi, w_hi = x.astype(BF16), w.astype(BF16)
    x_lo = (x - x_hi.astype(F32)).astype(BF16)
    w_lo = (w - w_hi.astype(F32)).astype(BF16)
    lg = (jnp.dot(x_hi, w_hi, preferred_element_type=F32) + jnp.dot(x_hi, w_lo, preferred_element_type=F32)
          + jnp.dot(x_lo, w_hi, preferred_element_type=F32)) + b_ref[...]
    lane = lax.broadcasted_iota(I32, lg.shape, 1)
    vals, hots = [], []
    idx_out = jnp.zeros(lg.shape, I32)
    for k in range(TOP_K):
        m = jnp.max(lg, axis=1, keepdims=True)
        sel = jnp.min(jnp.where(lg == m, lane, LANES), axis=1, keepdims=True)
        hot = lane == sel
        vals.append(m)
        hots.append(hot)
        idx_out = jnp.where(lane == k, sel, idx_out)
        lg = jnp.where(hot, -jnp.inf, lg)
    ex = [jnp.exp(v - vals[0]) for v in vals]
    den = sum(ex)
    gate_out = jnp.zeros(lg.shape, F32)
    for k in range(TOP_K):
        gate_out = jnp.where(lane == k, ex[k] / den, gate_out)
    chosen = sum(h.astype(F32) for h in hots)
    ri = lax.broadcasted_iota(I32, (tb, tb), 0)
    ci = lax.broadcasted_iota(I32, (tb, tb), 1)
    before = jnp.dot((ri > ci).astype(BF16), chosen.astype(BF16), preferred_element_type=F32) + carry_ref[...]
    rank_out = jnp.zeros(lg.shape, I32)
    for k in range(TOP_K):
        rk = jnp.sum(jnp.where(hots[k], before, 0.0), axis=1, keepdims=True).astype(I32)
        rank_out = jnp.where(lane == k, rk, rank_out)
    idx_ref[...] = idx_out
    gate_ref[...] = gate_out
    rank_ref[...] = rank_out
    carry_ref[...] = carry_ref[...] + jnp.sum(chosen, axis=0, keepdims=True)
    cnt_ref[...] = carry_ref[...].astype(I32)


def _route(x, w_router, b_router, *, tb=512):
    T, D = x.shape
    E = w_router.shape[1]
    wr = jnp.zeros((D, LANES), F32).at[:, :E].set(w_router.astype(F32))
    br = jnp.full((1, LANES), NEG_BIG, F32).at[0, :E].set(b_router.astype(F32))
    row = lambda dt: jax.ShapeDtypeStruct((T, LANES), dt)
    idx, gate, rank, cnt = pl.pallas_call(
        _route_kernel,
        name="moe_route",
        grid=(T // tb,),
        in_specs=[pl.BlockSpec((tb, D), lambda i: (i, 0)),
                  pl.BlockSpec((D, LANES), lambda i: (0, 0)),
                  pl.BlockSpec((1, LANES), lambda i: (0, 0))],
        out_specs=[pl.BlockSpec((tb, LANES), lambda i: (i, 0)),
                   pl.BlockSpec((tb, LANES), lambda i: (i, 0)),
                   pl.BlockSpec((tb, LANES), lambda i: (i, 0)),
                   pl.BlockSpec((1, LANES), lambda i: (0, 0))],
        out_shape=[row(I32), row(F32), row(I32), jax.ShapeDtypeStruct((1, LANES), I32)],
        scratch_shapes=[pltpu.VMEM((1, LANES), F32)],
        compiler_params=_cparams(("arbitrary",)),
    )(x, wr, br)
    return idx, gate, rank, cnt[0, :E]


def _zero_tail_kernel(row_ref, o_ref, z_ref, sem):
    z_ref[...] = jnp.zeros_like(z_ref)
    n = row_ref.shape[0]

    def copy(e):
        return pltpu.make_async_copy(z_ref, o_ref.at[pl.ds(pl.multiple_of(row_ref[e], MOE_BLOCK), MOE_BLOCK)], sem)

    def start(e, c):
        copy(e).start()
        return c

    def wait(e, c):
        copy(e).wait()
        return c

    lax.fori_loop(0, n, start, 0)
    lax.fori_loop(0, n, wait, 0)


def _zero_tails(tail_rows, n_rows, d):
    return pl.pallas_call(
        _zero_tail_kernel,
        name="moe_zero_tails",
        grid_spec=pltpu.PrefetchScalarGridSpec(
            num_scalar_prefetch=1, grid=(1,),
            in_specs=[],
            out_specs=pl.BlockSpec(memory_space=pl.ANY),
            scratch_shapes=[pltpu.VMEM((MOE_BLOCK, d), F32), pltpu.SemaphoreType.DMA(())]),
        out_shape=jax.ShapeDtypeStruct((n_rows, d), F32),
        compiler_params=_cparams(("arbitrary",)),
    )(tail_rows)


def _dispatch_kernel(dest_ref, x_ref, xs_in_ref, xs_ref, sem):
    del xs_in_ref
    tb = x_ref.shape[0]
    base = pl.program_id(0) * tb * TOP_K

    def copy(i, k):
        return pltpu.make_async_copy(x_ref.at[pl.ds(i, 1)],
                                     xs_ref.at[pl.ds(dest_ref[base + i * TOP_K + k], 1)], sem)

    def start(i, c):
        for k in range(TOP_K):
            copy(i, k).start()
        return c

    lax.fori_loop(0, tb, start, 0, unroll=4)
    for _ in range(TOP_K):
        pltpu.make_async_copy(x_ref, xs_ref.at[pl.ds(0, tb)], sem).wait()


def _dispatch(x, dest_flat, xs_init, *, tb=512):
    T, D = x.shape
    return pl.pallas_call(
        _dispatch_kernel,
        name="moe_dispatch",
        grid_spec=pltpu.PrefetchScalarGridSpec(
            num_scalar_prefetch=1, grid=(T // tb,),
            in_specs=[pl.BlockSpec((tb, D), lambda i, dest: (i, 0)),
                      pl.BlockSpec(memory_space=pl.ANY)],
            out_specs=pl.BlockSpec(memory_space=pl.ANY),
            scratch_shapes=[pltpu.SemaphoreType.DMA(())]),
        out_shape=jax.ShapeDtypeStruct(xs_init.shape, xs_init.dtype),
        input_output_aliases={2: 0},
        compiler_params=_cparams(("arbitrary",)),
    )(dest_flat, x, xs_init)


def _moe_kernel(item_e_ref, item_row_ref, item_nb_ref, xs_ref, wg_ref, wu_ref, bg_ref, bu_ref, wd_ref, bd_ref,
                ys_ref, xbuf_ref, land_ref, acc_ref, in_sem, out_sem, *, n_f):
    it = pl.program_id(0)
    f = pl.program_id(1)
    nb = item_nb_ref[it]
    row0 = item_row_ref[it]
    blk = MOE_BLOCK

    def acc_rows(j):
        return pl.ds(pl.multiple_of(j * blk, blk), blk)

    def in_copy(item, j):
        src = xs_ref.at[pl.ds(pl.multiple_of(item_row_ref[item] + j * blk, blk), blk)]
        return pltpu.make_async_copy(src, land_ref.at[acc_rows(j)], in_sem)

    def out_copy(j):
        dst = ys_ref.at[pl.ds(pl.multiple_of(row0 + j * blk, blk), blk)]
        return pltpu.make_async_copy(acc_ref.at[acc_rows(j)], dst, out_sem)

    def for_blocks(fn, n=nb):
        def body(j, c):
            fn(j)
            return c

        lax.fori_loop(0, n, body, 0)

    @pl.when((f == 0) & (nb > 0))
    def _():
        @pl.when(it == 0)
        def _():
            for_blocks(lambda j: in_copy(it, j).start())

        for_blocks(lambda j: in_copy(it, j).wait())

        def to_bf16(j):
            xbuf_ref[acc_rows(j), :] = land_ref[acc_rows(j), :].astype(BF16)
            acc_ref[acc_rows(j), :] = jnp.broadcast_to(bd_ref[...], (blk, acc_ref.shape[1]))

        for_blocks(to_bf16)

    @pl.when((f == 1) & (it + 1 < pl.num_programs(0)))
    def _():
        nxt = it + 1
        for_blocks(lambda j: in_copy(nxt, j).start(), item_nb_ref[nxt])

    @pl.when(nb > 0)
    def _():
        def compute(r0, n_rows, wg, wu, wd):
            rows = pl.ds(pl.multiple_of(r0, blk), n_rows)
            xb = xbuf_ref[rows, :]
            hg = jnp.dot(xb, wg, preferred_element_type=F32) + bg_ref[...]
            hu = jnp.dot(xb, wu, preferred_element_type=F32) + bu_ref[...]
            gate = jnp.minimum(hg, SWIGLU_LIMIT)
            up = jnp.clip(hu, -SWIGLU_LIMIT, SWIGLU_LIMIT)
            act = (up + 1.0) * (gate * jax.nn.sigmoid(SWIGLU_ALPHA * gate))
            acc_ref[rows, :] += jnp.dot(act.astype(BF16), wd, preferred_element_type=F32)

            @pl.when(f == n_f - 1)
            def _():
                for b in range(n_rows // blk):
                    out_copy(r0 // blk + b).start()

        def piece(r0, n_rows):
            compute(r0, n_rows, wg_ref[...].astype(BF16), wu_ref[...].astype(BF16), wd_ref[...].astype(BF16))

        n4 = nb // 4

        def quad(j, c):
            piece(j * (4 * blk), 4 * blk)
            return c

        lax.fori_loop(0, n4, quad, 0)

        @pl.when((nb & 2) != 0)
        def _():
            piece(n4 * (4 * blk), 2 * blk)

        @pl.when((nb & 1) != 0)
        def _():
            piece((nb - 1) * blk, blk)

    @pl.when((f == n_f - 1) & (nb > 0))
    def _():
        for_blocks(lambda j: out_copy(j).wait())


def _moe_experts(xs, items, w_gu, b_gu, w_down, b_down, layer, *, tf=512):
    item_e, item_row, item_nb = items
    n_items = item_e.shape[0]
    P, D = xs.shape
    L, E, _, F2 = w_gu.shape
    F = F2 // 2
    n_f = F // tf
    assert n_f >= 2
    rc = MOE_CHUNK_BLOCKS * MOE_BLOCK

    def fe(f, nb_ref, it):
        return jnp.where(nb_ref[it] > 0, f, n_f - 1)

    in_specs = [
        pl.BlockSpec(memory_space=pl.ANY),
        pl.BlockSpec((None, None, D, tf), lambda it, f, e, r, nb: (layer, e[it], 0, fe(f, nb, it))),
        pl.BlockSpec((None, None, D, tf), lambda it, f, e, r, nb: (layer, e[it], 0, n_f + fe(f, nb, it))),
        pl.BlockSpec((None, None, 1, tf), lambda it, f, e, r, nb: (layer, e[it], 0, fe(f, nb, it))),
        pl.BlockSpec((None, None, 1, tf), lambda it, f, e, r, nb: (layer, e[it], 0, n_f + fe(f, nb, it))),
        pl.BlockSpec((None, None, tf, D), lambda it, f, e, r, nb: (layer, e[it], fe(f, nb, it), 0)),
        pl.BlockSpec((None, None, 1, D), lambda it, f, e, r, nb: (layer, e[it], 0, 0)),
    ]
    return pl.pallas_call(
        functools.partial(_moe_kernel, n_f=n_f),
        name="moe_experts",
        grid_spec=pltpu.PrefetchScalarGridSpec(
            num_scalar_prefetch=3, grid=(n_items, n_f),
            in_specs=in_specs,
            out_specs=pl.BlockSpec(memory_space=pl.ANY),
            scratch_shapes=[pltpu.VMEM((rc, D), BF16),
                            pltpu.VMEM((rc, D), F32),
                            pltpu.VMEM((rc, D), F32),
                            pltpu.SemaphoreType.DMA(()),
                            pltpu.SemaphoreType.DMA(())]),
        out_shape=jax.ShapeDtypeStruct((P, D), F32),
        compiler_params=_cparams(("arbitrary", "arbitrary")),
    )(item_e, item_row, item_nb, xs, w_gu, w_gu, b_gu.reshape(L, E, 1, F2), b_gu.reshape(L, E, 1, F2),
      w_down, b_down.reshape(L, E, 1, D))


def _combine_kernel(dest_ref, ys_ref, gate_ref, x_ref, g_ref, b_ref, o_ref, buf_ref, sem, *scratch,
                    alpha, to_natural):
    tb = o_ref.shape[0]
    step = pl.program_id(0)
    slot = step % 2
    per_res = tb // DIL

    def token(blk, i):
        if not to_natural:
            return blk * tb + i
        n_a = pl.num_programs(0) * per_res
        return (i % DIL) * n_a + blk * per_res + i // DIL

    def copy(blk, sl, i, k):
        src = ys_ref.at[pl.ds(dest_ref[token(blk, i) * TOP_K + k], 1)]
        return pltpu.make_async_copy(src, buf_ref.at[sl, k, pl.ds(i, 1)], sem.at[sl])

    def for_rows(fn):
        def body(i, c):
            for k in range(TOP_K):
                fn(i, k)
            return c

        lax.fori_loop(0, tb, body, 0, unroll=4)

    @pl.when(step == 0)
    def _():
        for_rows(lambda i, k: copy(0, 0, i, k).start())

    @pl.when(step + 1 < pl.num_programs(0))
    def _():
        for_rows(lambda i, k: copy(step + 1, 1 - slot, i, k).start())

    for k in range(TOP_K):
        pltpu.make_async_copy(ys_ref.at[pl.ds(0, tb)], buf_ref.at[slot, k], sem.at[slot]).wait()
    if to_natural:
        xn_ref, gn_ref = scratch
        for r in range(DIL):
            gn_ref[pl.ds(r, per_res, stride=DIL), :] = gate_ref[r]
            for c in range(xn_ref.shape[0]):
                xn_ref[c, pl.ds(r, per_res, stride=DIL), :] = x_ref[r, :, c * LANES:(c + 1) * LANES]
        x = jnp.concatenate([xn_ref[c] for c in range(xn_ref.shape[0])], axis=1)
        gate = gn_ref[...]
    else:
        x = x_ref[...]
        gate = gate_ref[...]
    ffn = gate[:, 0:1] * buf_ref[slot, 0]
    for k in range(1, TOP_K):
        ffn = ffn + gate[:, k:k + 1] * buf_ref[slot, k]
    o_ref[...] = _layer_norm_rows(alpha * x + ffn, g_ref[...], b_ref[...])


def _combine_ln(ys, dest_flat, gates, x, g, b, *, alpha, to_natural, tb=128):
    T, D = x.shape
    gate_pad = gates
    scratch = [pltpu.VMEM((2, TOP_K, tb, D), F32), pltpu.SemaphoreType.DMA((2,))]
    if to_natural:
        na = T // DIL
        per_res = tb // DIL
        gate_in = gate_pad.reshape(DIL, na, LANES)
        x_in = x.reshape(DIL, na, D)
        gate_spec = pl.BlockSpec((DIL, per_res, LANES), lambda i, dest: (0, i, 0))
        x_spec = pl.BlockSpec((DIL, per_res, D), lambda i, dest: (0, i, 0))
        scratch += [pltpu.VMEM((D // LANES, tb, LANES), F32), pltpu.VMEM((tb, LANES), F32)]
    else:
        gate_in, x_in = gate_pad, x
        gate_spec = pl.BlockSpec((tb, LANES), lambda i, dest: (i, 0))
        x_spec = pl.BlockSpec((tb, D), lambda i, dest: (i, 0))
    return pl.pallas_call(
        functools.partial(_combine_kernel, alpha=alpha, to_natural=to_natural),
        name="moe_combine",
        grid_spec=pltpu.PrefetchScalarGridSpec(
            num_scalar_prefetch=1, grid=(T // tb,),
            in_specs=[pl.BlockSpec(memory_space=pl.ANY),
                      gate_spec,
                      x_spec,
                      pl.BlockSpec((1, D), lambda i, dest: (0, 0)),
                      pl.BlockSpec((1, D), lambda i, dest: (0, 0))],
            out_specs=pl.BlockSpec((tb, D), lambda i, dest: (i, 0)),
            scratch_shapes=scratch),
        out_shape=jax.ShapeDtypeStruct((T, D), F32),
        compiler_params=_cparams(("arbitrary",)),
    )(dest_flat, ys, gate_in, x_in, g.reshape(1, D), b.reshape(1, D))


def _moe_layout(idx, rank, counts, n_assign):
    E = counts.shape[0]
    blk = MOE_BLOCK
    nblk = (counts + blk - 1) // blk
    bend = jnp.cumsum(nblk)
    bstart = bend - nblk
    first_row = (bstart * blk).astype(I32)
    dest = rank
    for e in range(E):
        dest = dest + jnp.where(idx == e, first_row[e], 0)
    dest = dest[:, :TOP_K]
    tail = jnp.where(nblk > 0, (bend - 1) * blk, 0).astype(I32)
    cb = MOE_CHUNK_BLOCKS
    n_items_max = E + (n_assign // blk + E) // cb + 1
    per_e = (nblk + cb - 1) // cb
    iend = jnp.cumsum(per_e)
    istart = iend - per_e
    ids = jnp.arange(n_items_max, dtype=I32)
    e_of = jnp.minimum(jnp.searchsorted(iend, ids, side='right'), E - 1).astype(I32)
    valid = ids < iend[-1]
    last_e = jnp.max(jnp.where(per_e > 0, jnp.arange(E, dtype=I32), 0))
    e_of = jnp.where(valid, e_of, last_e)
    local = ids - istart[e_of]
    first_blk = bstart[e_of] + local * cb
    nb = jnp.where(valid, jnp.minimum(cb, nblk[e_of] - local * cb), 0)
    row = jnp.where(valid, first_blk * blk, 0)
    return dest.astype(I32), tail, (e_of, row.astype(I32), nb.astype(I32))


def _moe(x, w_router, b_router, w_gu, b_gu, w_down, b_down, ln_g, ln_b, *, alpha, layer, to_natural=False):
    T, D = x.shape
    E = w_router.shape[1]
    idx, gates, rank, counts = _route(x, w_router, b_router)
    n_assign = T * TOP_K
    dest, tail, items = _moe_layout(idx, rank, counts, n_assign)
    dest_flat = dest.reshape(n_assign)
    n_rows = n_assign + E * MOE_BLOCK
    xs = _dispatch(x, dest_flat, _zero_tails(tail, n_rows, D))
    ys = _moe_experts(xs, items, w_gu, b_gu, w_down, b_down, layer)
    return _combine_ln(ys, dest_flat, gates, x, ln_g, ln_b, alpha=alpha, to_natural=to_natural)


def _even_mixer_ln(x, w_in, lam_re, lam_im, log_step, b_re, b_im, c_re, c_im, d_skip, w_glu, b_glu,
                   w_gate2, b_gate2, norm_g, w_out, ln_g, ln_b, *, alpha, to_residue_major):
    T, D = x.shape
    W = d_skip.shape[0]
    qk = W // 2
    rank = w_gate2.shape[0]
    s4 = W + 2 * qk + W
    w_main = jnp.concatenate([w_in[:, :s4], w_in[:, s4 + rank:]], axis=1).astype(BF16)
    w_gate = jnp.zeros((D, LANES), BF16).at[:, :rank].set(w_in[:, s4:s4 + rank].astype(BF16))
    h, g_low = _proj_in(x, w_main, w_gate)
    tables = _s5_tables(lam_re, lam_im, log_step, b_re, b_im, c_re, c_im, T // S5_CHUNK)
    y = _s5_core(h, W, tables)
    ya = _s5_glu(y, h, d_skip.astype(F32), w_glu.astype(BF16), b_glu.astype(F32))
    yb = _gla(h, g_low, w_gate2, b_gate2, norm_g, width=W)
    w_out_b = w_out.astype(BF16)
    return _proj_ln([ya, yb], [w_out_b[:W], w_out_b[W:]], x, ln_g, ln_b, alpha=alpha,
                    to_residue_major=to_residue_major)


def _odd_mixer_ln(x, w_qkv, w_o, ln_g, ln_b, *, alpha):
    T, D = x.shape
    qkv = _proj_scaled(x, w_qkv.astype(BF16), scaled_cols=D, scale=(D // ATT_HEADS) ** -0.5)
    y = _dilated_attention(qkv, D)
    return _proj_ln([y], [w_o.astype(BF16)], x, ln_g, ln_b, alpha=alpha, to_residue_major=False)


def kernel(x, ab_w_in, s5_lam_re, s5_lam_im, s5_log_step, s5_b_re, s5_b_im, s5_c_re, s5_c_im, s5_d, s5_w_glu, s5_b_glu, gla_w_gate2, gla_b_gate2, gla_norm_g, ab_w_out, c_w_qkv, c_w_o, ln1_g, ln1_b, moe_w_router, moe_b_router, moe_w_gu, moe_b_gu, moe_w_down, moe_b_down, ln2_g, ln2_b):
    bsz, L, D = x.shape
    depth = ln1_g.shape[0]
    alpha = (2 * depth) ** 0.25
    outs = []
    for bi in range(bsz):
        xt = x[bi].astype(F32)
        for layer in range(depth):
            i = layer // 2
            odd = layer % 2 == 1
            if not odd:
                xt = _even_mixer_ln(xt, ab_w_in[i], s5_lam_re[i], s5_lam_im[i], s5_log_step[i], s5_b_re[i],
                                    s5_b_im[i], s5_c_re[i], s5_c_im[i], s5_d[i], s5_w_glu[i], s5_b_glu[i],
                                    gla_w_gate2[i], gla_b_gate2[i], gla_norm_g[i], ab_w_out[i],
                                    ln1_g[layer], ln1_b[layer], alpha=alpha, to_residue_major=layer + 1 < depth)
            else:
                xt = _odd_mixer_ln(xt, c_w_qkv[i], c_w_o[i], ln1_g[layer], ln1_b[layer], alpha=alpha)
            xt = _moe(xt, moe_w_router[layer], moe_b_router[layer], moe_w_gu, moe_b_gu,
                      moe_w_down, moe_b_down, ln2_g[layer], ln2_b[layer], alpha=alpha, layer=layer,
                      to_natural=odd)
        outs.append(xt.astype(x.dtype))
    return outs[0].reshape(1, L, D) if bsz == 1 else jnp.stack(outs)
```
